```python
import jax, jax.numpy as jnp
from jax import lax
import numpy as np

D_MODEL = 1024
BATCH = 8
SEQ = 4096
DEPTH = 4

CTX_LEN = 256
GRID_W = 64
HEAD_DIM = 64
BLOCK = 128
MIX_W = 512
N_BRANCH = 3
A_GROUPS = 4
A_GROUP_W = MIX_W // A_GROUPS
B_HEADS = 4
B_VDIM = 2 * HEAD_DIM
C_Q_HEADS = 8
C_KV_HEADS = 2
C_GROUP = C_Q_HEADS // C_KV_HEADS
C_WINDOW = 128
D_FF = 2816
CONV_W = 3
ROPE_BASE = 10000.0
EPS = 1e-6
SCALE = HEAD_DIM ** -0.5
NEG_INF = -1e30
IN_SIZES = (MIX_W, MIX_W,
            2 * B_HEADS * HEAD_DIM, 2 * B_HEADS * HEAD_DIM,
            B_HEADS * B_VDIM,
            C_Q_HEADS * HEAD_DIM,
            C_KV_HEADS * HEAD_DIM, C_KV_HEADS * HEAD_DIM,
            N_BRANCH * D_MODEL)
IN_WIDTH = 6400

kernel_name = 'hybrid_prefix_dit_block'


def rmsnorm(x, g):
    xf = x.astype(jnp.float32)
    y = xf * lax.rsqrt(jnp.mean(xf * xf, axis=-1, keepdims=True) + EPS)
    return (y * g.astype(jnp.float32)).astype(x.dtype)


def layernorm(x, g, b):
    xf = x.astype(jnp.float32)
    mu = jnp.mean(xf, axis=-1, keepdims=True)
    var = jnp.mean(jnp.square(xf - mu), axis=-1, keepdims=True)
    y = (xf - mu) * lax.rsqrt(var + EPS)
    return (y * g.astype(jnp.float32) + b.astype(jnp.float32)).astype(x.dtype)


def ada_params(v, w, b):
    m = (jax.nn.silu(v) @ w + b)[:, None, :]
    return jnp.split(m, 6, axis=-1)


def split_cols(p):
    outs = []
    start = 0
    for size in IN_SIZES:
        outs.append(p[..., start:start + size])
        start += size
    return outs


def rope_tables(length):
    rows_n = length // GRID_W
    rows = jnp.repeat(jnp.arange(rows_n), GRID_W).astype(jnp.float32)
    cols = jnp.tile(jnp.arange(GRID_W), rows_n).astype(jnp.float32)
    half = HEAD_DIM // 2
    inv = ROPE_BASE ** (-jnp.arange(0, half, 2, dtype=jnp.float32) / half)
    ang_r = rows[:, None] * inv[None, :]
    ang_c = cols[:, None] * inv[None, :]
    return (jnp.cos(ang_r), jnp.sin(ang_r), jnp.cos(ang_c), jnp.sin(ang_c))


def _rot(xh, cos, sin):
    x1, x2 = jnp.split(xh, 2, axis=-1)
    cos = cos[None, :, None, :]
    sin = sin[None, :, None, :]
    return jnp.concatenate([x1 * cos - x2 * sin, x2 * cos + x1 * sin], axis=-1)


def apply_rope(x, tables):
    cos_r, sin_r, cos_c, sin_c = tables
    xf = x.astype(jnp.float32)
    xr, xc = jnp.split(xf, 2, axis=-1)
    return jnp.concatenate([_rot(xr, cos_r, sin_r), _rot(xc, cos_c, sin_c)], axis=-1).astype(x.dtype)


def spatial_gating(u, v, ln_g, ln_b, w_s, b_s):
    bsz, length, _ = u.shape
    u = jax.nn.gelu(u)
    vn = layernorm(jax.nn.gelu(v), ln_g, ln_b)
    vz = vn.reshape(bsz, length // BLOCK, BLOCK, A_GROUPS, A_GROUP_W)
    mixed = jnp.einsum('gpq,bnqgc->bnpgc', w_s, vz) + b_s.T[None, None, :, :, None]
    return u * mixed.reshape(bsz, length, MIX_W)


def diff_core(q, k, v, lam):
    s = jnp.einsum('bqhmd,bkhmd->bhmqk', q, k).astype(jnp.float32) * SCALE
    p = jax.nn.softmax(s, axis=-1)
    a = p[:, :, 0] - lam * p[:, :, 1]
    return jnp.einsum('bhqk,bkhe->bqhe', a.astype(v.dtype), v)


def diff_post(o, sub_g, lam_init):
    bsz, length = o.shape[:2]
    return (rmsnorm(o, sub_g) * (1.0 - lam_init)).reshape(bsz, length, MIX_W)


def diff_attention_latent(q, k, v, k_ctx, v_ctx, lam):
    bsz, length = q.shape[:2]
    n_blk = length // BLOCK
    k_all = jnp.concatenate([k, k_ctx], axis=1)
    v_all = jnp.concatenate([v, v_ctx], axis=1)
    qb = q.reshape(bsz, n_blk, BLOCK, B_HEADS, 2, HEAD_DIM).swapaxes(0, 1)
    ob = lax.map(lambda qblk: diff_core(qblk, k_all, v_all, lam), qb)
    return ob.swapaxes(0, 1).reshape(bsz, length, B_HEADS, B_VDIM)


def window_gqa_latent(q, k, v, k_ctx, v_ctx, sink):
    bsz, length = q.shape[:2]
    n_blk = length // BLOCK
    qg = q.reshape(bsz, n_blk, BLOCK, C_KV_HEADS, C_GROUP, HEAD_DIM).swapaxes(0, 1)
    pad = ((0, 0), (BLOCK, BLOCK), (0, 0), (0, 0))
    k_pad = jnp.pad(k, pad)
    v_pad = jnp.pad(v, pad)
    sink_f = sink.astype(jnp.float32).reshape(C_KV_HEADS, C_GROUP)
    n_ctx = k_ctx.shape[1]

    def block(args):
        qblk, n = args
        start = n * BLOCK
        kb = lax.dynamic_slice_in_dim(k_pad, start, 3 * BLOCK, axis=1)
        vb = lax.dynamic_slice_in_dim(v_pad, start, 3 * BLOCK, axis=1)
        qpos = start + jnp.arange(BLOCK)
        kpos = start - BLOCK + jnp.arange(3 * BLOCK)
        rel = kpos[None, :] - qpos[:, None]
        valid = (jnp.abs(rel) <= C_WINDOW) & (kpos >= 0)[None, :] & (kpos < length)[None, :]
        s_loc = jnp.einsum('bqkgd,bjkd->bkgqj', qblk, kb).astype(jnp.float32) * SCALE
        s_loc = jnp.where(valid, s_loc, NEG_INF)
        s_ctx = jnp.einsum('bqkgd,bjkd->bkgqj', qblk, k_ctx).astype(jnp.float32) * SCALE
        s_sink = jnp.broadcast_to(sink_f[None, :, :, None, None], s_loc.shape[:-1] + (1,))
        p = jax.nn.softmax(jnp.concatenate([s_loc, s_ctx, s_sink], axis=-1), axis=-1)
        p_loc = p[..., :3 * BLOCK].astype(v.dtype)
        p_ctx = p[..., 3 * BLOCK:3 * BLOCK + n_ctx].astype(v.dtype)
        return (jnp.einsum('bkgqj,bjkd->bqkgd', p_loc, vb)
                + jnp.einsum('bkgqj,bjkd->bqkgd', p_ctx, v_ctx))

    out = lax.map(block, (qg, jnp.arange(n_blk)))
    return out.swapaxes(0, 1).reshape(bsz, length, MIX_W)


def window_gqa_context(q, k, v, sink):
    bsz, length = q.shape[:2]
    sink_f = sink.astype(jnp.float32).reshape(C_KV_HEADS, C_GROUP)
    s = jnp.einsum('bqkgd,bjkd->bkgqj', q, k).astype(jnp.float32) * SCALE
    s_sink = jnp.broadcast_to(sink_f[None, :, :, None, None], s.shape[:-1] + (1,))
    p = jax.nn.softmax(jnp.concatenate([s, s_sink], axis=-1), axis=-1)[..., :-1]
    out = jnp.einsum('bkgqj,bjkd->bqkgd', p.astype(v.dtype), v)
    return out.reshape(bsz, length, MIX_W)


def merge_branches(outs, gates, w_branch, w_out):
    bsz, length = gates.shape[:2]
    g = jax.nn.sigmoid(gates.reshape(bsz, length, N_BRANCH, D_MODEL))
    y = g[:, :, 0] * (outs[0] @ w_branch[0])
    for i in range(1, N_BRANCH):
        y = y + g[:, :, i] * (outs[i] @ w_branch[i])
    return y @ w_out


def conv_ffn(h, w_gate, conv_w, conv_b, w_up, w_down):
    a = h @ w_gate
    ap = jnp.pad(a, ((0, 0), (1, 1), (0, 0)))
    a = ap[:, :-2] * conv_w[0] + ap[:, 1:-1] * conv_w[1] + ap[:, 2:] * conv_w[2] + conv_b
    return (jax.nn.silu(a) * (h @ w_up)) @ w_down


def setup_inputs(seed: int = 0) -> dict:
    key = jax.random.key(seed)
    ks = jax.random.split(key, 32)

    def nrm(k, shape, scale):
        return jax.random.normal(k, shape, jnp.float32) * scale

    D = D_MODEL
    return {
        'x': nrm(ks[0], (BATCH, SEQ, D), 1.0),
        'c': nrm(ks[1], (BATCH, D), 1.0),
        'ctx': nrm(ks[2], (BATCH, CTX_LEN, D), 1.0),
        'c_ctx': nrm(ks[3], (D,), 1.0),
        'w_ada': nrm(ks[4], (DEPTH, D, 6 * D), 0.5 * D ** -0.5),
        'b_ada': nrm(ks[5], (DEPTH, 6 * D), 0.01),
        'norm1_g': 1.0 + nrm(ks[6], (DEPTH, D), 0.02),
        'w_in': nrm(ks[7], (DEPTH, D, IN_WIDTH), D ** -0.5),
        'sgu_ln_g': 1.0 + nrm(ks[8], (DEPTH, MIX_W), 0.02),
        'sgu_ln_b': nrm(ks[9], (DEPTH, MIX_W), 0.02),
        'w_s': nrm(ks[10], (DEPTH, A_GROUPS, BLOCK, BLOCK), BLOCK ** -0.5),
        'b_s': 1.0 + nrm(ks[11], (DEPTH, A_GROUPS, BLOCK), 0.02),
        'lam_q1': nrm(ks[12], (DEPTH, HEAD_DIM), 0.1),
        'lam_k1': nrm(ks[13], (DEPTH, HEAD_DIM), 0.1),
        'lam_q2': nrm(ks[14], (DEPTH, HEAD_DIM), 0.1),
        'lam_k2': nrm(ks[15], (DEPTH, HEAD_DIM), 0.1),
        'diff_subln_g': 1.0 + nrm(ks[16], (DEPTH, B_VDIM), 0.02),
        'sinks': nrm(ks[17], (DEPTH, C_Q_HEADS), 0.5),
        'w_branch': nrm(ks[18], (DEPTH, N_BRANCH, MIX_W, D), MIX_W ** -0.5),
        'w_out': nrm(ks[19], (DEPTH, D, D), D ** -0.5),
        'norm2_g': 1.0 + nrm(ks[20], (DEPTH, D), 0.02),
        'w_gate': nrm(ks[21], (DEPTH, D, D_FF), D ** -0.5),
        'conv_w': nrm(ks[22], (DEPTH, CONV_W, D_FF), CONV_W ** -0.5),
        'conv_b': nrm(ks[23], (DEPTH, D_FF), 0.01),
        'w_up': nrm(ks[24], (DEPTH, D, D_FF), D ** -0.5),
        'w_down': nrm(ks[25], (DEPTH, D_FF, D), D_FF ** -0.5),
        'final_g': 1.0 + nrm(ks[26], (D,), 0.02),
    }


def reference(x, c, ctx, c_ctx, w_ada, b_ada, norm1_g, w_in, sgu_ln_g, sgu_ln_b, w_s, b_s,
              lam_q1, lam_k1, lam_q2, lam_k2, diff_subln_g, sinks, w_branch, w_out,
              norm2_g, w_gate, conv_w, conv_b, w_up, w_down, final_g):
    bsz, length, _ = x.shape
    n_ctx = ctx.shape[1]
    tables = rope_tables(length)
    for l in range(DEPTH):
        last = l == DEPTH - 1
        sh1, sc1, g1, sh2, sc2, g2 = ada_params(c, w_ada[l], b_ada[l])
        csh1, csc1, cg1, csh2, csc2, cg2 = ada_params(c_ctx[None, :], w_ada[l], b_ada[l])

        h = rmsnorm(x, norm1_g[l]) * (1.0 + sc1) + sh1
        hc = rmsnorm(ctx, norm1_g[l]) * (1.0 + csc1) + csh1
        a_u, a_v, b_q, b_k, b_v, c_q, c_k, c_v, gates = split_cols(h @ w_in[l])
        ca_u, ca_v, cb_q, cb_k, cb_v, cc_q, cc_k, cc_v, cgates = split_cols(hc @ w_in[l])

        o_a = spatial_gating(a_u, a_v, sgu_ln_g[l], sgu_ln_b[l], w_s[l], b_s[l])

        lam_init = 0.8 - 0.6 * float(np.exp(-0.3 * l))
        lam = (jnp.exp(jnp.sum(lam_q1[l] * lam_k1[l]).astype(jnp.float32))
               - jnp.exp(jnp.sum(lam_q2[l] * lam_k2[l]).astype(jnp.float32)) + lam_init)
        qb = apply_rope(b_q.reshape(bsz, length, 2 * B_HEADS, HEAD_DIM), tables)
        kb = apply_rope(b_k.reshape(bsz, length, 2 * B_HEADS, HEAD_DIM), tables)
        qb = qb.reshape(bsz, length, B_HEADS, 2, HEAD_DIM)
        kb = kb.reshape(bsz, length, B_HEADS, 2, HEAD_DIM)
        vb = b_v.reshape(bsz, length, B_HEADS, B_VDIM)
        cqb = cb_q.reshape(bsz, n_ctx, B_HEADS, 2, HEAD_DIM)
        ckb = cb_k.reshape(bsz, n_ctx, B_HEADS, 2, HEAD_DIM)
        cvb = cb_v.reshape(bsz, n_ctx, B_HEADS, B_VDIM)
        o_b = diff_post(diff_attention_latent(qb, kb, vb, ckb, cvb, lam), diff_subln_g[l], lam_init)

        qc = apply_rope(c_q.reshape(bsz, length, C_Q_HEADS, HEAD_DIM), tables)
        qc = qc.reshape(bsz, length, C_KV_HEADS, C_GROUP, HEAD_DIM)
        kc = apply_rope(c_k.reshape(bsz, length, C_KV_HEADS, HEAD_DIM), tables)
        vc = c_v.reshape(bsz, length, C_KV_HEADS, HEAD_DIM)
        ckc = cc_k.reshape(bsz, n_ctx, C_KV_HEADS, HEAD_DIM)
        cvc = cc_v.reshape(bsz, n_ctx, C_KV_HEADS, HEAD_DIM)
        o_c = window_gqa_latent(qc, kc, vc, ckc, cvc, sinks[l])

        x = x + g1 * merge_branches((o_a, o_b, o_c), gates, w_branch[l], w_out[l])

        if not last:
            co_a = spatial_gating(ca_u, ca_v, sgu_ln_g[l], sgu_ln_b[l], w_s[l], b_s[l])
            co_b = diff_post(diff_core(cqb, ckb, cvb, lam), diff_subln_g[l], lam_init)
            cqc = cc_q.reshape(bsz, n_ctx, C_KV_HEADS, C_GROUP, HEAD_DIM)
            co_c = window_gqa_context(cqc, ckc, cvc, sinks[l])
            ctx = ctx + cg1 * merge_branches((co_a, co_b, co_c), cgates, w_branch[l], w_out[l])

        h2 = rmsnorm(x, norm2_g[l]) * (1.0 + sc2) + sh2
        x = x + g2 * conv_ffn(h2, w_gate[l], conv_w[l], conv_b[l], w_up[l], w_down[l])
        if not last:
            hc2 = rmsnorm(ctx, norm2_g[l]) * (1.0 + csc2) + csh2
            ctx = ctx + cg2 * conv_ffn(hc2, w_gate[l], conv_w[l], conv_b[l], w_up[l], w_down[l])

    return rmsnorm(x, final_g)
```

```python
import functools

import numpy as np
import jax
import jax.numpy as jnp
from jax import lax
from jax.experimental import pallas as pl
from jax.experimental.pallas import tpu as pltpu

F32 = jnp.float32
BF16 = jnp.bfloat16

D_MODEL = 1024
DEPTH = 4
GRID_W = 64
HEAD_DIM = 64
BLOCK = 128
MIX_W = 512
A_GROUPS = 4
B_HEADS = 4
C_Q_HEADS = 8
D_FF = 2816
ROPE_BASE = 10000.0
EPS = 1e-6
SCALE = HEAD_DIM ** -0.5
NEG_INF = -1e30

LANES = 128
ATTN_COLS = 2304
GATE_COL0 = 3328
VMEM_LIMIT = 56 * 1024 * 1024


def _cparams(n_axes):
    return pltpu.CompilerParams(
        dimension_semantics=("arbitrary",) * n_axes, vmem_limit_bytes=VMEM_LIMIT)


def _rms_mod(x, mul, sh):
    ms = jnp.mean(x * x, axis=-1, keepdims=True)
    return x * lax.rsqrt(ms + EPS) * mul + sh


def _gelu_tanh(x):
    c = np.float32(np.sqrt(2.0 / np.pi))
    return 0.5 * x * (1.0 + jnp.tanh(c * (x + 0.044715 * (x * x * x))))


def _dot(a, b):
    return jnp.dot(a, b, preferred_element_type=F32)


def _dot_nt(a, b):
    return lax.dot_general(a, b, (((1,), (1,)), ((), ())), preferred_element_type=F32)


def _ada_kernel(v_ref, w_ref, b_ref, o_ref):
    v = v_ref[...]
    s = v * jax.nn.sigmoid(v)
    o_ref[...] = _dot(s.astype(BF16), w_ref[...].astype(BF16)) + b_ref[...]


def _ada_call(vpad, w_ada, b_ada):
    rows = vpad.shape[0]
    tn = 1536
    return pl.pallas_call(
        _ada_kernel,
        grid=(DEPTH, 6 * D_MODEL // tn),
        in_specs=[
            pl.BlockSpec((rows, D_MODEL), lambda l, j: (0, 0)),
            pl.BlockSpec((None, D_MODEL, tn), lambda l, j: (l, 0, j)),
            pl.BlockSpec((None, 1, tn), lambda l, j: (l, 0, j)),
        ],
        out_specs=pl.BlockSpec((None, rows, tn), lambda l, j: (l, 0, j)),
        out_shape=jax.ShapeDtypeStruct((DEPTH, rows, 6 * D_MODEL), F32),
        compiler_params=_cparams(2),
        name="ada",
    )(vpad, w_ada, b_ada.reshape(DEPTH, 1, 6 * D_MODEL))


def _inproj_kernel(x_ref, mul_ref, sh_ref, w_ref, cos_ref, sa_ref, sb_ref,
                   bq_ref, bk_ref, bv_ref, cq_ref, ckd_ref, cvd_ref, *, rope):
    h = _rms_mod(x_ref[...], mul_ref[...], sh_ref[...]).astype(BF16)
    if rope:
        cos, sa, sb = cos_ref[...], sa_ref[...], sb_ref[...]

    def rot(y):
        if not rope:
            return y
        return y * cos + pltpu.roll(y, LANES - 16, 1) * sa + pltpu.roll(y, 16, 1) * sb

    def dup(y):
        lane = lax.broadcasted_iota(jnp.int32, y.shape, 1)
        sw = pltpu.roll(y, 64, 1)
        return jnp.concatenate([jnp.where(lane < 64, y, sw), jnp.where(lane < 64, sw, y)], axis=1)

    for col0, ref, roped in ((0, bq_ref, True), (512, bk_ref, True), (1024, bv_ref, False),
                             (1536, cq_ref, True)):
        y = _dot(h, w_ref[:, col0:col0 + 512])
        for g in range(4):
            yg = y[:, g * LANES:(g + 1) * LANES]
            ref[:, g * LANES:(g + 1) * LANES] = (rot(yg) if roped else yg).astype(BF16)
    ykv = _dot(h, w_ref[:, 2048:2304])
    ckd_ref[...] = dup(rot(ykv[:, :LANES])).astype(BF16)
    cvd_ref[...] = dup(ykv[:, LANES:]).astype(BF16)


def _inproj_call(x, mul, sh, w_attn, tables, *, tm, rope):
    bsz, length, _ = x.shape
    tok = lambda w: pl.BlockSpec((None, tm, w), lambda b, i: (b, i, 0))
    vec = pl.BlockSpec((None, 1, D_MODEL), lambda b, i: (b, 0, 0))
    tab = pl.BlockSpec((tm, LANES), lambda b, i: (i, 0))
    widths = (512, 512, 512, 512, 256, 256)
    return pl.pallas_call(
        functools.partial(_inproj_kernel, rope=rope),
        grid=(bsz, length // tm),
        in_specs=[tok(D_MODEL), vec, vec,
                  pl.BlockSpec((D_MODEL, ATTN_COLS), lambda b, i: (0, 0)), tab, tab, tab],
        out_specs=[tok(w) for w in widths],
        out_shape=[jax.ShapeDtypeStruct((bsz, length, w), BF16) for w in widths],
        compiler_params=_cparams(2),
        name="inproj",
    )(x, mul, sh, w_attn, *tables)


def _diff_kernel(sc_ref, q_ref, g_ref, *refs, seg_lens, kc):
    o_ref = refs[-1]
    q = q_ref[...]
    tq = q.shape[0]
    lane = lax.broadcasted_iota(jnp.int32, q.shape, 1)
    zero = jnp.zeros_like(q)
    qq = jnp.concatenate([jnp.where(lane < 64, q, zero), jnp.where(lane >= 64, q, zero)], axis=0)

    def step(kch, vch, carry):
        m, l, acc = carry
        s = _dot_nt(qq, kch)
        mn = jnp.maximum(m, jnp.max(s, axis=-1, keepdims=True))
        alpha = jnp.exp(m - mn)
        p = jnp.exp(s - mn)
        l = alpha * l + jnp.sum(p, axis=-1, keepdims=True)
        acc = alpha * acc + _dot(p.astype(BF16), vch)
        return mn, l, acc

    carry = (jnp.full((2 * tq, 1), NEG_INF, F32), jnp.zeros((2 * tq, 1), F32),
             jnp.zeros((2 * tq, LANES), F32))
    for si, n_keys in enumerate(seg_lens):
        k_ref, v_ref = refs[2 * si], refs[2 * si + 1]
        if n_keys > kc:
            def body(c, car, k_ref=k_ref, v_ref=v_ref):
                off = pl.multiple_of(c * kc, kc)
                return step(k_ref[pl.ds(off, kc), :], v_ref[pl.ds(off, kc), :], car)
            carry = lax.fori_loop(0, n_keys // kc, body, carry)
        else:
            carry = step(k_ref[...], v_ref[...], carry)
    _, l, acc = carry
    o = acc / l
    o = o[:tq] - sc_ref[0] * o[tq:]
    ms = jnp.mean(o * o, axis=-1, keepdims=True)
    o_ref[...] = (o * lax.rsqrt(ms + EPS) * g_ref[...] * sc_ref[1]).astype(BF16)


def _diff_call(scal, q, subln_g, segs, *, tq, kc):
    bsz, lq, _ = q.shape
    in_specs = [pl.BlockSpec(memory_space=pltpu.SMEM),
                pl.BlockSpec((None, tq, LANES), lambda b, h, i: (b, i, h)),
                pl.BlockSpec((1, LANES), lambda b, h, i: (0, 0))]
    args = [scal, q, subln_g]
    for k, v in segs:
        n_keys = k.shape[1]
        spec = pl.BlockSpec((None, n_keys, LANES), lambda b, h, i: (b, 0, h))
        in_specs += [spec, spec]
        args += [k, v]
    return pl.pallas_call(
        functools.partial(_diff_kernel, seg_lens=tuple(k.shape[1] for k, _ in segs), kc=kc),
        grid=(bsz, B_HEADS, lq // tq),
        in_specs=in_specs,
        out_specs=pl.BlockSpec((None, tq, LANES), lambda b, h, i: (b, i, h)),
        out_shape=jax.ShapeDtypeStruct((bsz, lq, MIX_W), BF16),
        compiler_params=_cparams(3),
        name="diff",
    )(*args)


def _win_kernel(sink_ref, q_ref, *refs, local):
    o_ref = refs[-1]
    kx_ref, vx_ref = refs[-3], refs[-2]
    n = pl.program_id(1)
    nb = pl.num_programs(1)
    if local:
        row = lax.broadcasted_iota(jnp.int32, (2 * BLOCK, 3 * BLOCK), 0) & (BLOCK - 1)
        col = lax.broadcasted_iota(jnp.int32, (2 * BLOCK, 3 * BLOCK), 1)
        rel = col - BLOCK - row
        valid = (jnp.abs(rel) <= BLOCK) & ((col >= BLOCK) | (n > 0)) & ((col < 2 * BLOCK) | (n < nb - 1))
    lane = lax.broadcasted_iota(jnp.int32, (BLOCK, LANES), 1)
    for kvh in range(2):
        ksl = slice(kvh * LANES, (kvh + 1) * LANES)
        kx, vx = kx_ref[:, ksl], vx_ref[:, ksl]
        if local:
            kl = jnp.concatenate([refs[j][:, ksl] for j in range(3)], axis=0)
            vl = jnp.concatenate([refs[3 + j][:, ksl] for j in range(3)], axis=0)
        for g2 in range(2):
            grp = kvh * 2 + g2
            qg = q_ref[:, grp * LANES:(grp + 1) * LANES]
            zero = jnp.zeros_like(qg)
            qs = jnp.concatenate([jnp.where(lane < 64, qg, zero), jnp.where(lane >= 64, qg, zero)], axis=0)
            rowi = lax.broadcasted_iota(jnp.int32, (2 * BLOCK, 1), 0)
            sink = jnp.where(rowi < BLOCK, sink_ref[2 * grp], sink_ref[2 * grp + 1])
            s_x = _dot_nt(qs, kx)
            m = jnp.maximum(jnp.max(s_x, axis=-1, keepdims=True), sink)
            if local:
                s_l = jnp.where(valid, _dot_nt(qs, kl), NEG_INF)
                m = jnp.maximum(m, jnp.max(s_l, axis=-1, keepdims=True))
            p_x = jnp.exp(s_x - m)
            l = jnp.sum(p_x, axis=-1, keepdims=True) + jnp.exp(sink - m)
            o = _dot(p_x.astype(BF16), vx)
            if local:
                p_l = jnp.exp(s_l - m)
                l = l + jnp.sum(p_l, axis=-1, keepdims=True)
                o = o + _dot(p_l.astype(BF16), vl)
            o = o / l
            o_ref[:, grp * LANES:(grp + 1) * LANES] = jnp.where(lane < 64, o[:BLOCK], o[BLOCK:]).astype(BF16)


def _win_call(sinks, q, kd, vd, kxd, vxd, *, local):
    bsz, lq, _ = q.shape
    nb = lq // BLOCK
    blk = lambda w, f: pl.BlockSpec((None, BLOCK, w), f)
    in_specs = [pl.BlockSpec(memory_space=pltpu.SMEM), blk(MIX_W, lambda b, n: (b, n, 0))]
    args = [sinks, q]
    if local:
        nbr = [lambda b, n: (b, jnp.maximum(n - 1, 0), 0), lambda b, n: (b, n, 0),
               lambda b, n: (b, jnp.minimum(n + 1, nb - 1), 0)]
        in_specs += [blk(2 * LANES, f) for f in nbr] * 2
        args += [kd] * 3 + [vd] * 3
    n_ctx = kxd.shape[1]
    in_specs += [pl.BlockSpec((None, n_ctx, 2 * LANES), lambda b, n: (b, 0, 0))] * 2
    args += [kxd, vxd]
    return pl.pallas_call(
        functools.partial(_win_kernel, local=local),
        grid=(bsz, nb),
        in_specs=in_specs,
        out_specs=blk(MIX_W, lambda b, n: (b, n, 0)),
        out_shape=jax.ShapeDtypeStruct((bsz, lq, MIX_W), BF16),
        compiler_params=_cparams(2),
        name="win",
    )(*args)


def _merge_kernel(x_ref, mul_ref, sh_ref, gate_ref, wuvg_ref, lng_ref, lnb_ref, ws_ref, bs_ref,
                  ob_ref, oc_ref, wbr_ref, wout_ref, o_ref):
    x = x_ref[...]
    tm = x.shape[0]
    h = _rms_mod(x, mul_ref[...], sh_ref[...]).astype(BF16)
    u = _gelu_tanh(_dot(h, wuvg_ref[:, 0:MIX_W]))
    v = _gelu_tanh(_dot(h, wuvg_ref[:, MIX_W:2 * MIX_W]))
    mu = jnp.mean(v, axis=-1, keepdims=True)
    vc = v - mu
    var = jnp.mean(vc * vc, axis=-1, keepdims=True)
    vn = (vc * lax.rsqrt(var + EPS) * lng_ref[...] + lnb_ref[...]).astype(BF16)
    rows = []
    for c in range(tm // BLOCK):
        cols = []
        for g in range(A_GROUPS):
            blk = vn[c * BLOCK:(c + 1) * BLOCK, g * LANES:(g + 1) * LANES]
            cols.append(_dot(ws_ref[g], blk) + bs_ref[g])
        rows.append(jnp.concatenate(cols, axis=1))
    o_a = (u * jnp.concatenate(rows, axis=0)).astype(BF16)
    y = None
    for i, o_i in enumerate((o_a, ob_ref[...], oc_ref[...])):
        col0 = 2 * MIX_W + i * D_MODEL
        gate = jax.nn.sigmoid(_dot(h, wuvg_ref[:, col0:col0 + D_MODEL]))
        t = gate * _dot(o_i, wbr_ref[i])
        y = t if y is None else y + t
    o_ref[...] = x + gate_ref[...] * _dot(y.astype(BF16), wout_ref[...])


def _merge_call(x, mul, sh, gate, w_uvg, ln_g, ln_b, w_s, b_s, o_b, o_c, w_br, w_out, *, tm):
    bsz, length, _ = x.shape
    tok = lambda w: pl.BlockSpec((None, tm, w), lambda b, i: (b, i, 0))
    vec = pl.BlockSpec((None, 1, D_MODEL), lambda b, i: (b, 0, 0))
    full = lambda a: pl.BlockSpec(a.shape, lambda b, i: (0,) * a.ndim)
    return pl.pallas_call(
        _merge_kernel,
        grid=(bsz, length // tm),
        in_specs=[tok(D_MODEL), vec, vec, vec, full(w_uvg), full(ln_g), full(ln_b), full(w_s), full(b_s),
                  tok(MIX_W), tok(MIX_W), full(w_br), full(w_out)],
        out_specs=tok(D_MODEL),
        out_shape=jax.ShapeDtypeStruct(x.shape, F32),
        compiler_params=_cparams(2),
        name="merge",
    )(x, mul, sh, gate, w_uvg, ln_g, ln_b, w_s, b_s, o_b, o_c, w_br, w_out)


HALO = 8
FF_CHUNK = 1408


def _ffn_kernel(x_ref, xp_ref, xn_ref, mul_ref, sh_ref, gate_ref, wg_ref, cw_ref, cb_ref, wu_ref, wd_ref,
                fg_ref, o_ref, *, final):
    i = pl.program_id(1)
    nt = pl.num_programs(1)
    x = x_ref[...]
    tm = x.shape[0]
    xe = jnp.concatenate([xp_ref[...], x, xn_ref[...]], axis=0)
    he = _rms_mod(xe, mul_ref[...], sh_ref[...]).astype(BF16)
    hm = he[HALO:HALO + tm]
    rowe = lax.broadcasted_iota(jnp.int32, (tm + 2 * HALO, 1), 0)
    keep = ((rowe >= HALO) | (i > 0)) & ((rowe < HALO + tm) | (i < nt - 1))
    acc = None
    for c0 in range(0, D_FF, FF_CHUNK):
        cs = slice(c0, c0 + FF_CHUNK)
        a = jnp.where(keep, _dot(he, wg_ref[:, cs]), 0.0)
        a_prev = pltpu.roll(a, 1, 0)[HALO:HALO + tm]
        a_next = pltpu.roll(a, tm + 2 * HALO - 1, 0)[HALO:HALO + tm]
        a = (a_prev * cw_ref[0:1, cs] + a[HALO:HALO + tm] * cw_ref[1:2, cs] + a_next * cw_ref[2:3, cs]
             + cb_ref[:, cs])
        z = (a * jax.nn.sigmoid(a) * _dot(hm, wu_ref[:, cs])).astype(BF16)
        t = _dot(z, wd_ref[cs, :])
        acc = t if acc is None else acc + t
    out = x + gate_ref[...] * acc
    if final:
        ms = jnp.mean(out * out, axis=-1, keepdims=True)
        out = out * lax.rsqrt(ms + EPS) * fg_ref[...]
    o_ref[...] = out


def _ffn_call(x, mul, sh, gate, w_gate, conv_w, conv_b, w_up, w_down, final_g, *, tm, final):
    bsz, length, _ = x.shape
    per = tm // HALO
    last = length // HALO - 1
    tok = pl.BlockSpec((None, tm, D_MODEL), lambda b, i: (b, i, 0))
    prv = pl.BlockSpec((None, HALO, D_MODEL), lambda b, i: (b, jnp.maximum(i * per - 1, 0), 0))
    nxt = pl.BlockSpec((None, HALO, D_MODEL), lambda b, i: (b, jnp.minimum((i + 1) * per, last), 0))
    vec = pl.BlockSpec((None, 1, D_MODEL), lambda b, i: (b, 0, 0))
    full = lambda a: pl.BlockSpec(a.shape, lambda b, i: (0,) * a.ndim)
    return pl.pallas_call(
        functools.partial(_ffn_kernel, final=final),
        grid=(bsz, length // tm),
        in_specs=[tok, prv, nxt, vec, vec, vec, full(w_gate), full(conv_w), full(conv_b), full(w_up),
                  full(w_down), full(final_g)],
        out_specs=tok,
        out_shape=jax.ShapeDtypeStruct(x.shape, F32),
        compiler_params=_cparams(2),
        name="ffn",
    )(x, x, x, mul, sh, gate, w_gate, conv_w, conv_b, w_up, w_down, final_g)


def _rope_tables(length):
    pos = jnp.arange(length)
    rows = (pos // GRID_W).astype(F32)
    cols = (pos % GRID_W).astype(F32)
    half = HEAD_DIM // 2
    inv = ROPE_BASE ** (-jnp.arange(0, half, 2, dtype=F32) / half)
    ang_r = rows[:, None] * inv[None, :]
    ang_c = cols[:, None] * inv[None, :]
    zero = jnp.zeros_like(ang_r)
    cos = jnp.concatenate([jnp.cos(ang_r)] * 2 + [jnp.cos(ang_c)] * 2, axis=1)
    sa = jnp.concatenate([-jnp.sin(ang_r), zero, -jnp.sin(ang_c), zero], axis=1)
    sb = jnp.concatenate([zero, jnp.sin(ang_r), zero, jnp.sin(ang_c)], axis=1)
    return tuple(jnp.tile(t, (1, LANES // HEAD_DIM)) for t in (cos, sa, sb))


def kernel(x, c, ctx, c_ctx, w_ada, b_ada, norm1_g, w_in, sgu_ln_g, sgu_ln_b, w_s, b_s, lam_q1, lam_k1,
           lam_q2, lam_k2, diff_subln_g, sinks, w_branch, w_out, norm2_g, w_gate, conv_w, conv_b, w_up,
           w_down, final_g):
    bsz, length, _ = x.shape
    n_ctx = ctx.shape[1]
    tables = _rope_tables(length)
    ctx_tables = tuple(t[:n_ctx] for t in tables)

    rows = -(-(bsz + 1) // 8) * 8
    vpad = jnp.zeros((rows, D_MODEL), F32).at[:bsz].set(c).at[bsz].set(c_ctx)
    ada = _ada_call(vpad, w_ada, b_ada)

    qscale = jnp.ones((ATTN_COLS,), F32).at[0:512].set(SCALE).at[1536:2048].set(SCALE)
    w_attn = (w_in[:, :, 2 * MIX_W:2 * MIX_W + ATTN_COLS] * qscale).astype(BF16)
    w_uvg = jnp.concatenate([w_in[:, :, :2 * MIX_W], w_in[:, :, GATE_COL0:]], axis=2).astype(BF16)
    w_s_b = w_s.astype(BF16)
    b_s_b = jnp.broadcast_to(b_s[..., None], b_s.shape + (LANES,))
    w_br_b = w_branch.astype(BF16)
    w_out_b = w_out.astype(BF16)
    w_gate_b = w_gate.astype(BF16)
    w_up_b = w_up.astype(BF16)
    w_down_b = w_down.astype(BF16)
    final_g2 = final_g.reshape(1, D_MODEL)

    for l in range(DEPTH):
        last = l == DEPTH - 1
        m = ada[l]
        sh1, sc1, g1, sh2, sc2, g2 = [m[:, j * D_MODEL:(j + 1) * D_MODEL] for j in range(6)]
        mul1 = norm1_g[l][None, :] * (1.0 + sc1)
        mul2 = norm2_g[l][None, :] * (1.0 + sc2)
        lat = lambda a: a[:bsz, None, :]
        cx = lambda a: jnp.broadcast_to(a[bsz][None, None, :], (bsz, 1, D_MODEL))

        lam_init = 0.8 - 0.6 * float(np.exp(-0.3 * l))
        lam = jnp.exp(jnp.sum(lam_q1[l] * lam_k1[l])) - jnp.exp(jnp.sum(lam_q2[l] * lam_k2[l])) + lam_init
        scal = jnp.stack([lam, jnp.float32(1.0 - lam_init)]).astype(F32)
        subln = diff_subln_g[l].reshape(1, LANES)
        ln_g = sgu_ln_g[l].reshape(1, MIX_W)
        ln_b = sgu_ln_b[l].reshape(1, MIX_W)
        cb2 = conv_b[l].reshape(1, D_FF)

        bq, bk, bv, cq, ckd, cvd = _inproj_call(x, lat(mul1), lat(sh1), w_attn[l], tables, tm=512, rope=True)
        xbq, xbk, xbv, xcq, xckd, xcvd = _inproj_call(ctx, cx(mul1), cx(sh1), w_attn[l], ctx_tables,
                                                     tm=n_ctx, rope=False)

        o_b = _diff_call(scal, bq, subln, [(bk, bv), (xbk, xbv)], tq=256, kc=512)
        o_c = _win_call(sinks[l], cq, ckd, cvd, xckd, xcvd, local=True)
        x_mid = _merge_call(x, lat(mul1), lat(sh1), lat(g1), w_uvg[l], ln_g, ln_b, w_s_b[l], b_s_b[l],
                            o_b, o_c, w_br_b[l], w_out_b[l], tm=256)
        if not last:
            xo_b = _diff_call(scal, xbq, subln, [(xbk, xbv)], tq=n_ctx, kc=512)
            xo_c = _win_call(sinks[l], xcq, None, None, xckd, xcvd, local=False)
            ctx_mid = _merge_call(ctx, cx(mul1), cx(sh1), cx(g1), w_uvg[l], ln_g, ln_b, w_s_b[l], b_s_b[l],
                                  xo_b, xo_c, w_br_b[l], w_out_b[l], tm=n_ctx)
        x = _ffn_call(x_mid, lat(mul2), lat(sh2), lat(g2), w_gate_b[l], conv_w[l], cb2, w_up_b[l],
                      w_down_b[l], final_g2, tm=256, final=last)
        if not last:
            ctx = _ffn_call(ctx_mid, cx(mul2), cx(sh2), cx(g2), w_gate_b[l], conv_w[l], cb2, w_up_b[l],
                            w_down_b[l], final_g2, tm=n_ctx, final=False)
    return x
```

```python
import functools

import numpy as np
import jax
import jax.numpy as jnp
from jax import lax
from jax.experimental import pallas as pl
from jax.experimental.pallas import tpu as pltpu

F32 = jnp.float32
BF16 = jnp.bfloat16

D_MODEL = 1024
DEPTH = 4
GRID_W = 64
HEAD_DIM = 64
BLOCK = 128
MIX_W = 512
A_GROUPS = 4
B_HEADS = 4
C_Q_HEADS = 8
D_FF = 2816
ROPE_BASE = 10000.0
EPS = 1e-6
SCALE = HEAD_DIM ** -0.5
NEG_INF = -1e30
LOG2E = float(np.log2(np.e))

LANES = 128
ATTN_COLS = 2304
GATE_COL0 = 3328
VMEM_LIMIT = 56 * 1024 * 1024


def _cparams(n_axes):
    return pltpu.CompilerParams(
        dimension_semantics=("arbitrary",) * n_axes, vmem_limit_bytes=VMEM_LIMIT)


def _rms_mod(x, mul, sh):
    ms = jnp.mean(x * x, axis=-1, keepdims=True)
    return x * lax.rsqrt(ms + EPS) * mul + sh


def _gelu_tanh(x):
    c = np.float32(np.sqrt(2.0 / np.pi))
    return 0.5 * x * (1.0 + jnp.tanh(c * (x + 0.044715 * (x * x * x))))


def _dot(a, b):
    return jnp.dot(a, b, preferred_element_type=F32)


def _dot_nt(a, b):
    return lax.dot_general(a, b, (((1,), (1,)), ((), ())), preferred_element_type=F32)


def _ada_kernel(v_ref, w_ref, b_ref, o_ref):
    v = v_ref[...]
    s = v * jax.nn.sigmoid(v)
    o_ref[...] = _dot(s.astype(BF16), w_ref[...].astype(BF16)) + b_ref[...]


def _ada_call(vpad, w_ada, b_ada):
    rows = vpad.shape[0]
    tn = 1536
    return pl.pallas_call(
        _ada_kernel,
        grid=(DEPTH, 6 * D_MODEL // tn),
        in_specs=[
            pl.BlockSpec((rows, D_MODEL), lambda l, j: (0, 0)),
            pl.BlockSpec((None, D_MODEL, tn), lambda l, j: (l, 0, j)),
            pl.BlockSpec((None, 1, tn), lambda l, j: (l, 0, j)),
        ],
        out_specs=pl.BlockSpec((None, rows, tn), lambda l, j: (l, 0, j)),
        out_shape=jax.ShapeDtypeStruct((DEPTH, rows, 6 * D_MODEL), F32),
        compiler_params=_cparams(2),
        name="ada",
    )(vpad, w_ada, b_ada.reshape(DEPTH, 1, 6 * D_MODEL))


def _inproj_kernel(x_ref, mul_ref, sh_ref, w_ref, cos_ref, sa_ref, sb_ref,
                   bq_ref, bk_ref, bv_ref, cq_ref, ckd_ref, cvd_ref, *, rope):
    h = _rms_mod(x_ref[...], mul_ref[...], sh_ref[...]).astype(BF16)
    if rope:
        cos, sa, sb = cos_ref[...], sa_ref[...], sb_ref[...]

    def rot(y):
        if not rope:
            return y
        return y * cos + pltpu.roll(y, LANES - 16, 1) * sa + pltpu.roll(y, 16, 1) * sb

    def dup(y):
        lane = lax.broadcasted_iota(jnp.int32, y.shape, 1)
        sw = pltpu.roll(y, 64, 1)
        return jnp.concatenate([jnp.where(lane < 64, y, sw), jnp.where(lane < 64, sw, y)], axis=1)

    for col0, ref, roped in ((0, bq_ref, True), (512, bk_ref, True), (1024, bv_ref, False),
                             (1536, cq_ref, True)):
        y = _dot(h, w_ref[:, col0:col0 + 512])
        for g in range(4):
            yg = y[:, g * LANES:(g + 1) * LANES]
            ref[:, g * LANES:(g + 1) * LANES] = (rot(yg) if roped else yg).astype(BF16)
    ykv = _dot(h, w_ref[:, 2048:2304])
    ckd_ref[...] = dup(rot(ykv[:, :LANES])).astype(BF16)
    cvd_ref[...] = dup(ykv[:, LANES:]).astype(BF16)


def _inproj_call(x, mul, sh, w_attn, tables, *, tm, rope):
    bsz, length, _ = x.shape
    tok = lambda w: pl.BlockSpec((None, tm, w), lambda b, i: (b, i, 0))
    vec = pl.BlockSpec((None, 1, D_MODEL), lambda b, i: (b, 0, 0))
    tab = pl.BlockSpec((tm, LANES), lambda b, i: (i, 0))
    widths = (512, 512, 512, 512, 256, 256)
    return pl.pallas_call(
        functools.partial(_inproj_kernel, rope=rope),
        grid=(bsz, length // tm),
        in_specs=[tok(D_MODEL), vec, vec,
                  pl.BlockSpec((D_MODEL, ATTN_COLS), lambda b, i: (0, 0)), tab, tab, tab],
        out_specs=[tok(w) for w in widths],
        out_shape=[jax.ShapeDtypeStruct((bsz, length, w), BF16) for w in widths],
        compiler_params=_cparams(2),
        name="inproj",
    )(x, mul, sh, w_attn, *tables)


def _diff_kernel(sc_ref, q_ref, g_ref, *refs, seg_lens, kc):
    o_ref, p_scr = refs[-2], refs[-1]
    q = q_ref[...]
    tq = q.shape[0]
    lane = lax.broadcasted_iota(jnp.int32, q.shape, 1)
    zero = jnp.zeros_like(q)
    qq = jnp.concatenate([jnp.where(lane < 64, q, zero), jnp.where(lane >= 64, q, zero)], axis=0)

    chunks = []
    col0 = 0
    for si, n_keys in enumerate(seg_lens):
        for st in range(0, n_keys, kc):
            size = min(kc, n_keys - st)
            chunks.append((refs[2 * si], refs[2 * si + 1], st, size, col0))
            col0 += size

    mcs, lcs = [], []
    for k_ref, _, st, size, c0 in chunks:
        s = _dot_nt(qq, k_ref[st:st + size, :])
        parts = [s[:, j * LANES:(j + 1) * LANES] for j in range(size // LANES)]
        mc = functools.reduce(jnp.maximum, parts)
        lc = None
        for j, sj in enumerate(parts):
            pj = jnp.exp2(sj - mc)
            lc = pj if lc is None else lc + pj
            p_scr[:, c0 + j * LANES:c0 + (j + 1) * LANES] = pj.astype(BF16)
        mcs.append(mc)
        lcs.append(lc)

    m_row = jnp.max(functools.reduce(jnp.maximum, mcs), axis=-1, keepdims=True)
    es = [jnp.exp2(mc - m_row) for mc in mcs]
    l_lane = None
    for lc, e in zip(lcs, es):
        l_lane = lc * e if l_lane is None else l_lane + lc * e
    l_row = jnp.sum(l_lane, axis=-1, keepdims=True)
    rowi = lax.broadcasted_iota(jnp.int32, (2 * tq, 1), 0)
    coef = jnp.where(rowi < tq, 1.0, sc_ref[0]) / l_row

    acc = None
    for (_, v_ref, st, size, c0), e in zip(chunks, es):
        f = (e * coef).astype(BF16)
        f1, f2 = f[:tq], f[tq:]
        a = jnp.concatenate(
            [p_scr[0:tq, c0 + j * LANES:c0 + (j + 1) * LANES] * f1
             - p_scr[tq:2 * tq, c0 + j * LANES:c0 + (j + 1) * LANES] * f2
             for j in range(size // LANES)], axis=1)
        t = _dot(a, v_ref[st:st + size, :])
        acc = t if acc is None else acc + t
    ms = jnp.mean(acc * acc, axis=-1, keepdims=True)
    o_ref[...] = (acc * lax.rsqrt(ms + EPS) * g_ref[...] * sc_ref[1]).astype(BF16)


def _diff_call(scal, q, subln_g, segs, *, tq, kc):
    bsz, lq, _ = q.shape
    in_specs = [pl.BlockSpec(memory_space=pltpu.SMEM),
                pl.BlockSpec((None, tq, LANES), lambda b, h, i: (b, i, h)),
                pl.BlockSpec((1, LANES), lambda b, h, i: (0, 0))]
    args = [scal, q, subln_g]
    for k, v in segs:
        n_keys = k.shape[1]
        spec = pl.BlockSpec((None, n_keys, LANES), lambda b, h, i: (b, 0, h))
        in_specs += [spec, spec]
        args += [k, v]
    return pl.pallas_call(
        functools.partial(_diff_kernel, seg_lens=tuple(k.shape[1] for k, _ in segs), kc=kc),
        grid=(bsz, B_HEADS, lq // tq),
        in_specs=in_specs,
        out_specs=pl.BlockSpec((None, tq, LANES), lambda b, h, i: (b, i, h)),
        out_shape=jax.ShapeDtypeStruct((bsz, lq, MIX_W), BF16),
        scratch_shapes=[pltpu.VMEM((2 * tq, sum(k.shape[1] for k, _ in segs)), BF16)],
        compiler_params=_cparams(3),
        name="diff",
    )(*args)


def _win_kernel(sink_ref, q_ref, *refs, local):
    o_ref = refs[-1]
    kx_ref, vx_ref = refs[-3], refs[-2]
    n = pl.program_id(1)
    nb = pl.num_programs(1)
    if local:
        row = lax.broadcasted_iota(jnp.int32, (2 * BLOCK, 3 * BLOCK), 0) & (BLOCK - 1)
        col = lax.broadcasted_iota(jnp.int32, (2 * BLOCK, 3 * BLOCK), 1)
        rel = col - BLOCK - row
        valid = (jnp.abs(rel) <= BLOCK) & ((col >= BLOCK) | (n > 0)) & ((col < 2 * BLOCK) | (n < nb - 1))
    lane = lax.broadcasted_iota(jnp.int32, (BLOCK, LANES), 1)
    for kvh in range(2):
        ksl = slice(kvh * LANES, (kvh + 1) * LANES)
        kx, vx = kx_ref[:, ksl], vx_ref[:, ksl]
        if local:
            kl = jnp.concatenate([refs[j][:, ksl] for j in range(3)], axis=0)
            vl = jnp.concatenate([refs[3 + j][:, ksl] for j in range(3)], axis=0)
        for g2 in range(2):
            grp = kvh * 2 + g2
            qg = q_ref[:, grp * LANES:(grp + 1) * LANES]
            zero = jnp.zeros_like(qg)
            qs = jnp.concatenate([jnp.where(lane < 64, qg, zero), jnp.where(lane >= 64, qg, zero)], axis=0)
            rowi = lax.broadcasted_iota(jnp.int32, (2 * BLOCK, 1), 0)
            sink = jnp.where(rowi < BLOCK, sink_ref[2 * grp], sink_ref[2 * grp + 1])
            s_x = _dot_nt(qs, kx)
            m = jnp.maximum(jnp.max(s_x, axis=-1, keepdims=True), sink)
            if local:
                s_l = jnp.where(valid, _dot_nt(qs, kl), NEG_INF)
                m = jnp.maximum(m, jnp.max(s_l, axis=-1, keepdims=True))
            p_x = jnp.exp(s_x - m)
            l = jnp.sum(p_x, axis=-1, keepdims=True) + jnp.exp(sink - m)
            o = _dot(p_x.astype(BF16), vx)
            if local:
                p_l = jnp.exp(s_l - m)
                l = l + jnp.sum(p_l, axis=-1, keepdims=True)
                o = o + _dot(p_l.astype(BF16), vl)
            o = o / l
            o_ref[:, grp * LANES:(grp + 1) * LANES] = jnp.where(lane < 64, o[:BLOCK], o[BLOCK:]).astype(BF16)


def _win_call(sinks, q, kd, vd, kxd, vxd, *, local):
    bsz, lq, _ = q.shape
    nb = lq // BLOCK
    blk = lambda w, f: pl.BlockSpec((None, BLOCK, w), f)
    in_specs = [pl.BlockSpec(memory_space=pltpu.SMEM), blk(MIX_W, lambda b, n: (b, n, 0))]
    args = [sinks, q]
    if local:
        nbr = [lambda b, n: (b, jnp.maximum(n - 1, 0), 0), lambda b, n: (b, n, 0),
               lambda b, n: (b, jnp.minimum(n + 1, nb - 1), 0)]
        in_specs += [blk(2 * LANES, f) for f in nbr] * 2
        args += [kd] * 3 + [vd] * 3
    n_ctx = kxd.shape[1]
    in_specs += [pl.BlockSpec((None, n_ctx, 2 * LANES), lambda b, n: (b, 0, 0))] * 2
    args += [kxd, vxd]
    return pl.pallas_call(
        functools.partial(_win_kernel, local=local),
        grid=(bsz, nb),
        in_specs=in_specs,
        out_specs=blk(MIX_W, lambda b, n: (b, n, 0)),
        out_shape=jax.ShapeDtypeStruct((bsz, lq, MIX_W), BF16),
        compiler_params=_cparams(2),
        name="win",
    )(*args)


def _merge_kernel(x_ref, mul_ref, sh_ref, gate_ref, wuvg_ref, lng_ref, lnb_ref, ws_ref, bs_ref,
                  ob_ref, oc_ref, wbr_ref, wout_ref, o_ref):
    x = x_ref[...]
    tm = x.shape[0]
    h = _rms_mod(x, mul_ref[...], sh_ref[...]).astype(BF16)
    u = _gelu_tanh(_dot(h, wuvg_ref[:, 0:MIX_W]))
    v = _gelu_tanh(_dot(h, wuvg_ref[:, MIX_W:2 * MIX_W]))
    mu = jnp.mean(v, axis=-1, keepdims=True)
    vc = v - mu
    var = jnp.mean(vc * vc, axis=-1, keepdims=True)
    vn = (vc * lax.rsqrt(var + EPS) * lng_ref[...] + lnb_ref[...]).astype(BF16)
    rows = []
    for c in range(tm // BLOCK):
        cols = []
        for g in range(A_GROUPS):
            blk = vn[c * BLOCK:(c + 1) * BLOCK, g * LANES:(g + 1) * LANES]
            cols.append(_dot(ws_ref[g], blk) + bs_ref[g])
        rows.append(jnp.concatenate(cols, axis=1))
    o_a = (u * jnp.concatenate(rows, axis=0)).astype(BF16)
    y = None
    for i, o_i in enumerate((o_a, ob_ref[...], oc_ref[...])):
        col0 = 2 * MIX_W + i * D_MODEL
        gate = jax.nn.sigmoid(_dot(h, wuvg_ref[:, col0:col0 + D_MODEL]))
        t = gate * _dot(o_i, wbr_ref[i])
        y = t if y is None else y + t
    o_ref[...] = x + gate_ref[...] * _dot(y.astype(BF16), wout_ref[...])


def _merge_call(x, mul, sh, gate, w_uvg, ln_g, ln_b, w_s, b_s, o_b, o_c, w_br, w_out, *, tm):
    bsz, length, _ = x.shape
    tok = lambda w: pl.BlockSpec((None, tm, w), lambda b, i: (b, i, 0))
    vec = pl.BlockSpec((None, 1, D_MODEL), lambda b, i: (b, 0, 0))
    full = lambda a: pl.BlockSpec(a.shape, lambda b, i: (0,) * a.ndim)
    return pl.pallas_call(
        _merge_kernel,
        grid=(bsz, length // tm),
        in_specs=[tok(D_MODEL), vec, vec, vec, full(w_uvg), full(ln_g), full(ln_b), full(w_s), full(b_s),
                  tok(MIX_W), tok(MIX_W), full(w_br), full(w_out)],
        out_specs=tok(D_MODEL),
        out_shape=jax.ShapeDtypeStruct(x.shape, F32),
        compiler_params=_cparams(2),
        name="merge",
    )(x, mul, sh, gate, w_uvg, ln_g, ln_b, w_s, b_s, o_b, o_c, w_br, w_out)


HALO = 8
FF_CHUNK = 1408


def _ffn_kernel(x_ref, xp_ref, xn_ref, mul_ref, sh_ref, gate_ref, wg_ref, cw_ref, cb_ref, wu_ref, wd_ref,
                fg_ref, o_ref, *, final):
    i = pl.program_id(1)
    nt = pl.num_programs(1)
    x = x_ref[...]
    tm = x.shape[0]
    xe = jnp.concatenate([xp_ref[...], x, xn_ref[...]], axis=0)
    he = _rms_mod(xe, mul_ref[...], sh_ref[...]).astype(BF16)
    hm = he[HALO:HALO + tm]
    rowe = lax.broadcasted_iota(jnp.int32, (tm + 2 * HALO, 1), 0)
    keep = ((rowe >= HALO) | (i > 0)) & ((rowe < HALO + tm) | (i < nt - 1))
    acc = None
    for c0 in range(0, D_FF, FF_CHUNK):
        cs = slice(c0, c0 + FF_CHUNK)
        a = jnp.where(keep, _dot(he, wg_ref[:, cs]), 0.0)
        a_prev = pltpu.roll(a, 1, 0)[HALO:HALO + tm]
        a_next = pltpu.roll(a, tm + 2 * HALO - 1, 0)[HALO:HALO + tm]
        a = (a_prev * cw_ref[0:1, cs] + a[HALO:HALO + tm] * cw_ref[1:2, cs] + a_next * cw_ref[2:3, cs]
             + cb_ref[:, cs])
        z = (a * jax.nn.sigmoid(a) * _dot(hm, wu_ref[:, cs])).astype(BF16)
        t = _dot(z, wd_ref[cs, :])
        acc = t if acc is None else acc + t
    out = x + gate_ref[...] * acc
    if final:
        ms = jnp.mean(out * out, axis=-1, keepdims=True)
        out = out * lax.rsqrt(ms + EPS) * fg_ref[...]
    o_ref[...] = out


def _ffn_call(x, mul, sh, gate, w_gate, conv_w, conv_b, w_up, w_down, final_g, *, tm, final):
    bsz, length, _ = x.shape
    per = tm // HALO
    last = length // HALO - 1
    tok = pl.BlockSpec((None, tm, D_MODEL), lambda b, i: (b, i, 0))
    prv = pl.BlockSpec((None, HALO, D_MODEL), lambda b, i: (b, jnp.maximum(i * per - 1, 0), 0))
    nxt = pl.BlockSpec((None, HALO, D_MODEL), lambda b, i: (b, jnp.minimum((i + 1) * per, last), 0))
    vec = pl.BlockSpec((None, 1, D_MODEL), lambda b, i: (b, 0, 0))
    full = lambda a: pl.BlockSpec(a.shape, lambda b, i: (0,) * a.ndim)
    return pl.pallas_call(
        functools.partial(_ffn_kernel, final=final),
        grid=(bsz, length // tm),
        in_specs=[tok, prv, nxt, vec, vec, vec, full(w_gate), full(conv_w), full(conv_b), full(w_up),
                  full(w_down), full(final_g)],
        out_specs=tok,
        out_shape=jax.ShapeDtypeStruct(x.shape, F32),
        compiler_params=_cparams(2),
        name="ffn",
    )(x, x, x, mul, sh, gate, w_gate, conv_w, conv_b, w_up, w_down, final_g)


def _rope_tables(length):
    pos = jnp.arange(length)
    rows = (pos // GRID_W).astype(F32)
    cols = (pos % GRID_W).astype(F32)
    half = HEAD_DIM // 2
    inv = ROPE_BASE ** (-jnp.arange(0, half, 2, dtype=F32) / half)
    ang_r = rows[:, None] * inv[None, :]
    ang_c = cols[:, None] * inv[None, :]
    zero = jnp.zeros_like(ang_r)
    cos = jnp.concatenate([jnp.cos(ang_r)] * 2 + [jnp.cos(ang_c)] * 2, axis=1)
    sa = jnp.concatenate([-jnp.sin(ang_r), zero, -jnp.sin(ang_c), zero], axis=1)
    sb = jnp.concatenate([zero, jnp.sin(ang_r), zero, jnp.sin(ang_c)], axis=1)
    return tuple(jnp.tile(t, (1, LANES // HEAD_DIM)) for t in (cos, sa, sb))


def kernel(x, c, ctx, c_ctx, w_ada, b_ada, norm1_g, w_in, sgu_ln_g, sgu_ln_b, w_s, b_s, lam_q1, lam_k1,
           lam_q2, lam_k2, diff_subln_g, sinks, w_branch, w_out, norm2_g, w_gate, conv_w, conv_b, w_up,
           w_down, final_g):
    bsz, length, _ = x.shape
    n_ctx = ctx.shape[1]
    tables = _rope_tables(length)
    ctx_tables = tuple(t[:n_ctx] for t in tables)

    rows = -(-(bsz + 1) // 8) * 8
    vpad = jnp.zeros((rows, D_MODEL), F32).at[:bsz].set(c).at[bsz].set(c_ctx)
    ada = _ada_call(vpad, w_ada, b_ada)

    qscale = jnp.ones((ATTN_COLS,), F32).at[0:512].set(SCALE * LOG2E).at[1536:2048].set(SCALE)
    w_attn = (w_in[:, :, 2 * MIX_W:2 * MIX_W + ATTN_COLS] * qscale).astype(BF16)
    w_uvg = jnp.concatenate([w_in[:, :, :2 * MIX_W], w_in[:, :, GATE_COL0:]], axis=2).astype(BF16)
    w_s_b = w_s.astype(BF16)
    b_s_b = jnp.broadcast_to(b_s[..., None], b_s.shape + (LANES,))
    w_br_b = w_branch.astype(BF16)
    w_out_b = w_out.astype(BF16)
    w_gate_b = w_gate.astype(BF16)
    w_up_b = w_up.astype(BF16)
    w_down_b = w_down.astype(BF16)
    final_g2 = final_g.reshape(1, D_MODEL)

    for l in range(DEPTH):
        last = l == DEPTH - 1
        m = ada[l]
        sh1, sc1, g1, sh2, sc2, g2 = [m[:, j * D_MODEL:(j + 1) * D_MODEL] for j in range(6)]
        mul1 = norm1_g[l][None, :] * (1.0 + sc1)
        mul2 = norm2_g[l][None, :] * (1.0 + sc2)
        lat = lambda a: a[:bsz, None, :]
        cx = lambda a: jnp.broadcast_to(a[bsz][None, None, :], (bsz, 1, D_MODEL))

        lam_init = 0.8 - 0.6 * float(np.exp(-0.3 * l))
        lam = jnp.exp(jnp.sum(lam_q1[l] * lam_k1[l])) - jnp.exp(jnp.sum(lam_q2[l] * lam_k2[l])) + lam_init
        scal = jnp.stack([lam, jnp.float32(1.0 - lam_init)]).astype(F32)
        subln = diff_subln_g[l].reshape(1, LANES)
        ln_g = sgu_ln_g[l].reshape(1, MIX_W)
        ln_b = sgu_ln_b[l].reshape(1, MIX_W)
        cb2 = conv_b[l].reshape(1, D_FF)

        bq, bk, bv, cq, ckd, cvd = _inproj_call(x, lat(mul1), lat(sh1), w_attn[l], tables, tm=512, rope=True)
        xbq, xbk, xbv, xcq, xckd, xcvd = _inproj_call(ctx, cx(mul1), cx(sh1), w_attn[l], ctx_tables,
                                                     tm=n_ctx, rope=False)

        o_b = _diff_call(scal, bq, subln, [(bk, bv), (xbk, xbv)], tq=256, kc=1024)
        o_c = _win_call(sinks[l], cq, ckd, cvd, xckd, xcvd, local=True)
        x_mid = _merge_call(x, lat(mul1), lat(sh1), lat(g1), w_uvg[l], ln_g, ln_b, w_s_b[l], b_s_b[l],
                            o_b, o_c, w_br_b[l], w_out_b[l], tm=256)
        if not last:
            xo_b = _diff_call(scal, xbq, subln, [(xbk, xbv)], tq=n_ctx, kc=1024)
            xo_c = _win_call(sinks[l], xcq, None, None, xckd, xcvd, local=False)
            ctx_mid = _merge_call(ctx, cx(mul1), cx(sh1), cx(g1), w_uvg[l], ln_g, ln_b, w_s_b[l], b_s_b[l],
                                  xo_b, xo_c, w_br_b[l], w_out_b[l], tm=n_ctx)
        x = _ffn_call(x_mid, lat(mul2), lat(sh2), lat(g2), w_gate_b[l], conv_w[l], cb2, w_up_b[l],
                      w_down_b[l], final_g2, tm=256, final=last)
        if not last:
            ctx = _ffn_call(ctx_mid, cx(mul2), cx(sh2), cx(g2), w_gate_b[l], conv_w[l], cb2, w_up_b[l],
                            w_down_b[l], final_g2, tm=n_ctx, final=False)
    return x
```

```python
import functools

import numpy as np
import jax
import jax.numpy as jnp
from jax import lax
from jax.experimental import pallas as pl
from jax.experimental.pallas import tpu as pltpu

F32 = jnp.float32
BF16 = jnp.bfloat16

D_MODEL = 1024
DEPTH = 4
GRID_W = 64
HEAD_DIM = 64
BLOCK = 128
MIX_W = 512
A_GROUPS = 4
B_HEADS = 4
C_Q_HEADS = 8
D_FF = 2816
ROPE_BASE = 10000.0
EPS = 1e-6
SCALE = HEAD_DIM ** -0.5
NEG_INF = -1e30
LOG2E = float(np.log2(np.e))

LANES = 128
ATTN_COLS = 2304
GATE_COL0 = 3328
VMEM_LIMIT = 56 * 1024 * 1024


def _cparams(n_axes):
    return pltpu.CompilerParams(
        dimension_semantics=("arbitrary",) * n_axes, vmem_limit_bytes=VMEM_LIMIT)


def _rms_mod(x, mul, sh):
    ms = jnp.mean(x * x, axis=-1, keepdims=True)
    return x * lax.rsqrt(ms + EPS) * mul + sh


def _gelu_tanh(x):
    c = np.float32(np.sqrt(2.0 / np.pi))
    return 0.5 * x * (1.0 + jnp.tanh(c * (x + 0.044715 * (x * x * x))))


def _dot(a, b):
    return jnp.dot(a, b, preferred_element_type=F32)


def _dot_nt(a, b):
    return lax.dot_general(a, b, (((1,), (1,)), ((), ())), preferred_element_type=F32)


def _ada_kernel(v_ref, w_ref, b_ref, o_ref):
    v = v_ref[...]
    s = v * jax.nn.sigmoid(v)
    o_ref[...] = _dot(s.astype(BF16), w_ref[...].astype(BF16)) + b_ref[...]


def _ada_call(vpad, w_ada, b_ada):
    rows = vpad.shape[0]
    tn = 1536
    return pl.pallas_call(
        _ada_kernel,
        grid=(DEPTH, 6 * D_MODEL // tn),
        in_specs=[
            pl.BlockSpec((rows, D_MODEL), lambda l, j: (0, 0)),
            pl.BlockSpec((None, D_MODEL, tn), lambda l, j: (l, 0, j)),
            pl.BlockSpec((None, 1, tn), lambda l, j: (l, 0, j)),
        ],
        out_specs=pl.BlockSpec((None, rows, tn), lambda l, j: (l, 0, j)),
        out_shape=jax.ShapeDtypeStruct((DEPTH, rows, 6 * D_MODEL), F32),
        compiler_params=_cparams(2),
        name="ada",
    )(vpad, w_ada, b_ada.reshape(DEPTH, 1, 6 * D_MODEL))


def _inproj_kernel(x_ref, mul_ref, sh_ref, w_ref, cos_ref, sa_ref, sb_ref,
                   bq_ref, bk_ref, bvt_ref, cq_ref, ckd_ref, cvd_ref, *, rope):
    h = _rms_mod(x_ref[...], mul_ref[...], sh_ref[...]).astype(BF16)
    if rope:
        cos, sa, sb = cos_ref[...], sa_ref[...], sb_ref[...]

    def rot(y):
        if not rope:
            return y
        return y * cos + pltpu.roll(y, LANES - 16, 1) * sa + pltpu.roll(y, 16, 1) * sb

    def dup(y):
        lane = lax.broadcasted_iota(jnp.int32, y.shape, 1)
        sw = pltpu.roll(y, 64, 1)
        return jnp.concatenate([jnp.where(lane < 64, y, sw), jnp.where(lane < 64, sw, y)], axis=1)

    for col0, ref in ((0, bq_ref), (512, bk_ref), (1536, cq_ref)):
        y = _dot(h, w_ref[:, col0:col0 + 512])
        for g in range(4):
            ref[:, g * LANES:(g + 1) * LANES] = rot(y[:, g * LANES:(g + 1) * LANES]).astype(BF16)
    bvt_ref[...] = _dot(h, w_ref[:, 1024:1536]).T.astype(BF16)
    ykv = _dot(h, w_ref[:, 2048:2304])
    ckd_ref[...] = dup(rot(ykv[:, :LANES])).astype(BF16)
    cvd_ref[...] = dup(ykv[:, LANES:]).astype(BF16)


def _inproj_call(x, mul, sh, w_attn, tables, *, tm, rope):
    bsz, length, _ = x.shape
    tok = lambda w: pl.BlockSpec((None, tm, w), lambda b, i: (b, i, 0))
    vec = pl.BlockSpec((None, 1, D_MODEL), lambda b, i: (b, 0, 0))
    tab = pl.BlockSpec((tm, LANES), lambda b, i: (i, 0))
    tok_t = pl.BlockSpec((None, MIX_W, tm), lambda b, i: (b, 0, i))
    out = lambda w: jax.ShapeDtypeStruct((bsz, length, w), BF16)
    return pl.pallas_call(
        functools.partial(_inproj_kernel, rope=rope),
        grid=(bsz, length // tm),
        in_specs=[tok(D_MODEL), vec, vec,
                  pl.BlockSpec((D_MODEL, ATTN_COLS), lambda b, i: (0, 0)), tab, tab, tab],
        out_specs=[tok(512), tok(512), tok_t, tok(512), tok(256), tok(256)],
        out_shape=[out(512), out(512), jax.ShapeDtypeStruct((bsz, MIX_W, length), BF16), out(512), out(256),
                   out(256)],
        compiler_params=_cparams(2),
        name="inproj",
    )(x, mul, sh, w_attn, *tables)


def _diff_kernel(sc_ref, q_ref, g_ref, *refs, seg_lens, kc):
    o_ref, p_scr = refs[-2], refs[-1]
    q = q_ref[...]
    tq = q.shape[0]
    lane = lax.broadcasted_iota(jnp.int32, q.shape, 1)
    zero = jnp.zeros_like(q)
    qq = jnp.concatenate([jnp.where(lane < 64, q, zero), jnp.where(lane >= 64, q, zero)], axis=0)

    chunks = []
    row0 = 0
    for si, n_keys in enumerate(seg_lens):
        for st in range(0, n_keys, kc):
            size = min(kc, n_keys - st)
            chunks.append((refs[2 * si], refs[2 * si + 1], st, size, row0))
            row0 += size

    mcs, lcs = [], []
    for k_ref, _, st, size, r0 in chunks:
        s = _dot_nt(k_ref[st:st + size, :], qq)
        mc = jnp.max(s, axis=0, keepdims=True)
        p = jnp.exp2(s - mc)
        lcs.append(jnp.sum(p, axis=0, keepdims=True))
        mcs.append(mc)
        p_scr[r0:r0 + size, :] = p.astype(BF16)

    m_all = functools.reduce(jnp.maximum, mcs)
    es = [jnp.exp2(mc - m_all) for mc in mcs]
    l_all = None
    for lc, e in zip(lcs, es):
        l_all = lc * e if l_all is None else l_all + lc * e
    lanei = lax.broadcasted_iota(jnp.int32, (1, 2 * tq), 1)
    coef = jnp.where(lanei < tq, 1.0, sc_ref[0]) / l_all

    acc = None
    for (_, vt_ref, st, size, r0), e in zip(chunks, es):
        f = e * coef
        f1 = jnp.broadcast_to(f[:, 0:tq], (size, tq)).astype(BF16)
        f2 = jnp.broadcast_to(f[:, tq:2 * tq], (size, tq)).astype(BF16)
        a_t = p_scr[r0:r0 + size, 0:tq] * f1 - p_scr[r0:r0 + size, tq:2 * tq] * f2
        t = _dot(vt_ref[:, st:st + size], a_t)
        acc = t if acc is None else acc + t
    o = acc.T
    ms = jnp.mean(o * o, axis=-1, keepdims=True)
    o_ref[...] = (o * lax.rsqrt(ms + EPS) * g_ref[...] * sc_ref[1]).astype(BF16)


def _diff_call(scal, q, subln_g, segs, *, tq, kc):
    bsz, lq, _ = q.shape
    in_specs = [pl.BlockSpec(memory_space=pltpu.SMEM),
                pl.BlockSpec((None, tq, LANES), lambda b, h, i: (b, i, h)),
                pl.BlockSpec((1, LANES), lambda b, h, i: (0, 0))]
    args = [scal, q, subln_g]
    for k, vt in segs:
        n_keys = k.shape[1]
        in_specs += [pl.BlockSpec((None, n_keys, LANES), lambda b, h, i: (b, 0, h)),
                     pl.BlockSpec((None, LANES, n_keys), lambda b, h, i: (b, h, 0))]
        args += [k, vt]
    return pl.pallas_call(
        functools.partial(_diff_kernel, seg_lens=tuple(k.shape[1] for k, _ in segs), kc=kc),
        grid=(bsz, B_HEADS, lq // tq),
        in_specs=in_specs,
        out_specs=pl.BlockSpec((None, tq, LANES), lambda b, h, i: (b, i, h)),
        out_shape=jax.ShapeDtypeStruct((bsz, lq, MIX_W), BF16),
        scratch_shapes=[pltpu.VMEM((sum(k.shape[1] for k, _ in segs), 2 * tq), BF16)],
        compiler_params=_cparams(3),
        name="diff",
    )(*args)


def _win_kernel(sink_ref, q_ref, *refs, local):
    o_ref = refs[-1]
    kx_ref, vx_ref = refs[-3], refs[-2]
    n = pl.program_id(1)
    nb = pl.num_programs(1)
    if local:
        row = lax.broadcasted_iota(jnp.int32, (2 * BLOCK, 3 * BLOCK), 0) & (BLOCK - 1)
        col = lax.broadcasted_iota(jnp.int32, (2 * BLOCK, 3 * BLOCK), 1)
        rel = col - BLOCK - row
        valid = (jnp.abs(rel) <= BLOCK) & ((col >= BLOCK) | (n > 0)) & ((col < 2 * BLOCK) | (n < nb - 1))
    lane = lax.broadcasted_iota(jnp.int32, (BLOCK, LANES), 1)
    for kvh in range(2):
        ksl = slice(kvh * LANES, (kvh + 1) * LANES)
        kx, vx = kx_ref[:, ksl], vx_ref[:, ksl]
        if local:
            kl = jnp.concatenate([refs[j][:, ksl] for j in range(3)], axis=0)
            vl = jnp.concatenate([refs[3 + j][:, ksl] for j in range(3)], axis=0)
        for g2 in range(2):
            grp = kvh * 2 + g2
            qg = q_ref[:, grp * LANES:(grp + 1) * LANES]
            zero = jnp.zeros_like(qg)
            qs = jnp.concatenate([jnp.where(lane < 64, qg, zero), jnp.where(lane >= 64, qg, zero)], axis=0)
            rowi = lax.broadcasted_iota(jnp.int32, (2 * BLOCK, 1), 0)
            sink = jnp.where(rowi < BLOCK, sink_ref[2 * grp], sink_ref[2 * grp + 1])
            s_x = _dot_nt(qs, kx)
            m = jnp.maximum(jnp.max(s_x, axis=-1, keepdims=True), sink)
            if local:
                s_l = jnp.where(valid, _dot_nt(qs, kl), NEG_INF)
                m = jnp.maximum(m, jnp.max(s_l, axis=-1, keepdims=True))
            p_x = jnp.exp(s_x - m)
            l = jnp.sum(p_x, axis=-1, keepdims=True) + jnp.exp(sink - m)
            o = _dot(p_x.astype(BF16), vx)
            if local:
                p_l = jnp.exp(s_l - m)
                l = l + jnp.sum(p_l, axis=-1, keepdims=True)
                o = o + _dot(p_l.astype(BF16), vl)
            o = o / l
            o_ref[:, grp * LANES:(grp + 1) * LANES] = jnp.where(lane < 64, o[:BLOCK], o[BLOCK:]).astype(BF16)


def _win_call(sinks, q, kd, vd, kxd, vxd, *, local):
    bsz, lq, _ = q.shape
    nb = lq // BLOCK
    blk = lambda w, f: pl.BlockSpec((None, BLOCK, w), f)
    in_specs = [pl.BlockSpec(memory_space=pltpu.SMEM), blk(MIX_W, lambda b, n: (b, n, 0))]
    args = [sinks, q]
    if local:
        nbr = [lambda b, n: (b, jnp.maximum(n - 1, 0), 0), lambda b, n: (b, n, 0),
               lambda b, n: (b, jnp.minimum(n + 1, nb - 1), 0)]
        in_specs += [blk(2 * LANES, f) for f in nbr] * 2
        args += [kd] * 3 + [vd] * 3
    n_ctx = kxd.shape[1]
    in_specs += [pl.BlockSpec((None, n_ctx, 2 * LANES), lambda b, n: (b, 0, 0))] * 2
    args += [kxd, vxd]
    return pl.pallas_call(
        functools.partial(_win_kernel, local=local),
        grid=(bsz, nb),
        in_specs=in_specs,
        out_specs=blk(MIX_W, lambda b, n: (b, n, 0)),
        out_shape=jax.ShapeDtypeStruct((bsz, lq, MIX_W), BF16),
        compiler_params=_cparams(2),
        name="win",
    )(*args)


def _merge_kernel(x_ref, mul_ref, sh_ref, gate_ref, wuvg_ref, lng_ref, lnb_ref, ws_ref, bs_ref,
                  ob_ref, oc_ref, wbr_ref, wout_ref, o_ref):
    x = x_ref[...]
    tm = x.shape[0]
    h = _rms_mod(x, mul_ref[...], sh_ref[...]).astype(BF16)
    u = _gelu_tanh(_dot(h, wuvg_ref[:, 0:MIX_W]))
    v = _gelu_tanh(_dot(h, wuvg_ref[:, MIX_W:2 * MIX_W]))
    mu = jnp.mean(v, axis=-1, keepdims=True)
    vc = v - mu
    var = jnp.mean(vc * vc, axis=-1, keepdims=True)
    vn = (vc * lax.rsqrt(var + EPS) * lng_ref[...] + lnb_ref[...]).astype(BF16)
    rows = []
    for c in range(tm // BLOCK):
        cols = []
        for g in range(A_GROUPS):
            blk = vn[c * BLOCK:(c + 1) * BLOCK, g * LANES:(g + 1) * LANES]
            cols.append(_dot(ws_ref[g], blk) + bs_ref[g])
        rows.append(jnp.concatenate(cols, axis=1))
    o_a = (u * jnp.concatenate(rows, axis=0)).astype(BF16)
    y = None
    for i, o_i in enumerate((o_a, ob_ref[...], oc_ref[...])):
        col0 = 2 * MIX_W + i * D_MODEL
        gate = jax.nn.sigmoid(_dot(h, wuvg_ref[:, col0:col0 + D_MODEL]))
        t = gate * _dot(o_i, wbr_ref[i])
        y = t if y is None else y + t
    o_ref[...] = x + gate_ref[...] * _dot(y.astype(BF16), wout_ref[...])


def _merge_call(x, mul, sh, gate, w_uvg, ln_g, ln_b, w_s, b_s, o_b, o_c, w_br, w_out, *, tm):
    bsz, length, _ = x.shape
    tok = lambda w: pl.BlockSpec((None, tm, w), lambda b, i: (b, i, 0))
    vec = pl.BlockSpec((None, 1, D_MODEL), lambda b, i: (b, 0, 0))
    full = lambda a: pl.BlockSpec(a.shape, lambda b, i: (0,) * a.ndim)
    return pl.pallas_call(
        _merge_kernel,
        grid=(bsz, length // tm),
        in_specs=[tok(D_MODEL), vec, vec, vec, full(w_uvg), full(ln_g), full(ln_b), full(w_s), full(b_s),
                  tok(MIX_W), tok(MIX_W), full(w_br), full(w_out)],
        out_specs=tok(D_MODEL),
        out_shape=jax.ShapeDtypeStruct(x.shape, F32),
        compiler_params=_cparams(2),
        name="merge",
    )(x, mul, sh, gate, w_uvg, ln_g, ln_b, w_s, b_s, o_b, o_c, w_br, w_out)


HALO = 8
FF_CHUNK = 1408


def _ffn_kernel(x_ref, xp_ref, xn_ref, mul_ref, sh_ref, gate_ref, wg_ref, cw_ref, cb_ref, wu_ref, wd_ref,
                fg_ref, o_ref, *, final):
    i = pl.program_id(1)
    nt = pl.num_programs(1)
    x = x_ref[...]
    tm = x.shape[0]
    xe = jnp.concatenate([xp_ref[...], x, xn_ref[...]], axis=0)
    he = _rms_mod(xe, mul_ref[...], sh_ref[...]).astype(BF16)
    hm = he[HALO:HALO + tm]
    rowe = lax.broadcasted_iota(jnp.int32, (tm + 2 * HALO, 1), 0)
    keep = ((rowe >= HALO) | (i > 0)) & ((rowe < HALO + tm) | (i < nt - 1))
    acc = None
    for c0 in range(0, D_FF, FF_CHUNK):
        cs = slice(c0, c0 + FF_CHUNK)
        a = jnp.where(keep, _dot(he, wg_ref[:, cs]), 0.0)
        a_prev = pltpu.roll(a, 1, 0)[HALO:HALO + tm]
        a_next = pltpu.roll(a, tm + 2 * HALO - 1, 0)[HALO:HALO + tm]
        a = (a_prev * cw_ref[0:1, cs] + a[HALO:HALO + tm] * cw_ref[1:2, cs] + a_next * cw_ref[2:3, cs]
             + cb_ref[:, cs])
        z = (a * jax.nn.sigmoid(a) * _dot(hm, wu_ref[:, cs])).astype(BF16)
        t = _dot(z, wd_ref[cs, :])
        acc = t if acc is None else acc + t
    out = x + gate_ref[...] * acc
    if final:
        ms = jnp.mean(out * out, axis=-1, keepdims=True)
        out = out * lax.rsqrt(ms + EPS) * fg_ref[...]
    o_ref[...] = out


def _ffn_call(x, mul, sh, gate, w_gate, conv_w, conv_b, w_up, w_down, final_g, *, tm, final):
    bsz, length, _ = x.shape
    per = tm // HALO
    last = length // HALO - 1
    tok = pl.BlockSpec((None, tm, D_MODEL), lambda b, i: (b, i, 0))
    prv = pl.BlockSpec((None, HALO, D_MODEL), lambda b, i: (b, jnp.maximum(i * per - 1, 0), 0))
    nxt = pl.BlockSpec((None, HALO, D_MODEL), lambda b, i: (b, jnp.minimum((i + 1) * per, last), 0))
    vec = pl.BlockSpec((None, 1, D_MODEL), lambda b, i: (b, 0, 0))
    full = lambda a: pl.BlockSpec(a.shape, lambda b, i: (0,) * a.ndim)
    return pl.pallas_call(
        functools.partial(_ffn_kernel, final=final),
        grid=(bsz, length // tm),
        in_specs=[tok, prv, nxt, vec, vec, vec, full(w_gate), full(conv_w), full(conv_b), full(w_up),
                  full(w_down), full(final_g)],
        out_specs=tok,
        out_shape=jax.ShapeDtypeStruct(x.shape, F32),
        compiler_params=_cparams(2),
        name="ffn",
    )(x, x, x, mul, sh, gate, w_gate, conv_w, conv_b, w_up, w_down, final_g)


def _rope_tables(length):
    pos = jnp.arange(length)
    rows = (pos // GRID_W).astype(F32)
    cols = (pos % GRID_W).astype(F32)
    half = HEAD_DIM // 2
    inv = ROPE_BASE ** (-jnp.arange(0, half, 2, dtype=F32) / half)
    ang_r = rows[:, None] * inv[None, :]
    ang_c = cols[:, None] * inv[None, :]
    zero = jnp.zeros_like(ang_r)
    cos = jnp.concatenate([jnp.cos(ang_r)] * 2 + [jnp.cos(ang_c)] * 2, axis=1)
    sa = jnp.concatenate([-jnp.sin(ang_r), zero, -jnp.sin(ang_c), zero], axis=1)
    sb = jnp.concatenate([zero, jnp.sin(ang_r), zero, jnp.sin(ang_c)], axis=1)
    return tuple(jnp.tile(t, (1, LANES // HEAD_DIM)) for t in (cos, sa, sb))


def kernel(x, c, ctx, c_ctx, w_ada, b_ada, norm1_g, w_in, sgu_ln_g, sgu_ln_b, w_s, b_s, lam_q1, lam_k1,
           lam_q2, lam_k2, diff_subln_g, sinks, w_branch, w_out, norm2_g, w_gate, conv_w, conv_b, w_up,
           w_down, final_g):
    bsz, length, _ = x.shape
    n_ctx = ctx.shape[1]
    tables = _rope_tables(length)
    ctx_tables = tuple(t[:n_ctx] for t in tables)

    rows = -(-(bsz + 1) // 8) * 8
    vpad = jnp.zeros((rows, D_MODEL), F32).at[:bsz].set(c).at[bsz].set(c_ctx)
    ada = _ada_call(vpad, w_ada, b_ada)

    qscale = jnp.ones((ATTN_COLS,), F32).at[0:512].set(SCALE * LOG2E).at[1536:2048].set(SCALE)
    w_attn = (w_in[:, :, 2 * MIX_W:2 * MIX_W + ATTN_COLS] * qscale).astype(BF16)
    w_uvg = jnp.concatenate([w_in[:, :, :2 * MIX_W], w_in[:, :, GATE_COL0:]], axis=2).astype(BF16)
    w_s_b = w_s.astype(BF16)
    b_s_b = jnp.broadcast_to(b_s[..., None], b_s.shape + (LANES,))
    w_br_b = w_branch.astype(BF16)
    w_out_b = w_out.astype(BF16)
    w_gate_b = w_gate.astype(BF16)
    w_up_b = w_up.astype(BF16)
    w_down_b = w_down.astype(BF16)
    final_g2 = final_g.reshape(1, D_MODEL)

    for l in range(DEPTH):
        last = l == DEPTH - 1
        m = ada[l]
        sh1, sc1, g1, sh2, sc2, g2 = [m[:, j * D_MODEL:(j + 1) * D_MODEL] for j in range(6)]
        mul1 = norm1_g[l][None, :] * (1.0 + sc1)
        mul2 = norm2_g[l][None, :] * (1.0 + sc2)
        lat = lambda a: a[:bsz, None, :]
        cx = lambda a: jnp.broadcast_to(a[bsz][None, None, :], (bsz, 1, D_MODEL))

        lam_init = 0.8 - 0.6 * float(np.exp(-0.3 * l))
        lam = jnp.exp(jnp.sum(lam_q1[l] * lam_k1[l])) - jnp.exp(jnp.sum(lam_q2[l] * lam_k2[l])) + lam_init
        scal = jnp.stack([lam, jnp.float32(1.0 - lam_init)]).astype(F32)
        subln = diff_subln_g[l].reshape(1, LANES)
        ln_g = sgu_ln_g[l].reshape(1, MIX_W)
        ln_b = sgu_ln_b[l].reshape(1, MIX_W)
        cb2 = conv_b[l].reshape(1, D_FF)

        bq, bk, bvt, cq, ckd, cvd = _inproj_call(x, lat(mul1), lat(sh1), w_attn[l], tables, tm=512, rope=True)
        xbq, xbk, xbvt, xcq, xckd, xcvd = _inproj_call(ctx, cx(mul1), cx(sh1), w_attn[l], ctx_tables,
                                                      tm=n_ctx, rope=False)

        o_b = _diff_call(scal, bq, subln, [(bk, bvt), (xbk, xbvt)], tq=256, kc=512)
        o_c = _win_call(sinks[l], cq, ckd, cvd, xckd, xcvd, local=True)
        x_mid = _merge_call(x, lat(mul1), lat(sh1), lat(g1), w_uvg[l], ln_g, ln_b, w_s_b[l], b_s_b[l],
                            o_b, o_c, w_br_b[l], w_out_b[l], tm=256)
        if not last:
            xo_b = _diff_call(scal, xbq, subln, [(xbk, xbvt)], tq=n_ctx, kc=512)
            xo_c = _win_call(sinks[l], xcq, None, None, xckd, xcvd, local=False)
            ctx_mid = _merge_call(ctx, cx(mul1), cx(sh1), cx(g1), w_uvg[l], ln_g, ln_b, w_s_b[l], b_s_b[l],
                                  xo_b, xo_c, w_br_b[l], w_out_b[l], tm=n_ctx)
        x = _ffn_call(x_mid, lat(mul2), lat(sh2), lat(g2), w_gate_b[l], conv_w[l], cb2, w_up_b[l],
                      w_down_b[l], final_g2, tm=256, final=last)
        if not last:
            ctx = _ffn_call(ctx_mid, cx(mul2), cx(sh2), cx(g2), w_gate_b[l], conv_w[l], cb2, w_up_b[l],
                            w_down_b[l], final_g2, tm=n_ctx, final=False)
    return x
```

```python
import functools

import numpy as np
import jax
import jax.numpy as jnp
from jax import lax
from jax.experimental import pallas as pl
from jax.experimental.pallas import tpu as pltpu

F32 = jnp.float32
BF16 = jnp.bfloat16

D_MODEL = 1024
DEPTH = 4
GRID_W = 64
HEAD_DIM = 64
BLOCK = 128
MIX_W = 512
A_GROUPS = 4
B_HEADS = 4
C_Q_HEADS = 8
D_FF = 2816
ROPE_BASE = 10000.0
EPS = 1e-6
SCALE = HEAD_DIM ** -0.5
NEG_INF = -1e30
LOG2E = float(np.log2(np.e))

LANES = 128
ATTN_COLS = 2304
GATE_COL0 = 3328
VMEM_LIMIT = 56 * 1024 * 1024


def _cparams(n_axes):
    return pltpu.CompilerParams(
        dimension_semantics=("arbitrary",) * n_axes, vmem_limit_bytes=VMEM_LIMIT)


def _rms_mod(x, mul, sh):
    ms = jnp.mean(x * x, axis=-1, keepdims=True)
    return x * lax.rsqrt(ms + EPS) * mul + sh


def _gelu_tanh(x):
    c = np.float32(np.sqrt(2.0 / np.pi))
    return 0.5 * x * (1.0 + jnp.tanh(c * (x + 0.044715 * (x * x * x))))


def _dot(a, b):
    return jnp.dot(a, b, preferred_element_type=F32)


def _dot_nt(a, b):
    return lax.dot_general(a, b, (((1,), (1,)), ((), ())), preferred_element_type=F32)


def _ada_kernel(v_ref, w_ref, b_ref, o_ref):
    v = v_ref[...]
    s = v * jax.nn.sigmoid(v)
    o_ref[...] = _dot(s.astype(BF16), w_ref[...].astype(BF16)) + b_ref[...]


def _ada_call(vpad, w_ada, b_ada):
    rows = vpad.shape[0]
    tn = 1536
    return pl.pallas_call(
        _ada_kernel,
        grid=(DEPTH, 6 * D_MODEL // tn),
        in_specs=[
            pl.BlockSpec((rows, D_MODEL), lambda l, j: (0, 0)),
            pl.BlockSpec((None, D_MODEL, tn), lambda l, j: (l, 0, j)),
            pl.BlockSpec((None, 1, tn), lambda l, j: (l, 0, j)),
        ],
        out_specs=pl.BlockSpec((None, rows, tn), lambda l, j: (l, 0, j)),
        out_shape=jax.ShapeDtypeStruct((DEPTH, rows, 6 * D_MODEL), F32),
        compiler_params=_cparams(2),
        name="ada",
    )(vpad, w_ada, b_ada.reshape(DEPTH, 1, 6 * D_MODEL))


def _inproj_kernel(x_ref, mul_ref, sh_ref, w_ref, cos_ref, sa_ref, sb_ref,
                   bq_ref, bk_ref, bvt_ref, cq_ref, ckd_ref, cvdt_ref, *, rope):
    h = _rms_mod(x_ref[...], mul_ref[...], sh_ref[...]).astype(BF16)
    if rope:
        cos, sa, sb = cos_ref[...], sa_ref[...], sb_ref[...]

    def rot(y):
        if not rope:
            return y
        return y * cos + pltpu.roll(y, LANES - 16, 1) * sa + pltpu.roll(y, 16, 1) * sb

    def dup(y):
        lane = lax.broadcasted_iota(jnp.int32, y.shape, 1)
        sw = pltpu.roll(y, 64, 1)
        return jnp.concatenate([jnp.where(lane < 64, y, sw), jnp.where(lane < 64, sw, y)], axis=1)

    for col0, ref in ((0, bq_ref), (512, bk_ref), (1536, cq_ref)):
        y = _dot(h, w_ref[:, col0:col0 + 512])
        for g in range(4):
            ref[:, g * LANES:(g + 1) * LANES] = rot(y[:, g * LANES:(g + 1) * LANES]).astype(BF16)
    bvt_ref[...] = _dot(h, w_ref[:, 1024:1536]).T.astype(BF16)
    ykv = _dot(h, w_ref[:, 2048:2304])
    ckd_ref[...] = dup(rot(ykv[:, :LANES])).astype(BF16)
    cvdt_ref[...] = dup(ykv[:, LANES:]).T.astype(BF16)


def _inproj_call(x, mul, sh, w_attn, tables, *, tm, rope):
    bsz, length, _ = x.shape
    tok = lambda w: pl.BlockSpec((None, tm, w), lambda b, i: (b, i, 0))
    vec = pl.BlockSpec((None, 1, D_MODEL), lambda b, i: (b, 0, 0))
    tab = pl.BlockSpec((tm, LANES), lambda b, i: (i, 0))
    tok_t = lambda w: pl.BlockSpec((None, w, tm), lambda b, i: (b, 0, i))
    out = lambda w: jax.ShapeDtypeStruct((bsz, length, w), BF16)
    out_t = lambda w: jax.ShapeDtypeStruct((bsz, w, length), BF16)
    return pl.pallas_call(
        functools.partial(_inproj_kernel, rope=rope),
        grid=(bsz, length // tm),
        in_specs=[tok(D_MODEL), vec, vec,
                  pl.BlockSpec((D_MODEL, ATTN_COLS), lambda b, i: (0, 0)), tab, tab, tab],
        out_specs=[tok(512), tok(512), tok_t(MIX_W), tok(512), tok(256), tok_t(2 * LANES)],
        out_shape=[out(512), out(512), out_t(MIX_W), out(512), out(256), out_t(2 * LANES)],
        compiler_params=_cparams(2),
        name="inproj",
    )(x, mul, sh, w_attn, *tables)


def _diff_kernel(sc_ref, q_ref, g_ref, *refs, seg_lens, kc):
    o_ref, p_scr = refs[-2], refs[-1]
    q = q_ref[...]
    tq = q.shape[0]
    lane = lax.broadcasted_iota(jnp.int32, q.shape, 1)
    zero = jnp.zeros_like(q)
    qq = jnp.concatenate([jnp.where(lane < 64, q, zero), jnp.where(lane >= 64, q, zero)], axis=0)

    chunks = []
    row0 = 0
    for si, n_keys in enumerate(seg_lens):
        for st in range(0, n_keys, kc):
            size = min(kc, n_keys - st)
            chunks.append((refs[2 * si], refs[2 * si + 1], st, size, row0))
            row0 += size

    mcs, lcs = [], []
    for k_ref, _, st, size, r0 in chunks:
        s = _dot_nt(k_ref[st:st + size, :], qq)
        mc = jnp.max(s, axis=0, keepdims=True)
        p = jnp.exp2(s - mc)
        lcs.append(jnp.sum(p, axis=0, keepdims=True))
        mcs.append(mc)
        p_scr[r0:r0 + size, :] = p.astype(BF16)

    m_all = functools.reduce(jnp.maximum, mcs)
    es = [jnp.exp2(mc - m_all) for mc in mcs]
    l_all = None
    for lc, e in zip(lcs, es):
        l_all = lc * e if l_all is None else l_all + lc * e
    lanei = lax.broadcasted_iota(jnp.int32, (1, 2 * tq), 1)
    coef = jnp.where(lanei < tq, 1.0, sc_ref[0]) / l_all

    acc = None
    for (_, vt_ref, st, size, r0), e in zip(chunks, es):
        f = e * coef
        f1 = f[:, 0:tq].astype(BF16)
        f2 = f[:, tq:2 * tq].astype(BF16)
        a_t = p_scr[r0:r0 + size, 0:tq] * f1 - p_scr[r0:r0 + size, tq:2 * tq] * f2
        t = _dot(vt_ref[:, st:st + size], a_t)
        acc = t if acc is None else acc + t
    o = acc.T
    ms = jnp.mean(o * o, axis=-1, keepdims=True)
    o_ref[...] = (o * lax.rsqrt(ms + EPS) * g_ref[...] * sc_ref[1]).astype(BF16)


def _diff_call(scal, q, subln_g, segs, *, tq, kc):
    bsz, lq, _ = q.shape
    in_specs = [pl.BlockSpec(memory_space=pltpu.SMEM),
                pl.BlockSpec((None, tq, LANES), lambda b, h, i: (b, i, h)),
                pl.BlockSpec((1, LANES), lambda b, h, i: (0, 0))]
    args = [scal, q, subln_g]
    for k, vt in segs:
        n_keys = k.shape[1]
        in_specs += [pl.BlockSpec((None, n_keys, LANES), lambda b, h, i: (b, 0, h)),
                     pl.BlockSpec((None, LANES, n_keys), lambda b, h, i: (b, h, 0))]
        args += [k, vt]
    return pl.pallas_call(
        functools.partial(_diff_kernel, seg_lens=tuple(k.shape[1] for k, _ in segs), kc=kc),
        grid=(bsz, B_HEADS, lq // tq),
        in_specs=in_specs,
        out_specs=pl.BlockSpec((None, tq, LANES), lambda b, h, i: (b, i, h)),
        out_shape=jax.ShapeDtypeStruct((bsz, lq, MIX_W), BF16),
        scratch_shapes=[pltpu.VMEM((sum(k.shape[1] for k, _ in segs), 2 * tq), BF16)],
        compiler_params=_cparams(3),
        name="diff",
    )(*args)


def _win_kernel(sink_ref, q_ref, *refs, local):
    o_ref = refs[-1]
    kx_ref, vxt_ref = refs[-3], refs[-2]
    n = pl.program_id(1)
    nb = pl.num_programs(1)
    nq = C_Q_HEADS // 2 * BLOCK
    lane = lax.broadcasted_iota(jnp.int32, (BLOCK, LANES), 1)
    if local:
        key = lax.broadcasted_iota(jnp.int32, (BLOCK, nq), 0)
        qry = lax.broadcasted_iota(jnp.int32, (BLOCK, nq), 1) & (BLOCK - 1)
        valid_prev = (key >= qry) & (n > 0)
        valid_next = (key <= qry) & (n < nb - 1)
    headi = lax.broadcasted_iota(jnp.int32, (1, nq), 1) // BLOCK
    s_l, sink_l, vt_l = [], [], []
    for kvh in range(2):
        ksl = slice(kvh * LANES, (kvh + 1) * LANES)
        parts = []
        for g in range(4):
            grp = kvh * 2 + g // 2
            qg = q_ref[:, grp * LANES:(grp + 1) * LANES]
            parts.append(jnp.where((lane < 64) if g % 2 == 0 else (lane >= 64), qg, jnp.zeros_like(qg)))
        qs = jnp.concatenate(parts, axis=0)
        sink = jnp.zeros((1, nq), F32)
        for g in range(4):
            sink = jnp.where(headi == g, sink_ref[kvh * 4 + g] * LOG2E, sink)
        if local:
            k_all = jnp.concatenate([refs[j][:, ksl] for j in range(3)] + [kx_ref[:, ksl]], axis=0)
            vt_all = jnp.concatenate([refs[3 + j][ksl, :] for j in range(3)] + [vxt_ref[ksl, :]], axis=1)
        else:
            k_all, vt_all = kx_ref[:, ksl], vxt_ref[ksl, :]
        s_l.append(_dot_nt(k_all, qs))
        sink_l.append(sink)
        vt_l.append(vt_all)
    if local:
        s_l = [jnp.concatenate([jnp.where(valid_prev, s[0:BLOCK], NEG_INF), s[BLOCK:2 * BLOCK],
                                jnp.where(valid_next, s[2 * BLOCK:3 * BLOCK], NEG_INF), s[3 * BLOCK:]], axis=0)
               for s in s_l]
    m_l = [jnp.maximum(jnp.max(s, axis=0, keepdims=True), sink) for s, sink in zip(s_l, sink_l)]
    p_l = [jnp.exp2(s - m) for s, m in zip(s_l, m_l)]
    l_l = [jnp.sum(p, axis=0, keepdims=True) + jnp.exp2(sink - m) for p, sink, m in zip(p_l, sink_l, m_l)]
    ot_l = [_dot(vt, p.astype(BF16)) / l for vt, p, l in zip(vt_l, p_l, l_l)]
    for kvh in range(2):
        o = ot_l[kvh].T
        for j in range(2):
            grp = kvh * 2 + j
            o_ref[:, grp * LANES:(grp + 1) * LANES] = jnp.where(
                lane < 64, o[2 * j * BLOCK:(2 * j + 1) * BLOCK], o[(2 * j + 1) * BLOCK:(2 * j + 2) * BLOCK]
            ).astype(BF16)


def _win_call(sinks, q, kd, vdt, kxd, vxdt, *, local):
    bsz, lq, _ = q.shape
    nb = lq // BLOCK
    blk = lambda w, f: pl.BlockSpec((None, BLOCK, w), f)
    in_specs = [pl.BlockSpec(memory_space=pltpu.SMEM), blk(MIX_W, lambda b, n: (b, n, 0))]
    args = [sinks, q]
    if local:
        nbr = [lambda n: jnp.maximum(n - 1, 0), lambda n: n, lambda n: jnp.minimum(n + 1, nb - 1)]
        in_specs += [blk(2 * LANES, lambda b, n, f=f: (b, f(n), 0)) for f in nbr]
        in_specs += [pl.BlockSpec((None, 2 * LANES, BLOCK), lambda b, n, f=f: (b, 0, f(n))) for f in nbr]
        args += [kd] * 3 + [vdt] * 3
    n_ctx = kxd.shape[1]
    in_specs += [pl.BlockSpec((None, n_ctx, 2 * LANES), lambda b, n: (b, 0, 0)),
                 pl.BlockSpec((None, 2 * LANES, n_ctx), lambda b, n: (b, 0, 0))]
    args += [kxd, vxdt]
    return pl.pallas_call(
        functools.partial(_win_kernel, local=local),
        grid=(bsz, nb),
        in_specs=in_specs,
        out_specs=blk(MIX_W, lambda b, n: (b, n, 0)),
        out_shape=jax.ShapeDtypeStruct((bsz, lq, MIX_W), BF16),
        compiler_params=_cparams(2),
        name="win",
    )(*args)


def _merge_kernel(x_ref, mul_ref, sh_ref, gate_ref, wuvg_ref, lng_ref, lnb_ref, ws_ref, bs_ref,
                  ob_ref, oc_ref, wbr_ref, wout_ref, o_ref):
    x = x_ref[...]
    tm = x.shape[0]
    h = _rms_mod(x, mul_ref[...], sh_ref[...]).astype(BF16)
    u = _gelu_tanh(_dot(h, wuvg_ref[:, 0:MIX_W]))
    v = _gelu_tanh(_dot(h, wuvg_ref[:, MIX_W:2 * MIX_W]))
    mu = jnp.mean(v, axis=-1, keepdims=True)
    vc = v - mu
    var = jnp.mean(vc * vc, axis=-1, keepdims=True)
    vn = (vc * lax.rsqrt(var + EPS) * lng_ref[...] + lnb_ref[...]).astype(BF16)
    rows = []
    for c in range(tm // BLOCK):
        cols = []
        for g in range(A_GROUPS):
            blk = vn[c * BLOCK:(c + 1) * BLOCK, g * LANES:(g + 1) * LANES]
            cols.append(_dot(ws_ref[g], blk) + bs_ref[g])
        rows.append(jnp.concatenate(cols, axis=1))
    o_a = (u * jnp.concatenate(rows, axis=0)).astype(BF16)
    y = None
    for i, o_i in enumerate((o_a, ob_ref[...], oc_ref[...])):
        col0 = 2 * MIX_W + i * D_MODEL
        gate = jax.nn.sigmoid(_dot(h, wuvg_ref[:, col0:col0 + D_MODEL]))
        t = gate * _dot(o_i, wbr_ref[i])
        y = t if y is None else y + t
    o_ref[...] = x + gate_ref[...] * _dot(y.astype(BF16), wout_ref[...])


def _merge_call(x, mul, sh, gate, w_uvg, ln_g, ln_b, w_s, b_s, o_b, o_c, w_br, w_out, *, tm):
    bsz, length, _ = x.shape
    tok = lambda w: pl.BlockSpec((None, tm, w), lambda b, i: (b, i, 0))
    vec = pl.BlockSpec((None, 1, D_MODEL), lambda b, i: (b, 0, 0))
    full = lambda a: pl.BlockSpec(a.shape, lambda b, i: (0,) * a.ndim)
    return pl.pallas_call(
        _merge_kernel,
        grid=(bsz, length // tm),
        in_specs=[tok(D_MODEL), vec, vec, vec, full(w_uvg), full(ln_g), full(ln_b), full(w_s), full(b_s),
                  tok(MIX_W), tok(MIX_W), full(w_br), full(w_out)],
        out_specs=tok(D_MODEL),
        out_shape=jax.ShapeDtypeStruct(x.shape, F32),
        compiler_params=_cparams(2),
        name="merge",
    )(x, mul, sh, gate, w_uvg, ln_g, ln_b, w_s, b_s, o_b, o_c, w_br, w_out)


HALO = 8
FF_CHUNK = 1408


def _ffn_kernel(x_ref, xp_ref, xn_ref, mul_ref, sh_ref, gate_ref, wg_ref, cw_ref, cb_ref, wu_ref, wd_ref,
                fg_ref, o_ref, *, final):
    i = pl.program_id(1)
    nt = pl.num_programs(1)
    x = x_ref[...]
    tm = x.shape[0]
    xe = jnp.concatenate([xp_ref[...], x, xn_ref[...]], axis=0)
    he = _rms_mod(xe, mul_ref[...], sh_ref[...]).astype(BF16)
    hm = he[HALO:HALO + tm]
    rowe = lax.broadcasted_iota(jnp.int32, (tm + 2 * HALO, 1), 0)
    keep = ((rowe >= HALO) | (i > 0)) & ((rowe < HALO + tm) | (i < nt - 1))
    acc = None
    for c0 in range(0, D_FF, FF_CHUNK):
        cs = slice(c0, c0 + FF_CHUNK)
        a = jnp.where(keep, _dot(he, wg_ref[:, cs]), 0.0)
        a_prev = pltpu.roll(a, 1, 0)[HALO:HALO + tm]
        a_next = pltpu.roll(a, tm + 2 * HALO - 1, 0)[HALO:HALO + tm]
        a = (a_prev * cw_ref[0:1, cs] + a[HALO:HALO + tm] * cw_ref[1:2, cs] + a_next * cw_ref[2:3, cs]
             + cb_ref[:, cs])
        z = (a * jax.nn.sigmoid(a) * _dot(hm, wu_ref[:, cs])).astype(BF16)
        t = _dot(z, wd_ref[cs, :])
        acc = t if acc is None else acc + t
    out = x + gate_ref[...] * acc
    if final:
        ms = jnp.mean(out * out, axis=-1, keepdims=True)
        out = out * lax.rsqrt(ms + EPS) * fg_ref[...]
    o_ref[...] = out


def _ffn_call(x, mul, sh, gate, w_gate, conv_w, conv_b, w_up, w_down, final_g, *, tm, final):
    bsz, length, _ = x.shape
    per = tm // HALO
    last = length // HALO - 1
    tok = pl.BlockSpec((None, tm, D_MODEL), lambda b, i: (b, i, 0))
    prv = pl.BlockSpec((None, HALO, D_MODEL), lambda b, i: (b, jnp.maximum(i * per - 1, 0), 0))
    nxt = pl.BlockSpec((None, HALO, D_MODEL), lambda b, i: (b, jnp.minimum((i + 1) * per, last), 0))
    vec = pl.BlockSpec((None, 1, D_MODEL), lambda b, i: (b, 0, 0))
    full = lambda a: pl.BlockSpec(a.shape, lambda b, i: (0,) * a.ndim)
    return pl.pallas_call(
        functools.partial(_ffn_kernel, final=final),
        grid=(bsz, length // tm),
        in_specs=[tok, prv, nxt, vec, vec, vec, full(w_gate), full(conv_w), full(conv_b), full(w_up),
                  full(w_down), full(final_g)],
        out_specs=tok,
        out_shape=jax.ShapeDtypeStruct(x.shape, F32),
        compiler_params=_cparams(2),
        name="ffn",
    )(x, x, x, mul, sh, gate, w_gate, conv_w, conv_b, w_up, w_down, final_g)


def _rope_tables(length):
    pos = jnp.arange(length)
    rows = (pos // GRID_W).astype(F32)
    cols = (pos % GRID_W).astype(F32)
    half = HEAD_DIM // 2
    inv = ROPE_BASE ** (-jnp.arange(0, half, 2, dtype=F32) / half)
    ang_r = rows[:, None] * inv[None, :]
    ang_c = cols[:, None] * inv[None, :]
    zero = jnp.zeros_like(ang_r)
    cos = jnp.concatenate([jnp.cos(ang_r)] * 2 + [jnp.cos(ang_c)] * 2, axis=1)
    sa = jnp.concatenate([-jnp.sin(ang_r), zero, -jnp.sin(ang_c), zero], axis=1)
    sb = jnp.concatenate([zero, jnp.sin(ang_r), zero, jnp.sin(ang_c)], axis=1)
    return tuple(jnp.tile(t, (1, LANES // HEAD_DIM)) for t in (cos, sa, sb))


def kernel(x, c, ctx, c_ctx, w_ada, b_ada, norm1_g, w_in, sgu_ln_g, sgu_ln_b, w_s, b_s, lam_q1, lam_k1,
           lam_q2, lam_k2, diff_subln_g, sinks, w_branch, w_out, norm2_g, w_gate, conv_w, conv_b, w_up,
           w_down, final_g):
    bsz, length, _ = x.shape
    n_ctx = ctx.shape[1]
    tables = _rope_tables(length)
    ctx_tables = tuple(t[:n_ctx] for t in tables)

    rows = -(-(bsz + 1) // 8) * 8
    vpad = jnp.zeros((rows, D_MODEL), F32).at[:bsz].set(c).at[bsz].set(c_ctx)
    ada = _ada_call(vpad, w_ada, b_ada)

    qscale = jnp.ones((ATTN_COLS,), F32).at[0:512].set(SCALE * LOG2E).at[1536:2048].set(SCALE * LOG2E)
    w_attn = (w_in[:, :, 2 * MIX_W:2 * MIX_W + ATTN_COLS] * qscale).astype(BF16)
    w_uvg = jnp.concatenate([w_in[:, :, :2 * MIX_W], w_in[:, :, GATE_COL0:]], axis=2).astype(BF16)
    w_s_b = w_s.astype(BF16)
    b_s_b = jnp.broadcast_to(b_s[..., None], b_s.shape + (LANES,))
    w_br_b = w_branch.astype(BF16)
    w_out_b = w_out.astype(BF16)
    w_gate_b = w_gate.astype(BF16)
    w_up_b = w_up.astype(BF16)
    w_down_b = w_down.astype(BF16)
    final_g2 = final_g.reshape(1, D_MODEL)

    for l in range(DEPTH):
        last = l == DEPTH - 1
        m = ada[l]
        sh1, sc1, g1, sh2, sc2, g2 = [m[:, j * D_MODEL:(j + 1) * D_MODEL] for j in range(6)]
        mul1 = norm1_g[l][None, :] * (1.0 + sc1)
        mul2 = norm2_g[l][None, :] * (1.0 + sc2)
        lat = lambda a: a[:bsz, None, :]
        cx = lambda a: jnp.broadcast_to(a[bsz][None, None, :], (bsz, 1, D_MODEL))

        lam_init = 0.8 - 0.6 * float(np.exp(-0.3 * l))
        lam = jnp.exp(jnp.sum(lam_q1[l] * lam_k1[l])) - jnp.exp(jnp.sum(lam_q2[l] * lam_k2[l])) + lam_init
        scal = jnp.stack([lam, jnp.float32(1.0 - lam_init)]).astype(F32)
        subln = diff_subln_g[l].reshape(1, LANES)
        ln_g = sgu_ln_g[l].reshape(1, MIX_W)
        ln_b = sgu_ln_b[l].reshape(1, MIX_W)
        cb2 = conv_b[l].reshape(1, D_FF)

        bq, bk, bvt, cq, ckd, cvd = _inproj_call(x, lat(mul1), lat(sh1), w_attn[l], tables, tm=512, rope=True)
        xbq, xbk, xbvt, xcq, xckd, xcvd = _inproj_call(ctx, cx(mul1), cx(sh1), w_attn[l], ctx_tables,
                                                      tm=n_ctx, rope=False)

        o_b = _diff_call(scal, bq, subln, [(bk, bvt), (xbk, xbvt)], tq=512, kc=256)
        o_c = _win_call(sinks[l], cq, ckd, cvd, xckd, xcvd, local=True)
        x_mid = _merge_call(x, lat(mul1), lat(sh1), lat(g1), w_uvg[l], ln_g, ln_b, w_s_b[l], b_s_b[l],
                            o_b, o_c, w_br_b[l], w_out_b[l], tm=256)
        if not last:
            xo_b = _diff_call(scal, xbq, subln, [(xbk, xbvt)], tq=n_ctx, kc=512)
            xo_c = _win_call(sinks[l], xcq, None, None, xckd, xcvd, local=False)
            ctx_mid = _merge_call(ctx, cx(mul1), cx(sh1), cx(g1), w_uvg[l], ln_g, ln_b, w_s_b[l], b_s_b[l],
                                  xo_b, xo_c, w_br_b[l], w_out_b[l], tm=n_ctx)
        x = _ffn_call(x_mid, lat(mul2), lat(sh2), lat(g2), w_gate_b[l], conv_w[l], cb2, w_up_b[l],
                      w_down_b[l], final_g2, tm=256, final=last)
        if not last:
            ctx = _ffn_call(ctx_mid, cx(mul2), cx(sh2), cx(g2), w_gate_b[l], conv_w[l], cb2, w_up_b[l],
                            w_down_b[l], final_g2, tm=n_ctx, final=False)
    return x
```

```python
import functools

import numpy as np
import jax
import jax.numpy as jnp
from jax import lax
from jax.experimental import pallas as pl
from jax.experimental.pallas import tpu as pltpu

F32 = jnp.float32
BF16 = jnp.bfloat16

D_MODEL = 1024
DEPTH = 4
GRID_W = 64
HEAD_DIM = 64
BLOCK = 128
MIX_W = 512
A_GROUPS = 4
B_HEADS = 4
C_Q_HEADS = 8
D_FF = 2816
ROPE_BASE = 10000.0
EPS = 1e-6
SCALE = HEAD_DIM ** -0.5
NEG_INF = -1e30
LOG2E = float(np.log2(np.e))

LANES = 128
ATTN_COLS = 2304
GATE_COL0 = 3328
VMEM_LIMIT = 56 * 1024 * 1024


def _cparams(n_axes):
    return pltpu.CompilerParams(
        dimension_semantics=("arbitrary",) * n_axes, vmem_limit_bytes=VMEM_LIMIT)


def _const_spec(a):
    return pl.BlockSpec(a.shape, lambda *_: (0,) * a.ndim, pipeline_mode=pl.Buffered(1))


def _rms_mod(x, mul, sh):
    ms = jnp.mean(x * x, axis=-1, keepdims=True)
    return x * lax.rsqrt(ms + EPS) * mul + sh


def _gelu_tanh(x):
    c = np.float32(np.sqrt(2.0 / np.pi))
    return 0.5 * x * (1.0 + jnp.tanh(c * (x + 0.044715 * (x * x * x))))


def _dot(a, b):
    return jnp.dot(a, b, preferred_element_type=F32)


def _dot_nt(a, b):
    return lax.dot_general(a, b, (((1,), (1,)), ((), ())), preferred_element_type=F32)


def _ada_kernel(v_ref, w_ref, b_ref, o_ref):
    v = v_ref[...]
    s = v * jax.nn.sigmoid(v)
    o_ref[...] = _dot(s.astype(BF16), w_ref[...].astype(BF16)) + b_ref[...]


def _ada_call(vpad, w_ada, b_ada):
    rows = vpad.shape[0]
    tn = 1536
    return pl.pallas_call(
        _ada_kernel,
        grid=(DEPTH, 6 * D_MODEL // tn),
        in_specs=[
            pl.BlockSpec((rows, D_MODEL), lambda l, j: (0, 0)),
            pl.BlockSpec((None, D_MODEL, tn), lambda l, j: (l, 0, j)),
            pl.BlockSpec((None, 1, tn), lambda l, j: (l, 0, j)),
        ],
        out_specs=pl.BlockSpec((None, rows, tn), lambda l, j: (l, 0, j)),
        out_shape=jax.ShapeDtypeStruct((DEPTH, rows, 6 * D_MODEL), F32),
        compiler_params=_cparams(2),
        name="ada",
    )(vpad, w_ada, b_ada.reshape(DEPTH, 1, 6 * D_MODEL))


def _inproj_kernel(x_ref, mul_ref, sh_ref, w_ref, cos_ref, sa_ref, sb_ref,
                   bq_ref, bk_ref, bvt_ref, cq_ref, ckd_ref, cvdt_ref, *, rope):
    h = _rms_mod(x_ref[...], mul_ref[...], sh_ref[...]).astype(BF16)
    if rope:
        cos, sa, sb = cos_ref[...], sa_ref[...], sb_ref[...]

    def rot(y):
        if not rope:
            return y
        return y * cos + pltpu.roll(y, LANES - 16, 1) * sa + pltpu.roll(y, 16, 1) * sb

    def dup(y):
        lane = lax.broadcasted_iota(jnp.int32, y.shape, 1)
        sw = pltpu.roll(y, 64, 1)
        return jnp.concatenate([jnp.where(lane < 64, y, sw), jnp.where(lane < 64, sw, y)], axis=1)

    for col0, ref in ((0, bq_ref), (512, bk_ref), (1536, cq_ref)):
        y = _dot(h, w_ref[:, col0:col0 + 512])
        for g in range(4):
            ref[:, g * LANES:(g + 1) * LANES] = rot(y[:, g * LANES:(g + 1) * LANES]).astype(BF16)
    bvt_ref[...] = _dot(h, w_ref[:, 1024:1536]).T.astype(BF16)
    ykv = _dot(h, w_ref[:, 2048:2304])
    ckd_ref[...] = dup(rot(ykv[:, :LANES])).astype(BF16)
    cvdt_ref[...] = dup(ykv[:, LANES:]).T.astype(BF16)


def _inproj_call(x, mul, sh, w_attn, tables, *, tm, rope):
    bsz, length, _ = x.shape
    tok = lambda w: pl.BlockSpec((None, tm, w), lambda b, i: (b, i, 0))
    vec = pl.BlockSpec((None, 1, D_MODEL), lambda b, i: (b, 0, 0))
    tab = pl.BlockSpec((tm, LANES), lambda b, i: (i, 0))
    tok_t = lambda w: pl.BlockSpec((None, w, tm), lambda b, i: (b, 0, i))
    out = lambda w: jax.ShapeDtypeStruct((bsz, length, w), BF16)
    out_t = lambda w: jax.ShapeDtypeStruct((bsz, w, length), BF16)
    return pl.pallas_call(
        functools.partial(_inproj_kernel, rope=rope),
        grid=(bsz, length // tm),
        in_specs=[tok(D_MODEL), vec, vec,
                  _const_spec(w_attn), tab, tab, tab],
        out_specs=[tok(512), tok(512), tok_t(MIX_W), tok(512), tok(256), tok_t(2 * LANES)],
        out_shape=[out(512), out(512), out_t(MIX_W), out(512), out(256), out_t(2 * LANES)],
        compiler_params=_cparams(2),
        name="inproj",
    )(x, mul, sh, w_attn, *tables)


def _diff_kernel(sc_ref, q_ref, g_ref, *refs, seg_lens, kc):
    o_ref, p_scr = refs[-2], refs[-1]
    q = q_ref[...]
    tq = q.shape[0]
    lane = lax.broadcasted_iota(jnp.int32, q.shape, 1)
    zero = jnp.zeros_like(q)
    qq = jnp.concatenate([jnp.where(lane < 64, q, zero), jnp.where(lane >= 64, q, zero)], axis=0)

    chunks = []
    row0 = 0
    for si, n_keys in enumerate(seg_lens):
        for st in range(0, n_keys, kc):
            size = min(kc, n_keys - st)
            chunks.append((refs[2 * si], refs[2 * si + 1], st, size, row0))
            row0 += size

    mcs, lcs = [], []
    for k_ref, _, st, size, r0 in chunks:
        s = _dot_nt(k_ref[st:st + size, :], qq)
        mc = jnp.max(s, axis=0, keepdims=True)
        p = jnp.exp2(s - mc)
        lcs.append(jnp.sum(p, axis=0, keepdims=True))
        mcs.append(mc)
        p_scr[r0:r0 + size, :] = p.astype(BF16)

    m_all = functools.reduce(jnp.maximum, mcs)
    es = [jnp.exp2(mc - m_all) for mc in mcs]
    l_all = None
    for lc, e in zip(lcs, es):
        l_all = lc * e if l_all is None else l_all + lc * e
    lanei = lax.broadcasted_iota(jnp.int32, (1, 2 * tq), 1)
    coef = jnp.where(lanei < tq, 1.0, sc_ref[0]) / l_all

    acc = None
    for (_, vt_ref, st, size, r0), e in zip(chunks, es):
        f = e * coef
        f1 = f[:, 0:tq].astype(BF16)
        f2 = f[:, tq:2 * tq].astype(BF16)
        a_t = p_scr[r0:r0 + size, 0:tq] * f1 - p_scr[r0:r0 + size, tq:2 * tq] * f2
        t = _dot(vt_ref[:, st:st + size], a_t)
        acc = t if acc is None else acc + t
    o = acc.T
    ms = jnp.mean(o * o, axis=-1, keepdims=True)
    o_ref[...] = (o * lax.rsqrt(ms + EPS) * g_ref[...] * sc_ref[1]).astype(BF16)


def _diff_call(scal, q, subln_g, segs, *, tq, kc):
    bsz, lq, _ = q.shape
    in_specs = [pl.BlockSpec(memory_space=pltpu.SMEM),
                pl.BlockSpec((None, tq, LANES), lambda b, h, i: (b, i, h)),
                pl.BlockSpec((1, LANES), lambda b, h, i: (0, 0))]
    args = [scal, q, subln_g]
    for k, vt in segs:
        n_keys = k.shape[1]
        in_specs += [pl.BlockSpec((None, n_keys, LANES), lambda b, h, i: (b, 0, h)),
                     pl.BlockSpec((None, LANES, n_keys), lambda b, h, i: (b, h, 0))]
        args += [k, vt]
    return pl.pallas_call(
        functools.partial(_diff_kernel, seg_lens=tuple(k.shape[1] for k, _ in segs), kc=kc),
        grid=(bsz, B_HEADS, lq // tq),
        in_specs=in_specs,
        out_specs=pl.BlockSpec((None, tq, LANES), lambda b, h, i: (b, i, h)),
        out_shape=jax.ShapeDtypeStruct((bsz, lq, MIX_W), BF16),
        scratch_shapes=[pltpu.VMEM((sum(k.shape[1] for k, _ in segs), 2 * tq), BF16)],
        compiler_params=_cparams(3),
        name="diff",
    )(*args)


def _win_kernel(sink_ref, q_ref, *refs, local):
    o_ref = refs[-1]
    kx_ref, vxt_ref = refs[-3], refs[-2]
    n = pl.program_id(1)
    nb = pl.num_programs(1)
    nq = C_Q_HEADS // 2 * BLOCK
    lane = lax.broadcasted_iota(jnp.int32, (BLOCK, LANES), 1)
    if local:
        key = lax.broadcasted_iota(jnp.int32, (BLOCK, nq), 0)
        qry = lax.broadcasted_iota(jnp.int32, (BLOCK, nq), 1) & (BLOCK - 1)
        valid_prev = (key >= qry) & (n > 0)
        valid_next = (key <= qry) & (n < nb - 1)
    headi = lax.broadcasted_iota(jnp.int32, (1, nq), 1) // BLOCK
    s_l, sink_l, vt_l = [], [], []
    for kvh in range(2):
        ksl = slice(kvh * LANES, (kvh + 1) * LANES)
        parts = []
        for g in range(4):
            grp = kvh * 2 + g // 2
            qg = q_ref[:, grp * LANES:(grp + 1) * LANES]
            parts.append(jnp.where((lane < 64) if g % 2 == 0 else (lane >= 64), qg, jnp.zeros_like(qg)))
        qs = jnp.concatenate(parts, axis=0)
        sink = jnp.zeros((1, nq), F32)
        for g in range(4):
            sink = jnp.where(headi == g, sink_ref[kvh * 4 + g] * LOG2E, sink)
        if local:
            k_all = jnp.concatenate([refs[j][:, ksl] for j in range(3)] + [kx_ref[:, ksl]], axis=0)
            vt_all = jnp.concatenate([refs[3 + j][ksl, :] for j in range(3)] + [vxt_ref[ksl, :]], axis=1)
        else:
            k_all, vt_all = kx_ref[:, ksl], vxt_ref[ksl, :]
        s_l.append(_dot_nt(k_all, qs))
        sink_l.append(sink)
        vt_l.append(vt_all)
    if local:
        s_l = [jnp.concatenate([jnp.where(valid_prev, s[0:BLOCK], NEG_INF), s[BLOCK:2 * BLOCK],
                                jnp.where(valid_next, s[2 * BLOCK:3 * BLOCK], NEG_INF), s[3 * BLOCK:]], axis=0)
               for s in s_l]
    m_l = [jnp.maximum(jnp.max(s, axis=0, keepdims=True), sink) for s, sink in zip(s_l, sink_l)]
    p_l = [jnp.exp2(s - m) for s, m in zip(s_l, m_l)]
    l_l = [jnp.sum(p, axis=0, keepdims=True) + jnp.exp2(sink - m) for p, sink, m in zip(p_l, sink_l, m_l)]
    ot_l = [_dot(vt, p.astype(BF16)) / l for vt, p, l in zip(vt_l, p_l, l_l)]
    for kvh in range(2):
        o = ot_l[kvh].T
        for j in range(2):
            grp = kvh * 2 + j
            o_ref[:, grp * LANES:(grp + 1) * LANES] = jnp.where(
                lane < 64, o[2 * j * BLOCK:(2 * j + 1) * BLOCK], o[(2 * j + 1) * BLOCK:(2 * j + 2) * BLOCK]
            ).astype(BF16)


def _win_call(sinks, q, kd, vdt, kxd, vxdt, *, local):
    bsz, lq, _ = q.shape
    nb = lq // BLOCK
    blk = lambda w, f: pl.BlockSpec((None, BLOCK, w), f)
    in_specs = [pl.BlockSpec(memory_space=pltpu.SMEM), blk(MIX_W, lambda b, n: (b, n, 0))]
    args = [sinks, q]
    if local:
        nbr = [lambda n: jnp.maximum(n - 1, 0), lambda n: n, lambda n: jnp.minimum(n + 1, nb - 1)]
        in_specs += [blk(2 * LANES, lambda b, n, f=f: (b, f(n), 0)) for f in nbr]
        in_specs += [pl.BlockSpec((None, 2 * LANES, BLOCK), lambda b, n, f=f: (b, 0, f(n))) for f in nbr]
        args += [kd] * 3 + [vdt] * 3
    n_ctx = kxd.shape[1]
    in_specs += [pl.BlockSpec((None, n_ctx, 2 * LANES), lambda b, n: (b, 0, 0)),
                 pl.BlockSpec((None, 2 * LANES, n_ctx), lambda b, n: (b, 0, 0))]
    args += [kxd, vxdt]
    return pl.pallas_call(
        functools.partial(_win_kernel, local=local),
        grid=(bsz, nb),
        in_specs=in_specs,
        out_specs=blk(MIX_W, lambda b, n: (b, n, 0)),
        out_shape=jax.ShapeDtypeStruct((bsz, lq, MIX_W), BF16),
        compiler_params=_cparams(2),
        name="win",
    )(*args)


def _merge_kernel(x_ref, mul_ref, sh_ref, gate_ref, wuvg_ref, lng_ref, lnb_ref, ws_ref, bs_ref,
                  ob_ref, oc_ref, wbr_ref, wout_ref, o_ref):
    x = x_ref[...]
    tm = x.shape[0]
    h = _rms_mod(x, mul_ref[...], sh_ref[...]).astype(BF16)
    u = _gelu_tanh(_dot(h, wuvg_ref[:, 0:MIX_W]))
    v = _gelu_tanh(_dot(h, wuvg_ref[:, MIX_W:2 * MIX_W]))
    mu = jnp.mean(v, axis=-1, keepdims=True)
    vc = v - mu
    var = jnp.mean(vc * vc, axis=-1, keepdims=True)
    vn = (vc * lax.rsqrt(var + EPS) * lng_ref[...] + lnb_ref[...]).astype(BF16)
    n_chunks = tm // BLOCK
    mixed = []
    for g in range(A_GROUPS):
        gs = slice(g * LANES, (g + 1) * LANES)
        rhs = jnp.concatenate([vn[c * BLOCK:(c + 1) * BLOCK, gs] for c in range(n_chunks)], axis=1)
        mixed.append(_dot(ws_ref[g], rhs))
    o_a = (u * jnp.concatenate(
        [jnp.concatenate([mixed[g][:, c * LANES:(c + 1) * LANES] + bs_ref[g] for g in range(A_GROUPS)], axis=1)
         for c in range(n_chunks)], axis=0)).astype(BF16)
    y = None
    for i, o_i in enumerate((o_a, ob_ref[...], oc_ref[...])):
        col0 = 2 * MIX_W + i * D_MODEL
        gate = jax.nn.sigmoid(_dot(h, wuvg_ref[:, col0:col0 + D_MODEL]))
        t = gate * _dot(o_i, wbr_ref[i])
        y = t if y is None else y + t
    o_ref[...] = x + gate_ref[...] * _dot(y.astype(BF16), wout_ref[...])


def _merge_call(x, mul, sh, gate, w_uvg, ln_g, ln_b, w_s, b_s, o_b, o_c, w_br, w_out, *, tm):
    bsz, length, _ = x.shape
    tok = lambda w: pl.BlockSpec((None, tm, w), lambda b, i: (b, i, 0))
    vec = pl.BlockSpec((None, 1, D_MODEL), lambda b, i: (b, 0, 0))
    full = _const_spec
    return pl.pallas_call(
        _merge_kernel,
        grid=(bsz, length // tm),
        in_specs=[tok(D_MODEL), vec, vec, vec, full(w_uvg), full(ln_g), full(ln_b), full(w_s), full(b_s),
                  tok(MIX_W), tok(MIX_W), full(w_br), full(w_out)],
        out_specs=tok(D_MODEL),
        out_shape=jax.ShapeDtypeStruct(x.shape, F32),
        compiler_params=_cparams(2),
        name="merge",
    )(x, mul, sh, gate, w_uvg, ln_g, ln_b, w_s, b_s, o_b, o_c, w_br, w_out)


HALO = 8
MXU_TILE = 256
FF_EDGES = (0, 4 * MXU_TILE, 8 * MXU_TILE, D_FF)


def _ffn_kernel(x_ref, xp_ref, xn_ref, mul_ref, sh_ref, gate_ref, wg_ref, cw_ref, cb_ref, wu_ref, wd_ref,
                fg_ref, o_ref, *, final):
    i = pl.program_id(1)
    nt = pl.num_programs(1)
    x = x_ref[...]
    tm = x.shape[0]
    xe = jnp.concatenate([xp_ref[...], x, xn_ref[...]], axis=0)
    he = _rms_mod(xe, mul_ref[...], sh_ref[...]).astype(BF16)
    hm = he[HALO:HALO + tm]
    rowe = lax.broadcasted_iota(jnp.int32, (tm + 2 * HALO, 1), 0)
    keep = ((rowe >= HALO) | (i > 0)) & ((rowe < HALO + tm) | (i < nt - 1))
    acc = None
    for c0, c1 in zip(FF_EDGES[:-1], FF_EDGES[1:]):
        cs = slice(c0, c1)
        a = jnp.where(keep, _dot(he, wg_ref[:, cs]), 0.0)
        a_prev = pltpu.roll(a, 1, 0)[HALO:HALO + tm]
        a_next = pltpu.roll(a, tm + 2 * HALO - 1, 0)[HALO:HALO + tm]
        a = (a_prev * cw_ref[0:1, cs] + a[HALO:HALO + tm] * cw_ref[1:2, cs] + a_next * cw_ref[2:3, cs]
             + cb_ref[:, cs])
        z = (a * jax.nn.sigmoid(a) * _dot(hm, wu_ref[:, cs])).astype(BF16)
        t = _dot(z, wd_ref[cs, :])
        acc = t if acc is None else acc + t
    out = x + gate_ref[...] * acc
    if final:
        ms = jnp.mean(out * out, axis=-1, keepdims=True)
        out = out * lax.rsqrt(ms + EPS) * fg_ref[...]
    o_ref[...] = out


def _ffn_call(x, mul, sh, gate, w_gate, conv_w, conv_b, w_up, w_down, final_g, *, tm, final):
    bsz, length, _ = x.shape
    per = tm // HALO
    last = length // HALO - 1
    tok = pl.BlockSpec((None, tm, D_MODEL), lambda b, i: (b, i, 0))
    prv = pl.BlockSpec((None, HALO, D_MODEL), lambda b, i: (b, jnp.maximum(i * per - 1, 0), 0))
    nxt = pl.BlockSpec((None, HALO, D_MODEL), lambda b, i: (b, jnp.minimum((i + 1) * per, last), 0))
    vec = pl.BlockSpec((None, 1, D_MODEL), lambda b, i: (b, 0, 0))
    full = _const_spec
    return pl.pallas_call(
        functools.partial(_ffn_kernel, final=final),
        grid=(bsz, length // tm),
        in_specs=[tok, prv, nxt, vec, vec, vec, full(w_gate), full(conv_w), full(conv_b), full(w_up),
                  full(w_down), full(final_g)],
        out_specs=tok,
        out_shape=jax.ShapeDtypeStruct(x.shape, F32),
        compiler_params=_cparams(2),
        name="ffn",
    )(x, x, x, mul, sh, gate, w_gate, conv_w, conv_b, w_up, w_down, final_g)


def _rope_tables(length):
    pos = jnp.arange(length)
    rows = (pos // GRID_W).astype(F32)
    cols = (pos % GRID_W).astype(F32)
    half = HEAD_DIM // 2
    inv = ROPE_BASE ** (-jnp.arange(0, half, 2, dtype=F32) / half)
    ang_r = rows[:, None] * inv[None, :]
    ang_c = cols[:, None] * inv[None, :]
    zero = jnp.zeros_like(ang_r)
    cos = jnp.concatenate([jnp.cos(ang_r)] * 2 + [jnp.cos(ang_c)] * 2, axis=1)
    sa = jnp.concatenate([-jnp.sin(ang_r), zero, -jnp.sin(ang_c), zero], axis=1)
    sb = jnp.concatenate([zero, jnp.sin(ang_r), zero, jnp.sin(ang_c)], axis=1)
    return tuple(jnp.tile(t, (1, LANES // HEAD_DIM)) for t in (cos, sa, sb))


def kernel(x, c, ctx, c_ctx, w_ada, b_ada, norm1_g, w_in, sgu_ln_g, sgu_ln_b, w_s, b_s, lam_q1, lam_k1,
           lam_q2, lam_k2, diff_subln_g, sinks, w_branch, w_out, norm2_g, w_gate, conv_w, conv_b, w_up,
           w_down, final_g):
    bsz, length, _ = x.shape
    n_ctx = ctx.shape[1]
    tables = _rope_tables(length)
    ctx_tables = tuple(t[:n_ctx] for t in tables)

    rows = -(-(bsz + 1) // 8) * 8
    vpad = jnp.zeros((rows, D_MODEL), F32).at[:bsz].set(c).at[bsz].set(c_ctx)
    ada = _ada_call(vpad, w_ada, b_ada)

    qscale = jnp.ones((ATTN_COLS,), F32).at[0:512].set(SCALE * LOG2E).at[1536:2048].set(SCALE * LOG2E)
    w_attn = (w_in[:, :, 2 * MIX_W:2 * MIX_W + ATTN_COLS] * qscale).astype(BF16)
    w_uvg = jnp.concatenate([w_in[:, :, :2 * MIX_W], w_in[:, :, GATE_COL0:]], axis=2).astype(BF16)
    w_s_b = w_s.astype(BF16)
    b_s_b = jnp.broadcast_to(b_s[..., None], b_s.shape + (LANES,))
    w_br_b = w_branch.astype(BF16)
    w_out_b = w_out.astype(BF16)
    w_gate_b = w_gate.astype(BF16)
    w_up_b = w_up.astype(BF16)
    w_down_b = w_down.astype(BF16)
    final_g2 = final_g.reshape(1, D_MODEL)

    for l in range(DEPTH):
        last = l == DEPTH - 1
        m = ada[l]
        sh1, sc1, g1, sh2, sc2, g2 = [m[:, j * D_MODEL:(j + 1) * D_MODEL] for j in range(6)]
        mul1 = norm1_g[l][None, :] * (1.0 + sc1)
        mul2 = norm2_g[l][None, :] * (1.0 + sc2)
        lat = lambda a: a[:bsz, None, :]
        cx = lambda a: jnp.broadcast_to(a[bsz][None, None, :], (bsz, 1, D_MODEL))

        lam_init = 0.8 - 0.6 * float(np.exp(-0.3 * l))
        lam = jnp.exp(jnp.sum(lam_q1[l] * lam_k1[l])) - jnp.exp(jnp.sum(lam_q2[l] * lam_k2[l])) + lam_init
        scal = jnp.stack([lam, jnp.float32(1.0 - lam_init)]).astype(F32)
        subln = diff_subln_g[l].reshape(1, LANES)
        ln_g = sgu_ln_g[l].reshape(1, MIX_W)
        ln_b = sgu_ln_b[l].reshape(1, MIX_W)
        cb2 = conv_b[l].reshape(1, D_FF)

        bq, bk, bvt, cq, ckd, cvd = _inproj_call(x, lat(mul1), lat(sh1), w_attn[l], tables, tm=512, rope=True)
        xbq, xbk, xbvt, xcq, xckd, xcvd = _inproj_call(ctx, cx(mul1), cx(sh1), w_attn[l], ctx_tables,
                                                      tm=n_ctx, rope=False)

        o_b = _diff_call(scal, bq, subln, [(bk, bvt), (xbk, xbvt)], tq=512, kc=256)
        o_c = _win_call(sinks[l], cq, ckd, cvd, xckd, xcvd, local=True)
        x_mid = _merge_call(x, lat(mul1), lat(sh1), lat(g1), w_uvg[l], ln_g, ln_b, w_s_b[l], b_s_b[l],
                            o_b, o_c, w_br_b[l], w_out_b[l], tm=512)
        if not last:
            xo_b = _diff_call(scal, xbq, subln, [(xbk, xbvt)], tq=n_ctx, kc=512)
            xo_c = _win_call(sinks[l], xcq, None, None, xckd, xcvd, local=False)
            ctx_mid = _merge_call(ctx, cx(mul1), cx(sh1), cx(g1), w_uvg[l], ln_g, ln_b, w_s_b[l], b_s_b[l],
                                  xo_b, xo_c, w_br_b[l], w_out_b[l], tm=n_ctx)
        x = _ffn_call(x_mid, lat(mul2), lat(sh2), lat(g2), w_gate_b[l], conv_w[l], cb2, w_up_b[l],
                      w_down_b[l], final_g2, tm=512, final=last)
        if not last:
            ctx = _ffn_call(ctx_mid, cx(mul2), cx(sh2), cx(g2), w_gate_b[l], conv_w[l], cb2, w_up_b[l],
                            w_down_b[l], final_g2, tm=n_ctx, final=False)
    return x
```

```python
import functools

import numpy as np
import jax
import jax.numpy as jnp
from jax import lax
from jax.experimental import pallas as pl
from jax.experimental.pallas import tpu as pltpu

F32 = jnp.float32
BF16 = jnp.bfloat16

D_MODEL = 1024
DEPTH = 4
GRID_W = 64
HEAD_DIM = 64
BLOCK = 128
MIX_W = 512
A_GROUPS = 4
B_HEADS = 4
C_Q_HEADS = 8
D_FF = 2816
ROPE_BASE = 10000.0
EPS = 1e-6
SCALE = HEAD_DIM ** -0.5
NEG_INF = -1e30
LOG2E = float(np.log2(np.e))

LANES = 128
ATTN_COLS = 2304
GATE_COL0 = 3328
VMEM_LIMIT = 56 * 1024 * 1024


def _cparams(n_axes):
    return pltpu.CompilerParams(
        dimension_semantics=("arbitrary",) * n_axes, vmem_limit_bytes=VMEM_LIMIT)


def _const_spec(a):
    return pl.BlockSpec(a.shape, lambda *_: (0,) * a.ndim, pipeline_mode=pl.Buffered(1))


def _rms_mod(x, mul, sh):
    ms = jnp.mean(x * x, axis=-1, keepdims=True)
    return x * lax.rsqrt(ms + EPS) * mul + sh


def _gelu_tanh(x):
    c = np.float32(np.sqrt(2.0 / np.pi))
    return 0.5 * x * (1.0 + jnp.tanh(c * (x + 0.044715 * (x * x * x))))


def _dot(a, b):
    return jnp.dot(a, b, preferred_element_type=F32)


def _dot_nt(a, b):
    return lax.dot_general(a, b, (((1,), (1,)), ((), ())), preferred_element_type=F32)


def _ada_kernel(v_ref, w_ref, b_ref, o_ref):
    v = v_ref[...]
    s = v * jax.nn.sigmoid(v)
    o_ref[...] = _dot(s.astype(BF16), w_ref[...].astype(BF16)) + b_ref[...]


def _ada_call(vpad, w_ada, b_ada):
    rows = vpad.shape[0]
    tn = 1536
    return pl.pallas_call(
        _ada_kernel,
        grid=(DEPTH, 6 * D_MODEL // tn),
        in_specs=[
            pl.BlockSpec((rows, D_MODEL), lambda l, j: (0, 0)),
            pl.BlockSpec((None, D_MODEL, tn), lambda l, j: (l, 0, j)),
            pl.BlockSpec((None, 1, tn), lambda l, j: (l, 0, j)),
        ],
        out_specs=pl.BlockSpec((None, rows, tn), lambda l, j: (l, 0, j)),
        out_shape=jax.ShapeDtypeStruct((DEPTH, rows, 6 * D_MODEL), F32),
        compiler_params=_cparams(2),
        name="ada",
    )(vpad, w_ada, b_ada.reshape(DEPTH, 1, 6 * D_MODEL))


def _inproj_kernel(x_ref, mul_ref, sh_ref, w_ref, cos_ref, sa_ref, sb_ref,
                   bq_ref, bk_ref, bvt_ref, cq_ref, ckd_ref, cvdt_ref, *, rope):
    h = _rms_mod(x_ref[...], mul_ref[...], sh_ref[...]).astype(BF16)
    if rope:
        cos, sa, sb = cos_ref[...], sa_ref[...], sb_ref[...]

    def rot(y):
        if not rope:
            return y
        return y * cos + pltpu.roll(y, LANES - 16, 1) * sa + pltpu.roll(y, 16, 1) * sb

    def dup(y):
        lane = lax.broadcasted_iota(jnp.int32, y.shape, 1)
        sw = pltpu.roll(y, 64, 1)
        return jnp.concatenate([jnp.where(lane < 64, y, sw), jnp.where(lane < 64, sw, y)], axis=1)

    for col0, ref in ((0, bq_ref), (512, bk_ref), (1536, cq_ref)):
        y = _dot(h, w_ref[:, col0:col0 + 512])
        for g in range(4):
            ref[:, g * LANES:(g + 1) * LANES] = rot(y[:, g * LANES:(g + 1) * LANES]).astype(BF16)
    bvt_ref[...] = _dot(h, w_ref[:, 1024:1536]).T.astype(BF16)
    ykv = _dot(h, w_ref[:, 2048:2304])
    ckd_ref[...] = dup(rot(ykv[:, :LANES])).astype(BF16)
    cvdt_ref[...] = dup(ykv[:, LANES:]).T.astype(BF16)


def _inproj_call(x, mul, sh, w_attn, tables, *, tm, rope):
    bsz, length, _ = x.shape
    tok = lambda w: pl.BlockSpec((None, tm, w), lambda b, i: (b, i, 0))
    vec = pl.BlockSpec((None, 1, D_MODEL), lambda b, i: (b, 0, 0))
    tab = pl.BlockSpec((tm, LANES), lambda b, i: (i, 0))
    tok_t = lambda w: pl.BlockSpec((None, w, tm), lambda b, i: (b, 0, i))
    out = lambda w: jax.ShapeDtypeStruct((bsz, length, w), BF16)
    out_t = lambda w: jax.ShapeDtypeStruct((bsz, w, length), BF16)
    return pl.pallas_call(
        functools.partial(_inproj_kernel, rope=rope),
        grid=(bsz, length // tm),
        in_specs=[tok(D_MODEL), vec, vec,
                  _const_spec(w_attn), tab, tab, tab],
        out_specs=[tok(512), tok(512), tok_t(MIX_W), tok(512), tok(256), tok_t(2 * LANES)],
        out_shape=[out(512), out(512), out_t(MIX_W), out(512), out(256), out_t(2 * LANES)],
        compiler_params=_cparams(2),
        name="inproj",
    )(x, mul, sh, w_attn, *tables)


SAFE_EXP2_RANGE = 90.0


def _diff_kernel(sc_ref, q_ref, qall_ref, g_ref, ind_ref, *refs, seg_lens, kc):
    o_ref, p_scr, safe_scr = refs[-3:]
    q = q_ref[...]
    tq = q.shape[0]
    lane = lax.broadcasted_iota(jnp.int32, q.shape, 1)
    zero = jnp.zeros_like(q)
    qq = jnp.concatenate([jnp.where(lane < 64, q, zero), jnp.where(lane >= 64, q, zero)], axis=0)
    lanei = lax.broadcasted_iota(jnp.int32, (1, 2 * tq), 1)

    chunks = []
    row0 = 0
    for si, n_keys in enumerate(seg_lens):
        for st in range(0, n_keys, kc):
            size = min(kc, n_keys - st)
            chunks.append((refs[2 * si], refs[2 * si + 1], st, size, row0))
            row0 += size

    def max_sq_norm(x):
        xf = x.astype(F32)
        return jnp.max(_dot((xf * xf).astype(BF16), ind_ref[...]), axis=0, keepdims=True)

    @pl.when(pl.program_id(2) == 0)
    def _():
        kn = None
        for k_ref, _, st, size, _ in chunks:
            n2 = max_sq_norm(k_ref[st:st + size, :])
            kn = n2 if kn is None else jnp.maximum(kn, n2)
        qn = None
        n_q = qall_ref.shape[0]
        for st in range(0, n_q, 2 * kc):
            n2 = max_sq_norm(qall_ref[st:min(st + 2 * kc, n_q), :])
            qn = n2 if qn is None else jnp.maximum(qn, n2)
        bound2 = jnp.max(kn * qn)
        safe_scr[0] = (bound2 < SAFE_EXP2_RANGE * SAFE_EXP2_RANGE).astype(jnp.int32)

    safe = safe_scr[0] == 1

    def finish(o_t):
        o = o_t.T
        ms = jnp.mean(o * o, axis=-1, keepdims=True)
        o_ref[...] = (o * lax.rsqrt(ms + EPS) * g_ref[...] * sc_ref[1]).astype(BF16)

    @pl.when(safe)
    def _():
        l_all = None
        for k_ref, _, st, size, r0 in chunks:
            p = jnp.exp2(_dot_nt(k_ref[st:st + size, :], qq))
            lc = jnp.sum(p, axis=0, keepdims=True)
            l_all = lc if l_all is None else l_all + lc
            p_scr[r0:r0 + size, :] = p.astype(BF16)
        coef = jnp.where(lanei < tq, 1.0, sc_ref[0]) / l_all
        c1 = coef[:, 0:tq]
        ratio = (coef[:, tq:2 * tq] / c1).astype(BF16)
        acc = None
        for _, vt_ref, st, size, r0 in chunks:
            a_t = p_scr[r0:r0 + size, 0:tq] - p_scr[r0:r0 + size, tq:2 * tq] * ratio
            part = _dot(vt_ref[:, st:st + size], a_t)
            acc = part if acc is None else acc + part
        finish(acc * c1)

    @pl.when(jnp.logical_not(safe))
    def _():
        mcs, lcs = [], []
        for k_ref, _, st, size, r0 in chunks:
            s = _dot_nt(k_ref[st:st + size, :], qq)
            mc = jnp.max(s, axis=0, keepdims=True)
            p = jnp.exp2(s - mc)
            lcs.append(jnp.sum(p, axis=0, keepdims=True))
            mcs.append(mc)
            p_scr[r0:r0 + size, :] = p.astype(BF16)
        m_all = functools.reduce(jnp.maximum, mcs)
        es = [jnp.exp2(mc - m_all) for mc in mcs]
        l_all = None
        for lc, e in zip(lcs, es):
            l_all = lc * e if l_all is None else l_all + lc * e
        coef = jnp.where(lanei < tq, 1.0, sc_ref[0]) / l_all
        acc = None
        for (_, vt_ref, st, size, r0), e in zip(chunks, es):
            f = e * coef
            f1 = f[:, 0:tq].astype(BF16)
            f2 = f[:, tq:2 * tq].astype(BF16)
            a_t = p_scr[r0:r0 + size, 0:tq] * f1 - p_scr[r0:r0 + size, tq:2 * tq] * f2
            part = _dot(vt_ref[:, st:st + size], a_t)
            acc = part if acc is None else acc + part
        finish(acc)


def _diff_call(scal, q, subln_g, segs, *, tq, kc):
    bsz, lq, _ = q.shape
    half = np.arange(LANES) // HEAD_DIM
    ind = jnp.asarray(half[:, None] == half[None, :], BF16)
    in_specs = [pl.BlockSpec(memory_space=pltpu.SMEM),
                pl.BlockSpec((None, tq, LANES), lambda b, h, i: (b, i, h)),
                pl.BlockSpec((None, lq, LANES), lambda b, h, i: (b, 0, h)),
                pl.BlockSpec((1, LANES), lambda b, h, i: (0, 0)),
                _const_spec(ind)]
    args = [scal, q, q, subln_g, ind]
    for k, vt in segs:
        n_keys = k.shape[1]
        in_specs += [pl.BlockSpec((None, n_keys, LANES), lambda b, h, i: (b, 0, h)),
                     pl.BlockSpec((None, LANES, n_keys), lambda b, h, i: (b, h, 0))]
        args += [k, vt]
    return pl.pallas_call(
        functools.partial(_diff_kernel, seg_lens=tuple(k.shape[1] for k, _ in segs), kc=kc),
        grid=(bsz, B_HEADS, lq // tq),
        in_specs=in_specs,
        out_specs=pl.BlockSpec((None, tq, LANES), lambda b, h, i: (b, i, h)),
        out_shape=jax.ShapeDtypeStruct((bsz, lq, MIX_W), BF16),
        scratch_shapes=[pltpu.VMEM((sum(k.shape[1] for k, _ in segs), 2 * tq), BF16),
                        pltpu.SMEM((1,), jnp.int32)],
        compiler_params=_cparams(3),
        name="diff",
    )(*args)


def _win_kernel(sink_ref, q_ref, *refs, local):
    o_ref = refs[-1]
    kx_ref, vxt_ref = refs[-3], refs[-2]
    n = pl.program_id(1)
    nb = pl.num_programs(1)
    nq = C_Q_HEADS // 2 * BLOCK
    lane = lax.broadcasted_iota(jnp.int32, (BLOCK, LANES), 1)
    if local:
        key = lax.broadcasted_iota(jnp.int32, (BLOCK, nq), 0)
        qry = lax.broadcasted_iota(jnp.int32, (BLOCK, nq), 1) & (BLOCK - 1)
        valid_prev = (key >= qry) & (n > 0)
        valid_next = (key <= qry) & (n < nb - 1)
    headi = lax.broadcasted_iota(jnp.int32, (1, nq), 1) // BLOCK
    s_l, sink_l, vt_l = [], [], []
    for kvh in range(2):
        ksl = slice(kvh * LANES, (kvh + 1) * LANES)
        parts = []
        for g in range(4):
            grp = kvh * 2 + g // 2
            qg = q_ref[:, grp * LANES:(grp + 1) * LANES]
            parts.append(jnp.where((lane < 64) if g % 2 == 0 else (lane >= 64), qg, jnp.zeros_like(qg)))
        qs = jnp.concatenate(parts, axis=0)
        sink = jnp.zeros((1, nq), F32)
        for g in range(4):
            sink = jnp.where(headi == g, sink_ref[kvh * 4 + g] * LOG2E, sink)
        if local:
            k_all = jnp.concatenate([refs[j][:, ksl] for j in range(3)] + [kx_ref[:, ksl]], axis=0)
            vt_all = jnp.concatenate([refs[3 + j][ksl, :] for j in range(3)] + [vxt_ref[ksl, :]], axis=1)
        else:
            k_all, vt_all = kx_ref[:, ksl], vxt_ref[ksl, :]
        s_l.append(_dot_nt(k_all, qs))
        sink_l.append(sink)
        vt_l.append(vt_all)
    if local:
        s_l = [jnp.concatenate([jnp.where(valid_prev, s[0:BLOCK], NEG_INF), s[BLOCK:2 * BLOCK],
                                jnp.where(valid_next, s[2 * BLOCK:3 * BLOCK], NEG_INF), s[3 * BLOCK:]], axis=0)
               for s in s_l]
    m_l = [jnp.maximum(jnp.max(s, axis=0, keepdims=True), sink) for s, sink in zip(s_l, sink_l)]
    p_l = [jnp.exp2(s - m) for s, m in zip(s_l, m_l)]
    l_l = [jnp.sum(p, axis=0, keepdims=True) + jnp.exp2(sink - m) for p, sink, m in zip(p_l, sink_l, m_l)]
    ot_l = [_dot(vt, p.astype(BF16)) / l for vt, p, l in zip(vt_l, p_l, l_l)]
    for kvh in range(2):
        o = ot_l[kvh].T
        for j in range(2):
            grp = kvh * 2 + j
            o_ref[:, grp * LANES:(grp + 1) * LANES] = jnp.where(
                lane < 64, o[2 * j * BLOCK:(2 * j + 1) * BLOCK], o[(2 * j + 1) * BLOCK:(2 * j + 2) * BLOCK]
            ).astype(BF16)


def _win_call(sinks, q, kd, vdt, kxd, vxdt, *, local):
    bsz, lq, _ = q.shape
    nb = lq // BLOCK
    blk = lambda w, f: pl.BlockSpec((None, BLOCK, w), f)
    in_specs = [pl.BlockSpec(memory_space=pltpu.SMEM), blk(MIX_W, lambda b, n: (b, n, 0))]
    args = [sinks, q]
    if local:
        nbr = [lambda n: jnp.maximum(n - 1, 0), lambda n: n, lambda n: jnp.minimum(n + 1, nb - 1)]
        in_specs += [blk(2 * LANES, lambda b, n, f=f: (b, f(n), 0)) for f in nbr]
        in_specs += [pl.BlockSpec((None, 2 * LANES, BLOCK), lambda b, n, f=f: (b, 0, f(n))) for f in nbr]
        args += [kd] * 3 + [vdt] * 3
    n_ctx = kxd.shape[1]
    in_specs += [pl.BlockSpec((None, n_ctx, 2 * LANES), lambda b, n: (b, 0, 0)),
                 pl.BlockSpec((None, 2 * LANES, n_ctx), lambda b, n: (b, 0, 0))]
    args += [kxd, vxdt]
    return pl.pallas_call(
        functools.partial(_win_kernel, local=local),
        grid=(bsz, nb),
        in_specs=in_specs,
        out_specs=blk(MIX_W, lambda b, n: (b, n, 0)),
        out_shape=jax.ShapeDtypeStruct((bsz, lq, MIX_W), BF16),
        compiler_params=_cparams(2),
        name="win",
    )(*args)


def _merge_kernel(x_ref, mul_ref, sh_ref, gate_ref, wuvg_ref, lng_ref, lnb_ref, ws_ref, bs_ref,
                  ob_ref, oc_ref, wbr_ref, wout_ref, o_ref):
    x = x_ref[...]
    tm = x.shape[0]
    h = _rms_mod(x, mul_ref[...], sh_ref[...]).astype(BF16)
    u = _gelu_tanh(_dot(h, wuvg_ref[:, 0:MIX_W]))
    v = _gelu_tanh(_dot(h, wuvg_ref[:, MIX_W:2 * MIX_W]))
    mu = jnp.mean(v, axis=-1, keepdims=True)
    vc = v - mu
    var = jnp.mean(vc * vc, axis=-1, keepdims=True)
    vn = (vc * lax.rsqrt(var + EPS) * lng_ref[...] + lnb_ref[...]).astype(BF16)
    n_chunks = tm // BLOCK
    mixed = []
    for g in range(A_GROUPS):
        gs = slice(g * LANES, (g + 1) * LANES)
        rhs = jnp.concatenate([vn[c * BLOCK:(c + 1) * BLOCK, gs] for c in range(n_chunks)], axis=1)
        mixed.append(_dot(ws_ref[g], rhs))
    o_a = (u * jnp.concatenate(
        [jnp.concatenate([mixed[g][:, c * LANES:(c + 1) * LANES] + bs_ref[g] for g in range(A_GROUPS)], axis=1)
         for c in range(n_chunks)], axis=0)).astype(BF16)
    y = None
    for i, o_i in enumerate((o_a, ob_ref[...], oc_ref[...])):
        col0 = 2 * MIX_W + i * D_MODEL
        gate = jax.nn.sigmoid(_dot(h, wuvg_ref[:, col0:col0 + D_MODEL]))
        t = gate * _dot(o_i, wbr_ref[i])
        y = t if y is None else y + t
    o_ref[...] = x + gate_ref[...] * _dot(y.astype(BF16), wout_ref[...])


def _merge_call(x, mul, sh, gate, w_uvg, ln_g, ln_b, w_s, b_s, o_b, o_c, w_br, w_out, *, tm):
    bsz, length, _ = x.shape
    tok = lambda w: pl.BlockSpec((None, tm, w), lambda b, i: (b, i, 0))
    vec = pl.BlockSpec((None, 1, D_MODEL), lambda b, i: (b, 0, 0))
    full = _const_spec
    return pl.pallas_call(
        _merge_kernel,
        grid=(bsz, length // tm),
        in_specs=[tok(D_MODEL), vec, vec, vec, full(w_uvg), full(ln_g), full(ln_b), full(w_s), full(b_s),
                  tok(MIX_W), tok(MIX_W), full(w_br), full(w_out)],
        out_specs=tok(D_MODEL),
        out_shape=jax.ShapeDtypeStruct(x.shape, F32),
        compiler_params=_cparams(2),
        name="merge",
    )(x, mul, sh, gate, w_uvg, ln_g, ln_b, w_s, b_s, o_b, o_c, w_br, w_out)


HALO = 8
MXU_TILE = 256
FF_EDGES = (0, 4 * MXU_TILE, 8 * MXU_TILE, D_FF)


def _ffn_kernel(x_ref, xp_ref, xn_ref, mul_ref, sh_ref, gate_ref, wg_ref, cw_ref, cb_ref, wu_ref, wd_ref,
                fg_ref, o_ref, *, final):
    i = pl.program_id(1)
    nt = pl.num_programs(1)
    x = x_ref[...]
    tm = x.shape[0]
    xe = jnp.concatenate([xp_ref[...], x, xn_ref[...]], axis=0)
    he = _rms_mod(xe, mul_ref[...], sh_ref[...]).astype(BF16)
    hm = he[HALO:HALO + tm]
    rowe = lax.broadcasted_iota(jnp.int32, (tm + 2 * HALO, 1), 0)
    keep = ((rowe >= HALO) | (i > 0)) & ((rowe < HALO + tm) | (i < nt - 1))
    acc = None
    for c0, c1 in zip(FF_EDGES[:-1], FF_EDGES[1:]):
        cs = slice(c0, c1)
        a = jnp.where(keep, _dot(he, wg_ref[:, cs]), 0.0)
        a_prev = pltpu.roll(a, 1, 0)[HALO:HALO + tm]
        a_next = pltpu.roll(a, tm + 2 * HALO - 1, 0)[HALO:HALO + tm]
        a = (a_prev * cw_ref[0:1, cs] + a[HALO:HALO + tm] * cw_ref[1:2, cs] + a_next * cw_ref[2:3, cs]
             + cb_ref[:, cs])
        z = (a * jax.nn.sigmoid(a) * _dot(hm, wu_ref[:, cs])).astype(BF16)
        t = _dot(z, wd_ref[cs, :])
        acc = t if acc is None else acc + t
    out = x + gate_ref[...] * acc
    if final:
        ms = jnp.mean(out * out, axis=-1, keepdims=True)
        out = out * lax.rsqrt(ms + EPS) * fg_ref[...]
    o_ref[...] = out


def _ffn_call(x, mul, sh, gate, w_gate, conv_w, conv_b, w_up, w_down, final_g, *, tm, final):
    bsz, length, _ = x.shape
    per = tm // HALO
    last = length // HALO - 1
    tok = pl.BlockSpec((None, tm, D_MODEL), lambda b, i: (b, i, 0))
    prv = pl.BlockSpec((None, HALO, D_MODEL), lambda b, i: (b, jnp.maximum(i * per - 1, 0), 0))
    nxt = pl.BlockSpec((None, HALO, D_MODEL), lambda b, i: (b, jnp.minimum((i + 1) * per, last), 0))
    vec = pl.BlockSpec((None, 1, D_MODEL), lambda b, i: (b, 0, 0))
    full = _const_spec
    return pl.pallas_call(
        functools.partial(_ffn_kernel, final=final),
        grid=(bsz, length // tm),
        in_specs=[tok, prv, nxt, vec, vec, vec, full(w_gate), full(conv_w), full(conv_b), full(w_up),
                  full(w_down), full(final_g)],
        out_specs=tok,
        out_shape=jax.ShapeDtypeStruct(x.shape, F32),
        compiler_params=_cparams(2),
        name="ffn",
    )(x, x, x, mul, sh, gate, w_gate, conv_w, conv_b, w_up, w_down, final_g)


def _rope_tables(length):
    pos = jnp.arange(length)
    rows = (pos // GRID_W).astype(F32)
    cols = (pos % GRID_W).astype(F32)
    half = HEAD_DIM // 2
    inv = ROPE_BASE ** (-jnp.arange(0, half, 2, dtype=F32) / half)
    ang_r = rows[:, None] * inv[None, :]
    ang_c = cols[:, None] * inv[None, :]
    zero = jnp.zeros_like(ang_r)
    cos = jnp.concatenate([jnp.cos(ang_r)] * 2 + [jnp.cos(ang_c)] * 2, axis=1)
    sa = jnp.concatenate([-jnp.sin(ang_r), zero, -jnp.sin(ang_c), zero], axis=1)
    sb = jnp.concatenate([zero, jnp.sin(ang_r), zero, jnp.sin(ang_c)], axis=1)
    return tuple(jnp.tile(t, (1, LANES // HEAD_DIM)) for t in (cos, sa, sb))


def kernel(x, c, ctx, c_ctx, w_ada, b_ada, norm1_g, w_in, sgu_ln_g, sgu_ln_b, w_s, b_s, lam_q1, lam_k1,
           lam_q2, lam_k2, diff_subln_g, sinks, w_branch, w_out, norm2_g, w_gate, conv_w, conv_b, w_up,
           w_down, final_g):
    bsz, length, _ = x.shape
    n_ctx = ctx.shape[1]
    tables = _rope_tables(length)
    ctx_tables = tuple(t[:n_ctx] for t in tables)

    rows = -(-(bsz + 1) // 8) * 8
    vpad = jnp.zeros((rows, D_MODEL), F32).at[:bsz].set(c).at[bsz].set(c_ctx)
    ada = _ada_call(vpad, w_ada, b_ada)

    qscale = jnp.ones((ATTN_COLS,), F32).at[0:512].set(SCALE * LOG2E).at[1536:2048].set(SCALE * LOG2E)
    w_attn = (w_in[:, :, 2 * MIX_W:2 * MIX_W + ATTN_COLS] * qscale).astype(BF16)
    w_uvg = jnp.concatenate([w_in[:, :, :2 * MIX_W], w_in[:, :, GATE_COL0:]], axis=2).astype(BF16)
    w_s_b = w_s.astype(BF16)
    b_s_b = jnp.broadcast_to(b_s[..., None], b_s.shape + (LANES,))
    w_br_b = w_branch.astype(BF16)
    w_out_b = w_out.astype(BF16)
    w_gate_b = w_gate.astype(BF16)
    w_up_b = w_up.astype(BF16)
    w_down_b = w_down.astype(BF16)
    final_g2 = final_g.reshape(1, D_MODEL)

    for l in range(DEPTH):
        last = l == DEPTH - 1
        m = ada[l]
        sh1, sc1, g1, sh2, sc2, g2 = [m[:, j * D_MODEL:(j + 1) * D_MODEL] for j in range(6)]
        mul1 = norm1_g[l][None, :] * (1.0 + sc1)
        mul2 = norm2_g[l][None, :] * (1.0 + sc2)
        lat = lambda a: a[:bsz, None, :]
        cx = lambda a: jnp.broadcast_to(a[bsz][None, None, :], (bsz, 1, D_MODEL))

        lam_init = 0.8 - 0.6 * float(np.exp(-0.3 * l))
        lam = jnp.exp(jnp.sum(lam_q1[l] * lam_k1[l])) - jnp.exp(jnp.sum(lam_q2[l] * lam_k2[l])) + lam_init
        scal = jnp.stack([lam, jnp.float32(1.0 - lam_init)]).astype(F32)
        subln = diff_subln_g[l].reshape(1, LANES)
        ln_g = sgu_ln_g[l].reshape(1, MIX_W)
        ln_b = sgu_ln_b[l].reshape(1, MIX_W)
        cb2 = conv_b[l].reshape(1, D_FF)

        bq, bk, bvt, cq, ckd, cvdt = _inproj_call(x, lat(mul1), lat(sh1), w_attn[l], tables, tm=512, rope=True)
        xbq, xbk, xbvt, xcq, xckd, xcvdt = _inproj_call(ctx, cx(mul1), cx(sh1), w_attn[l], ctx_tables,
                                                        tm=n_ctx, rope=False)

        o_b = _diff_call(scal, bq, subln, [(bk, bvt), (xbk, xbvt)], tq=512, kc=256)
        o_c = _win_call(sinks[l], cq, ckd, cvdt, xckd, xcvdt, local=True)
        x_mid = _merge_call(x, lat(mul1), lat(sh1), lat(g1), w_uvg[l], ln_g, ln_b, w_s_b[l], b_s_b[l],
                            o_b, o_c, w_br_b[l], w_out_b[l], tm=512)
        if not last:
            xo_b = _diff_call(scal, xbq, subln, [(xbk, xbvt)], tq=n_ctx, kc=512)
            xo_c = _win_call(sinks[l], xcq, None, None, xckd, xcvdt, local=False)
            ctx_mid = _merge_call(ctx, cx(mul1), cx(sh1), cx(g1), w_uvg[l], ln_g, ln_b, w_s_b[l], b_s_b[l],
                                  xo_b, xo_c, w_br_b[l], w_out_b[l], tm=n_ctx)
        x = _ffn_call(x_mid, lat(mul2), lat(sh2), lat(g2), w_gate_b[l], conv_w[l], cb2, w_up_b[l],
                      w_down_b[l], final_g2, tm=512, final=last)
        if not last:
            ctx = _ffn_call(ctx_mid, cx(mul2), cx(sh2), cx(g2), w_gate_b[l], conv_w[l], cb2, w_up_b[l],
                            w_down_b[l], final_g2, tm=n_ctx, final=False)
    return x
```

```python
import functools

import numpy as np
import jax
import jax.numpy as jnp
from jax import lax
from jax.experimental import pallas as pl
from jax.experimental.pallas import tpu as pltpu

F32 = jnp.float32
BF16 = jnp.bfloat16

D_MODEL = 1024
DEPTH = 4
GRID_W = 64
HEAD_DIM = 64
BLOCK = 128
MIX_W = 512
A_GROUPS = 4
B_HEADS = 4
C_Q_HEADS = 8
D_FF = 2816
ROPE_BASE = 10000.0
EPS = 1e-6
SCALE = HEAD_DIM ** -0.5
NEG_INF = -1e30
LOG2E = float(np.log2(np.e))

LANES = 128
ATTN_COLS = 2304
GATE_COL0 = 3328
VMEM_LIMIT = 56 * 1024 * 1024


def _cparams(n_axes):
    return pltpu.CompilerParams(
        dimension_semantics=("arbitrary",) * n_axes, vmem_limit_bytes=VMEM_LIMIT)


def _const_spec(a):
    return pl.BlockSpec(a.shape, lambda *_: (0,) * a.ndim, pipeline_mode=pl.Buffered(1))


def _rms_mod(x, mul, sh):
    ms = jnp.mean(x * x, axis=-1, keepdims=True)
    return x * lax.rsqrt(ms + EPS) * mul + sh


def _gelu_tanh(x):
    c = np.float32(np.sqrt(2.0 / np.pi))
    return 0.5 * x * (1.0 + jnp.tanh(c * (x + 0.044715 * (x * x * x))))


def _dot(a, b):
    return jnp.dot(a, b, preferred_element_type=F32)


def _dot_nt(a, b):
    return lax.dot_general(a, b, (((1,), (1,)), ((), ())), preferred_element_type=F32)


def _ada_kernel(v_ref, w_ref, b_ref, o_ref):
    v = v_ref[...]
    s = v * jax.nn.sigmoid(v)
    o_ref[...] = _dot(s.astype(BF16), w_ref[...].astype(BF16)) + b_ref[...]


def _ada_call(vpad, w_ada, b_ada):
    rows = vpad.shape[0]
    tn = 1536
    return pl.pallas_call(
        _ada_kernel,
        grid=(DEPTH, 6 * D_MODEL // tn),
        in_specs=[
            pl.BlockSpec((rows, D_MODEL), lambda l, j: (0, 0)),
            pl.BlockSpec((None, D_MODEL, tn), lambda l, j: (l, 0, j)),
            pl.BlockSpec((None, 1, tn), lambda l, j: (l, 0, j)),
        ],
        out_specs=pl.BlockSpec((None, rows, tn), lambda l, j: (l, 0, j)),
        out_shape=jax.ShapeDtypeStruct((DEPTH, rows, 6 * D_MODEL), F32),
        compiler_params=_cparams(2),
        name="ada",
    )(vpad, w_ada, b_ada.reshape(DEPTH, 1, 6 * D_MODEL))


def _inproj_kernel(x_ref, mul_ref, sh_ref, w_ref, cos_ref, sa_ref, sb_ref,
                   bq_ref, bk_ref, bvt_ref, cq_ref, ckd_ref, cvdt_ref, *, rope):
    h = _rms_mod(x_ref[...], mul_ref[...], sh_ref[...]).astype(BF16)
    if rope:
        cos, sa, sb = cos_ref[...], sa_ref[...], sb_ref[...]

    def rot(y):
        if not rope:
            return y
        return y * cos + pltpu.roll(y, LANES - 16, 1) * sa + pltpu.roll(y, 16, 1) * sb

    def dup(y):
        lane = lax.broadcasted_iota(jnp.int32, y.shape, 1)
        sw = pltpu.roll(y, 64, 1)
        return jnp.concatenate([jnp.where(lane < 64, y, sw), jnp.where(lane < 64, sw, y)], axis=1)

    for col0, ref in ((0, bq_ref), (512, bk_ref), (1536, cq_ref)):
        y = _dot(h, w_ref[:, col0:col0 + 512])
        for g in range(4):
            ref[:, g * LANES:(g + 1) * LANES] = rot(y[:, g * LANES:(g + 1) * LANES]).astype(BF16)
    bvt_ref[...] = _dot(h, w_ref[:, 1024:1536]).T.astype(BF16)
    ykv = _dot(h, w_ref[:, 2048:2304])
    ckd_ref[...] = dup(rot(ykv[:, :LANES])).astype(BF16)
    cvdt_ref[...] = dup(ykv[:, LANES:]).T.astype(BF16)


def _inproj_call(x, mul, sh, w_attn, tables, *, tm, rope):
    bsz, length, _ = x.shape
    tok = lambda w: pl.BlockSpec((None, tm, w), lambda b, i: (b, i, 0))
    vec = pl.BlockSpec((None, 1, D_MODEL), lambda b, i: (b, 0, 0))
    tab = pl.BlockSpec((tm, LANES), lambda b, i: (i, 0))
    tok_t = lambda w: pl.BlockSpec((None, w, tm), lambda b, i: (b, 0, i))
    out = lambda w: jax.ShapeDtypeStruct((bsz, length, w), BF16)
    out_t = lambda w: jax.ShapeDtypeStruct((bsz, w, length), BF16)
    return pl.pallas_call(
        functools.partial(_inproj_kernel, rope=rope),
        grid=(bsz, length // tm),
        in_specs=[tok(D_MODEL), vec, vec,
                  _const_spec(w_attn), tab, tab, tab],
        out_specs=[tok(512), tok(512), tok_t(MIX_W), tok(512), tok(256), tok_t(2 * LANES)],
        out_shape=[out(512), out(512), out_t(MIX_W), out(512), out(256), out_t(2 * LANES)],
        compiler_params=_cparams(2),
        name="inproj",
    )(x, mul, sh, w_attn, *tables)


SAFE_EXP2_RANGE = 90.0


def _diff_kernel(sc_ref, q_ref, qall_ref, g_ref, ind_ref, *refs, seg_lens, kc):
    o_ref, p_scr, safe_scr = refs[-3:]
    q = q_ref[...]
    tq = q.shape[0]
    lane = lax.broadcasted_iota(jnp.int32, q.shape, 1)
    zero = jnp.zeros_like(q)
    qq = jnp.concatenate([jnp.where(lane < 64, q, zero), jnp.where(lane >= 64, q, zero)], axis=0)
    lanei = lax.broadcasted_iota(jnp.int32, (1, 2 * tq), 1)

    chunks = []
    row0 = 0
    for si, n_keys in enumerate(seg_lens):
        for st in range(0, n_keys, kc):
            size = min(kc, n_keys - st)
            chunks.append((refs[2 * si], refs[2 * si + 1], st, size, row0))
            row0 += size

    def max_sq_norm(x):
        xf = x.astype(F32)
        return jnp.max(_dot((xf * xf).astype(BF16), ind_ref[...]), axis=0, keepdims=True)

    @pl.when(pl.program_id(2) == 0)
    def _():
        kn = None
        for k_ref, _, st, size, _ in chunks:
            n2 = max_sq_norm(k_ref[st:st + size, :])
            kn = n2 if kn is None else jnp.maximum(kn, n2)
        qn = None
        n_q = qall_ref.shape[0]
        for st in range(0, n_q, 2 * kc):
            n2 = max_sq_norm(qall_ref[st:min(st + 2 * kc, n_q), :])
            qn = n2 if qn is None else jnp.maximum(qn, n2)
        bound2 = jnp.max(kn * qn)
        safe_scr[0] = (bound2 < SAFE_EXP2_RANGE * SAFE_EXP2_RANGE).astype(jnp.int32)

    safe = safe_scr[0] == 1

    def finish(o_t):
        o = o_t.T
        ms = jnp.mean(o * o, axis=-1, keepdims=True)
        o_ref[...] = (o * lax.rsqrt(ms + EPS) * g_ref[...] * sc_ref[1]).astype(BF16)

    @pl.when(safe)
    def _():
        l_all = None
        for k_ref, _, st, size, r0 in chunks:
            p = jnp.exp2(_dot_nt(k_ref[st:st + size, :], qq))
            lc = jnp.sum(p, axis=0, keepdims=True)
            l_all = lc if l_all is None else l_all + lc
            p_scr[r0:r0 + size, :] = p.astype(BF16)
        coef = jnp.where(lanei < tq, 1.0, sc_ref[0]) / l_all
        c1 = coef[:, 0:tq]
        ratio = (coef[:, tq:2 * tq] / c1).astype(BF16)
        acc = None
        for _, vt_ref, st, size, r0 in chunks:
            a_t = p_scr[r0:r0 + size, 0:tq] - p_scr[r0:r0 + size, tq:2 * tq] * ratio
            part = _dot(vt_ref[:, st:st + size], a_t)
            acc = part if acc is None else acc + part
        finish(acc * c1)

    @pl.when(jnp.logical_not(safe))
    def _():
        mcs, lcs = [], []
        for k_ref, _, st, size, r0 in chunks:
            s = _dot_nt(k_ref[st:st + size, :], qq)
            mc = jnp.max(s, axis=0, keepdims=True)
            p = jnp.exp2(s - mc)
            lcs.append(jnp.sum(p, axis=0, keepdims=True))
            mcs.append(mc)
            p_scr[r0:r0 + size, :] = p.astype(BF16)
        m_all = functools.reduce(jnp.maximum, mcs)
        es = [jnp.exp2(mc - m_all) for mc in mcs]
        l_all = None
        for lc, e in zip(lcs, es):
            l_all = lc * e if l_all is None else l_all + lc * e
        coef = jnp.where(lanei < tq, 1.0, sc_ref[0]) / l_all
        acc = None
        for (_, vt_ref, st, size, r0), e in zip(chunks, es):
            f = e * coef
            f1 = f[:, 0:tq].astype(BF16)
            f2 = f[:, tq:2 * tq].astype(BF16)
            a_t = p_scr[r0:r0 + size, 0:tq] * f1 - p_scr[r0:r0 + size, tq:2 * tq] * f2
            part = _dot(vt_ref[:, st:st + size], a_t)
            acc = part if acc is None else acc + part
        finish(acc)


def _diff_call(scal, q, subln_g, segs, *, tq, kc):
    bsz, lq, _ = q.shape
    half = np.arange(LANES) // HEAD_DIM
    ind = jnp.asarray(half[:, None] == half[None, :], BF16)
    in_specs = [pl.BlockSpec(memory_space=pltpu.SMEM),
                pl.BlockSpec((None, tq, LANES), lambda b, h, i: (b, i, h)),
                pl.BlockSpec((None, lq, LANES), lambda b, h, i: (b, 0, h)),
                pl.BlockSpec((1, LANES), lambda b, h, i: (0, 0)),
                _const_spec(ind)]
    args = [scal, q, q, subln_g, ind]
    for k, vt in segs:
        n_keys = k.shape[1]
        in_specs += [pl.BlockSpec((None, n_keys, LANES), lambda b, h, i: (b, 0, h)),
                     pl.BlockSpec((None, LANES, n_keys), lambda b, h, i: (b, h, 0))]
        args += [k, vt]
    return pl.pallas_call(
        functools.partial(_diff_kernel, seg_lens=tuple(k.shape[1] for k, _ in segs), kc=kc),
        grid=(bsz, B_HEADS, lq // tq),
        in_specs=in_specs,
        out_specs=pl.BlockSpec((None, tq, LANES), lambda b, h, i: (b, i, h)),
        out_shape=jax.ShapeDtypeStruct((bsz, lq, MIX_W), BF16),
        scratch_shapes=[pltpu.VMEM((sum(k.shape[1] for k, _ in segs), 2 * tq), BF16),
                        pltpu.SMEM((1,), jnp.int32)],
        compiler_params=_cparams(3),
        name="diff",
    )(*args)


WIN_SUB = 4


def _win_kernel(sink_ref, q_ref, *refs, local):
    o_ref = refs[-1]
    n_sub = q_ref.shape[0] // BLOCK
    kx_ref, vxt_ref = refs[-3], refs[-2]
    n = pl.program_id(1)
    nb = pl.num_programs(1)
    nq = C_Q_HEADS // 2 * BLOCK
    lane = lax.broadcasted_iota(jnp.int32, (BLOCK, LANES), 1)
    headi = lax.broadcasted_iota(jnp.int32, (1, nq), 1) // BLOCK
    if local:
        kp_ref, kc_ref, kn_ref, vpt_ref, vct_ref, vnt_ref = refs[:6]
        key = lax.broadcasted_iota(jnp.int32, (BLOCK, nq), 0)
        qry = lax.broadcasted_iota(jnp.int32, (BLOCK, nq), 1) & (BLOCK - 1)
        tri_prev, tri_next = key >= qry, key <= qry

    probs = [(sub, kvh) for sub in range(n_sub) for kvh in range(2)]
    s_l, sink_l, vt_l = [], [], []
    for sub, kvh in probs:
        ksl = slice(kvh * LANES, (kvh + 1) * LANES)
        rows = slice(sub * BLOCK, (sub + 1) * BLOCK)
        parts = []
        for g in range(4):
            grp = kvh * 2 + g // 2
            qg = q_ref[rows, grp * LANES:(grp + 1) * LANES]
            parts.append(jnp.where((lane < 64) if g % 2 == 0 else (lane >= 64), qg, jnp.zeros_like(qg)))
        qs = jnp.concatenate(parts, axis=0)
        sink = jnp.zeros((1, nq), F32)
        for g in range(4):
            sink = jnp.where(headi == g, sink_ref[kvh * 4 + g] * LOG2E, sink)
        if local:
            kb, vb = [], []
            for j in (sub - 1, sub, sub + 1):
                if j < 0:
                    kb.append(kp_ref[:, ksl]), vb.append(vpt_ref[ksl, :])
                elif j >= n_sub:
                    kb.append(kn_ref[:, ksl]), vb.append(vnt_ref[ksl, :])
                else:
                    kb.append(kc_ref[j * BLOCK:(j + 1) * BLOCK, ksl])
                    vb.append(vct_ref[ksl, j * BLOCK:(j + 1) * BLOCK])
            k_all = jnp.concatenate(kb + [kx_ref[:, ksl]], axis=0)
            vt_all = jnp.concatenate(vb + [vxt_ref[ksl, :]], axis=1)
        else:
            k_all, vt_all = kx_ref[:, ksl], vxt_ref[ksl, :]
        s = _dot_nt(k_all, qs)
        if local:
            valid_prev = tri_prev if sub > 0 else tri_prev & (n > 0)
            valid_next = tri_next if sub < n_sub - 1 else tri_next & (n < nb - 1)
            s = jnp.concatenate([jnp.where(valid_prev, s[0:BLOCK], NEG_INF), s[BLOCK:2 * BLOCK],
                                 jnp.where(valid_next, s[2 * BLOCK:3 * BLOCK], NEG_INF), s[3 * BLOCK:]], axis=0)
        s_l.append(s)
        sink_l.append(sink)
        vt_l.append(vt_all)
    m_l = [jnp.maximum(jnp.max(s, axis=0, keepdims=True), sink) for s, sink in zip(s_l, sink_l)]
    p_l = [jnp.exp2(s - m) for s, m in zip(s_l, m_l)]
    l_l = [jnp.sum(p, axis=0, keepdims=True) + jnp.exp2(sink - m) for p, sink, m in zip(p_l, sink_l, m_l)]
    ot_l = [_dot(vt, p.astype(BF16)) / l for vt, p, l in zip(vt_l, p_l, l_l)]
    for (sub, kvh), o_t in zip(probs, ot_l):
        o = o_t.T
        for j in range(2):
            grp = kvh * 2 + j
            o_ref[sub * BLOCK:(sub + 1) * BLOCK, grp * LANES:(grp + 1) * LANES] = jnp.where(
                lane < 64, o[2 * j * BLOCK:(2 * j + 1) * BLOCK], o[(2 * j + 1) * BLOCK:(2 * j + 2) * BLOCK]
            ).astype(BF16)


def _win_call(sinks, q, kd, vdt, kxd, vxdt, *, local):
    bsz, lq, _ = q.shape
    nb = lq // BLOCK
    n_sub = min(WIN_SUB, nb)
    tq = n_sub * BLOCK
    in_specs = [pl.BlockSpec(memory_space=pltpu.SMEM), pl.BlockSpec((None, tq, MIX_W), lambda b, n: (b, n, 0))]
    args = [sinks, q]
    if local:
        prv = lambda n: jnp.maximum(n * n_sub - 1, 0)
        nxt = lambda n: jnp.minimum((n + 1) * n_sub, nb - 1)
        in_specs += [pl.BlockSpec((None, BLOCK, 2 * LANES), lambda b, n: (b, prv(n), 0)),
                     pl.BlockSpec((None, tq, 2 * LANES), lambda b, n: (b, n, 0)),
                     pl.BlockSpec((None, BLOCK, 2 * LANES), lambda b, n: (b, nxt(n), 0)),
                     pl.BlockSpec((None, 2 * LANES, BLOCK), lambda b, n: (b, 0, prv(n))),
                     pl.BlockSpec((None, 2 * LANES, tq), lambda b, n: (b, 0, n)),
                     pl.BlockSpec((None, 2 * LANES, BLOCK), lambda b, n: (b, 0, nxt(n)))]
        args += [kd] * 3 + [vdt] * 3
    n_ctx = kxd.shape[1]
    in_specs += [pl.BlockSpec((None, n_ctx, 2 * LANES), lambda b, n: (b, 0, 0)),
                 pl.BlockSpec((None, 2 * LANES, n_ctx), lambda b, n: (b, 0, 0))]
    args += [kxd, vxdt]
    return pl.pallas_call(
        functools.partial(_win_kernel, local=local),
        grid=(bsz, lq // tq),
        in_specs=in_specs,
        out_specs=pl.BlockSpec((None, tq, MIX_W), lambda b, n: (b, n, 0)),
        out_shape=jax.ShapeDtypeStruct((bsz, lq, MIX_W), BF16),
        compiler_params=_cparams(2),
        name="win",
    )(*args)


def _merge_kernel(x_ref, mul_ref, sh_ref, gate_ref, wuvg_ref, lng_ref, lnb_ref, ws_ref, bs_ref,
                  ob_ref, oc_ref, wbr_ref, wout_ref, o_ref):
    x = x_ref[...]
    tm = x.shape[0]
    h = _rms_mod(x, mul_ref[...], sh_ref[...]).astype(BF16)
    u = _gelu_tanh(_dot(h, wuvg_ref[:, 0:MIX_W]))
    v = _gelu_tanh(_dot(h, wuvg_ref[:, MIX_W:2 * MIX_W]))
    mu = jnp.mean(v, axis=-1, keepdims=True)
    vc = v - mu
    var = jnp.mean(vc * vc, axis=-1, keepdims=True)
    vn = (vc * lax.rsqrt(var + EPS) * lng_ref[...] + lnb_ref[...]).astype(BF16)
    n_chunks = tm // BLOCK
    mixed = []
    for g in range(A_GROUPS):
        gs = slice(g * LANES, (g + 1) * LANES)
        rhs = jnp.concatenate([vn[c * BLOCK:(c + 1) * BLOCK, gs] for c in range(n_chunks)], axis=1)
        mixed.append(_dot(ws_ref[g], rhs))
    o_a = (u * jnp.concatenate(
        [jnp.concatenate([mixed[g][:, c * LANES:(c + 1) * LANES] + bs_ref[g] for g in range(A_GROUPS)], axis=1)
         for c in range(n_chunks)], axis=0)).astype(BF16)
    y = None
    for i, o_i in enumerate((o_a, ob_ref[...], oc_ref[...])):
        col0 = 2 * MIX_W + i * D_MODEL
        gate = jax.nn.sigmoid(_dot(h, wuvg_ref[:, col0:col0 + D_MODEL]))
        t = gate * _dot(o_i, wbr_ref[i])
        y = t if y is None else y + t
    o_ref[...] = x + gate_ref[...] * _dot(y.astype(BF16), wout_ref[...])


def _merge_call(x, mul, sh, gate, w_uvg, ln_g, ln_b, w_s, b_s, o_b, o_c, w_br, w_out, *, tm):
    bsz, length, _ = x.shape
    tok = lambda w: pl.BlockSpec((None, tm, w), lambda b, i: (b, i, 0))
    vec = pl.BlockSpec((None, 1, D_MODEL), lambda b, i: (b, 0, 0))
    full = _const_spec
    return pl.pallas_call(
        _merge_kernel,
        grid=(bsz, length // tm),
        in_specs=[tok(D_MODEL), vec, vec, vec, full(w_uvg), full(ln_g), full(ln_b), full(w_s), full(b_s),
                  tok(MIX_W), tok(MIX_W), full(w_br), full(w_out)],
        out_specs=tok(D_MODEL),
        out_shape=jax.ShapeDtypeStruct(x.shape, F32),
        compiler_params=_cparams(2),
        name="merge",
    )(x, mul, sh, gate, w_uvg, ln_g, ln_b, w_s, b_s, o_b, o_c, w_br, w_out)


HALO = 8
MXU_TILE = 256
FF_EDGES = (0, 4 * MXU_TILE, 8 * MXU_TILE, D_FF)


def _ffn_kernel(x_ref, xp_ref, xn_ref, mul_ref, sh_ref, gate_ref, wg_ref, cw_ref, cb_ref, wu_ref, wd_ref,
                fg_ref, o_ref, *, final):
    i = pl.program_id(1)
    nt = pl.num_programs(1)
    x = x_ref[...]
    tm = x.shape[0]
    xe = jnp.concatenate([xp_ref[...], x, xn_ref[...]], axis=0)
    he = _rms_mod(xe, mul_ref[...], sh_ref[...]).astype(BF16)
    hm = he[HALO:HALO + tm]
    rowe = lax.broadcasted_iota(jnp.int32, (tm + 2 * HALO, 1), 0)
    keep = ((rowe >= HALO) | (i > 0)) & ((rowe < HALO + tm) | (i < nt - 1))
    acc = None
    for c0, c1 in zip(FF_EDGES[:-1], FF_EDGES[1:]):
        cs = slice(c0, c1)
        a = jnp.where(keep, _dot(he, wg_ref[:, cs]), 0.0)
        a_prev = pltpu.roll(a, 1, 0)[HALO:HALO + tm]
        a_next = pltpu.roll(a, tm + 2 * HALO - 1, 0)[HALO:HALO + tm]
        a = (a_prev * cw_ref[0:1, cs] + a[HALO:HALO + tm] * cw_ref[1:2, cs] + a_next * cw_ref[2:3, cs]
             + cb_ref[:, cs])
        z = (a * jax.nn.sigmoid(a) * _dot(hm, wu_ref[:, cs])).astype(BF16)
        t = _dot(z, wd_ref[cs, :])
        acc = t if acc is None else acc + t
    out = x + gate_ref[...] * acc
    if final:
        ms = jnp.mean(out * out, axis=-1, keepdims=True)
        out = out * lax.rsqrt(ms + EPS) * fg_ref[...]
    o_ref[...] = out


def _ffn_call(x, mul, sh, gate, w_gate, conv_w, conv_b, w_up, w_down, final_g, *, tm, final):
    bsz, length, _ = x.shape
    per = tm // HALO
    last = length // HALO - 1
    tok = pl.BlockSpec((None, tm, D_MODEL), lambda b, i: (b, i, 0))
    prv = pl.BlockSpec((None, HALO, D_MODEL), lambda b, i: (b, jnp.maximum(i * per - 1, 0), 0))
    nxt = pl.BlockSpec((None, HALO, D_MODEL), lambda b, i: (b, jnp.minimum((i + 1) * per, last), 0))
    vec = pl.BlockSpec((None, 1, D_MODEL), lambda b, i: (b, 0, 0))
    full = _const_spec
    return pl.pallas_call(
        functools.partial(_ffn_kernel, final=final),
        grid=(bsz, length // tm),
        in_specs=[tok, prv, nxt, vec, vec, vec, full(w_gate), full(conv_w), full(conv_b), full(w_up),
                  full(w_down), full(final_g)],
        out_specs=tok,
        out_shape=jax.ShapeDtypeStruct(x.shape, F32),
        compiler_params=_cparams(2),
        name="ffn",
    )(x, x, x, mul, sh, gate, w_gate, conv_w, conv_b, w_up, w_down, final_g)


def _rope_tables(length):
    pos = jnp.arange(length)
    rows = (pos // GRID_W).astype(F32)
    cols = (pos % GRID_W).astype(F32)
    half = HEAD_DIM // 2
    inv = ROPE_BASE ** (-jnp.arange(0, half, 2, dtype=F32) / half)
    ang_r = rows[:, None] * inv[None, :]
    ang_c = cols[:, None] * inv[None, :]
    zero = jnp.zeros_like(ang_r)
    cos = jnp.concatenate([jnp.cos(ang_r)] * 2 + [jnp.cos(ang_c)] * 2, axis=1)
    sa = jnp.concatenate([-jnp.sin(ang_r), zero, -jnp.sin(ang_c), zero], axis=1)
    sb = jnp.concatenate([zero, jnp.sin(ang_r), zero, jnp.sin(ang_c)], axis=1)
    return tuple(jnp.tile(t, (1, LANES // HEAD_DIM)) for t in (cos, sa, sb))


def kernel(x, c, ctx, c_ctx, w_ada, b_ada, norm1_g, w_in, sgu_ln_g, sgu_ln_b, w_s, b_s, lam_q1, lam_k1,
           lam_q2, lam_k2, diff_subln_g, sinks, w_branch, w_out, norm2_g, w_gate, conv_w, conv_b, w_up,
           w_down, final_g):
    bsz, length, _ = x.shape
    n_ctx = ctx.shape[1]
    tables = _rope_tables(length)
    ctx_tables = tuple(t[:n_ctx] for t in tables)

    rows = -(-(bsz + 1) // 8) * 8
    vpad = jnp.zeros((rows, D_MODEL), F32).at[:bsz].set(c).at[bsz].set(c_ctx)
    ada = _ada_call(vpad, w_ada, b_ada)

    qscale = jnp.ones((ATTN_COLS,), F32).at[0:512].set(SCALE * LOG2E).at[1536:2048].set(SCALE * LOG2E)
    w_attn = (w_in[:, :, 2 * MIX_W:2 * MIX_W + ATTN_COLS] * qscale).astype(BF16)
    w_uvg = jnp.concatenate([w_in[:, :, :2 * MIX_W], w_in[:, :, GATE_COL0:]], axis=2).astype(BF16)
    w_s_b = w_s.astype(BF16)
    b_s_b = jnp.broadcast_to(b_s[..., None], b_s.shape + (LANES,))
    w_br_b = w_branch.astype(BF16)
    w_out_b = w_out.astype(BF16)
    w_gate_b = w_gate.astype(BF16)
    w_up_b = w_up.astype(BF16)
    w_down_b = w_down.astype(BF16)
    final_g2 = final_g.reshape(1, D_MODEL)

    for l in range(DEPTH):
        last = l == DEPTH - 1
        m = ada[l]
        sh1, sc1, g1, sh2, sc2, g2 = [m[:, j * D_MODEL:(j + 1) * D_MODEL] for j in range(6)]
        mul1 = norm1_g[l][None, :] * (1.0 + sc1)
        mul2 = norm2_g[l][None, :] * (1.0 + sc2)
        lat = lambda a: a[:bsz, None, :]
        cx = lambda a: jnp.broadcast_to(a[bsz][None, None, :], (bsz, 1, D_MODEL))

        lam_init = 0.8 - 0.6 * float(np.exp(-0.3 * l))
        lam = jnp.exp(jnp.sum(lam_q1[l] * lam_k1[l])) - jnp.exp(jnp.sum(lam_q2[l] * lam_k2[l])) + lam_init
        scal = jnp.stack([lam, jnp.float32(1.0 - lam_init)]).astype(F32)
        subln = diff_subln_g[l].reshape(1, LANES)
        ln_g = sgu_ln_g[l].reshape(1, MIX_W)
        ln_b = sgu_ln_b[l].reshape(1, MIX_W)
        cb2 = conv_b[l].reshape(1, D_FF)

        bq, bk, bvt, cq, ckd, cvdt = _inproj_call(x, lat(mul1), lat(sh1), w_attn[l], tables, tm=512, rope=True)
        xbq, xbk, xbvt, xcq, xckd, xcvdt = _inproj_call(ctx, cx(mul1), cx(sh1), w_attn[l], ctx_tables,
                                                        tm=n_ctx, rope=False)

        o_b = _diff_call(scal, bq, subln, [(bk, bvt), (xbk, xbvt)], tq=512, kc=256)
        o_c = _win_call(sinks[l], cq, ckd, cvdt, xckd, xcvdt, local=True)
        x_mid = _merge_call(x, lat(mul1), lat(sh1), lat(g1), w_uvg[l], ln_g, ln_b, w_s_b[l], b_s_b[l],
                            o_b, o_c, w_br_b[l], w_out_b[l], tm=512)
        if not last:
            xo_b = _diff_call(scal, xbq, subln, [(xbk, xbvt)], tq=n_ctx, kc=512)
            xo_c = _win_call(sinks[l], xcq, None, None, xckd, xcvdt, local=False)
            ctx_mid = _merge_call(ctx, cx(mul1), cx(sh1), cx(g1), w_uvg[l], ln_g, ln_b, w_s_b[l], b_s_b[l],
                                  xo_b, xo_c, w_br_b[l], w_out_b[l], tm=n_ctx)
        x = _ffn_call(x_mid, lat(mul2), lat(sh2), lat(g2), w_gate_b[l], conv_w[l], cb2, w_up_b[l],
                      w_down_b[l], final_g2, tm=512, final=last)
        if not last:
            ctx = _ffn_call(ctx_mid, cx(mul2), cx(sh2), cx(g2), w_gate_b[l], conv_w[l], cb2, w_up_b[l],
                            w_down_b[l], final_g2, tm=n_ctx, final=False)
    return x
```

```python
import functools

import numpy as np
import jax
import jax.numpy as jnp
from jax import lax
from jax.experimental import pallas as pl
from jax.experimental.pallas import tpu as pltpu

F32 = jnp.float32
BF16 = jnp.bfloat16

D_MODEL = 1024
DEPTH = 4
GRID_W = 64
HEAD_DIM = 64
BLOCK = 128
MIX_W = 512
A_GROUPS = 4
B_HEADS = 4
C_Q_HEADS = 8
D_FF = 2816
ROPE_BASE = 10000.0
EPS = 1e-6
SCALE = HEAD_DIM ** -0.5
NEG_INF = -1e30
LOG2E = float(np.log2(np.e))

LANES = 128
ATTN_COLS = 2304
GATE_COL0 = 3328
VMEM_LIMIT = 56 * 1024 * 1024


def _cparams(n_axes):
    return pltpu.CompilerParams(
        dimension_semantics=("arbitrary",) * n_axes, vmem_limit_bytes=VMEM_LIMIT)


def _const_spec(a):
    return pl.BlockSpec(a.shape, lambda *_: (0,) * a.ndim, pipeline_mode=pl.Buffered(1))


def _rms_mod(x, mul, sh):
    ms = jnp.mean(x * x, axis=-1, keepdims=True)
    return x * lax.rsqrt(ms + EPS) * mul + sh


def _gelu_tanh(x):
    c = np.float32(np.sqrt(2.0 / np.pi))
    return 0.5 * x * (1.0 + jnp.tanh(c * (x + 0.044715 * (x * x * x))))


def _dot(a, b):
    return jnp.dot(a, b, preferred_element_type=F32)


def _dot_nt(a, b):
    return lax.dot_general(a, b, (((1,), (1,)), ((), ())), preferred_element_type=F32)


def _ada_kernel(v_ref, w_ref, b_ref, o_ref):
    v = v_ref[...]
    s = v * jax.nn.sigmoid(v)
    o_ref[...] = _dot(s.astype(BF16), w_ref[...].astype(BF16)) + b_ref[...]


def _ada_call(vpad, w_ada, b_ada):
    rows = vpad.shape[0]
    tn = 1536
    return pl.pallas_call(
        _ada_kernel,
        grid=(DEPTH, 6 * D_MODEL // tn),
        in_specs=[
            pl.BlockSpec((rows, D_MODEL), lambda l, j: (0, 0)),
            pl.BlockSpec((None, D_MODEL, tn), lambda l, j: (l, 0, j)),
            pl.BlockSpec((None, 1, tn), lambda l, j: (l, 0, j)),
        ],
        out_specs=pl.BlockSpec((None, rows, tn), lambda l, j: (l, 0, j)),
        out_shape=jax.ShapeDtypeStruct((DEPTH, rows, 6 * D_MODEL), F32),
        compiler_params=_cparams(2),
        name="ada",
    )(vpad, w_ada, b_ada.reshape(DEPTH, 1, 6 * D_MODEL))


def _inproj_kernel(x_ref, mul_ref, sh_ref, w_ref, cos_ref, sa_ref, sb_ref,
                   bq_ref, bk_ref, bvt_ref, cq_ref, ckd_ref, cvdt_ref, *, rope):
    h = _rms_mod(x_ref[...], mul_ref[...], sh_ref[...]).astype(BF16)
    if rope:
        cos, sa, sb = cos_ref[...], sa_ref[...], sb_ref[...]

    def rot(y):
        if not rope:
            return y
        return y * cos + pltpu.roll(y, LANES - 16, 1) * sa + pltpu.roll(y, 16, 1) * sb

    def dup(y):
        lane = lax.broadcasted_iota(jnp.int32, y.shape, 1)
        sw = pltpu.roll(y, 64, 1)
        return jnp.concatenate([jnp.where(lane < 64, y, sw), jnp.where(lane < 64, sw, y)], axis=1)

    for col0, ref in ((0, bq_ref), (512, bk_ref), (1536, cq_ref)):
        y = _dot(h, w_ref[:, col0:col0 + 512])
        for g in range(4):
            ref[:, g * LANES:(g + 1) * LANES] = rot(y[:, g * LANES:(g + 1) * LANES]).astype(BF16)
    bvt_ref[...] = _dot(h, w_ref[:, 1024:1536]).T.astype(BF16)
    ykv = _dot(h, w_ref[:, 2048:2304])
    ckd_ref[...] = dup(rot(ykv[:, :LANES])).astype(BF16)
    cvdt_ref[...] = dup(ykv[:, LANES:]).T.astype(BF16)


def _inproj_call(x, mul, sh, w_attn, tables, *, tm, rope):
    bsz, length, _ = x.shape
    tok = lambda w: pl.BlockSpec((None, tm, w), lambda b, i: (b, i, 0))
    vec = pl.BlockSpec((None, 1, D_MODEL), lambda b, i: (b, 0, 0))
    tab = pl.BlockSpec((tm, LANES), lambda b, i: (i, 0))
    tok_t = lambda w: pl.BlockSpec((None, w, tm), lambda b, i: (b, 0, i))
    out = lambda w: jax.ShapeDtypeStruct((bsz, length, w), BF16)
    out_t = lambda w: jax.ShapeDtypeStruct((bsz, w, length), BF16)
    return pl.pallas_call(
        functools.partial(_inproj_kernel, rope=rope),
        grid=(bsz, length // tm),
        in_specs=[tok(D_MODEL), vec, vec,
                  _const_spec(w_attn), tab, tab, tab],
        out_specs=[tok(512), tok(512), tok_t(MIX_W), tok(512), tok(256), tok_t(2 * LANES)],
        out_shape=[out(512), out(512), out_t(MIX_W), out(512), out(256), out_t(2 * LANES)],
        compiler_params=_cparams(2),
        name="inproj",
    )(x, mul, sh, w_attn, *tables)


SAFE_EXP2_RANGE = 90.0


def _diff_kernel(sc_ref, q_ref, qall_ref, g_ref, ind_ref, *refs, seg_lens, kc, nq):
    o_ref, p_scr, f_scr, safe_scr = refs[-4:]
    t = pl.program_id(0)
    tq = q_ref.shape[0]
    lanei = lax.broadcasted_iota(jnp.int32, (1, 2 * tq), 1)

    @pl.when(t == 0)
    def _():
        p_scr[...] = jnp.zeros(p_scr.shape, BF16)
        f_scr[...] = jnp.zeros(f_scr.shape, F32)

    chunks = []
    row0 = 0
    for si, n_keys in enumerate(seg_lens):
        for st in range(0, n_keys, kc):
            size = min(kc, n_keys - st)
            chunks.append((refs[2 * si], refs[2 * si + 1], st, size, row0))
            row0 += size

    def max_sq_norm(x):
        xf = x.astype(F32)
        return jnp.max(_dot((xf * xf).astype(BF16), ind_ref[...]), axis=0, keepdims=True)

    @pl.when(t % nq == 0)
    def _():
        kn = None
        for k_ref, _, st, size, _ in chunks:
            n2 = max_sq_norm(k_ref[st:st + size, :])
            kn = n2 if kn is None else jnp.maximum(kn, n2)
        qn = None
        n_q = qall_ref.shape[0]
        for st in range(0, n_q, 2 * kc):
            n2 = max_sq_norm(qall_ref[st:min(st + 2 * kc, n_q), :])
            qn = n2 if qn is None else jnp.maximum(qn, n2)
        bound2 = jnp.max(kn * qn)
        safe_scr[0] = (bound2 < SAFE_EXP2_RANGE * SAFE_EXP2_RANGE).astype(jnp.int32)

    safe = safe_scr[0] == 1

    def step(max_free):
        q = q_ref[...]
        lane = lax.broadcasted_iota(jnp.int32, q.shape, 1)
        zero = jnp.zeros_like(q)
        qq = jnp.concatenate([jnp.where(lane < 64, q, zero), jnp.where(lane >= 64, q, zero)], axis=0)
        f_prev = f_scr[...]
        mcs, lcs = [], []
        acc = None
        for c, (k_ref, vt_ref, st, size, r0) in enumerate(chunks):
            f1 = f_prev[c:c + 1, 0:tq].astype(BF16)
            f2 = f_prev[c:c + 1, tq:2 * tq].astype(BF16)
            a_t = p_scr[r0:r0 + size, 0:tq] * f1 - p_scr[r0:r0 + size, tq:2 * tq] * f2
            part = _dot(vt_ref[:, st:st + size], a_t)
            acc = part if acc is None else acc + part
            s = _dot_nt(k_ref[st:st + size, :], qq)
            if max_free:
                p = jnp.exp2(s)
            else:
                mc = jnp.max(s, axis=0, keepdims=True)
                p = jnp.exp2(s - mc)
                mcs.append(mc)
            lcs.append(jnp.sum(p, axis=0, keepdims=True))
            p_scr[r0:r0 + size, :] = p.astype(BF16)

        if max_free:
            es = [None] * len(chunks)
            l_all = functools.reduce(lambda a, b: a + b, lcs)
        else:
            m_all = functools.reduce(jnp.maximum, mcs)
            es = [jnp.exp2(mc - m_all) for mc in mcs]
            l_all = functools.reduce(lambda a, b: a + b, [lc * e for lc, e in zip(lcs, es)])
        coef = jnp.where(lanei < tq, 1.0, sc_ref[0]) / l_all
        for c, e in enumerate(es):
            f_scr[c:c + 1, :] = coef if e is None else e * coef

        o = acc.T
        ms = jnp.mean(o * o, axis=-1, keepdims=True)
        o_ref[...] = (o * lax.rsqrt(ms + EPS) * g_ref[...] * sc_ref[1]).astype(BF16)

    pl.when(safe)(lambda: step(True))
    pl.when(jnp.logical_not(safe))(lambda: step(False))


def _diff_call(scal, q, subln_g, segs, *, tq, kc):
    bsz, lq, _ = q.shape
    nq = lq // tq
    n_blocks = bsz * B_HEADS * nq
    half = np.arange(LANES) // HEAD_DIM
    ind = jnp.asarray(half[:, None] == half[None, :], BF16)

    def block(t):
        b, r = t // (B_HEADS * nq), t % (B_HEADS * nq)
        return b, r // nq, r % nq

    cur = lambda t: block(jnp.minimum(t, n_blocks - 1))
    prev = lambda t: block(jnp.maximum(t - 1, 0))

    def q_map(t):
        b, h, i = cur(t)
        return b, i, h

    def kq_map(t):
        b, h, _ = cur(t)
        return b, 0, h

    def vt_map(t):
        b, h, _ = prev(t)
        return b, h, 0

    def o_map(t):
        b, h, i = prev(t)
        return b, i, h

    in_specs = [pl.BlockSpec(memory_space=pltpu.SMEM),
                pl.BlockSpec((None, tq, LANES), q_map),
                pl.BlockSpec((None, lq, LANES), kq_map),
                pl.BlockSpec((1, LANES), lambda t: (0, 0)),
                _const_spec(ind)]
    args = [scal, q, q, subln_g, ind]
    for k, vt in segs:
        n_keys = k.shape[1]
        in_specs += [pl.BlockSpec((None, n_keys, LANES), kq_map),
                     pl.BlockSpec((None, LANES, n_keys), vt_map)]
        args += [k, vt]
    seg_lens = tuple(k.shape[1] for k, _ in segs)
    n_chunks = sum(-(-n // kc) for n in seg_lens)
    return pl.pallas_call(
        functools.partial(_diff_kernel, seg_lens=seg_lens, kc=kc, nq=nq),
        grid=(n_blocks + 1,),
        in_specs=in_specs,
        out_specs=pl.BlockSpec((None, tq, LANES), o_map),
        out_shape=jax.ShapeDtypeStruct((bsz, lq, MIX_W), BF16),
        scratch_shapes=[pltpu.VMEM((sum(seg_lens), 2 * tq), BF16),
                        pltpu.VMEM((-(-n_chunks // 8) * 8, 2 * tq), F32),
                        pltpu.SMEM((1,), jnp.int32)],
        compiler_params=_cparams(1),
        name="diff",
    )(*args)


WIN_SUB = 4


def _win_kernel(sink_ref, q_ref, *refs, local):
    o_ref = refs[-1]
    n_sub = q_ref.shape[0] // BLOCK
    kx_ref, vxt_ref = refs[-3], refs[-2]
    n = pl.program_id(1)
    nb = pl.num_programs(1)
    nq = C_Q_HEADS // 2 * BLOCK
    lane = lax.broadcasted_iota(jnp.int32, (BLOCK, LANES), 1)
    headi = lax.broadcasted_iota(jnp.int32, (1, nq), 1) // BLOCK
    if local:
        kp_ref, kc_ref, kn_ref, vpt_ref, vct_ref, vnt_ref = refs[:6]
        key = lax.broadcasted_iota(jnp.int32, (BLOCK, nq), 0)
        qry = lax.broadcasted_iota(jnp.int32, (BLOCK, nq), 1) & (BLOCK - 1)
        tri_prev, tri_next = key >= qry, key <= qry

    probs = [(sub, kvh) for sub in range(n_sub) for kvh in range(2)]
    s_l, sink_l, vt_l = [], [], []
    for sub, kvh in probs:
        ksl = slice(kvh * LANES, (kvh + 1) * LANES)
        rows = slice(sub * BLOCK, (sub + 1) * BLOCK)
        parts = []
        for g in range(4):
            grp = kvh * 2 + g // 2
            qg = q_ref[rows, grp * LANES:(grp + 1) * LANES]
            parts.append(jnp.where((lane < 64) if g % 2 == 0 else (lane >= 64), qg, jnp.zeros_like(qg)))
        qs = jnp.concatenate(parts, axis=0)
        sink = jnp.zeros((1, nq), F32)
        for g in range(4):
            sink = jnp.where(headi == g, sink_ref[kvh * 4 + g] * LOG2E, sink)
        if local:
            kb, vb = [], []
            for j in (sub - 1, sub, sub + 1):
                if j < 0:
                    kb.append(kp_ref[:, ksl]), vb.append(vpt_ref[ksl, :])
                elif j >= n_sub:
                    kb.append(kn_ref[:, ksl]), vb.append(vnt_ref[ksl, :])
                else:
                    kb.append(kc_ref[j * BLOCK:(j + 1) * BLOCK, ksl])
                    vb.append(vct_ref[ksl, j * BLOCK:(j + 1) * BLOCK])
            k_all = jnp.concatenate(kb + [kx_ref[:, ksl]], axis=0)
            vt_all = jnp.concatenate(vb + [vxt_ref[ksl, :]], axis=1)
        else:
            k_all, vt_all = kx_ref[:, ksl], vxt_ref[ksl, :]
        s = _dot_nt(k_all, qs)
        if local:
            valid_prev = tri_prev if sub > 0 else tri_prev & (n > 0)
            valid_next = tri_next if sub < n_sub - 1 else tri_next & (n < nb - 1)
            s = jnp.concatenate([jnp.where(valid_prev, s[0:BLOCK], NEG_INF), s[BLOCK:2 * BLOCK],
                                 jnp.where(valid_next, s[2 * BLOCK:3 * BLOCK], NEG_INF), s[3 * BLOCK:]], axis=0)
        s_l.append(s)
        sink_l.append(sink)
        vt_l.append(vt_all)
    m_l = [jnp.maximum(jnp.max(s, axis=0, keepdims=True), sink) for s, sink in zip(s_l, sink_l)]
    p_l = [jnp.exp2(s - m) for s, m in zip(s_l, m_l)]
    l_l = [jnp.sum(p, axis=0, keepdims=True) + jnp.exp2(sink - m) for p, sink, m in zip(p_l, sink_l, m_l)]
    ot_l = [_dot(vt, p.astype(BF16)) / l for vt, p, l in zip(vt_l, p_l, l_l)]
    for (sub, kvh), o_t in zip(probs, ot_l):
        o = o_t.T
        for j in range(2):
            grp = kvh * 2 + j
            o_ref[sub * BLOCK:(sub + 1) * BLOCK, grp * LANES:(grp + 1) * LANES] = jnp.where(
                lane < 64, o[2 * j * BLOCK:(2 * j + 1) * BLOCK], o[(2 * j + 1) * BLOCK:(2 * j + 2) * BLOCK]
            ).astype(BF16)


def _win_call(sinks, q, kd, vdt, kxd, vxdt, *, local):
    bsz, lq, _ = q.shape
    nb = lq // BLOCK
    n_sub = min(WIN_SUB, nb)
    tq = n_sub * BLOCK
    in_specs = [pl.BlockSpec(memory_space=pltpu.SMEM), pl.BlockSpec((None, tq, MIX_W), lambda b, n: (b, n, 0))]
    args = [sinks, q]
    if local:
        prv = lambda n: jnp.maximum(n * n_sub - 1, 0)
        nxt = lambda n: jnp.minimum((n + 1) * n_sub, nb - 1)
        in_specs += [pl.BlockSpec((None, BLOCK, 2 * LANES), lambda b, n: (b, prv(n), 0)),
                     pl.BlockSpec((None, tq, 2 * LANES), lambda b, n: (b, n, 0)),
                     pl.BlockSpec((None, BLOCK, 2 * LANES), lambda b, n: (b, nxt(n), 0)),
                     pl.BlockSpec((None, 2 * LANES, BLOCK), lambda b, n: (b, 0, prv(n))),
                     pl.BlockSpec((None, 2 * LANES, tq), lambda b, n: (b, 0, n)),
                     pl.BlockSpec((None, 2 * LANES, BLOCK), lambda b, n: (b, 0, nxt(n)))]
        args += [kd] * 3 + [vdt] * 3
    n_ctx = kxd.shape[1]
    in_specs += [pl.BlockSpec((None, n_ctx, 2 * LANES), lambda b, n: (b, 0, 0)),
                 pl.BlockSpec((None, 2 * LANES, n_ctx), lambda b, n: (b, 0, 0))]
    args += [kxd, vxdt]
    return pl.pallas_call(
        functools.partial(_win_kernel, local=local),
        grid=(bsz, lq // tq),
        in_specs=in_specs,
        out_specs=pl.BlockSpec((None, tq, MIX_W), lambda b, n: (b, n, 0)),
        out_shape=jax.ShapeDtypeStruct((bsz, lq, MIX_W), BF16),
        compiler_params=_cparams(2),
        name="win",
    )(*args)


def _merge_kernel(x_ref, mul_ref, sh_ref, gate_ref, wuvg_ref, lng_ref, lnb_ref, ws_ref, bs_ref,
                  ob_ref, oc_ref, wbr_ref, wout_ref, o_ref):
    x = x_ref[...]
    tm = x.shape[0]
    h = _rms_mod(x, mul_ref[...], sh_ref[...]).astype(BF16)
    u = _gelu_tanh(_dot(h, wuvg_ref[:, 0:MIX_W]))
    v = _gelu_tanh(_dot(h, wuvg_ref[:, MIX_W:2 * MIX_W]))
    mu = jnp.mean(v, axis=-1, keepdims=True)
    vc = v - mu
    var = jnp.mean(vc * vc, axis=-1, keepdims=True)
    vn = (vc * lax.rsqrt(var + EPS) * lng_ref[...] + lnb_ref[...]).astype(BF16)
    n_chunks = tm // BLOCK
    mixed = []
    for g in range(A_GROUPS):
        gs = slice(g * LANES, (g + 1) * LANES)
        rhs = jnp.concatenate([vn[c * BLOCK:(c + 1) * BLOCK, gs] for c in range(n_chunks)], axis=1)
        mixed.append(_dot(ws_ref[g], rhs))
    o_a = (u * jnp.concatenate(
        [jnp.concatenate([mixed[g][:, c * LANES:(c + 1) * LANES] + bs_ref[g] for g in range(A_GROUPS)], axis=1)
         for c in range(n_chunks)], axis=0)).astype(BF16)
    y = None
    for i, o_i in enumerate((o_a, ob_ref[...], oc_ref[...])):
        col0 = 2 * MIX_W + i * D_MODEL
        gate = jax.nn.sigmoid(_dot(h, wuvg_ref[:, col0:col0 + D_MODEL]))
        t = gate * _dot(o_i, wbr_ref[i])
        y = t if y is None else y + t
    o_ref[...] = x + gate_ref[...] * _dot(y.astype(BF16), wout_ref[...])


def _merge_call(x, mul, sh, gate, w_uvg, ln_g, ln_b, w_s, b_s, o_b, o_c, w_br, w_out, *, tm):
    bsz, length, _ = x.shape
    tok = lambda w: pl.BlockSpec((None, tm, w), lambda b, i: (b, i, 0))
    vec = pl.BlockSpec((None, 1, D_MODEL), lambda b, i: (b, 0, 0))
    full = _const_spec
    return pl.pallas_call(
        _merge_kernel,
        grid=(bsz, length // tm),
        in_specs=[tok(D_MODEL), vec, vec, vec, full(w_uvg), full(ln_g), full(ln_b), full(w_s), full(b_s),
                  tok(MIX_W), tok(MIX_W), full(w_br), full(w_out)],
        out_specs=tok(D_MODEL),
        out_shape=jax.ShapeDtypeStruct(x.shape, F32),
        compiler_params=_cparams(2),
        name="merge",
    )(x, mul, sh, gate, w_uvg, ln_g, ln_b, w_s, b_s, o_b, o_c, w_br, w_out)


HALO = 8
MXU_TILE = 256
FF_EDGES = (0, 4 * MXU_TILE, 8 * MXU_TILE, D_FF)


def _ffn_kernel(x_ref, xp_ref, xn_ref, mul_ref, sh_ref, gate_ref, wg_ref, cw_ref, cb_ref, wu_ref, wd_ref,
                fg_ref, o_ref, *, final):
    i = pl.program_id(1)
    nt = pl.num_programs(1)
    x = x_ref[...]
    tm = x.shape[0]
    xe = jnp.concatenate([xp_ref[...], x, xn_ref[...]], axis=0)
    he = _rms_mod(xe, mul_ref[...], sh_ref[...]).astype(BF16)
    hm = he[HALO:HALO + tm]
    rowe = lax.broadcasted_iota(jnp.int32, (tm + 2 * HALO, 1), 0)
    keep = ((rowe >= HALO) | (i > 0)) & ((rowe < HALO + tm) | (i < nt - 1))
    acc = None
    for c0, c1 in zip(FF_EDGES[:-1], FF_EDGES[1:]):
        cs = slice(c0, c1)
        a = jnp.where(keep, _dot(he, wg_ref[:, cs]), 0.0)
        a_prev = pltpu.roll(a, 1, 0)[HALO:HALO + tm]
        a_next = pltpu.roll(a, tm + 2 * HALO - 1, 0)[HALO:HALO + tm]
        a = (a_prev * cw_ref[0:1, cs] + a[HALO:HALO + tm] * cw_ref[1:2, cs] + a_next * cw_ref[2:3, cs]
             + cb_ref[:, cs])
        z = (a * jax.nn.sigmoid(a) * _dot(hm, wu_ref[:, cs])).astype(BF16)
        t = _dot(z, wd_ref[cs, :])
        acc = t if acc is None else acc + t
    out = x + gate_ref[...] * acc
    if final:
        ms = jnp.mean(out * out, axis=-1, keepdims=True)
        out = out * lax.rsqrt(ms + EPS) * fg_ref[...]
    o_ref[...] = out


def _ffn_call(x, mul, sh, gate, w_gate, conv_w, conv_b, w_up, w_down, final_g, *, tm, final):
    bsz, length, _ = x.shape
    per = tm // HALO
    last = length // HALO - 1
    tok = pl.BlockSpec((None, tm, D_MODEL), lambda b, i: (b, i, 0))
    prv = pl.BlockSpec((None, HALO, D_MODEL), lambda b, i: (b, jnp.maximum(i * per - 1, 0), 0))
    nxt = pl.BlockSpec((None, HALO, D_MODEL), lambda b, i: (b, jnp.minimum((i + 1) * per, last), 0))
    vec = pl.BlockSpec((None, 1, D_MODEL), lambda b, i: (b, 0, 0))
    full = _const_spec
    return pl.pallas_call(
        functools.partial(_ffn_kernel, final=final),
        grid=(bsz, length // tm),
        in_specs=[tok, prv, nxt, vec, vec, vec, full(w_gate), full(conv_w), full(conv_b), full(w_up),
                  full(w_down), full(final_g)],
        out_specs=tok,
        out_shape=jax.ShapeDtypeStruct(x.shape, F32),
        compiler_params=_cparams(2),
        name="ffn",
    )(x, x, x, mul, sh, gate, w_gate, conv_w, conv_b, w_up, w_down, final_g)


def _rope_tables(length):
    pos = jnp.arange(length)
    rows = (pos // GRID_W).astype(F32)
    cols = (pos % GRID_W).astype(F32)
    half = HEAD_DIM // 2
    inv = ROPE_BASE ** (-jnp.arange(0, half, 2, dtype=F32) / half)
    ang_r = rows[:, None] * inv[None, :]
    ang_c = cols[:, None] * inv[None, :]
    zero = jnp.zeros_like(ang_r)
    cos = jnp.concatenate([jnp.cos(ang_r)] * 2 + [jnp.cos(ang_c)] * 2, axis=1)
    sa = jnp.concatenate([-jnp.sin(ang_r), zero, -jnp.sin(ang_c), zero], axis=1)
    sb = jnp.concatenate([zero, jnp.sin(ang_r), zero, jnp.sin(ang_c)], axis=1)
    return tuple(jnp.tile(t, (1, LANES // HEAD_DIM)) for t in (cos, sa, sb))


def kernel(x, c, ctx, c_ctx, w_ada, b_ada, norm1_g, w_in, sgu_ln_g, sgu_ln_b, w_s, b_s, lam_q1, lam_k1,
           lam_q2, lam_k2, diff_subln_g, sinks, w_branch, w_out, norm2_g, w_gate, conv_w, conv_b, w_up,
           w_down, final_g):
    bsz, length, _ = x.shape
    n_ctx = ctx.shape[1]
    tables = _rope_tables(length)
    ctx_tables = tuple(t[:n_ctx] for t in tables)

    rows = -(-(bsz + 1) // 8) * 8
    vpad = jnp.zeros((rows, D_MODEL), F32).at[:bsz].set(c).at[bsz].set(c_ctx)
    ada = _ada_call(vpad, w_ada, b_ada)

    qscale = jnp.ones((ATTN_COLS,), F32).at[0:512].set(SCALE * LOG2E).at[1536:2048].set(SCALE * LOG2E)
    w_attn = (w_in[:, :, 2 * MIX_W:2 * MIX_W + ATTN_COLS] * qscale).astype(BF16)
    w_uvg = jnp.concatenate([w_in[:, :, :2 * MIX_W], w_in[:, :, GATE_COL0:]], axis=2).astype(BF16)
    w_s_b = w_s.astype(BF16)
    b_s_b = jnp.broadcast_to(b_s[..., None], b_s.shape + (LANES,))
    w_br_b = w_branch.astype(BF16)
    w_out_b = w_out.astype(BF16)
    w_gate_b = w_gate.astype(BF16)
    w_up_b = w_up.astype(BF16)
    w_down_b = w_down.astype(BF16)
    final_g2 = final_g.reshape(1, D_MODEL)

    for l in range(DEPTH):
        last = l == DEPTH - 1
        m = ada[l]
        sh1, sc1, g1, sh2, sc2, g2 = [m[:, j * D_MODEL:(j + 1) * D_MODEL] for j in range(6)]
        mul1 = norm1_g[l][None, :] * (1.0 + sc1)
        mul2 = norm2_g[l][None, :] * (1.0 + sc2)
        lat = lambda a: a[:bsz, None, :]
        cx = lambda a: jnp.broadcast_to(a[bsz][None, None, :], (bsz, 1, D_MODEL))

        lam_init = 0.8 - 0.6 * float(np.exp(-0.3 * l))
        lam = jnp.exp(jnp.sum(lam_q1[l] * lam_k1[l])) - jnp.exp(jnp.sum(lam_q2[l] * lam_k2[l])) + lam_init
        scal = jnp.stack([lam, jnp.float32(1.0 - lam_init)]).astype(F32)
        subln = diff_subln_g[l].reshape(1, LANES)
        ln_g = sgu_ln_g[l].reshape(1, MIX_W)
        ln_b = sgu_ln_b[l].reshape(1, MIX_W)
        cb2 = conv_b[l].reshape(1, D_FF)

        bq, bk, bvt, cq, ckd, cvdt = _inproj_call(x, lat(mul1), lat(sh1), w_attn[l], tables, tm=512, rope=True)
        xbq, xbk, xbvt, xcq, xckd, xcvdt = _inproj_call(ctx, cx(mul1), cx(sh1), w_attn[l], ctx_tables,
                                                        tm=n_ctx, rope=False)

        o_b = _diff_call(scal, bq, subln, [(bk, bvt), (xbk, xbvt)], tq=512, kc=512)
        o_c = _win_call(sinks[l], cq, ckd, cvdt, xckd, xcvdt, local=True)
        x_mid = _merge_call(x, lat(mul1), lat(sh1), lat(g1), w_uvg[l], ln_g, ln_b, w_s_b[l], b_s_b[l],
                            o_b, o_c, w_br_b[l], w_out_b[l], tm=512)
        if not last:
            xo_b = _diff_call(scal, xbq, subln, [(xbk, xbvt)], tq=n_ctx, kc=512)
            xo_c = _win_call(sinks[l], xcq, None, None, xckd, xcvdt, local=False)
            ctx_mid = _merge_call(ctx, cx(mul1), cx(sh1), cx(g1), w_uvg[l], ln_g, ln_b, w_s_b[l], b_s_b[l],
                                  xo_b, xo_c, w_br_b[l], w_out_b[l], tm=n_ctx)
        x = _ffn_call(x_mid, lat(mul2), lat(sh2), lat(g2), w_gate_b[l], conv_w[l], cb2, w_up_b[l],
                      w_down_b[l], final_g2, tm=512, final=last)
        if not last:
            ctx = _ffn_call(ctx_mid, cx(mul2), cx(sh2), cx(g2), w_gate_b[l], conv_w[l], cb2, w_up_b[l],
                            w_down_b[l], final_g2, tm=n_ctx, final=False)
    return x
```

```python
import functools

import numpy as np
import jax
import jax.numpy as jnp
from jax import lax
from jax.experimental import pallas as pl
from jax.experimental.pallas import tpu as pltpu

F32 = jnp.float32
BF16 = jnp.bfloat16

D_MODEL = 1024
DEPTH = 4
GRID_W = 64
HEAD_DIM = 64
BLOCK = 128
MIX_W = 512
A_GROUPS = 4
B_HEADS = 4
C_Q_HEADS = 8
D_FF = 2816
ROPE_BASE = 10000.0
EPS = 1e-6
SCALE = HEAD_DIM ** -0.5
NEG_INF = -1e30
LOG2E = float(np.log2(np.e))

LANES = 128
ATTN_COLS = 2304
GATE_COL0 = 3328
VMEM_LIMIT = 56 * 1024 * 1024


def _cparams(n_axes):
    return pltpu.CompilerParams(
        dimension_semantics=("arbitrary",) * n_axes, vmem_limit_bytes=VMEM_LIMIT)


def _const_spec(a):
    return pl.BlockSpec(a.shape, lambda *_: (0,) * a.ndim, pipeline_mode=pl.Buffered(1))


def _rms_mod(x, mul, sh):
    ms = jnp.mean(x * x, axis=-1, keepdims=True)
    return x * lax.rsqrt(ms + EPS) * mul + sh


def _gelu_tanh(x):
    c = np.float32(np.sqrt(2.0 / np.pi))
    return 0.5 * x * (1.0 + jnp.tanh(c * (x + 0.044715 * (x * x * x))))


def _dot(a, b):
    return jnp.dot(a, b, preferred_element_type=F32)


def _dot_nt(a, b):
    return lax.dot_general(a, b, (((1,), (1,)), ((), ())), preferred_element_type=F32)


def _ada_kernel(v_ref, w_ref, b_ref, o_ref):
    v = v_ref[...]
    s = v * jax.nn.sigmoid(v)
    o_ref[...] = _dot(s.astype(BF16), w_ref[...].astype(BF16)) + b_ref[...]


def _ada_call(vpad, w_ada, b_ada):
    rows = vpad.shape[0]
    tn = 1536
    return pl.pallas_call(
        _ada_kernel,
        grid=(DEPTH, 6 * D_MODEL // tn),
        in_specs=[
            pl.BlockSpec((rows, D_MODEL), lambda l, j: (0, 0)),
            pl.BlockSpec((None, D_MODEL, tn), lambda l, j: (l, 0, j)),
            pl.BlockSpec((None, 1, tn), lambda l, j: (l, 0, j)),
        ],
        out_specs=pl.BlockSpec((None, rows, tn), lambda l, j: (l, 0, j)),
        out_shape=jax.ShapeDtypeStruct((DEPTH, rows, 6 * D_MODEL), F32),
        compiler_params=_cparams(2),
        name="ada",
    )(vpad, w_ada, b_ada.reshape(DEPTH, 1, 6 * D_MODEL))


def _inproj_kernel(x_ref, mul_ref, sh_ref, w_ref, cos_ref, sa_ref, sb_ref,
                   bq_ref, bk_ref, bvt_ref, cq_ref, ckd_ref, cvdt_ref, *, rope):
    h = _rms_mod(x_ref[...], mul_ref[...], sh_ref[...]).astype(BF16)
    if rope:
        cos, sa, sb = cos_ref[...], sa_ref[...], sb_ref[...]

    def rot(y):
        if not rope:
            return y
        return y * cos + pltpu.roll(y, LANES - 16, 1) * sa + pltpu.roll(y, 16, 1) * sb

    def dup(y):
        lane = lax.broadcasted_iota(jnp.int32, y.shape, 1)
        sw = pltpu.roll(y, 64, 1)
        return jnp.concatenate([jnp.where(lane < 64, y, sw), jnp.where(lane < 64, sw, y)], axis=1)

    for col0, ref in ((0, bq_ref), (512, bk_ref), (1536, cq_ref)):
        y = _dot(h, w_ref[:, col0:col0 + 512])
        for g in range(4):
            ref[:, g * LANES:(g + 1) * LANES] = rot(y[:, g * LANES:(g + 1) * LANES]).astype(BF16)
    bvt_ref[...] = _dot(h, w_ref[:, 1024:1536]).T.astype(BF16)
    ykv = _dot(h, w_ref[:, 2048:2304])
    ckd_ref[...] = dup(rot(ykv[:, :LANES])).astype(BF16)
    cvdt_ref[...] = dup(ykv[:, LANES:]).T.astype(BF16)


def _inproj_call(x, mul, sh, w_attn, tables, *, tm, rope):
    bsz, length, _ = x.shape
    tok = lambda w: pl.BlockSpec((None, tm, w), lambda b, i: (b, i, 0))
    vec = pl.BlockSpec((None, 1, D_MODEL), lambda b, i: (b, 0, 0))
    tab = pl.BlockSpec((tm, LANES), lambda b, i: (i, 0))
    tok_t = lambda w: pl.BlockSpec((None, w, tm), lambda b, i: (b, 0, i))
    out = lambda w: jax.ShapeDtypeStruct((bsz, length, w), BF16)
    out_t = lambda w: jax.ShapeDtypeStruct((bsz, w, length), BF16)
    return pl.pallas_call(
        functools.partial(_inproj_kernel, rope=rope),
        grid=(bsz, length // tm),
        in_specs=[tok(D_MODEL), vec, vec,
                  _const_spec(w_attn), tab, tab, tab],
        out_specs=[tok(512), tok(512), tok_t(MIX_W), tok(512), tok(256), tok_t(2 * LANES)],
        out_shape=[out(512), out(512), out_t(MIX_W), out(512), out(256), out_t(2 * LANES)],
        compiler_params=_cparams(2),
        name="inproj",
    )(x, mul, sh, w_attn, *tables)


SAFE_EXP2_RANGE = 90.0


def _diff_kernel(sc_ref, q_ref, qall_ref, g_ref, ind_ref, *refs, seg_lens, kc, nq):
    o_ref, p_scr, f_scr, safe_scr = refs[-4:]
    t = pl.program_id(0)
    tq = q_ref.shape[0]
    lanei = lax.broadcasted_iota(jnp.int32, (1, 2 * tq), 1)

    @pl.when(t == 0)
    def _():
        p_scr[...] = jnp.zeros(p_scr.shape, BF16)
        f_scr[...] = jnp.zeros(f_scr.shape, F32)

    chunks = []
    row0 = 0
    for si, n_keys in enumerate(seg_lens):
        for st in range(0, n_keys, kc):
            size = min(kc, n_keys - st)
            chunks.append((refs[2 * si], refs[2 * si + 1], st, size, row0))
            row0 += size

    def max_sq_norm(x):
        xf = x.astype(F32)
        return jnp.max(_dot((xf * xf).astype(BF16), ind_ref[...]), axis=0, keepdims=True)

    @pl.when(t % nq == 0)
    def _():
        kn = None
        for k_ref, _, st, size, _ in chunks:
            n2 = max_sq_norm(k_ref[st:st + size, :])
            kn = n2 if kn is None else jnp.maximum(kn, n2)
        qn = None
        n_q = qall_ref.shape[0]
        for st in range(0, n_q, 2 * kc):
            n2 = max_sq_norm(qall_ref[st:min(st + 2 * kc, n_q), :])
            qn = n2 if qn is None else jnp.maximum(qn, n2)
        bound2 = jnp.max(kn * qn)
        safe_scr[0] = (bound2 < SAFE_EXP2_RANGE * SAFE_EXP2_RANGE).astype(jnp.int32)

    safe = safe_scr[0] == 1

    def step(max_free):
        q_t = q_ref[...].astype(F32).T
        feat = lax.broadcasted_iota(jnp.int32, q_t.shape, 0)
        qq_t = jnp.concatenate([jnp.where(feat < 64, q_t, 0.0), jnp.where(feat >= 64, q_t, 0.0)],
                               axis=1).astype(BF16)
        f_prev = f_scr[...]
        mcs, lcs = [], []
        acc = None
        for c, (k_ref, vt_ref, st, size, r0) in enumerate(chunks):
            f1 = f_prev[c:c + 1, 0:tq].astype(BF16)
            f2 = f_prev[c:c + 1, tq:2 * tq].astype(BF16)
            a_t = p_scr[r0:r0 + size, 0:tq] * f1 - p_scr[r0:r0 + size, tq:2 * tq] * f2
            part = _dot(vt_ref[:, st:st + size], a_t)
            acc = part if acc is None else acc + part
            s = _dot(k_ref[st:st + size, :], qq_t)
            if max_free:
                p = jnp.exp2(s)
            else:
                mc = jnp.max(s, axis=0, keepdims=True)
                p = jnp.exp2(s - mc)
                mcs.append(mc)
            lcs.append(jnp.sum(p, axis=0, keepdims=True))
            p_scr[r0:r0 + size, :] = p.astype(BF16)

        if max_free:
            es = [None] * len(chunks)
            l_all = functools.reduce(lambda a, b: a + b, lcs)
        else:
            m_all = functools.reduce(jnp.maximum, mcs)
            es = [jnp.exp2(mc - m_all) for mc in mcs]
            l_all = functools.reduce(lambda a, b: a + b, [lc * e for lc, e in zip(lcs, es)])
        coef = jnp.where(lanei < tq, 1.0, sc_ref[0]) / l_all
        for c, e in enumerate(es):
            f_scr[c:c + 1, :] = coef if e is None else e * coef

        o = acc.T
        ms = jnp.mean(o * o, axis=-1, keepdims=True)
        o_ref[...] = (o * lax.rsqrt(ms + EPS) * g_ref[...] * sc_ref[1]).astype(BF16)

    pl.when(safe)(lambda: step(True))
    pl.when(jnp.logical_not(safe))(lambda: step(False))


def _diff_call(scal, q, subln_g, segs, *, tq, kc):
    bsz, lq, _ = q.shape
    nq = lq // tq
    n_blocks = bsz * B_HEADS * nq
    half = np.arange(LANES) // HEAD_DIM
    ind = jnp.asarray(half[:, None] == half[None, :], BF16)

    def block(t):
        b, r = t // (B_HEADS * nq), t % (B_HEADS * nq)
        return b, r // nq, r % nq

    cur = lambda t: block(jnp.minimum(t, n_blocks - 1))
    prev = lambda t: block(jnp.maximum(t - 1, 0))

    def q_map(t):
        b, h, i = cur(t)
        return b, i, h

    def kq_map(t):
        b, h, _ = cur(t)
        return b, 0, h

    def vt_map(t):
        b, h, _ = prev(t)
        return b, h, 0

    def o_map(t):
        b, h, i = prev(t)
        return b, i, h

    in_specs = [pl.BlockSpec(memory_space=pltpu.SMEM),
                pl.BlockSpec((None, tq, LANES), q_map),
                pl.BlockSpec((None, lq, LANES), kq_map),
                pl.BlockSpec((1, LANES), lambda t: (0, 0)),
                _const_spec(ind)]
    args = [scal, q, q, subln_g, ind]
    for k, vt in segs:
        n_keys = k.shape[1]
        in_specs += [pl.BlockSpec((None, n_keys, LANES), kq_map),
                     pl.BlockSpec((None, LANES, n_keys), vt_map)]
        args += [k, vt]
    seg_lens = tuple(k.shape[1] for k, _ in segs)
    n_chunks = sum(-(-n // kc) for n in seg_lens)
    return pl.pallas_call(
        functools.partial(_diff_kernel, seg_lens=seg_lens, kc=kc, nq=nq),
        grid=(n_blocks + 1,),
        in_specs=in_specs,
        out_specs=pl.BlockSpec((None, tq, LANES), o_map),
        out_shape=jax.ShapeDtypeStruct((bsz, lq, MIX_W), BF16),
        scratch_shapes=[pltpu.VMEM((sum(seg_lens), 2 * tq), BF16),
                        pltpu.VMEM((-(-n_chunks // 8) * 8, 2 * tq), F32),
                        pltpu.SMEM((1,), jnp.int32)],
        compiler_params=_cparams(1),
        name="diff",
    )(*args)


WIN_SUB = 4


def _win_kernel(sink_ref, q_ref, *refs, local):
    o_ref = refs[-1]
    n_sub = q_ref.shape[0] // BLOCK
    kx_ref, vxt_ref = refs[-3], refs[-2]
    n = pl.program_id(1)
    nb = pl.num_programs(1)
    nq = C_Q_HEADS // 2 * BLOCK
    lane = lax.broadcasted_iota(jnp.int32, (BLOCK, LANES), 1)
    headi = lax.broadcasted_iota(jnp.int32, (1, nq), 1) // BLOCK
    if local:
        kp_ref, kc_ref, kn_ref, vpt_ref, vct_ref, vnt_ref = refs[:6]
        key = lax.broadcasted_iota(jnp.int32, (BLOCK, nq), 0)
        qry = lax.broadcasted_iota(jnp.int32, (BLOCK, nq), 1) & (BLOCK - 1)
        tri_prev, tri_next = key >= qry, key <= qry

    probs = [(sub, kvh) for sub in range(n_sub) for kvh in range(2)]
    s_l, sink_l, vt_l = [], [], []
    for sub, kvh in probs:
        ksl = slice(kvh * LANES, (kvh + 1) * LANES)
        rows = slice(sub * BLOCK, (sub + 1) * BLOCK)
        parts = []
        for g in range(4):
            grp = kvh * 2 + g // 2
            qg = q_ref[rows, grp * LANES:(grp + 1) * LANES]
            parts.append(jnp.where((lane < 64) if g % 2 == 0 else (lane >= 64), qg, jnp.zeros_like(qg)))
        qs = jnp.concatenate(parts, axis=0)
        sink = jnp.zeros((1, nq), F32)
        for g in range(4):
            sink = jnp.where(headi == g, sink_ref[kvh * 4 + g] * LOG2E, sink)
        if local:
            kb, vb = [], []
            for j in (sub - 1, sub, sub + 1):
                if j < 0:
                    kb.append(kp_ref[:, ksl]), vb.append(vpt_ref[ksl, :])
                elif j >= n_sub:
                    kb.append(kn_ref[:, ksl]), vb.append(vnt_ref[ksl, :])
                else:
                    kb.append(kc_ref[j * BLOCK:(j + 1) * BLOCK, ksl])
                    vb.append(vct_ref[ksl, j * BLOCK:(j + 1) * BLOCK])
            k_all = jnp.concatenate(kb + [kx_ref[:, ksl]], axis=0)
            vt_all = jnp.concatenate(vb + [vxt_ref[ksl, :]], axis=1)
        else:
            k_all, vt_all = kx_ref[:, ksl], vxt_ref[ksl, :]
        s = _dot_nt(k_all, qs)
        if local:
            valid_prev = tri_prev if sub > 0 else tri_prev & (n > 0)
            valid_next = tri_next if sub < n_sub - 1 else tri_next & (n < nb - 1)
            s = jnp.concatenate([jnp.where(valid_prev, s[0:BLOCK], NEG_INF), s[BLOCK:2 * BLOCK],
                                 jnp.where(valid_next, s[2 * BLOCK:3 * BLOCK], NEG_INF), s[3 * BLOCK:]], axis=0)
        s_l.append(s)
        sink_l.append(sink)
        vt_l.append(vt_all)
    m_l = [jnp.maximum(jnp.max(s, axis=0, keepdims=True), sink) for s, sink in zip(s_l, sink_l)]
    p_l = [jnp.exp2(s - m) for s, m in zip(s_l, m_l)]
    l_l = [jnp.sum(p, axis=0, keepdims=True) + jnp.exp2(sink - m) for p, sink, m in zip(p_l, sink_l, m_l)]
    ot_l = [_dot(vt, p.astype(BF16)) / l for vt, p, l in zip(vt_l, p_l, l_l)]
    for (sub, kvh), o_t in zip(probs, ot_l):
        o = o_t.T
        for j in range(2):
            grp = kvh * 2 + j
            o_ref[sub * BLOCK:(sub + 1) * BLOCK, grp * LANES:(grp + 1) * LANES] = jnp.where(
                lane < 64, o[2 * j * BLOCK:(2 * j + 1) * BLOCK], o[(2 * j + 1) * BLOCK:(2 * j + 2) * BLOCK]
            ).astype(BF16)


def _win_call(sinks, q, kd, vdt, kxd, vxdt, *, local):
    bsz, lq, _ = q.shape
    nb = lq // BLOCK
    n_sub = min(WIN_SUB, nb)
    tq = n_sub * BLOCK
    in_specs = [pl.BlockSpec(memory_space=pltpu.SMEM), pl.BlockSpec((None, tq, MIX_W), lambda b, n: (b, n, 0))]
    args = [sinks, q]
    if local:
        prv = lambda n: jnp.maximum(n * n_sub - 1, 0)
        nxt = lambda n: jnp.minimum((n + 1) * n_sub, nb - 1)
        in_specs += [pl.BlockSpec((None, BLOCK, 2 * LANES), lambda b, n: (b, prv(n), 0)),
                     pl.BlockSpec((None, tq, 2 * LANES), lambda b, n: (b, n, 0)),
                     pl.BlockSpec((None, BLOCK, 2 * LANES), lambda b, n: (b, nxt(n), 0)),
                     pl.BlockSpec((None, 2 * LANES, BLOCK), lambda b, n: (b, 0, prv(n))),
                     pl.BlockSpec((None, 2 * LANES, tq), lambda b, n: (b, 0, n)),
                     pl.BlockSpec((None, 2 * LANES, BLOCK), lambda b, n: (b, 0, nxt(n)))]
        args += [kd] * 3 + [vdt] * 3
    n_ctx = kxd.shape[1]
    in_specs += [pl.BlockSpec((None, n_ctx, 2 * LANES), lambda b, n: (b, 0, 0)),
                 pl.BlockSpec((None, 2 * LANES, n_ctx), lambda b, n: (b, 0, 0))]
    args += [kxd, vxdt]
    return pl.pallas_call(
        functools.partial(_win_kernel, local=local),
        grid=(bsz, lq // tq),
        in_specs=in_specs,
        out_specs=pl.BlockSpec((None, tq, MIX_W), lambda b, n: (b, n, 0)),
        out_shape=jax.ShapeDtypeStruct((bsz, lq, MIX_W), BF16),
        compiler_params=_cparams(2),
        name="win",
    )(*args)


def _merge_kernel(x_ref, mul_ref, sh_ref, gate_ref, wuvg_ref, lng_ref, lnb_ref, ws_ref, bs_ref,
                  ob_ref, oc_ref, wbr_ref, wout_ref, o_ref):
    x = x_ref[...]
    tm = x.shape[0]
    h = _rms_mod(x, mul_ref[...], sh_ref[...]).astype(BF16)
    u = _gelu_tanh(_dot(h, wuvg_ref[:, 0:MIX_W]))
    v = _gelu_tanh(_dot(h, wuvg_ref[:, MIX_W:2 * MIX_W]))
    mu = jnp.mean(v, axis=-1, keepdims=True)
    vc = v - mu
    var = jnp.mean(vc * vc, axis=-1, keepdims=True)
    vn = (vc * lax.rsqrt(var + EPS) * lng_ref[...] + lnb_ref[...]).astype(BF16)
    n_chunks = tm // BLOCK
    mixed = []
    for g in range(A_GROUPS):
        gs = slice(g * LANES, (g + 1) * LANES)
        rhs = jnp.concatenate([vn[c * BLOCK:(c + 1) * BLOCK, gs] for c in range(n_chunks)], axis=1)
        mixed.append(_dot(ws_ref[g], rhs))
    o_a = (u * jnp.concatenate(
        [jnp.concatenate([mixed[g][:, c * LANES:(c + 1) * LANES] + bs_ref[g] for g in range(A_GROUPS)], axis=1)
         for c in range(n_chunks)], axis=0)).astype(BF16)
    y = None
    for i, o_i in enumerate((o_a, ob_ref[...], oc_ref[...])):
        col0 = 2 * MIX_W + i * D_MODEL
        gate = jax.nn.sigmoid(_dot(h, wuvg_ref[:, col0:col0 + D_MODEL]))
        t = gate * _dot(o_i, wbr_ref[i])
        y = t if y is None else y + t
    o_ref[...] = x + gate_ref[...] * _dot(y.astype(BF16), wout_ref[...])


def _merge_call(x, mul, sh, gate, w_uvg, ln_g, ln_b, w_s, b_s, o_b, o_c, w_br, w_out, *, tm):
    bsz, length, _ = x.shape
    tok = lambda w: pl.BlockSpec((None, tm, w), lambda b, i: (b, i, 0))
    vec = pl.BlockSpec((None, 1, D_MODEL), lambda b, i: (b, 0, 0))
    full = _const_spec
    return pl.pallas_call(
        _merge_kernel,
        grid=(bsz, length // tm),
        in_specs=[tok(D_MODEL), vec, vec, vec, full(w_uvg), full(ln_g), full(ln_b), full(w_s), full(b_s),
                  tok(MIX_W), tok(MIX_W), full(w_br), full(w_out)],
        out_specs=tok(D_MODEL),
        out_shape=jax.ShapeDtypeStruct(x.shape, F32),
        compiler_params=_cparams(2),
        name="merge",
    )(x, mul, sh, gate, w_uvg, ln_g, ln_b, w_s, b_s, o_b, o_c, w_br, w_out)


HALO = 8
MXU_TILE = 256
FF_EDGES = (0, 4 * MXU_TILE, 8 * MXU_TILE, D_FF)


def _ffn_kernel(x_ref, xp_ref, xn_ref, mul_ref, sh_ref, gate_ref, wg_ref, cw_ref, cb_ref, wu_ref, wd_ref,
                fg_ref, o_ref, *, final):
    i = pl.program_id(1)
    nt = pl.num_programs(1)
    x = x_ref[...]
    tm = x.shape[0]
    xe = jnp.concatenate([xp_ref[...], x, xn_ref[...]], axis=0)
    he = _rms_mod(xe, mul_ref[...], sh_ref[...]).astype(BF16)
    hm = he[HALO:HALO + tm]
    rowe = lax.broadcasted_iota(jnp.int32, (tm + 2 * HALO, 1), 0)
    keep = ((rowe >= HALO) | (i > 0)) & ((rowe < HALO + tm) | (i < nt - 1))
    acc = None
    for c0, c1 in zip(FF_EDGES[:-1], FF_EDGES[1:]):
        cs = slice(c0, c1)
        a = jnp.where(keep, _dot(he, wg_ref[:, cs]), 0.0)
        a_prev = pltpu.roll(a, 1, 0)[HALO:HALO + tm]
        a_next = pltpu.roll(a, tm + 2 * HALO - 1, 0)[HALO:HALO + tm]
        a = (a_prev * cw_ref[0:1, cs] + a[HALO:HALO + tm] * cw_ref[1:2, cs] + a_next * cw_ref[2:3, cs]
             + cb_ref[:, cs])
        z = (a * jax.nn.sigmoid(a) * _dot(hm, wu_ref[:, cs])).astype(BF16)
        t = _dot(z, wd_ref[cs, :])
        acc = t if acc is None else acc + t
    out = x + gate_ref[...] * acc
    if final:
        ms = jnp.mean(out * out, axis=-1, keepdims=True)
        out = out * lax.rsqrt(ms + EPS) * fg_ref[...]
    o_ref[...] = out


def _ffn_call(x, mul, sh, gate, w_gate, conv_w, conv_b, w_up, w_down, final_g, *, tm, final):
    bsz, length, _ = x.shape
    per = tm // HALO
    last = length // HALO - 1
    tok = pl.BlockSpec((None, tm, D_MODEL), lambda b, i: (b, i, 0))
    prv = pl.BlockSpec((None, HALO, D_MODEL), lambda b, i: (b, jnp.maximum(i * per - 1, 0), 0))
    nxt = pl.BlockSpec((None, HALO, D_MODEL), lambda b, i: (b, jnp.minimum((i + 1) * per, last), 0))
    vec = pl.BlockSpec((None, 1, D_MODEL), lambda b, i: (b, 0, 0))
    full = _const_spec
    return pl.pallas_call(
        functools.partial(_ffn_kernel, final=final),
        grid=(bsz, length // tm),
        in_specs=[tok, prv, nxt, vec, vec, vec, full(w_gate), full(conv_w), full(conv_b), full(w_up),
                  full(w_down), full(final_g)],
        out_specs=tok,
        out_shape=jax.ShapeDtypeStruct(x.shape, F32),
        compiler_params=_cparams(2),
        name="ffn",
    )(x, x, x, mul, sh, gate, w_gate, conv_w, conv_b, w_up, w_down, final_g)


def _rope_tables(length):
    pos = jnp.arange(length)
    rows = (pos // GRID_W).astype(F32)
    cols = (pos % GRID_W).astype(F32)
    half = HEAD_DIM // 2
    inv = ROPE_BASE ** (-jnp.arange(0, half, 2, dtype=F32) / half)
    ang_r = rows[:, None] * inv[None, :]
    ang_c = cols[:, None] * inv[None, :]
    zero = jnp.zeros_like(ang_r)
    cos = jnp.concatenate([jnp.cos(ang_r)] * 2 + [jnp.cos(ang_c)] * 2, axis=1)
    sa = jnp.concatenate([-jnp.sin(ang_r), zero, -jnp.sin(ang_c), zero], axis=1)
    sb = jnp.concatenate([zero, jnp.sin(ang_r), zero, jnp.sin(ang_c)], axis=1)
    return tuple(jnp.tile(t, (1, LANES // HEAD_DIM)) for t in (cos, sa, sb))


def kernel(x, c, ctx, c_ctx, w_ada, b_ada, norm1_g, w_in, sgu_ln_g, sgu_ln_b, w_s, b_s, lam_q1, lam_k1,
           lam_q2, lam_k2, diff_subln_g, sinks, w_branch, w_out, norm2_g, w_gate, conv_w, conv_b, w_up,
           w_down, final_g):
    bsz, length, _ = x.shape
    n_ctx = ctx.shape[1]
    tables = _rope_tables(length)
    ctx_tables = tuple(t[:n_ctx] for t in tables)

    rows = -(-(bsz + 1) // 8) * 8
    vpad = jnp.zeros((rows, D_MODEL), F32).at[:bsz].set(c).at[bsz].set(c_ctx)
    ada = _ada_call(vpad, w_ada, b_ada)

    qscale = jnp.ones((ATTN_COLS,), F32).at[0:512].set(SCALE * LOG2E).at[1536:2048].set(SCALE * LOG2E)
    w_attn = (w_in[:, :, 2 * MIX_W:2 * MIX_W + ATTN_COLS] * qscale).astype(BF16)
    w_uvg = jnp.concatenate([w_in[:, :, :2 * MIX_W], w_in[:, :, GATE_COL0:]], axis=2).astype(BF16)
    w_s_b = w_s.astype(BF16)
    b_s_b = jnp.broadcast_to(b_s[..., None], b_s.shape + (LANES,))
    w_br_b = w_branch.astype(BF16)
    w_out_b = w_out.astype(BF16)
    w_gate_b = w_gate.astype(BF16)
    w_up_b = w_up.astype(BF16)
    w_down_b = w_down.astype(BF16)
    final_g2 = final_g.reshape(1, D_MODEL)

    for l in range(DEPTH):
        last = l == DEPTH - 1
        m = ada[l]
        sh1, sc1, g1, sh2, sc2, g2 = [m[:, j * D_MODEL:(j + 1) * D_MODEL] for j in range(6)]
        mul1 = norm1_g[l][None, :] * (1.0 + sc1)
        mul2 = norm2_g[l][None, :] * (1.0 + sc2)
        lat = lambda a: a[:bsz, None, :]
        cx = lambda a: jnp.broadcast_to(a[bsz][None, None, :], (bsz, 1, D_MODEL))

        lam_init = 0.8 - 0.6 * float(np.exp(-0.3 * l))
        lam = jnp.exp(jnp.sum(lam_q1[l] * lam_k1[l])) - jnp.exp(jnp.sum(lam_q2[l] * lam_k2[l])) + lam_init
        scal = jnp.stack([lam, jnp.float32(1.0 - lam_init)]).astype(F32)
        subln = diff_subln_g[l].reshape(1, LANES)
        ln_g = sgu_ln_g[l].reshape(1, MIX_W)
        ln_b = sgu_ln_b[l].reshape(1, MIX_W)
        cb2 = conv_b[l].reshape(1, D_FF)

        bq, bk, bvt, cq, ckd, cvdt = _inproj_call(x, lat(mul1), lat(sh1), w_attn[l], tables, tm=512, rope=True)
        xbq, xbk, xbvt, xcq, xckd, xcvdt = _inproj_call(ctx, cx(mul1), cx(sh1), w_attn[l], ctx_tables,
                                                        tm=n_ctx, rope=False)

        o_b = _diff_call(scal, bq, subln, [(bk, bvt), (xbk, xbvt)], tq=512, kc=256)
        o_c = _win_call(sinks[l], cq, ckd, cvdt, xckd, xcvdt, local=True)
        x_mid = _merge_call(x, lat(mul1), lat(sh1), lat(g1), w_uvg[l], ln_g, ln_b, w_s_b[l], b_s_b[l],
                            o_b, o_c, w_br_b[l], w_out_b[l], tm=512)
        if not last:
            xo_b = _diff_call(scal, xbq, subln, [(xbk, xbvt)], tq=n_ctx, kc=512)
            xo_c = _win_call(sinks[l], xcq, None, None, xckd, xcvdt, local=False)
            ctx_mid = _merge_call(ctx, cx(mul1), cx(sh1), cx(g1), w_uvg[l], ln_g, ln_b, w_s_b[l], b_s_b[l],
                                  xo_b, xo_c, w_br_b[l], w_out_b[l], tm=n_ctx)
        x = _ffn_call(x_mid, lat(mul2), lat(sh2), lat(g2), w_gate_b[l], conv_w[l], cb2, w_up_b[l],
                      w_down_b[l], final_g2, tm=512, final=last)
        if not last:
            ctx = _ffn_call(ctx_mid, cx(mul2), cx(sh2), cx(g2), w_gate_b[l], conv_w[l], cb2, w_up_b[l],
                            w_down_b[l], final_g2, tm=n_ctx, final=False)
    return x
```

```python
import functools

import numpy as np
import jax
import jax.numpy as jnp
from jax import lax
from jax.experimental import pallas as pl
from jax.experimental.pallas import tpu as pltpu

F32 = jnp.float32
BF16 = jnp.bfloat16

D_MODEL = 1024
DEPTH = 4
GRID_W = 64
HEAD_DIM = 64
BLOCK = 128
MIX_W = 512
A_GROUPS = 4
B_HEADS = 4
C_Q_HEADS = 8
D_FF = 2816
ROPE_BASE = 10000.0
EPS = 1e-6
SCALE = HEAD_DIM ** -0.5
NEG_INF = -1e30
LOG2E = float(np.log2(np.e))

LANES = 128
ATTN_COLS = 2304
GATE_COL0 = 3328
VMEM_LIMIT = 56 * 1024 * 1024


def _cparams(n_axes):
    return pltpu.CompilerParams(
        dimension_semantics=("arbitrary",) * n_axes, vmem_limit_bytes=VMEM_LIMIT)


def _const_spec(a):
    return pl.BlockSpec(a.shape, lambda *_: (0,) * a.ndim, pipeline_mode=pl.Buffered(1))


def _rms_mod(x, mul, sh):
    ms = jnp.mean(x * x, axis=-1, keepdims=True)
    return x * lax.rsqrt(ms + EPS) * mul + sh


def _gelu_tanh(x):
    c = np.float32(np.sqrt(2.0 / np.pi))
    return 0.5 * x * (1.0 + jnp.tanh(c * (x + 0.044715 * (x * x * x))))


def _dot(a, b):
    return jnp.dot(a, b, preferred_element_type=F32)


def _dot_nt(a, b):
    return lax.dot_general(a, b, (((1,), (1,)), ((), ())), preferred_element_type=F32)


def _ada_kernel(v_ref, w_ref, b_ref, o_ref):
    v = v_ref[...]
    s = v * jax.nn.sigmoid(v)
    o_ref[...] = _dot(s.astype(BF16), w_ref[...].astype(BF16)) + b_ref[...]


def _ada_call(vpad, w_ada, b_ada):
    rows = vpad.shape[0]
    tn = 1536
    return pl.pallas_call(
        _ada_kernel,
        grid=(DEPTH, 6 * D_MODEL // tn),
        in_specs=[
            pl.BlockSpec((rows, D_MODEL), lambda l, j: (0, 0)),
            pl.BlockSpec((None, D_MODEL, tn), lambda l, j: (l, 0, j)),
            pl.BlockSpec((None, 1, tn), lambda l, j: (l, 0, j)),
        ],
        out_specs=pl.BlockSpec((None, rows, tn), lambda l, j: (l, 0, j)),
        out_shape=jax.ShapeDtypeStruct((DEPTH, rows, 6 * D_MODEL), F32),
        compiler_params=_cparams(2),
        name="ada",
    )(vpad, w_ada, b_ada.reshape(DEPTH, 1, 6 * D_MODEL))


def _inproj_kernel(x_ref, mul_ref, sh_ref, w_ref, cos_ref, sa_ref, sb_ref,
                   bq_ref, bk_ref, bvt_ref, cq_ref, ckd_ref, cvdt_ref, *, rope):
    h = _rms_mod(x_ref[...], mul_ref[...], sh_ref[...]).astype(BF16)
    if rope:
        cos, sa, sb = cos_ref[...], sa_ref[...], sb_ref[...]

    def rot(y):
        if not rope:
            return y
        return y * cos + pltpu.roll(y, LANES - 16, 1) * sa + pltpu.roll(y, 16, 1) * sb

    def dup(y):
        lane = lax.broadcasted_iota(jnp.int32, y.shape, 1)
        sw = pltpu.roll(y, 64, 1)
        return jnp.concatenate([jnp.where(lane < 64, y, sw), jnp.where(lane < 64, sw, y)], axis=1)

    for col0, ref in ((0, bq_ref), (512, bk_ref), (1536, cq_ref)):
        y = _dot(h, w_ref[:, col0:col0 + 512])
        for g in range(4):
            ref[:, g * LANES:(g + 1) * LANES] = rot(y[:, g * LANES:(g + 1) * LANES]).astype(BF16)
    ykv = _dot(h, w_ref[:, 2048:2304])
    ckd_ref[...] = dup(rot(ykv[:, :LANES])).astype(BF16)
    cvdt_ref[...] = dup(ykv[:, LANES:]).T.astype(BF16)
    bvt_ref[...] = _dot(h, w_ref[:, 1024:1536]).T.astype(BF16)


def _inproj_call(x, mul, sh, w_attn, tables, *, tm, rope):
    bsz, length, _ = x.shape
    tok = lambda w: pl.BlockSpec((None, tm, w), lambda b, i: (b, i, 0))
    vec = pl.BlockSpec((None, 1, D_MODEL), lambda b, i: (b, 0, 0))
    tab = pl.BlockSpec((tm, LANES), lambda b, i: (i, 0))
    tok_t = lambda w: pl.BlockSpec((None, w, tm), lambda b, i: (b, 0, i))
    out = lambda w: jax.ShapeDtypeStruct((bsz, length, w), BF16)
    out_t = lambda w: jax.ShapeDtypeStruct((bsz, w, length), BF16)
    return pl.pallas_call(
        functools.partial(_inproj_kernel, rope=rope),
        grid=(bsz, length // tm),
        in_specs=[tok(D_MODEL), vec, vec,
                  _const_spec(w_attn), tab, tab, tab],
        out_specs=[tok(512), tok(512), tok_t(MIX_W), tok(512), tok(256), tok_t(2 * LANES)],
        out_shape=[out(512), out(512), out_t(MIX_W), out(512), out(256), out_t(2 * LANES)],
        compiler_params=_cparams(2),
        name="inproj",
    )(x, mul, sh, w_attn, *tables)


SAFE_EXP2_RANGE = 90.0


def _diff_kernel(sc_ref, q_ref, qall_ref, g_ref, ind_ref, *refs, seg_lens, kc, nq):
    o_ref, p_scr, f_scr, safe_scr = refs[-4:]
    t = pl.program_id(0)
    tq = q_ref.shape[0]
    lanei = lax.broadcasted_iota(jnp.int32, (1, 2 * tq), 1)

    @pl.when(t == 0)
    def _():
        p_scr[...] = jnp.zeros(p_scr.shape, BF16)
        f_scr[...] = jnp.zeros(f_scr.shape, F32)

    chunks = []
    row0 = 0
    for si, n_keys in enumerate(seg_lens):
        for st in range(0, n_keys, kc):
            size = min(kc, n_keys - st)
            chunks.append((refs[2 * si], refs[2 * si + 1], st, size, row0))
            row0 += size

    def max_sq_norm(x):
        xf = x.astype(F32)
        return jnp.max(_dot((xf * xf).astype(BF16), ind_ref[...]), axis=0, keepdims=True)

    @pl.when(t % nq == 0)
    def _():
        kn = None
        for k_ref, _, st, size, _ in chunks:
            n2 = max_sq_norm(k_ref[st:st + size, :])
            kn = n2 if kn is None else jnp.maximum(kn, n2)
        qn = None
        n_q = qall_ref.shape[0]
        for st in range(0, n_q, 2 * kc):
            n2 = max_sq_norm(qall_ref[st:min(st + 2 * kc, n_q), :])
            qn = n2 if qn is None else jnp.maximum(qn, n2)
        bound2 = jnp.max(kn * qn)
        safe_scr[0] = (bound2 < SAFE_EXP2_RANGE * SAFE_EXP2_RANGE).astype(jnp.int32)

    safe = safe_scr[0] == 1

    def step(max_free):
        q = q_ref[...]
        lane = lax.broadcasted_iota(jnp.int32, q.shape, 1)
        zero = jnp.zeros_like(q)
        qq = jnp.concatenate([jnp.where(lane < 64, q, zero), jnp.where(lane >= 64, q, zero)], axis=0)
        f_prev = f_scr[...]
        mcs, lcs = [], []
        acc = None
        for c, (k_ref, vt_ref, st, size, r0) in enumerate(chunks):
            f1 = f_prev[c:c + 1, 0:tq].astype(BF16)
            f2 = f_prev[c:c + 1, tq:2 * tq].astype(BF16)
            a_t = p_scr[r0:r0 + size, 0:tq] * f1 - p_scr[r0:r0 + size, tq:2 * tq] * f2
            part = _dot(vt_ref[:, st:st + size], a_t)
            acc = part if acc is None else acc + part
            s = _dot_nt(k_ref[st:st + size, :], qq)
            if max_free:
                p = jnp.exp2(s)
            else:
                mc = jnp.max(s, axis=0, keepdims=True)
                p = jnp.exp2(s - mc)
                mcs.append(mc)
            lcs.append(jnp.sum(p, axis=0, keepdims=True))
            p_scr[r0:r0 + size, :] = p.astype(BF16)

        if max_free:
            es = [None] * len(chunks)
            l_all = functools.reduce(lambda a, b: a + b, lcs)
        else:
            m_all = functools.reduce(jnp.maximum, mcs)
            es = [jnp.exp2(mc - m_all) for mc in mcs]
            l_all = functools.reduce(lambda a, b: a + b, [lc * e for lc, e in zip(lcs, es)])
        coef = jnp.where(lanei < tq, 1.0, sc_ref[0]) / l_all
        for c, e in enumerate(es):
            f_scr[c:c + 1, :] = coef if e is None else e * coef

        o = acc.T
        ms = jnp.mean(o * o, axis=-1, keepdims=True)
        o_ref[...] = (o * lax.rsqrt(ms + EPS) * g_ref[...] * sc_ref[1]).astype(BF16)

    pl.when(safe)(lambda: step(True))
    pl.when(jnp.logical_not(safe))(lambda: step(False))


def _diff_call(scal, q, subln_g, segs, *, tq, kc):
    bsz, lq, _ = q.shape
    nq = lq // tq
    n_blocks = bsz * B_HEADS * nq
    half = np.arange(LANES) // HEAD_DIM
    ind = jnp.asarray(half[:, None] == half[None, :], BF16)

    def block(t):
        b, r = t // (B_HEADS * nq), t % (B_HEADS * nq)
        return b, r // nq, r % nq

    cur = lambda t: block(jnp.minimum(t, n_blocks - 1))
    prev = lambda t: block(jnp.maximum(t - 1, 0))

    def q_map(t):
        b, h, i = cur(t)
        return b, i, h

    def kq_map(t):
        b, h, _ = cur(t)
        return b, 0, h

    def vt_map(t):
        b, h, _ = prev(t)
        return b, h, 0

    def o_map(t):
        b, h, i = prev(t)
        return b, i, h

    in_specs = [pl.BlockSpec(memory_space=pltpu.SMEM),
                pl.BlockSpec((None, tq, LANES), q_map),
                pl.BlockSpec((None, lq, LANES), kq_map),
                pl.BlockSpec((1, LANES), lambda t: (0, 0)),
                _const_spec(ind)]
    args = [scal, q, q, subln_g, ind]
    for k, vt in segs:
        n_keys = k.shape[1]
        in_specs += [pl.BlockSpec((None, n_keys, LANES), kq_map),
                     pl.BlockSpec((None, LANES, n_keys), vt_map)]
        args += [k, vt]
    seg_lens = tuple(k.shape[1] for k, _ in segs)
    n_chunks = sum(-(-n // kc) for n in seg_lens)
    return pl.pallas_call(
        functools.partial(_diff_kernel, seg_lens=seg_lens, kc=kc, nq=nq),
        grid=(n_blocks + 1,),
        in_specs=in_specs,
        out_specs=pl.BlockSpec((None, tq, LANES), o_map),
        out_shape=jax.ShapeDtypeStruct((bsz, lq, MIX_W), BF16),
        scratch_shapes=[pltpu.VMEM((sum(seg_lens), 2 * tq), BF16),
                        pltpu.VMEM((-(-n_chunks // 8) * 8, 2 * tq), F32),
                        pltpu.SMEM((1,), jnp.int32)],
        compiler_params=_cparams(1),
        name="diff",
    )(*args)


WIN_SUB = 4


def _win_kernel(sink_ref, q_ref, *refs, local):
    o_ref = refs[-1]
    n_sub = q_ref.shape[0] // BLOCK
    kx_ref, vxt_ref = refs[-3], refs[-2]
    n = pl.program_id(1)
    nb = pl.num_programs(1)
    nq = C_Q_HEADS // 2 * BLOCK
    lane = lax.broadcasted_iota(jnp.int32, (BLOCK, LANES), 1)
    headi = lax.broadcasted_iota(jnp.int32, (1, nq), 1) // BLOCK
    if local:
        kp_ref, kc_ref, kn_ref, vpt_ref, vct_ref, vnt_ref = refs[:6]
        key = lax.broadcasted_iota(jnp.int32, (BLOCK, nq), 0)
        qry = lax.broadcasted_iota(jnp.int32, (BLOCK, nq), 1) & (BLOCK - 1)
        tri_prev, tri_next = key >= qry, key <= qry

    probs = [(sub, kvh) for sub in range(n_sub) for kvh in range(2)]
    s_l, sink_l, vt_l = [], [], []
    for sub, kvh in probs:
        ksl = slice(kvh * LANES, (kvh + 1) * LANES)
        rows = slice(sub * BLOCK, (sub + 1) * BLOCK)
        parts = []
        for g in range(4):
            grp = kvh * 2 + g // 2
            qg = q_ref[rows, grp * LANES:(grp + 1) * LANES]
            parts.append(jnp.where((lane < 64) if g % 2 == 0 else (lane >= 64), qg, jnp.zeros_like(qg)))
        qs = jnp.concatenate(parts, axis=0)
        sink = jnp.zeros((1, nq), F32)
        for g in range(4):
            sink = jnp.where(headi == g, sink_ref[kvh * 4 + g] * LOG2E, sink)
        if local:
            kb, vb = [], []
            for j in (sub - 1, sub, sub + 1):
                if j < 0:
                    kb.append(kp_ref[:, ksl]), vb.append(vpt_ref[ksl, :])
                elif j >= n_sub:
                    kb.append(kn_ref[:, ksl]), vb.append(vnt_ref[ksl, :])
                else:
                    kb.append(kc_ref[j * BLOCK:(j + 1) * BLOCK, ksl])
                    vb.append(vct_ref[ksl, j * BLOCK:(j + 1) * BLOCK])
            k_all = jnp.concatenate(kb + [kx_ref[:, ksl]], axis=0)
            vt_all = jnp.concatenate(vb + [vxt_ref[ksl, :]], axis=1)
        else:
            k_all, vt_all = kx_ref[:, ksl], vxt_ref[ksl, :]
        s = _dot_nt(k_all, qs)
        if local:
            valid_prev = tri_prev if sub > 0 else tri_prev & (n > 0)
            valid_next = tri_next if sub < n_sub - 1 else tri_next & (n < nb - 1)
            s = jnp.concatenate([jnp.where(valid_prev, s[0:BLOCK], NEG_INF), s[BLOCK:2 * BLOCK],
                                 jnp.where(valid_next, s[2 * BLOCK:3 * BLOCK], NEG_INF), s[3 * BLOCK:]], axis=0)
        s_l.append(s)
        sink_l.append(sink)
        vt_l.append(vt_all)
    m_l = [jnp.maximum(jnp.max(s, axis=0, keepdims=True), sink) for s, sink in zip(s_l, sink_l)]
    p_l = [jnp.exp2(s - m) for s, m in zip(s_l, m_l)]
    l_l = [jnp.sum(p, axis=0, keepdims=True) + jnp.exp2(sink - m) for p, sink, m in zip(p_l, sink_l, m_l)]
    ot_l = [_dot(vt, p.astype(BF16)) / l for vt, p, l in zip(vt_l, p_l, l_l)]
    for (sub, kvh), o_t in zip(probs, ot_l):
        o = o_t.T
        for j in range(2):
            grp = kvh * 2 + j
            o_ref[sub * BLOCK:(sub + 1) * BLOCK, grp * LANES:(grp + 1) * LANES] = jnp.where(
                lane < 64, o[2 * j * BLOCK:(2 * j + 1) * BLOCK], o[(2 * j + 1) * BLOCK:(2 * j + 2) * BLOCK]
            ).astype(BF16)


def _win_call(sinks, q, kd, vdt, kxd, vxdt, *, local):
    bsz, lq, _ = q.shape
    nb = lq // BLOCK
    n_sub = min(WIN_SUB, nb)
    tq = n_sub * BLOCK
    in_specs = [pl.BlockSpec(memory_space=pltpu.SMEM), pl.BlockSpec((None, tq, MIX_W), lambda b, n: (b, n, 0))]
    args = [sinks, q]
    if local:
        prv = lambda n: jnp.maximum(n * n_sub - 1, 0)
        nxt = lambda n: jnp.minimum((n + 1) * n_sub, nb - 1)
        in_specs += [pl.BlockSpec((None, BLOCK, 2 * LANES), lambda b, n: (b, prv(n), 0)),
                     pl.BlockSpec((None, tq, 2 * LANES), lambda b, n: (b, n, 0)),
                     pl.BlockSpec((None, BLOCK, 2 * LANES), lambda b, n: (b, nxt(n), 0)),
                     pl.BlockSpec((None, 2 * LANES, BLOCK), lambda b, n: (b, 0, prv(n))),
                     pl.BlockSpec((None, 2 * LANES, tq), lambda b, n: (b, 0, n)),
                     pl.BlockSpec((None, 2 * LANES, BLOCK), lambda b, n: (b, 0, nxt(n)))]
        args += [kd] * 3 + [vdt] * 3
    n_ctx = kxd.shape[1]
    in_specs += [pl.BlockSpec((None, n_ctx, 2 * LANES), lambda b, n: (b, 0, 0)),
                 pl.BlockSpec((None, 2 * LANES, n_ctx), lambda b, n: (b, 0, 0))]
    args += [kxd, vxdt]
    return pl.pallas_call(
        functools.partial(_win_kernel, local=local),
        grid=(bsz, lq // tq),
        in_specs=in_specs,
        out_specs=pl.BlockSpec((None, tq, MIX_W), lambda b, n: (b, n, 0)),
        out_shape=jax.ShapeDtypeStruct((bsz, lq, MIX_W), BF16),
        compiler_params=_cparams(2),
        name="win",
    )(*args)


def _merge_kernel(x_ref, mul_ref, sh_ref, gate_ref, wuvg_ref, lng_ref, lnb_ref, ws_ref, bs_ref,
                  ob_ref, oc_ref, wbr_ref, wout_ref, o_ref):
    x = x_ref[...]
    tm = x.shape[0]
    h = _rms_mod(x, mul_ref[...], sh_ref[...]).astype(BF16)
    u = _dot(h, wuvg_ref[:, 0:MIX_W])
    v = _dot(h, wuvg_ref[:, MIX_W:2 * MIX_W])
    gates = [_dot(h, wuvg_ref[:, 2 * MIX_W + i * D_MODEL:2 * MIX_W + (i + 1) * D_MODEL]) for i in range(3)]
    t_b = _dot(ob_ref[...], wbr_ref[1])
    t_c = _dot(oc_ref[...], wbr_ref[2])
    u = _gelu_tanh(u)
    v = _gelu_tanh(v)
    mu = jnp.mean(v, axis=-1, keepdims=True)
    vc = v - mu
    var = jnp.mean(vc * vc, axis=-1, keepdims=True)
    vn = (vc * lax.rsqrt(var + EPS) * lng_ref[...] + lnb_ref[...]).astype(BF16)
    n_chunks = tm // BLOCK
    mixed = []
    for g in range(A_GROUPS):
        gs = slice(g * LANES, (g + 1) * LANES)
        rhs = jnp.concatenate([vn[c * BLOCK:(c + 1) * BLOCK, gs] for c in range(n_chunks)], axis=1)
        mixed.append(_dot(ws_ref[g], rhs))
    o_a = (u * jnp.concatenate(
        [jnp.concatenate([mixed[g][:, c * LANES:(c + 1) * LANES] + bs_ref[g] for g in range(A_GROUPS)], axis=1)
         for c in range(n_chunks)], axis=0)).astype(BF16)
    y = (jax.nn.sigmoid(gates[0]) * _dot(o_a, wbr_ref[0]) + jax.nn.sigmoid(gates[1]) * t_b
         + jax.nn.sigmoid(gates[2]) * t_c)
    o_ref[...] = x + gate_ref[...] * _dot(y.astype(BF16), wout_ref[...])


def _merge_call(x, mul, sh, gate, w_uvg, ln_g, ln_b, w_s, b_s, o_b, o_c, w_br, w_out, *, tm):
    bsz, length, _ = x.shape
    tok = lambda w: pl.BlockSpec((None, tm, w), lambda b, i: (b, i, 0))
    vec = pl.BlockSpec((None, 1, D_MODEL), lambda b, i: (b, 0, 0))
    full = _const_spec
    return pl.pallas_call(
        _merge_kernel,
        grid=(bsz, length // tm),
        in_specs=[tok(D_MODEL), vec, vec, vec, full(w_uvg), full(ln_g), full(ln_b), full(w_s), full(b_s),
                  tok(MIX_W), tok(MIX_W), full(w_br), full(w_out)],
        out_specs=tok(D_MODEL),
        out_shape=jax.ShapeDtypeStruct(x.shape, F32),
        compiler_params=_cparams(2),
        name="merge",
    )(x, mul, sh, gate, w_uvg, ln_g, ln_b, w_s, b_s, o_b, o_c, w_br, w_out)


HALO = 8
MXU_TILE = 256
FF_EDGES = (0, 4 * MXU_TILE, 8 * MXU_TILE, D_FF)


def _ffn_kernel(x_ref, xp_ref, xn_ref, mul_ref, sh_ref, gate_ref, wg_ref, cw_ref, cb_ref, wu_ref, wd_ref,
                fg_ref, o_ref, *, final):
    i = pl.program_id(1)
    nt = pl.num_programs(1)
    x = x_ref[...]
    tm = x.shape[0]
    xe = jnp.concatenate([xp_ref[...], x, xn_ref[...]], axis=0)
    he = _rms_mod(xe, mul_ref[...], sh_ref[...]).astype(BF16)
    hm = he[HALO:HALO + tm]
    rowe = lax.broadcasted_iota(jnp.int32, (tm + 2 * HALO, 1), 0)
    keep = ((rowe >= HALO) | (i > 0)) & ((rowe < HALO + tm) | (i < nt - 1))
    spans = [slice(c0, c1) for c0, c1 in zip(FF_EDGES[:-1], FF_EDGES[1:])]
    a_l = [jnp.where(keep, _dot(he, wg_ref[:, cs]), 0.0) for cs in spans]
    u_l = [_dot(hm, wu_ref[:, cs]) for cs in spans]
    z_l = []
    for cs, a, u in zip(spans, a_l, u_l):
        a_prev = pltpu.roll(a, 1, 0)[HALO:HALO + tm]
        a_next = pltpu.roll(a, tm + 2 * HALO - 1, 0)[HALO:HALO + tm]
        a = (a_prev * cw_ref[0:1, cs] + a[HALO:HALO + tm] * cw_ref[1:2, cs] + a_next * cw_ref[2:3, cs]
             + cb_ref[:, cs])
        z_l.append((a * jax.nn.sigmoid(a) * u).astype(BF16))
    acc = None
    for cs, z in zip(spans, z_l):
        t = _dot(z, wd_ref[cs, :])
        acc = t if acc is None else acc + t
    out = x + gate_ref[...] * acc
    if final:
        ms = jnp.mean(out * out, axis=-1, keepdims=True)
        out = out * lax.rsqrt(ms + EPS) * fg_ref[...]
    o_ref[...] = out


def _ffn_call(x, mul, sh, gate, w_gate, conv_w, conv_b, w_up, w_down, final_g, *, tm, final):
    bsz, length, _ = x.shape
    per = tm // HALO
    last = length // HALO - 1
    tok = pl.BlockSpec((None, tm, D_MODEL), lambda b, i: (b, i, 0))
    prv = pl.BlockSpec((None, HALO, D_MODEL), lambda b, i: (b, jnp.maximum(i * per - 1, 0), 0))
    nxt = pl.BlockSpec((None, HALO, D_MODEL), lambda b, i: (b, jnp.minimum((i + 1) * per, last), 0))
    vec = pl.BlockSpec((None, 1, D_MODEL), lambda b, i: (b, 0, 0))
    full = _const_spec
    return pl.pallas_call(
        functools.partial(_ffn_kernel, final=final),
        grid=(bsz, length // tm),
        in_specs=[tok, prv, nxt, vec, vec, vec, full(w_gate), full(conv_w), full(conv_b), full(w_up),
                  full(w_down), full(final_g)],
        out_specs=tok,
        out_shape=jax.ShapeDtypeStruct(x.shape, F32),
        compiler_params=_cparams(2),
        name="ffn",
    )(x, x, x, mul, sh, gate, w_gate, conv_w, conv_b, w_up, w_down, final_g)


def _rope_tables(length):
    pos = jnp.arange(length)
    rows = (pos // GRID_W).astype(F32)
    cols = (pos % GRID_W).astype(F32)
    half = HEAD_DIM // 2
    inv = ROPE_BASE ** (-jnp.arange(0, half, 2, dtype=F32) / half)
    ang_r = rows[:, None] * inv[None, :]
    ang_c = cols[:, None] * inv[None, :]
    zero = jnp.zeros_like(ang_r)
    cos = jnp.concatenate([jnp.cos(ang_r)] * 2 + [jnp.cos(ang_c)] * 2, axis=1)
    sa = jnp.concatenate([-jnp.sin(ang_r), zero, -jnp.sin(ang_c), zero], axis=1)
    sb = jnp.concatenate([zero, jnp.sin(ang_r), zero, jnp.sin(ang_c)], axis=1)
    return tuple(jnp.tile(t, (1, LANES // HEAD_DIM)) for t in (cos, sa, sb))


def kernel(x, c, ctx, c_ctx, w_ada, b_ada, norm1_g, w_in, sgu_ln_g, sgu_ln_b, w_s, b_s, lam_q1, lam_k1,
           lam_q2, lam_k2, diff_subln_g, sinks, w_branch, w_out, norm2_g, w_gate, conv_w, conv_b, w_up,
           w_down, final_g):
    bsz, length, _ = x.shape
    n_ctx = ctx.shape[1]
    tables = _rope_tables(length)
    ctx_tables = tuple(t[:n_ctx] for t in tables)

    rows = -(-(bsz + 1) // 8) * 8
    vpad = jnp.zeros((rows, D_MODEL), F32).at[:bsz].set(c).at[bsz].set(c_ctx)
    ada = _ada_call(vpad, w_ada, b_ada)

    qscale = jnp.ones((ATTN_COLS,), F32).at[0:512].set(SCALE * LOG2E).at[1536:2048].set(SCALE * LOG2E)
    w_attn = (w_in[:, :, 2 * MIX_W:2 * MIX_W + ATTN_COLS] * qscale).astype(BF16)
    w_uvg = jnp.concatenate([w_in[:, :, :2 * MIX_W], w_in[:, :, GATE_COL0:]], axis=2).astype(BF16)
    w_s_b = w_s.astype(BF16)
    b_s_b = jnp.broadcast_to(b_s[..., None], b_s.shape + (LANES,))
    w_br_b = w_branch.astype(BF16)
    w_out_b = w_out.astype(BF16)
    w_gate_b = w_gate.astype(BF16)
    w_up_b = w_up.astype(BF16)
    w_down_b = w_down.astype(BF16)
    final_g2 = final_g.reshape(1, D_MODEL)

    for l in range(DEPTH):
        last = l == DEPTH - 1
        m = ada[l]
        sh1, sc1, g1, sh2, sc2, g2 = [m[:, j * D_MODEL:(j + 1) * D_MODEL] for j in range(6)]
        mul1 = norm1_g[l][None, :] * (1.0 + sc1)
        mul2 = norm2_g[l][None, :] * (1.0 + sc2)
        lat = lambda a: a[:bsz, None, :]
        cx = lambda a: jnp.broadcast_to(a[bsz][None, None, :], (bsz, 1, D_MODEL))

        lam_init = 0.8 - 0.6 * float(np.exp(-0.3 * l))
        lam = jnp.exp(jnp.sum(lam_q1[l] * lam_k1[l])) - jnp.exp(jnp.sum(lam_q2[l] * lam_k2[l])) + lam_init
        scal = jnp.stack([lam, jnp.float32(1.0 - lam_init)]).astype(F32)
        subln = diff_subln_g[l].reshape(1, LANES)
        ln_g = sgu_ln_g[l].reshape(1, MIX_W)
        ln_b = sgu_ln_b[l].reshape(1, MIX_W)
        cb2 = conv_b[l].reshape(1, D_FF)

        bq, bk, bvt, cq, ckd, cvdt = _inproj_call(x, lat(mul1), lat(sh1), w_attn[l], tables, tm=512, rope=True)
        xbq, xbk, xbvt, xcq, xckd, xcvdt = _inproj_call(ctx, cx(mul1), cx(sh1), w_attn[l], ctx_tables,
                                                        tm=n_ctx, rope=False)

        o_b = _diff_call(scal, bq, subln, [(bk, bvt), (xbk, xbvt)], tq=512, kc=512)
        o_c = _win_call(sinks[l], cq, ckd, cvdt, xckd, xcvdt, local=True)
        x_mid = _merge_call(x, lat(mul1), lat(sh1), lat(g1), w_uvg[l], ln_g, ln_b, w_s_b[l], b_s_b[l],
                            o_b, o_c, w_br_b[l], w_out_b[l], tm=512)
        if not last:
            xo_b = _diff_call(scal, xbq, subln, [(xbk, xbvt)], tq=n_ctx, kc=512)
            xo_c = _win_call(sinks[l], xcq, None, None, xckd, xcvdt, local=False)
            ctx_mid = _merge_call(ctx, cx(mul1), cx(sh1), cx(g1), w_uvg[l], ln_g, ln_b, w_s_b[l], b_s_b[l],
                                  xo_b, xo_c, w_br_b[l], w_out_b[l], tm=n_ctx)
        x = _ffn_call(x_mid, lat(mul2), lat(sh2), lat(g2), w_gate_b[l], conv_w[l], cb2, w_up_b[l],
                      w_down_b[l], final_g2, tm=512, final=last)
        if not last:
            ctx = _ffn_call(ctx_mid, cx(mul2), cx(sh2), cx(g2), w_gate_b[l], conv_w[l], cb2, w_up_b[l],
                            w_down_b[l], final_g2, tm=n_ctx, final=False)
    return x
```

```python
import functools

import numpy as np
import jax
import jax.numpy as jnp
from jax import lax
from jax.experimental import pallas as pl
from jax.experimental.pallas import tpu as pltpu

F32 = jnp.float32
BF16 = jnp.bfloat16

D_MODEL = 1024
DEPTH = 4
GRID_W = 64
HEAD_DIM = 64
BLOCK = 128
MIX_W = 512
A_GROUPS = 4
B_HEADS = 4
C_Q_HEADS = 8
D_FF = 2816
ROPE_BASE = 10000.0
EPS = 1e-6
SCALE = HEAD_DIM ** -0.5
NEG_INF = -1e30
LOG2E = float(np.log2(np.e))

LANES = 128
ATTN_COLS = 2304
GATE_COL0 = 3328
VMEM_LIMIT = 56 * 1024 * 1024


def _cparams(n_axes):
    return pltpu.CompilerParams(
        dimension_semantics=("arbitrary",) * n_axes, vmem_limit_bytes=VMEM_LIMIT)


def _const_spec(a):
    return pl.BlockSpec(a.shape, lambda *_: (0,) * a.ndim, pipeline_mode=pl.Buffered(1))


def _layer_spec(a, layer):
    return pl.BlockSpec((None,) + a.shape[1:], lambda *_: (layer,) + (0,) * (a.ndim - 1),
                        pipeline_mode=pl.Buffered(1))


def _rms_mod(x, mul, sh):
    ms = jnp.mean(x * x, axis=-1, keepdims=True)
    return x * lax.rsqrt(ms + EPS) * mul + sh


def _gelu_tanh(x):
    c = np.float32(np.sqrt(2.0 / np.pi))
    return 0.5 * x * (1.0 + jnp.tanh(c * (x + 0.044715 * (x * x * x))))


def _dot(a, b):
    return jnp.dot(a, b, preferred_element_type=F32)


def _dot_nt(a, b):
    return lax.dot_general(a, b, (((1,), (1,)), ((), ())), preferred_element_type=F32)


def _ada_kernel(v_ref, w_ref, b_ref, o_ref):
    v = v_ref[...]
    s = v * jax.nn.sigmoid(v)
    o_ref[...] = _dot(s.astype(BF16), w_ref[...].astype(BF16)) + b_ref[...]


def _ada_call(vpad, w_ada, b_ada):
    rows = vpad.shape[0]
    tn = 1536
    return pl.pallas_call(
        _ada_kernel,
        grid=(DEPTH, 6 * D_MODEL // tn),
        in_specs=[
            pl.BlockSpec((rows, D_MODEL), lambda l, j: (0, 0)),
            pl.BlockSpec((None, D_MODEL, tn), lambda l, j: (l, 0, j)),
            pl.BlockSpec((None, 1, tn), lambda l, j: (l, 0, j)),
        ],
        out_specs=pl.BlockSpec((None, rows, tn), lambda l, j: (l, 0, j)),
        out_shape=jax.ShapeDtypeStruct((DEPTH, rows, 6 * D_MODEL), F32),
        compiler_params=_cparams(2),
        name="ada",
    )(vpad, w_ada, b_ada.reshape(DEPTH, 1, 6 * D_MODEL))


def _inproj_kernel(x_ref, mul_ref, sh_ref, w_ref, cos_ref, sa_ref, sb_ref,
                   bq_ref, bk_ref, bvt_ref, cq_ref, ckd_ref, cvdt_ref, *, rope):
    h = _rms_mod(x_ref[...], mul_ref[...], sh_ref[...]).astype(BF16)
    if rope:
        cos, sa, sb = cos_ref[...], sa_ref[...], sb_ref[...]

    def rot(y):
        if not rope:
            return y
        return y * cos + pltpu.roll(y, LANES - 16, 1) * sa + pltpu.roll(y, 16, 1) * sb

    def dup(y):
        lane = lax.broadcasted_iota(jnp.int32, y.shape, 1)
        sw = pltpu.roll(y, 64, 1)
        return jnp.concatenate([jnp.where(lane < 64, y, sw), jnp.where(lane < 64, sw, y)], axis=1)

    for col0, ref in ((0, bq_ref), (512, bk_ref), (1536, cq_ref)):
        y = _dot(h, w_ref[:, col0:col0 + 512])
        for g in range(4):
            ref[:, g * LANES:(g + 1) * LANES] = rot(y[:, g * LANES:(g + 1) * LANES]).astype(BF16)
    ykv = _dot(h, w_ref[:, 2048:2304])
    ckd_ref[...] = dup(rot(ykv[:, :LANES])).astype(BF16)
    cvdt_ref[...] = dup(ykv[:, LANES:]).T.astype(BF16)
    bvt_ref[...] = _dot(h, w_ref[:, 1024:1536]).T.astype(BF16)


def _inproj_call(x, mul, sh, w_attn, tables, *, layer, tm, rope):
    bsz, length, _ = x.shape
    tok = lambda w: pl.BlockSpec((None, tm, w), lambda b, i: (b, i, 0))
    vec = pl.BlockSpec((None, 1, D_MODEL), lambda b, i: (b, 0, 0))
    tab = pl.BlockSpec((tm, LANES), lambda b, i: (i, 0))
    tok_t = lambda w: pl.BlockSpec((None, w, tm), lambda b, i: (b, 0, i))
    out = lambda w: jax.ShapeDtypeStruct((bsz, length, w), BF16)
    out_t = lambda w: jax.ShapeDtypeStruct((bsz, w, length), BF16)
    return pl.pallas_call(
        functools.partial(_inproj_kernel, rope=rope),
        grid=(bsz, length // tm),
        in_specs=[tok(D_MODEL), vec, vec,
                  _layer_spec(w_attn, layer), tab, tab, tab],
        out_specs=[tok(512), tok(512), tok_t(MIX_W), tok(512), tok(256), tok_t(2 * LANES)],
        out_shape=[out(512), out(512), out_t(MIX_W), out(512), out(256), out_t(2 * LANES)],
        compiler_params=_cparams(2),
        name="inproj",
    )(x, mul, sh, w_attn, *tables)


SAFE_EXP2_RANGE = 90.0


def _diff_kernel(sc_ref, q_ref, qall_ref, g_ref, ind_ref, *refs, seg_lens, kc, nq):
    o_ref, p_scr, f_scr, safe_scr = refs[-4:]
    t = pl.program_id(0)
    tq = q_ref.shape[0]
    lanei = lax.broadcasted_iota(jnp.int32, (1, 2 * tq), 1)

    @pl.when(t == 0)
    def _():
        p_scr[...] = jnp.zeros(p_scr.shape, BF16)
        f_scr[...] = jnp.zeros(f_scr.shape, F32)

    chunks = []
    row0 = 0
    for si, n_keys in enumerate(seg_lens):
        for st in range(0, n_keys, kc):
            size = min(kc, n_keys - st)
            chunks.append((refs[2 * si], refs[2 * si + 1], st, size, row0))
            row0 += size

    def max_sq_norm(x):
        xf = x.astype(F32)
        return jnp.max(_dot((xf * xf).astype(BF16), ind_ref[...]), axis=0, keepdims=True)

    @pl.when(t % nq == 0)
    def _():
        kn = None
        for k_ref, _, st, size, _ in chunks:
            n2 = max_sq_norm(k_ref[st:st + size, :])
            kn = n2 if kn is None else jnp.maximum(kn, n2)
        qn = None
        n_q = qall_ref.shape[0]
        for st in range(0, n_q, 2 * kc):
            n2 = max_sq_norm(qall_ref[st:min(st + 2 * kc, n_q), :])
            qn = n2 if qn is None else jnp.maximum(qn, n2)
        bound2 = jnp.max(kn * qn)
        safe_scr[0] = (bound2 < SAFE_EXP2_RANGE * SAFE_EXP2_RANGE).astype(jnp.int32)

    safe = safe_scr[0] == 1

    def step(max_free):
        q = q_ref[...]
        lane = lax.broadcasted_iota(jnp.int32, q.shape, 1)
        zero = jnp.zeros_like(q)
        qq = jnp.concatenate([jnp.where(lane < 64, q, zero), jnp.where(lane >= 64, q, zero)], axis=0)
        f_prev = f_scr[...]
        mcs, lcs = [], []
        acc = None
        for c, (k_ref, vt_ref, st, size, r0) in enumerate(chunks):
            f1 = f_prev[c:c + 1, 0:tq].astype(BF16)
            f2 = f_prev[c:c + 1, tq:2 * tq].astype(BF16)
            a_t = p_scr[r0:r0 + size, 0:tq] * f1 - p_scr[r0:r0 + size, tq:2 * tq] * f2
            part = _dot(vt_ref[:, st:st + size], a_t)
            acc = part if acc is None else acc + part
            s = _dot_nt(k_ref[st:st + size, :], qq)
            if max_free:
                p = jnp.exp2(s)
            else:
                mc = jnp.max(s, axis=0, keepdims=True)
                p = jnp.exp2(s - mc)
                mcs.append(mc)
            lcs.append(jnp.sum(p, axis=0, keepdims=True))
            p_scr[r0:r0 + size, :] = p.astype(BF16)

        if max_free:
            es = [None] * len(chunks)
            l_all = functools.reduce(lambda a, b: a + b, lcs)
        else:
            m_all = functools.reduce(jnp.maximum, mcs)
            es = [jnp.exp2(mc - m_all) for mc in mcs]
            l_all = functools.reduce(lambda a, b: a + b, [lc * e for lc, e in zip(lcs, es)])
        coef = jnp.where(lanei < tq, 1.0, sc_ref[0]) / l_all
        for c, e in enumerate(es):
            f_scr[c:c + 1, :] = coef if e is None else e * coef

        o = acc.T
        ms = jnp.mean(o * o, axis=-1, keepdims=True)
        o_ref[...] = (o * lax.rsqrt(ms + EPS) * g_ref[...] * sc_ref[1]).astype(BF16)

    pl.when(safe)(lambda: step(True))
    pl.when(jnp.logical_not(safe))(lambda: step(False))


def _diff_call(scal, q, subln_g, segs, *, tq, kc):
    bsz, lq, _ = q.shape
    nq = lq // tq
    n_blocks = bsz * B_HEADS * nq
    half = np.arange(LANES) // HEAD_DIM
    ind = jnp.asarray(half[:, None] == half[None, :], BF16)

    def block(t):
        b, r = t // (B_HEADS * nq), t % (B_HEADS * nq)
        return b, r // nq, r % nq

    cur = lambda t: block(jnp.minimum(t, n_blocks - 1))
    prev = lambda t: block(jnp.maximum(t - 1, 0))

    def q_map(t):
        b, h, i = cur(t)
        return b, i, h

    def kq_map(t):
        b, h, _ = cur(t)
        return b, 0, h

    def vt_map(t):
        b, h, _ = prev(t)
        return b, h, 0

    def o_map(t):
        b, h, i = prev(t)
        return b, i, h

    in_specs = [pl.BlockSpec(memory_space=pltpu.SMEM),
                pl.BlockSpec((None, tq, LANES), q_map),
                pl.BlockSpec((None, lq, LANES), kq_map),
                pl.BlockSpec((1, LANES), lambda t: (0, 0)),
                _const_spec(ind)]
    args = [scal, q, q, subln_g, ind]
    for k, vt in segs:
        n_keys = k.shape[1]
        in_specs += [pl.BlockSpec((None, n_keys, LANES), kq_map),
                     pl.BlockSpec((None, LANES, n_keys), vt_map)]
        args += [k, vt]
    seg_lens = tuple(k.shape[1] for k, _ in segs)
    n_chunks = sum(-(-n // kc) for n in seg_lens)
    return pl.pallas_call(
        functools.partial(_diff_kernel, seg_lens=seg_lens, kc=kc, nq=nq),
        grid=(n_blocks + 1,),
        in_specs=in_specs,
        out_specs=pl.BlockSpec((None, tq, LANES), o_map),
        out_shape=jax.ShapeDtypeStruct((bsz, lq, MIX_W), BF16),
        scratch_shapes=[pltpu.VMEM((sum(seg_lens), 2 * tq), BF16),
                        pltpu.VMEM((-(-n_chunks // 8) * 8, 2 * tq), F32),
                        pltpu.SMEM((1,), jnp.int32)],
        compiler_params=_cparams(1),
        name="diff",
    )(*args)


WIN_SUB = 4


def _win_kernel(sink_ref, q_ref, *refs, local):
    o_ref = refs[-1]
    n_sub = q_ref.shape[0] // BLOCK
    kx_ref, vxt_ref = refs[-3], refs[-2]
    n = pl.program_id(1)
    nb = pl.num_programs(1)
    nq = C_Q_HEADS // 2 * BLOCK
    lane = lax.broadcasted_iota(jnp.int32, (BLOCK, LANES), 1)
    headi = lax.broadcasted_iota(jnp.int32, (1, nq), 1) // BLOCK
    if local:
        kp_ref, kc_ref, kn_ref, vpt_ref, vct_ref, vnt_ref = refs[:6]
        key = lax.broadcasted_iota(jnp.int32, (BLOCK, nq), 0)
        qry = lax.broadcasted_iota(jnp.int32, (BLOCK, nq), 1) & (BLOCK - 1)
        tri_prev, tri_next = key >= qry, key <= qry

    probs = [(sub, kvh) for sub in range(n_sub) for kvh in range(2)]
    s_l, sink_l, vt_l = [], [], []
    for sub, kvh in probs:
        ksl = slice(kvh * LANES, (kvh + 1) * LANES)
        rows = slice(sub * BLOCK, (sub + 1) * BLOCK)
        parts = []
        for g in range(4):
            grp = kvh * 2 + g // 2
            qg = q_ref[rows, grp * LANES:(grp + 1) * LANES]
            parts.append(jnp.where((lane < 64) if g % 2 == 0 else (lane >= 64), qg, jnp.zeros_like(qg)))
        qs = jnp.concatenate(parts, axis=0)
        sink = jnp.zeros((1, nq), F32)
        for g in range(4):
            sink = jnp.where(headi == g, sink_ref[kvh * 4 + g] * LOG2E, sink)
        if local:
            kb, vb = [], []
            for j in (sub - 1, sub, sub + 1):
                if j < 0:
                    kb.append(kp_ref[:, ksl]), vb.append(vpt_ref[ksl, :])
                elif j >= n_sub:
                    kb.append(kn_ref[:, ksl]), vb.append(vnt_ref[ksl, :])
                else:
                    kb.append(kc_ref[j * BLOCK:(j + 1) * BLOCK, ksl])
                    vb.append(vct_ref[ksl, j * BLOCK:(j + 1) * BLOCK])
            k_all = jnp.concatenate(kb + [kx_ref[:, ksl]], axis=0)
            vt_all = jnp.concatenate(vb + [vxt_ref[ksl, :]], axis=1)
        else:
            k_all, vt_all = kx_ref[:, ksl], vxt_ref[ksl, :]
        s = _dot_nt(k_all, qs)
        if local:
            valid_prev = tri_prev if sub > 0 else tri_prev & (n > 0)
            valid_next = tri_next if sub < n_sub - 1 else tri_next & (n < nb - 1)
            s = jnp.concatenate([jnp.where(valid_prev, s[0:BLOCK], NEG_INF), s[BLOCK:2 * BLOCK],
                                 jnp.where(valid_next, s[2 * BLOCK:3 * BLOCK], NEG_INF), s[3 * BLOCK:]], axis=0)
        s_l.append(s)
        sink_l.append(sink)
        vt_l.append(vt_all)
    m_l = [jnp.maximum(jnp.max(s, axis=0, keepdims=True), sink) for s, sink in zip(s_l, sink_l)]
    p_l = [jnp.exp2(s - m) for s, m in zip(s_l, m_l)]
    l_l = [jnp.sum(p, axis=0, keepdims=True) + jnp.exp2(sink - m) for p, sink, m in zip(p_l, sink_l, m_l)]
    ot_l = [_dot(vt, p.astype(BF16)) / l for vt, p, l in zip(vt_l, p_l, l_l)]
    for (sub, kvh), o_t in zip(probs, ot_l):
        o = o_t.T
        for j in range(2):
            grp = kvh * 2 + j
            o_ref[sub * BLOCK:(sub + 1) * BLOCK, grp * LANES:(grp + 1) * LANES] = jnp.where(
                lane < 64, o[2 * j * BLOCK:(2 * j + 1) * BLOCK], o[(2 * j + 1) * BLOCK:(2 * j + 2) * BLOCK]
            ).astype(BF16)


def _win_call(sinks, q, kd, vdt, kxd, vxdt, *, local):
    bsz, lq, _ = q.shape
    nb = lq // BLOCK
    n_sub = min(WIN_SUB, nb)
    tq = n_sub * BLOCK
    in_specs = [pl.BlockSpec(memory_space=pltpu.SMEM), pl.BlockSpec((None, tq, MIX_W), lambda b, n: (b, n, 0))]
    args = [sinks, q]
    if local:
        prv = lambda n: jnp.maximum(n * n_sub - 1, 0)
        nxt = lambda n: jnp.minimum((n + 1) * n_sub, nb - 1)
        in_specs += [pl.BlockSpec((None, BLOCK, 2 * LANES), lambda b, n: (b, prv(n), 0)),
                     pl.BlockSpec((None, tq, 2 * LANES), lambda b, n: (b, n, 0)),
                     pl.BlockSpec((None, BLOCK, 2 * LANES), lambda b, n: (b, nxt(n), 0)),
                     pl.BlockSpec((None, 2 * LANES, BLOCK), lambda b, n: (b, 0, prv(n))),
                     pl.BlockSpec((None, 2 * LANES, tq), lambda b, n: (b, 0, n)),
                     pl.BlockSpec((None, 2 * LANES, BLOCK), lambda b, n: (b, 0, nxt(n)))]
        args += [kd] * 3 + [vdt] * 3
    n_ctx = kxd.shape[1]
    in_specs += [pl.BlockSpec((None, n_ctx, 2 * LANES), lambda b, n: (b, 0, 0)),
                 pl.BlockSpec((None, 2 * LANES, n_ctx), lambda b, n: (b, 0, 0))]
    args += [kxd, vxdt]
    return pl.pallas_call(
        functools.partial(_win_kernel, local=local),
        grid=(bsz, lq // tq),
        in_specs=in_specs,
        out_specs=pl.BlockSpec((None, tq, MIX_W), lambda b, n: (b, n, 0)),
        out_shape=jax.ShapeDtypeStruct((bsz, lq, MIX_W), BF16),
        compiler_params=_cparams(2),
        name="win",
    )(*args)


def _merge_kernel(x_ref, mul_ref, sh_ref, gate_ref, wuvg_ref, lng_ref, lnb_ref, ws_ref, bs_ref,
                  ob_ref, oc_ref, wbr_ref, wout_ref, o_ref):
    x = x_ref[...]
    tm = x.shape[0]
    h = _rms_mod(x, mul_ref[...], sh_ref[...]).astype(BF16)
    u = _gelu_tanh(_dot(h, wuvg_ref[:, 0:MIX_W]))
    v = _gelu_tanh(_dot(h, wuvg_ref[:, MIX_W:2 * MIX_W]))
    mu = jnp.mean(v, axis=-1, keepdims=True)
    vc = v - mu
    var = jnp.mean(vc * vc, axis=-1, keepdims=True)
    vn = (vc * lax.rsqrt(var + EPS) * lng_ref[...] + lnb_ref[...]).astype(BF16)
    n_chunks = tm // BLOCK
    mixed = []
    for g in range(A_GROUPS):
        gs = slice(g * LANES, (g + 1) * LANES)
        rhs = jnp.concatenate([vn[c * BLOCK:(c + 1) * BLOCK, gs] for c in range(n_chunks)], axis=1)
        mixed.append(_dot(ws_ref[g], rhs))
    o_a = (u * jnp.concatenate(
        [jnp.concatenate([mixed[g][:, c * LANES:(c + 1) * LANES] + bs_ref[g] for g in range(A_GROUPS)], axis=1)
         for c in range(n_chunks)], axis=0)).astype(BF16)
    y = None
    for i, o_i in enumerate((o_a, ob_ref[...], oc_ref[...])):
        col0 = 2 * MIX_W + i * D_MODEL
        gate = jax.nn.sigmoid(_dot(h, wuvg_ref[:, col0:col0 + D_MODEL]))
        t = gate * _dot(o_i, wbr_ref[i])
        y = t if y is None else y + t
    o_ref[...] = x + gate_ref[...] * _dot(y.astype(BF16), wout_ref[...])


def _merge_call(x, mul, sh, gate, w_uvg, ln_g, ln_b, w_s, b_s, o_b, o_c, w_br, w_out, *, layer, tm):
    bsz, length, _ = x.shape
    tok = lambda w: pl.BlockSpec((None, tm, w), lambda b, i: (b, i, 0))
    vec = pl.BlockSpec((None, 1, D_MODEL), lambda b, i: (b, 0, 0))
    full = functools.partial(_layer_spec, layer=layer)
    return pl.pallas_call(
        _merge_kernel,
        grid=(bsz, length // tm),
        in_specs=[tok(D_MODEL), vec, vec, vec, full(w_uvg), full(ln_g), full(ln_b), full(w_s), full(b_s),
                  tok(MIX_W), tok(MIX_W), full(w_br), full(w_out)],
        out_specs=tok(D_MODEL),
        out_shape=jax.ShapeDtypeStruct(x.shape, F32),
        compiler_params=_cparams(2),
        name="merge",
    )(x, mul, sh, gate, w_uvg, ln_g, ln_b, w_s, b_s, o_b, o_c, w_br, w_out)


HALO = 8
MXU_TILE = 256
FF_EDGES = (0, 4 * MXU_TILE, 8 * MXU_TILE, D_FF)


def _ffn_kernel(x_ref, xp_ref, xn_ref, mul_ref, sh_ref, gate_ref, wg_ref, cw_ref, cb_ref, wu_ref, wd_ref,
                fg_ref, o_ref, *, final):
    i = pl.program_id(1)
    nt = pl.num_programs(1)
    x = x_ref[...]
    tm = x.shape[0]
    xe = jnp.concatenate([xp_ref[...], x, xn_ref[...]], axis=0)
    he = _rms_mod(xe, mul_ref[...], sh_ref[...]).astype(BF16)
    hm = he[HALO:HALO + tm]
    rowe = lax.broadcasted_iota(jnp.int32, (tm + 2 * HALO, 1), 0)
    keep = ((rowe >= HALO) | (i > 0)) & ((rowe < HALO + tm) | (i < nt - 1))
    spans = [slice(c0, c1) for c0, c1 in zip(FF_EDGES[:-1], FF_EDGES[1:])]
    a_l = [jnp.where(keep, _dot(he, wg_ref[:, cs]), 0.0) for cs in spans]
    u_l = [_dot(hm, wu_ref[:, cs]) for cs in spans]
    z_l = []
    for cs, a, u in zip(spans, a_l, u_l):
        a_prev = pltpu.roll(a, 1, 0)[HALO:HALO + tm]
        a_next = pltpu.roll(a, tm + 2 * HALO - 1, 0)[HALO:HALO + tm]
        a = (a_prev * cw_ref[0:1, cs] + a[HALO:HALO + tm] * cw_ref[1:2, cs] + a_next * cw_ref[2:3, cs]
             + cb_ref[:, cs])
        z_l.append((a * jax.nn.sigmoid(a) * u).astype(BF16))
    acc = None
    for cs, z in zip(spans, z_l):
        t = _dot(z, wd_ref[cs, :])
        acc = t if acc is None else acc + t
    out = x + gate_ref[...] * acc
    if final:
        ms = jnp.mean(out * out, axis=-1, keepdims=True)
        out = out * lax.rsqrt(ms + EPS) * fg_ref[...]
    o_ref[...] = out


def _ffn_call(x, mul, sh, gate, w_gate, conv_w, conv_b, w_up, w_down, final_g, *, layer, tm, final):
    bsz, length, _ = x.shape
    per = tm // HALO
    last = length // HALO - 1
    tok = pl.BlockSpec((None, tm, D_MODEL), lambda b, i: (b, i, 0))
    prv = pl.BlockSpec((None, HALO, D_MODEL), lambda b, i: (b, jnp.maximum(i * per - 1, 0), 0))
    nxt = pl.BlockSpec((None, HALO, D_MODEL), lambda b, i: (b, jnp.minimum((i + 1) * per, last), 0))
    vec = pl.BlockSpec((None, 1, D_MODEL), lambda b, i: (b, 0, 0))
    full = functools.partial(_layer_spec, layer=layer)
    return pl.pallas_call(
        functools.partial(_ffn_kernel, final=final),
        grid=(bsz, length // tm),
        in_specs=[tok, prv, nxt, vec, vec, vec, full(w_gate), full(conv_w), full(conv_b), full(w_up),
                  full(w_down), _const_spec(final_g)],
        out_specs=tok,
        out_shape=jax.ShapeDtypeStruct(x.shape, F32),
        compiler_params=_cparams(2),
        name="ffn",
    )(x, x, x, mul, sh, gate, w_gate, conv_w, conv_b, w_up, w_down, final_g)


def _rope_tables(length):
    pos = jnp.arange(length)
    rows = (pos // GRID_W).astype(F32)
    cols = (pos % GRID_W).astype(F32)
    half = HEAD_DIM // 2
    inv = ROPE_BASE ** (-jnp.arange(0, half, 2, dtype=F32) / half)
    ang_r = rows[:, None] * inv[None, :]
    ang_c = cols[:, None] * inv[None, :]
    zero = jnp.zeros_like(ang_r)
    cos = jnp.concatenate([jnp.cos(ang_r)] * 2 + [jnp.cos(ang_c)] * 2, axis=1)
    sa = jnp.concatenate([-jnp.sin(ang_r), zero, -jnp.sin(ang_c), zero], axis=1)
    sb = jnp.concatenate([zero, jnp.sin(ang_r), zero, jnp.sin(ang_c)], axis=1)
    return tuple(jnp.tile(t, (1, LANES // HEAD_DIM)) for t in (cos, sa, sb))


def kernel(x, c, ctx, c_ctx, w_ada, b_ada, norm1_g, w_in, sgu_ln_g, sgu_ln_b, w_s, b_s, lam_q1, lam_k1,
           lam_q2, lam_k2, diff_subln_g, sinks, w_branch, w_out, norm2_g, w_gate, conv_w, conv_b, w_up,
           w_down, final_g):
    bsz, length, _ = x.shape
    n_ctx = ctx.shape[1]
    tables = _rope_tables(length)
    ctx_tables = tuple(t[:n_ctx] for t in tables)

    rows = -(-(bsz + 1) // 8) * 8
    vpad = jnp.zeros((rows, D_MODEL), F32).at[:bsz].set(c).at[bsz].set(c_ctx)
    ada = _ada_call(vpad, w_ada, b_ada)

    qscale = jnp.ones((ATTN_COLS,), F32).at[0:512].set(SCALE * LOG2E).at[1536:2048].set(SCALE * LOG2E)
    w_attn = (w_in[:, :, 2 * MIX_W:2 * MIX_W + ATTN_COLS] * qscale).astype(BF16)
    w_uvg = jnp.concatenate([w_in[:, :, :2 * MIX_W], w_in[:, :, GATE_COL0:]], axis=2).astype(BF16)
    w_s_b = w_s.astype(BF16)
    b_s_b = jnp.broadcast_to(b_s[..., None], b_s.shape + (LANES,))
    w_br_b = w_branch.astype(BF16)
    w_out_b = w_out.astype(BF16)
    w_gate_b = w_gate.astype(BF16)
    w_up_b = w_up.astype(BF16)
    w_down_b = w_down.astype(BF16)
    final_g2 = final_g.reshape(1, D_MODEL)
    mix_w = (w_uvg, sgu_ln_g.reshape(DEPTH, 1, MIX_W), sgu_ln_b.reshape(DEPTH, 1, MIX_W), w_s_b, b_s_b)
    ffn_w = (w_gate_b, conv_w, conv_b.reshape(DEPTH, 1, D_FF), w_up_b, w_down_b, final_g2)

    for l in range(DEPTH):
        last = l == DEPTH - 1
        m = ada[l]
        sh1, sc1, g1, sh2, sc2, g2 = [m[:, j * D_MODEL:(j + 1) * D_MODEL] for j in range(6)]
        mul1 = norm1_g[l][None, :] * (1.0 + sc1)
        mul2 = norm2_g[l][None, :] * (1.0 + sc2)
        lat = lambda a: a[:bsz, None, :]
        cx = lambda a: jnp.broadcast_to(a[bsz][None, None, :], (bsz, 1, D_MODEL))

        lam_init = 0.8 - 0.6 * float(np.exp(-0.3 * l))
        lam = jnp.exp(jnp.sum(lam_q1[l] * lam_k1[l])) - jnp.exp(jnp.sum(lam_q2[l] * lam_k2[l])) + lam_init
        scal = jnp.stack([lam, jnp.float32(1.0 - lam_init)]).astype(F32)
        subln = diff_subln_g[l].reshape(1, LANES)

        bq, bk, bvt, cq, ckd, cvdt = _inproj_call(x, lat(mul1), lat(sh1), w_attn, tables, layer=l, tm=512,
                                                  rope=True)
        xbq, xbk, xbvt, xcq, xckd, xcvdt = _inproj_call(ctx, cx(mul1), cx(sh1), w_attn, ctx_tables, layer=l,
                                                        tm=n_ctx, rope=False)

        o_b = _diff_call(scal, bq, subln, [(bk, bvt), (xbk, xbvt)], tq=512, kc=512)
        o_c = _win_call(sinks[l], cq, ckd, cvdt, xckd, xcvdt, local=True)
        x_mid = _merge_call(x, lat(mul1), lat(sh1), lat(g1), *mix_w, o_b, o_c, w_br_b, w_out_b, layer=l, tm=512)
        if not last:
            xo_b = _diff_call(scal, xbq, subln, [(xbk, xbvt)], tq=n_ctx, kc=512)
            xo_c = _win_call(sinks[l], xcq, None, None, xckd, xcvdt, local=False)
            ctx_mid = _merge_call(ctx, cx(mul1), cx(sh1), cx(g1), *mix_w, xo_b, xo_c, w_br_b, w_out_b, layer=l,
                                  tm=n_ctx)
        x = _ffn_call(x_mid, lat(mul2), lat(sh2), lat(g2), *ffn_w, layer=l, tm=512, final=last)
        if not last:
            ctx = _ffn_call(ctx_mid, cx(mul2), cx(sh2), cx(g2), *ffn_w, layer=l, tm=n_ctx, final=False)
    return x
```

```python
import functools

import numpy as np
import jax
import jax.numpy as jnp
from jax import lax
from jax.experimental import pallas as pl
from jax.experimental.pallas import tpu as pltpu

F32 = jnp.float32
BF16 = jnp.bfloat16

D_MODEL = 1024
DEPTH = 4
GRID_W = 64
HEAD_DIM = 64
BLOCK = 128
MIX_W = 512
A_GROUPS = 4
B_HEADS = 4
C_Q_HEADS = 8
D_FF = 2816
ROPE_BASE = 10000.0
EPS = 1e-6
SCALE = HEAD_DIM ** -0.5
NEG_INF = -1e30
LOG2E = float(np.log2(np.e))

LANES = 128
ATTN_COLS = 2304
GATE_COL0 = 3328
VMEM_LIMIT = 56 * 1024 * 1024


def _cparams(n_axes):
    return pltpu.CompilerParams(
        dimension_semantics=("arbitrary",) * n_axes, vmem_limit_bytes=VMEM_LIMIT)


def _const_spec(a):
    return pl.BlockSpec(a.shape, lambda *_: (0,) * a.ndim, pipeline_mode=pl.Buffered(1))


def _layer_spec(a, layer):
    return pl.BlockSpec((None,) + a.shape[1:], lambda *_: (layer,) + (0,) * (a.ndim - 1),
                        pipeline_mode=pl.Buffered(1))


def _rms_mod(x, mul, sh):
    ms = jnp.mean(x * x, axis=-1, keepdims=True)
    return x * lax.rsqrt(ms + EPS) * mul + sh


def _gelu_tanh(x):
    c = np.float32(np.sqrt(2.0 / np.pi))
    return 0.5 * x * (1.0 + jnp.tanh(c * (x + 0.044715 * (x * x * x))))


def _dot(a, b):
    return jnp.dot(a, b, preferred_element_type=F32)


def _dot_nt(a, b):
    return lax.dot_general(a, b, (((1,), (1,)), ((), ())), preferred_element_type=F32)


def _ada_kernel(v_ref, w_ref, b_ref, o_ref):
    v = v_ref[...]
    s = v * jax.nn.sigmoid(v)
    o_ref[...] = _dot(s.astype(BF16), w_ref[...].astype(BF16)) + b_ref[...]


def _ada_call(vpad, w_ada, b_ada):
    rows = vpad.shape[0]
    tn = 1536
    return pl.pallas_call(
        _ada_kernel,
        grid=(DEPTH, 6 * D_MODEL // tn),
        in_specs=[
            pl.BlockSpec((rows, D_MODEL), lambda l, j: (0, 0)),
            pl.BlockSpec((None, D_MODEL, tn), lambda l, j: (l, 0, j)),
            pl.BlockSpec((None, 1, tn), lambda l, j: (l, 0, j)),
        ],
        out_specs=pl.BlockSpec((None, rows, tn), lambda l, j: (l, 0, j)),
        out_shape=jax.ShapeDtypeStruct((DEPTH, rows, 6 * D_MODEL), F32),
        compiler_params=_cparams(2),
        name="ada",
    )(vpad, w_ada, b_ada.reshape(DEPTH, 1, 6 * D_MODEL))


def _unit1(lane):
    return (lane & 32) != 0


def _inproj_kernel(x_ref, mul_ref, sh_ref, w_ref, cos_ref, sin_ref,
                   bq_ref, bk_ref, bvt_ref, cq_ref, ckd_ref, cvdt_ref, *, rope):
    h = _rms_mod(x_ref[...], mul_ref[...], sh_ref[...]).astype(BF16)
    if rope:
        cos, sin = cos_ref[...], sin_ref[...]

    def rot(y):
        if not rope:
            return y
        return y * cos + pltpu.roll(y, 64, 1) * sin

    def dup_k(y):
        u1 = _unit1(lax.broadcasted_iota(jnp.int32, y.shape, 1))
        return jnp.concatenate([jnp.where(u1, pltpu.roll(y, 32, 1), y),
                                jnp.where(u1, y, pltpu.roll(y, LANES - 32, 1))], axis=1)

    def dup(y):
        lane = lax.broadcasted_iota(jnp.int32, y.shape, 1)
        sw = pltpu.roll(y, 64, 1)
        return jnp.concatenate([jnp.where(lane < 64, y, sw), jnp.where(lane < 64, sw, y)], axis=1)

    for col0, ref in ((0, bq_ref), (512, bk_ref), (1536, cq_ref)):
        y = _dot(h, w_ref[:, col0:col0 + 512])
        for g in range(4):
            ref[:, g * LANES:(g + 1) * LANES] = rot(y[:, g * LANES:(g + 1) * LANES]).astype(BF16)
    ykv = _dot(h, w_ref[:, 2048:2304])
    ckd_ref[...] = dup_k(rot(ykv[:, :LANES])).astype(BF16)
    cvdt_ref[...] = dup(ykv[:, LANES:]).T.astype(BF16)
    bvt_ref[...] = _dot(h, w_ref[:, 1024:1536]).T.astype(BF16)


def _inproj_call(x, mul, sh, w_attn, tables, *, layer, tm, rope):
    bsz, length, _ = x.shape
    tok = lambda w: pl.BlockSpec((None, tm, w), lambda b, i: (b, i, 0))
    vec = pl.BlockSpec((None, 1, D_MODEL), lambda b, i: (b, 0, 0))
    tab = pl.BlockSpec((tm, LANES), lambda b, i: (i, 0))
    tok_t = lambda w: pl.BlockSpec((None, w, tm), lambda b, i: (b, 0, i))
    out = lambda w: jax.ShapeDtypeStruct((bsz, length, w), BF16)
    out_t = lambda w: jax.ShapeDtypeStruct((bsz, w, length), BF16)
    return pl.pallas_call(
        functools.partial(_inproj_kernel, rope=rope),
        grid=(bsz, length // tm),
        in_specs=[tok(D_MODEL), vec, vec,
                  _layer_spec(w_attn, layer), tab, tab],
        out_specs=[tok(512), tok(512), tok_t(MIX_W), tok(512), tok(256), tok_t(2 * LANES)],
        out_shape=[out(512), out(512), out_t(MIX_W), out(512), out(256), out_t(2 * LANES)],
        compiler_params=_cparams(2),
        name="inproj",
    )(x, mul, sh, w_attn, *tables)


SAFE_EXP2_RANGE = 90.0


def _diff_kernel(sc_ref, q_ref, qall_ref, g_ref, ind_ref, *refs, seg_lens, kc, nq):
    o_ref, p_scr, f_scr, safe_scr = refs[-4:]
    t = pl.program_id(0)
    tq = q_ref.shape[0]
    lanei = lax.broadcasted_iota(jnp.int32, (1, 2 * tq), 1)

    @pl.when(t == 0)
    def _():
        p_scr[...] = jnp.zeros(p_scr.shape, BF16)
        f_scr[...] = jnp.zeros(f_scr.shape, F32)

    chunks = []
    row0 = 0
    for si, n_keys in enumerate(seg_lens):
        for st in range(0, n_keys, kc):
            size = min(kc, n_keys - st)
            chunks.append((refs[2 * si], refs[2 * si + 1], st, size, row0))
            row0 += size

    def max_sq_norm(x):
        xf = x.astype(F32)
        return jnp.max(_dot((xf * xf).astype(BF16), ind_ref[...]), axis=0, keepdims=True)

    @pl.when(t % nq == 0)
    def _():
        kn = None
        for k_ref, _, st, size, _ in chunks:
            n2 = max_sq_norm(k_ref[st:st + size, :])
            kn = n2 if kn is None else jnp.maximum(kn, n2)
        qn = None
        n_q = qall_ref.shape[0]
        for st in range(0, n_q, 2 * kc):
            n2 = max_sq_norm(qall_ref[st:min(st + 2 * kc, n_q), :])
            qn = n2 if qn is None else jnp.maximum(qn, n2)
        bound2 = jnp.max(kn * qn)
        safe_scr[0] = (bound2 < SAFE_EXP2_RANGE * SAFE_EXP2_RANGE).astype(jnp.int32)

    safe = safe_scr[0] == 1

    def step(max_free):
        q = q_ref[...]
        lane = lax.broadcasted_iota(jnp.int32, q.shape, 1)
        zero = jnp.zeros_like(q)
        qq = jnp.concatenate([jnp.where(_unit1(lane), zero, q), jnp.where(_unit1(lane), q, zero)], axis=0)
        f_prev = f_scr[...]
        mcs, lcs = [], []
        acc = None
        for c, (k_ref, vt_ref, st, size, r0) in enumerate(chunks):
            f1 = f_prev[c:c + 1, 0:tq].astype(BF16)
            f2 = f_prev[c:c + 1, tq:2 * tq].astype(BF16)
            a_t = p_scr[r0:r0 + size, 0:tq] * f1 - p_scr[r0:r0 + size, tq:2 * tq] * f2
            part = _dot(vt_ref[:, st:st + size], a_t)
            acc = part if acc is None else acc + part
            s = _dot_nt(k_ref[st:st + size, :], qq)
            if max_free:
                p = jnp.exp2(s)
            else:
                mc = jnp.max(s, axis=0, keepdims=True)
                p = jnp.exp2(s - mc)
                mcs.append(mc)
            lcs.append(jnp.sum(p, axis=0, keepdims=True))
            p_scr[r0:r0 + size, :] = p.astype(BF16)

        if max_free:
            es = [None] * len(chunks)
            l_all = functools.reduce(lambda a, b: a + b, lcs)
        else:
            m_all = functools.reduce(jnp.maximum, mcs)
            es = [jnp.exp2(mc - m_all) for mc in mcs]
            l_all = functools.reduce(lambda a, b: a + b, [lc * e for lc, e in zip(lcs, es)])
        coef = jnp.where(lanei < tq, 1.0, sc_ref[0]) / l_all
        for c, e in enumerate(es):
            f_scr[c:c + 1, :] = coef if e is None else e * coef

        o = acc.T
        ms = jnp.mean(o * o, axis=-1, keepdims=True)
        o_ref[...] = (o * lax.rsqrt(ms + EPS) * g_ref[...] * sc_ref[1]).astype(BF16)

    pl.when(safe)(lambda: step(True))
    pl.when(jnp.logical_not(safe))(lambda: step(False))


def _diff_call(scal, q, subln_g, segs, *, tq, kc):
    bsz, lq, _ = q.shape
    nq = lq // tq
    n_blocks = bsz * B_HEADS * nq
    unit = (np.arange(LANES) // 32) % 2
    ind = jnp.asarray(unit[:, None] == unit[None, :], BF16)

    def block(t):
        b, r = t // (B_HEADS * nq), t % (B_HEADS * nq)
        return b, r // nq, r % nq

    cur = lambda t: block(jnp.minimum(t, n_blocks - 1))
    prev = lambda t: block(jnp.maximum(t - 1, 0))

    def q_map(t):
        b, h, i = cur(t)
        return b, i, h

    def kq_map(t):
        b, h, _ = cur(t)
        return b, 0, h

    def vt_map(t):
        b, h, _ = prev(t)
        return b, h, 0

    def o_map(t):
        b, h, i = prev(t)
        return b, i, h

    in_specs = [pl.BlockSpec(memory_space=pltpu.SMEM),
                pl.BlockSpec((None, tq, LANES), q_map),
                pl.BlockSpec((None, lq, LANES), kq_map),
                pl.BlockSpec((1, LANES), lambda t: (0, 0)),
                _const_spec(ind)]
    args = [scal, q, q, subln_g, ind]
    for k, vt in segs:
        n_keys = k.shape[1]
        in_specs += [pl.BlockSpec((None, n_keys, LANES), kq_map),
                     pl.BlockSpec((None, LANES, n_keys), vt_map)]
        args += [k, vt]
    seg_lens = tuple(k.shape[1] for k, _ in segs)
    n_chunks = sum(-(-n // kc) for n in seg_lens)
    return pl.pallas_call(
        functools.partial(_diff_kernel, seg_lens=seg_lens, kc=kc, nq=nq),
        grid=(n_blocks + 1,),
        in_specs=in_specs,
        out_specs=pl.BlockSpec((None, tq, LANES), o_map),
        out_shape=jax.ShapeDtypeStruct((bsz, lq, MIX_W), BF16),
        scratch_shapes=[pltpu.VMEM((sum(seg_lens), 2 * tq), BF16),
                        pltpu.VMEM((-(-n_chunks // 8) * 8, 2 * tq), F32),
                        pltpu.SMEM((1,), jnp.int32)],
        compiler_params=_cparams(1),
        name="diff",
    )(*args)


WIN_SUB = 4


def _win_kernel(sink_ref, q_ref, *refs, local):
    o_ref = refs[-1]
    n_sub = q_ref.shape[0] // BLOCK
    kx_ref, vxt_ref = refs[-3], refs[-2]
    n = pl.program_id(1)
    nb = pl.num_programs(1)
    nq = C_Q_HEADS // 2 * BLOCK
    lane = lax.broadcasted_iota(jnp.int32, (BLOCK, LANES), 1)
    headi = lax.broadcasted_iota(jnp.int32, (1, nq), 1) // BLOCK
    if local:
        kp_ref, kc_ref, kn_ref, vpt_ref, vct_ref, vnt_ref = refs[:6]
        key = lax.broadcasted_iota(jnp.int32, (BLOCK, nq), 0)
        qry = lax.broadcasted_iota(jnp.int32, (BLOCK, nq), 1) & (BLOCK - 1)
        tri_prev, tri_next = key >= qry, key <= qry

    probs = [(sub, kvh) for sub in range(n_sub) for kvh in range(2)]
    s_l, sink_l, vt_l = [], [], []
    for sub, kvh in probs:
        ksl = slice(kvh * LANES, (kvh + 1) * LANES)
        rows = slice(sub * BLOCK, (sub + 1) * BLOCK)
        parts = []
        for g in range(4):
            grp = kvh * 2 + g // 2
            qg = q_ref[rows, grp * LANES:(grp + 1) * LANES]
            parts.append(jnp.where(_unit1(lane) == (g % 2 == 1), qg, jnp.zeros_like(qg)))
        qs = jnp.concatenate(parts, axis=0)
        sink = jnp.zeros((1, nq), F32)
        for g in range(4):
            sink = jnp.where(headi == g, sink_ref[kvh * 4 + g] * LOG2E, sink)
        if local:
            kb, vb = [], []
            for j in (sub - 1, sub, sub + 1):
                if j < 0:
                    kb.append(kp_ref[:, ksl]), vb.append(vpt_ref[ksl, :])
                elif j >= n_sub:
                    kb.append(kn_ref[:, ksl]), vb.append(vnt_ref[ksl, :])
                else:
                    kb.append(kc_ref[j * BLOCK:(j + 1) * BLOCK, ksl])
                    vb.append(vct_ref[ksl, j * BLOCK:(j + 1) * BLOCK])
            k_all = jnp.concatenate(kb + [kx_ref[:, ksl]], axis=0)
            vt_all = jnp.concatenate(vb + [vxt_ref[ksl, :]], axis=1)
        else:
            k_all, vt_all = kx_ref[:, ksl], vxt_ref[ksl, :]
        s = _dot_nt(k_all, qs)
        if local:
            valid_prev = tri_prev if sub > 0 else tri_prev & (n > 0)
            valid_next = tri_next if sub < n_sub - 1 else tri_next & (n < nb - 1)
            s = jnp.concatenate([jnp.where(valid_prev, s[0:BLOCK], NEG_INF), s[BLOCK:2 * BLOCK],
                                 jnp.where(valid_next, s[2 * BLOCK:3 * BLOCK], NEG_INF), s[3 * BLOCK:]], axis=0)
        s_l.append(s)
        sink_l.append(sink)
        vt_l.append(vt_all)
    m_l = [jnp.maximum(jnp.max(s, axis=0, keepdims=True), sink) for s, sink in zip(s_l, sink_l)]
    p_l = [jnp.exp2(s - m) for s, m in zip(s_l, m_l)]
    l_l = [jnp.sum(p, axis=0, keepdims=True) + jnp.exp2(sink - m) for p, sink, m in zip(p_l, sink_l, m_l)]
    ot_l = [_dot(vt, p.astype(BF16)) / l for vt, p, l in zip(vt_l, p_l, l_l)]
    for (sub, kvh), o_t in zip(probs, ot_l):
        o = o_t.T
        for j in range(2):
            grp = kvh * 2 + j
            o_ref[sub * BLOCK:(sub + 1) * BLOCK, grp * LANES:(grp + 1) * LANES] = jnp.where(
                lane < 64, o[2 * j * BLOCK:(2 * j + 1) * BLOCK], o[(2 * j + 1) * BLOCK:(2 * j + 2) * BLOCK]
            ).astype(BF16)


def _win_call(sinks, q, kd, vdt, kxd, vxdt, *, local):
    bsz, lq, _ = q.shape
    nb = lq // BLOCK
    n_sub = min(WIN_SUB, nb)
    tq = n_sub * BLOCK
    in_specs = [pl.BlockSpec(memory_space=pltpu.SMEM), pl.BlockSpec((None, tq, MIX_W), lambda b, n: (b, n, 0))]
    args = [sinks, q]
    if local:
        prv = lambda n: jnp.maximum(n * n_sub - 1, 0)
        nxt = lambda n: jnp.minimum((n + 1) * n_sub, nb - 1)
        in_specs += [pl.BlockSpec((None, BLOCK, 2 * LANES), lambda b, n: (b, prv(n), 0)),
                     pl.BlockSpec((None, tq, 2 * LANES), lambda b, n: (b, n, 0)),
                     pl.BlockSpec((None, BLOCK, 2 * LANES), lambda b, n: (b, nxt(n), 0)),
                     pl.BlockSpec((None, 2 * LANES, BLOCK), lambda b, n: (b, 0, prv(n))),
                     pl.BlockSpec((None, 2 * LANES, tq), lambda b, n: (b, 0, n)),
                     pl.BlockSpec((None, 2 * LANES, BLOCK), lambda b, n: (b, 0, nxt(n)))]
        args += [kd] * 3 + [vdt] * 3
    n_ctx = kxd.shape[1]
    in_specs += [pl.BlockSpec((None, n_ctx, 2 * LANES), lambda b, n: (b, 0, 0)),
                 pl.BlockSpec((None, 2 * LANES, n_ctx), lambda b, n: (b, 0, 0))]
    args += [kxd, vxdt]
    return pl.pallas_call(
        functools.partial(_win_kernel, local=local),
        grid=(bsz, lq // tq),
        in_specs=in_specs,
        out_specs=pl.BlockSpec((None, tq, MIX_W), lambda b, n: (b, n, 0)),
        out_shape=jax.ShapeDtypeStruct((bsz, lq, MIX_W), BF16),
        compiler_params=_cparams(2),
        name="win",
    )(*args)


def _merge_kernel(x_ref, mul_ref, sh_ref, gate_ref, wuvg_ref, lng_ref, lnb_ref, ws_ref, bs_ref,
                  ob_ref, oc_ref, wbr_ref, wout_ref, o_ref):
    x = x_ref[...]
    tm = x.shape[0]
    h = _rms_mod(x, mul_ref[...], sh_ref[...]).astype(BF16)
    u = _gelu_tanh(_dot(h, wuvg_ref[:, 0:MIX_W]))
    v = _gelu_tanh(_dot(h, wuvg_ref[:, MIX_W:2 * MIX_W]))
    mu = jnp.mean(v, axis=-1, keepdims=True)
    vc = v - mu
    var = jnp.mean(vc * vc, axis=-1, keepdims=True)
    vn = (vc * lax.rsqrt(var + EPS) * lng_ref[...] + lnb_ref[...]).astype(BF16)
    n_chunks = tm // BLOCK
    mixed = []
    for g in range(A_GROUPS):
        gs = slice(g * LANES, (g + 1) * LANES)
        rhs = jnp.concatenate([vn[c * BLOCK:(c + 1) * BLOCK, gs] for c in range(n_chunks)], axis=1)
        mixed.append(_dot(ws_ref[g], rhs))
    o_a = (u * jnp.concatenate(
        [jnp.concatenate([mixed[g][:, c * LANES:(c + 1) * LANES] + bs_ref[g] for g in range(A_GROUPS)], axis=1)
         for c in range(n_chunks)], axis=0)).astype(BF16)
    y = None
    for i, o_i in enumerate((o_a, ob_ref[...], oc_ref[...])):
        col0 = 2 * MIX_W + i * D_MODEL
        gate = jax.nn.sigmoid(_dot(h, wuvg_ref[:, col0:col0 + D_MODEL]))
        t = gate * _dot(o_i, wbr_ref[i])
        y = t if y is None else y + t
    o_ref[...] = x + gate_ref[...] * _dot(y.astype(BF16), wout_ref[...])


def _merge_call(x, mul, sh, gate, w_uvg, ln_g, ln_b, w_s, b_s, o_b, o_c, w_br, w_out, *, layer, tm):
    bsz, length, _ = x.shape
    tok = lambda w: pl.BlockSpec((None, tm, w), lambda b, i: (b, i, 0))
    vec = pl.BlockSpec((None, 1, D_MODEL), lambda b, i: (b, 0, 0))
    full = functools.partial(_layer_spec, layer=layer)
    return pl.pallas_call(
        _merge_kernel,
        grid=(bsz, length // tm),
        in_specs=[tok(D_MODEL), vec, vec, vec, full(w_uvg), full(ln_g), full(ln_b), full(w_s), full(b_s),
                  tok(MIX_W), tok(MIX_W), full(w_br), full(w_out)],
        out_specs=tok(D_MODEL),
        out_shape=jax.ShapeDtypeStruct(x.shape, F32),
        compiler_params=_cparams(2),
        name="merge",
    )(x, mul, sh, gate, w_uvg, ln_g, ln_b, w_s, b_s, o_b, o_c, w_br, w_out)


HALO = 8
MXU_TILE = 256
FF_EDGES = (0, 4 * MXU_TILE, 8 * MXU_TILE, D_FF)


def _ffn_kernel(x_ref, xp_ref, xn_ref, mul_ref, sh_ref, gate_ref, wg_ref, cw_ref, cb_ref, wu_ref, wd_ref,
                fg_ref, o_ref, *, final):
    i = pl.program_id(1)
    nt = pl.num_programs(1)
    x = x_ref[...]
    tm = x.shape[0]
    xe = jnp.concatenate([xp_ref[...], x, xn_ref[...]], axis=0)
    he = _rms_mod(xe, mul_ref[...], sh_ref[...]).astype(BF16)
    hm = he[HALO:HALO + tm]
    rowe = lax.broadcasted_iota(jnp.int32, (tm + 2 * HALO, 1), 0)
    keep = ((rowe >= HALO) | (i > 0)) & ((rowe < HALO + tm) | (i < nt - 1))
    spans = [slice(c0, c1) for c0, c1 in zip(FF_EDGES[:-1], FF_EDGES[1:])]
    a_l = [jnp.where(keep, _dot(he, wg_ref[:, cs]), 0.0) for cs in spans]
    u_l = [_dot(hm, wu_ref[:, cs]) for cs in spans]
    z_l = []
    for cs, a, u in zip(spans, a_l, u_l):
        a_prev = pltpu.roll(a, 1, 0)[HALO:HALO + tm]
        a_next = pltpu.roll(a, tm + 2 * HALO - 1, 0)[HALO:HALO + tm]
        a = (a_prev * cw_ref[0:1, cs] + a[HALO:HALO + tm] * cw_ref[1:2, cs] + a_next * cw_ref[2:3, cs]
             + cb_ref[:, cs])
        z_l.append((a * jax.nn.sigmoid(a) * u).astype(BF16))
    acc = None
    for cs, z in zip(spans, z_l):
        t = _dot(z, wd_ref[cs, :])
        acc = t if acc is None else acc + t
    out = x + gate_ref[...] * acc
    if final:
        ms = jnp.mean(out * out, axis=-1, keepdims=True)
        out = out * lax.rsqrt(ms + EPS) * fg_ref[...]
    o_ref[...] = out


def _ffn_call(x, mul, sh, gate, w_gate, conv_w, conv_b, w_up, w_down, final_g, *, layer, tm, final):
    bsz, length, _ = x.shape
    per = tm // HALO
    last = length // HALO - 1
    tok = pl.BlockSpec((None, tm, D_MODEL), lambda b, i: (b, i, 0))
    prv = pl.BlockSpec((None, HALO, D_MODEL), lambda b, i: (b, jnp.maximum(i * per - 1, 0), 0))
    nxt = pl.BlockSpec((None, HALO, D_MODEL), lambda b, i: (b, jnp.minimum((i + 1) * per, last), 0))
    vec = pl.BlockSpec((None, 1, D_MODEL), lambda b, i: (b, 0, 0))
    full = functools.partial(_layer_spec, layer=layer)
    return pl.pallas_call(
        functools.partial(_ffn_kernel, final=final),
        grid=(bsz, length // tm),
        in_specs=[tok, prv, nxt, vec, vec, vec, full(w_gate), full(conv_w), full(conv_b), full(w_up),
                  full(w_down), _const_spec(final_g)],
        out_specs=tok,
        out_shape=jax.ShapeDtypeStruct(x.shape, F32),
        compiler_params=_cparams(2),
        name="ffn",
    )(x, x, x, mul, sh, gate, w_gate, conv_w, conv_b, w_up, w_down, final_g)


def _rotary_perm():
    i = np.arange(LANES)
    unit, second, j = (i // 32) % 2, i // 64, i % 32
    dim = np.where(j < 16, j, 32 + (j - 16)) + 16 * second
    return unit * HEAD_DIM + dim


def _attn_col_perm():
    ident = np.arange(LANES)
    groups = [_rotary_perm()] * 8 + [ident] * 4 + [_rotary_perm()] * 4 + [_rotary_perm(), ident]
    return np.concatenate([g * LANES + p for g, p in enumerate(groups)])


def _rope_tables(length):
    pos = jnp.arange(length)
    rows = (pos // GRID_W).astype(F32)
    cols = (pos % GRID_W).astype(F32)
    half = HEAD_DIM // 2
    inv = ROPE_BASE ** (-jnp.arange(0, half, 2, dtype=F32) / half)
    ang = jnp.concatenate([rows[:, None] * inv[None, :], cols[:, None] * inv[None, :]], axis=1)
    cos = jnp.tile(jnp.cos(ang), (1, 4))
    sin = jnp.concatenate([-jnp.sin(ang)] * 2 + [jnp.sin(ang)] * 2, axis=1)
    return cos, sin


def kernel(x, c, ctx, c_ctx, w_ada, b_ada, norm1_g, w_in, sgu_ln_g, sgu_ln_b, w_s, b_s, lam_q1, lam_k1,
           lam_q2, lam_k2, diff_subln_g, sinks, w_branch, w_out, norm2_g, w_gate, conv_w, conv_b, w_up,
           w_down, final_g):
    bsz, length, _ = x.shape
    n_ctx = ctx.shape[1]
    tables = _rope_tables(length)
    ctx_tables = tuple(t[:n_ctx] for t in tables)

    rows = -(-(bsz + 1) // 8) * 8
    vpad = jnp.zeros((rows, D_MODEL), F32).at[:bsz].set(c).at[bsz].set(c_ctx)
    ada = _ada_call(vpad, w_ada, b_ada)

    qscale = jnp.ones((ATTN_COLS,), F32).at[0:512].set(SCALE * LOG2E).at[1536:2048].set(SCALE * LOG2E)
    w_attn = jnp.take(w_in[:, :, 2 * MIX_W:2 * MIX_W + ATTN_COLS] * qscale, _attn_col_perm(), axis=2).astype(BF16)
    w_uvg = jnp.concatenate([w_in[:, :, :2 * MIX_W], w_in[:, :, GATE_COL0:]], axis=2).astype(BF16)
    w_s_b = w_s.astype(BF16)
    b_s_b = jnp.broadcast_to(b_s[..., None], b_s.shape + (LANES,))
    w_br_b = w_branch.astype(BF16)
    w_out_b = w_out.astype(BF16)
    w_gate_b = w_gate.astype(BF16)
    w_up_b = w_up.astype(BF16)
    w_down_b = w_down.astype(BF16)
    final_g2 = final_g.reshape(1, D_MODEL)
    mix_w = (w_uvg, sgu_ln_g.reshape(DEPTH, 1, MIX_W), sgu_ln_b.reshape(DEPTH, 1, MIX_W), w_s_b, b_s_b)
    ffn_w = (w_gate_b, conv_w, conv_b.reshape(DEPTH, 1, D_FF), w_up_b, w_down_b, final_g2)

    for l in range(DEPTH):
        last = l == DEPTH - 1
        m = ada[l]
        sh1, sc1, g1, sh2, sc2, g2 = [m[:, j * D_MODEL:(j + 1) * D_MODEL] for j in range(6)]
        mul1 = norm1_g[l][None, :] * (1.0 + sc1)
        mul2 = norm2_g[l][None, :] * (1.0 + sc2)
        lat = lambda a: a[:bsz, None, :]
        cx = lambda a: jnp.broadcast_to(a[bsz][None, None, :], (bsz, 1, D_MODEL))

        lam_init = 0.8 - 0.6 * float(np.exp(-0.3 * l))
        lam = jnp.exp(jnp.sum(lam_q1[l] * lam_k1[l])) - jnp.exp(jnp.sum(lam_q2[l] * lam_k2[l])) + lam_init
        scal = jnp.stack([lam, jnp.float32(1.0 - lam_init)]).astype(F32)
        subln = diff_subln_g[l].reshape(1, LANES)

        bq, bk, bvt, cq, ckd, cvdt = _inproj_call(x, lat(mul1), lat(sh1), w_attn, tables, layer=l, tm=512,
                                                  rope=True)
        xbq, xbk, xbvt, xcq, xckd, xcvdt = _inproj_call(ctx, cx(mul1), cx(sh1), w_attn, ctx_tables, layer=l,
                                                        tm=n_ctx, rope=False)

        o_b = _diff_call(scal, bq, subln, [(bk, bvt), (xbk, xbvt)], tq=512, kc=512)
        o_c = _win_call(sinks[l], cq, ckd, cvdt, xckd, xcvdt, local=True)
        x_mid = _merge_call(x, lat(mul1), lat(sh1), lat(g1), *mix_w, o_b, o_c, w_br_b, w_out_b, layer=l, tm=512)
        if not last:
            xo_b = _diff_call(scal, xbq, subln, [(xbk, xbvt)], tq=n_ctx, kc=512)
            xo_c = _win_call(sinks[l], xcq, None, None, xckd, xcvdt, local=False)
            ctx_mid = _merge_call(ctx, cx(mul1), cx(sh1), cx(g1), *mix_w, xo_b, xo_c, w_br_b, w_out_b, layer=l,
                                  tm=n_ctx)
        x = _ffn_call(x_mid, lat(mul2), lat(sh2), lat(g2), *ffn_w, layer=l, tm=512, final=last)
        if not last:
            ctx = _ffn_call(ctx_mid, cx(mul2), cx(sh2), cx(g2), *ffn_w, layer=l, tm=n_ctx, final=False)
    return x
```

```python
import functools

import numpy as np
import jax
import jax.numpy as jnp
from jax import lax
from jax.experimental import pallas as pl
from jax.experimental.pallas import tpu as pltpu

F32 = jnp.float32
BF16 = jnp.bfloat16

D_MODEL = 1024
DEPTH = 4
GRID_W = 64
HEAD_DIM = 64
BLOCK = 128
MIX_W = 512
A_GROUPS = 4
B_HEADS = 4
C_Q_HEADS = 8
D_FF = 2816
ROPE_BASE = 10000.0
EPS = 1e-6
SCALE = HEAD_DIM ** -0.5
NEG_INF = -1e30
LOG2E = float(np.log2(np.e))

LANES = 128
ATTN_COLS = 2304
GATE_COL0 = 3328
VMEM_LIMIT = 56 * 1024 * 1024


def _cparams(n_axes):
    return pltpu.CompilerParams(
        dimension_semantics=("arbitrary",) * n_axes, vmem_limit_bytes=VMEM_LIMIT)


def _const_spec(a):
    return pl.BlockSpec(a.shape, lambda *_: (0,) * a.ndim, pipeline_mode=pl.Buffered(1))


def _layer_spec(a, layer):
    return pl.BlockSpec((None,) + a.shape[1:], lambda *_: (layer,) + (0,) * (a.ndim - 1),
                        pipeline_mode=pl.Buffered(1))


def _rms_mod(x, mul, sh):
    ms = jnp.mean(x * x, axis=-1, keepdims=True)
    return x * lax.rsqrt(ms + EPS) * mul + sh


def _gelu_tanh(x):
    c = np.float32(np.sqrt(2.0 / np.pi))
    return 0.5 * x * (1.0 + jnp.tanh(c * (x + 0.044715 * (x * x * x))))


def _dot(a, b):
    return jnp.dot(a, b, preferred_element_type=F32)


def _dot_nt(a, b):
    return lax.dot_general(a, b, (((1,), (1,)), ((), ())), preferred_element_type=F32)


def _ada_kernel(v_ref, w_ref, b_ref, o_ref):
    v = v_ref[...]
    s = v * jax.nn.sigmoid(v)
    o_ref[...] = _dot(s.astype(BF16), w_ref[...].astype(BF16)) + b_ref[...]


def _ada_call(vpad, w_ada, b_ada):
    rows = vpad.shape[0]
    tn = 1536
    return pl.pallas_call(
        _ada_kernel,
        grid=(DEPTH, 6 * D_MODEL // tn),
        in_specs=[
            pl.BlockSpec((rows, D_MODEL), lambda l, j: (0, 0)),
            pl.BlockSpec((None, D_MODEL, tn), lambda l, j: (l, 0, j)),
            pl.BlockSpec((None, 1, tn), lambda l, j: (l, 0, j)),
        ],
        out_specs=pl.BlockSpec((None, rows, tn), lambda l, j: (l, 0, j)),
        out_shape=jax.ShapeDtypeStruct((DEPTH, rows, 6 * D_MODEL), F32),
        compiler_params=_cparams(2),
        name="ada",
    )(vpad, w_ada, b_ada.reshape(DEPTH, 1, 6 * D_MODEL))


def _unit1(lane):
    return (lane & 32) != 0


def _inproj_kernel(x_ref, mul_ref, sh_ref, w_ref, cos_ref, sin_ref,
                   bq_ref, bk_ref, bvt_ref, cq_ref, ckd_ref, cvdt_ref, *, rope):
    h = _rms_mod(x_ref[...], mul_ref[...], sh_ref[...]).astype(BF16)
    if rope:
        cos, sin = cos_ref[...], sin_ref[...]

    def rot(y):
        if not rope:
            return y
        return y * cos + pltpu.roll(y, 64, 1) * sin

    def dup_k(y):
        u1 = _unit1(lax.broadcasted_iota(jnp.int32, y.shape, 1))
        return jnp.concatenate([jnp.where(u1, pltpu.roll(y, 32, 1), y),
                                jnp.where(u1, y, pltpu.roll(y, LANES - 32, 1))], axis=1)

    def dup(y):
        lane = lax.broadcasted_iota(jnp.int32, y.shape, 1)
        sw = pltpu.roll(y, 64, 1)
        return jnp.concatenate([jnp.where(lane < 64, y, sw), jnp.where(lane < 64, sw, y)], axis=1)

    for col0, ref in ((0, bq_ref), (512, bk_ref), (1536, cq_ref)):
        y = _dot(h, w_ref[:, col0:col0 + 512])
        for g in range(4):
            ref[:, g * LANES:(g + 1) * LANES] = rot(y[:, g * LANES:(g + 1) * LANES]).astype(BF16)
    ykv = _dot(h, w_ref[:, 2048:2304])
    ckd_ref[...] = dup_k(rot(ykv[:, :LANES])).astype(BF16)
    cvdt_ref[...] = dup(ykv[:, LANES:]).T.astype(BF16)
    bvt_ref[...] = _dot(h, w_ref[:, 1024:1536]).T.astype(BF16)


def _inproj_call(x, mul, sh, w_attn, tables, *, layer, tm, rope):
    bsz, length, _ = x.shape
    tok = lambda w: pl.BlockSpec((None, tm, w), lambda b, i: (b, i, 0))
    vec = pl.BlockSpec((None, 1, D_MODEL), lambda b, i: (b, 0, 0))
    tab = pl.BlockSpec((tm, LANES), lambda b, i: (i, 0))
    tok_t = lambda w: pl.BlockSpec((None, w, tm), lambda b, i: (b, 0, i))
    out = lambda w: jax.ShapeDtypeStruct((bsz, length, w), BF16)
    out_t = lambda w: jax.ShapeDtypeStruct((bsz, w, length), BF16)
    return pl.pallas_call(
        functools.partial(_inproj_kernel, rope=rope),
        grid=(bsz, length // tm),
        in_specs=[tok(D_MODEL), vec, vec,
                  _layer_spec(w_attn, layer), tab, tab],
        out_specs=[tok(512), tok(512), tok_t(MIX_W), tok(512), tok(256), tok_t(2 * LANES)],
        out_shape=[out(512), out(512), out_t(MIX_W), out(512), out(256), out_t(2 * LANES)],
        compiler_params=_cparams(2),
        name="inproj",
    )(x, mul, sh, w_attn, *tables)


SAFE_EXP2_RANGE = 90.0


def _diff_kernel(sc_ref, q_ref, qall_ref, g_ref, ind_ref, *refs, seg_lens, kc, nq):
    o_ref, p_scr, f_scr, safe_scr = refs[-4:]
    t = pl.program_id(0)
    tq = q_ref.shape[0]
    lanei = lax.broadcasted_iota(jnp.int32, (1, 2 * tq), 1)

    @pl.when(t == 0)
    def _():
        p_scr[...] = jnp.zeros(p_scr.shape, BF16)
        f_scr[...] = jnp.zeros(f_scr.shape, F32)

    chunks = []
    row0 = 0
    for si, n_keys in enumerate(seg_lens):
        for st in range(0, n_keys, kc):
            size = min(kc, n_keys - st)
            chunks.append((refs[2 * si], refs[2 * si + 1], st, size, row0))
            row0 += size

    def max_sq_norm(x):
        xf = x.astype(F32)
        return jnp.max(_dot((xf * xf).astype(BF16), ind_ref[...]), axis=0, keepdims=True)

    @pl.when(t % nq == 0)
    def _():
        kn = None
        for k_ref, _, st, size, _ in chunks:
            n2 = max_sq_norm(k_ref[st:st + size, :])
            kn = n2 if kn is None else jnp.maximum(kn, n2)
        qn = None
        n_q = qall_ref.shape[0]
        for st in range(0, n_q, 2 * kc):
            n2 = max_sq_norm(qall_ref[st:min(st + 2 * kc, n_q), :])
            qn = n2 if qn is None else jnp.maximum(qn, n2)
        bound2 = jnp.max(kn * qn)
        safe_scr[0] = (bound2 < SAFE_EXP2_RANGE * SAFE_EXP2_RANGE).astype(jnp.int32)

    safe = safe_scr[0] == 1

    def step(max_free):
        q = q_ref[...]
        lane = lax.broadcasted_iota(jnp.int32, q.shape, 1)
        zero = jnp.zeros_like(q)
        qq = jnp.concatenate([jnp.where(_unit1(lane), zero, q), jnp.where(_unit1(lane), q, zero)], axis=0)
        f_prev = f_scr[...]
        mcs, lcs = [], []
        acc = None
        for c, (k_ref, vt_ref, st, size, r0) in enumerate(chunks):
            f1 = f_prev[c:c + 1, 0:tq].astype(BF16)
            f2 = f_prev[c:c + 1, tq:2 * tq].astype(BF16)
            a_t = p_scr[r0:r0 + size, 0:tq] * f1 - p_scr[r0:r0 + size, tq:2 * tq] * f2
            part = _dot(vt_ref[:, st:st + size], a_t)
            acc = part if acc is None else acc + part
            s = _dot_nt(k_ref[st:st + size, :], qq)
            if max_free:
                p = jnp.exp2(s)
            else:
                mc = jnp.max(s, axis=0, keepdims=True)
                p = jnp.exp2(s - mc)
                mcs.append(mc)
            lcs.append(jnp.sum(p, axis=0, keepdims=True))
            p_scr[r0:r0 + size, :] = p.astype(BF16)

        if max_free:
            es = [None] * len(chunks)
            l_all = functools.reduce(lambda a, b: a + b, lcs)
        else:
            m_all = functools.reduce(jnp.maximum, mcs)
            es = [jnp.exp2(mc - m_all) for mc in mcs]
            l_all = functools.reduce(lambda a, b: a + b, [lc * e for lc, e in zip(lcs, es)])
        coef = jnp.where(lanei < tq, 1.0, sc_ref[0]) / l_all
        for c, e in enumerate(es):
            f_scr[c:c + 1, :] = coef if e is None else e * coef

        o = acc.T
        ms = jnp.mean(o * o, axis=-1, keepdims=True)
        o_ref[...] = (o * lax.rsqrt(ms + EPS) * g_ref[...] * sc_ref[1]).astype(BF16)

    pl.when(safe)(lambda: step(True))
    pl.when(jnp.logical_not(safe))(lambda: step(False))


def _diff_call(scal, q, subln_g, segs, *, tq, kc):
    bsz, lq, _ = q.shape
    nq = lq // tq
    n_blocks = bsz * B_HEADS * nq
    unit = (np.arange(LANES) // 32) % 2
    ind = jnp.asarray(unit[:, None] == unit[None, :], BF16)

    def block(t):
        b, r = t // (B_HEADS * nq), t % (B_HEADS * nq)
        return b, r // nq, r % nq

    cur = lambda t: block(jnp.minimum(t, n_blocks - 1))
    prev = lambda t: block(jnp.maximum(t - 1, 0))

    def q_map(t):
        b, h, i = cur(t)
        return b, i, h

    def kq_map(t):
        b, h, _ = cur(t)
        return b, 0, h

    def vt_map(t):
        b, h, _ = prev(t)
        return b, h, 0

    def o_map(t):
        b, h, i = prev(t)
        return b, i, h

    in_specs = [pl.BlockSpec(memory_space=pltpu.SMEM),
                pl.BlockSpec((None, tq, LANES), q_map),
                pl.BlockSpec((None, lq, LANES), kq_map),
                pl.BlockSpec((1, LANES), lambda t: (0, 0)),
                _const_spec(ind)]
    args = [scal, q, q, subln_g, ind]
    for k, vt in segs:
        n_keys = k.shape[1]
        in_specs += [pl.BlockSpec((None, n_keys, LANES), kq_map),
                     pl.BlockSpec((None, LANES, n_keys), vt_map)]
        args += [k, vt]
    seg_lens = tuple(k.shape[1] for k, _ in segs)
    n_chunks = sum(-(-n // kc) for n in seg_lens)
    return pl.pallas_call(
        functools.partial(_diff_kernel, seg_lens=seg_lens, kc=kc, nq=nq),
        grid=(n_blocks + 1,),
        in_specs=in_specs,
        out_specs=pl.BlockSpec((None, tq, LANES), o_map),
        out_shape=jax.ShapeDtypeStruct((bsz, lq, MIX_W), BF16),
        scratch_shapes=[pltpu.VMEM((sum(seg_lens), 2 * tq), BF16),
                        pltpu.VMEM((-(-n_chunks // 8) * 8, 2 * tq), F32),
                        pltpu.SMEM((1,), jnp.int32)],
        compiler_params=_cparams(1),
        name="diff",
    )(*args)


WIN_SUB = 4


def _win_kernel(sink_ref, q_ref, *refs, local):
    o_ref = refs[-1]
    n_sub = q_ref.shape[0] // BLOCK
    kx_ref, vxt_ref = refs[-3], refs[-2]
    n = pl.program_id(1)
    nb = pl.num_programs(1)
    nq = C_Q_HEADS // 2 * BLOCK
    lane = lax.broadcasted_iota(jnp.int32, (BLOCK, LANES), 1)
    headi = lax.broadcasted_iota(jnp.int32, (1, nq), 1) // BLOCK
    if local:
        kp_ref, kc_ref, kn_ref, vpt_ref, vct_ref, vnt_ref = refs[:6]
        key = lax.broadcasted_iota(jnp.int32, (BLOCK, nq), 0)
        qry = lax.broadcasted_iota(jnp.int32, (BLOCK, nq), 1) & (BLOCK - 1)
        tri_prev, tri_next = key >= qry, key <= qry

    probs = [(sub, kvh) for sub in range(n_sub) for kvh in range(2)]
    s_l, sink_l, vt_l = [], [], []
    for sub, kvh in probs:
        ksl = slice(kvh * LANES, (kvh + 1) * LANES)
        rows = slice(sub * BLOCK, (sub + 1) * BLOCK)
        parts = []
        for g in range(4):
            grp = kvh * 2 + g // 2
            qg = q_ref[rows, grp * LANES:(grp + 1) * LANES]
            parts.append(jnp.where(_unit1(lane) == (g % 2 == 1), qg, jnp.zeros_like(qg)))
        qs = jnp.concatenate(parts, axis=0)
        sink = jnp.zeros((1, nq), F32)
        for g in range(4):
            sink = jnp.where(headi == g, sink_ref[kvh * 4 + g] * LOG2E, sink)
        if local:
            kb, vb = [], []
            for j in (sub - 1, sub, sub + 1):
                if j < 0:
                    kb.append(kp_ref[:, ksl]), vb.append(vpt_ref[ksl, :])
                elif j >= n_sub:
                    kb.append(kn_ref[:, ksl]), vb.append(vnt_ref[ksl, :])
                else:
                    kb.append(kc_ref[j * BLOCK:(j + 1) * BLOCK, ksl])
                    vb.append(vct_ref[ksl, j * BLOCK:(j + 1) * BLOCK])
            k_all = jnp.concatenate(kb + [kx_ref[:, ksl]], axis=0)
            vt_all = jnp.concatenate(vb + [vxt_ref[ksl, :]], axis=1)
        else:
            k_all, vt_all = kx_ref[:, ksl], vxt_ref[ksl, :]
        s = _dot_nt(k_all, qs)
        if local:
            valid_prev = tri_prev if sub > 0 else tri_prev & (n > 0)
            valid_next = tri_next if sub < n_sub - 1 else tri_next & (n < nb - 1)
            s = jnp.concatenate([jnp.where(valid_prev, s[0:BLOCK], NEG_INF), s[BLOCK:2 * BLOCK],
                                 jnp.where(valid_next, s[2 * BLOCK:3 * BLOCK], NEG_INF), s[3 * BLOCK:]], axis=0)
        s_l.append(s)
        sink_l.append(sink)
        vt_l.append(vt_all)
    m_l = [jnp.maximum(jnp.max(s, axis=0, keepdims=True), sink) for s, sink in zip(s_l, sink_l)]
    p_l = [jnp.exp2(s - m) for s, m in zip(s_l, m_l)]
    l_l = [jnp.sum(p, axis=0, keepdims=True) + jnp.exp2(sink - m) for p, sink, m in zip(p_l, sink_l, m_l)]
    ot_l = [_dot(vt, p.astype(BF16)) / l for vt, p, l in zip(vt_l, p_l, l_l)]
    for (sub, kvh), o_t in zip(probs, ot_l):
        o = o_t.T
        for j in range(2):
            grp = kvh * 2 + j
            o_ref[sub * BLOCK:(sub + 1) * BLOCK, grp * LANES:(grp + 1) * LANES] = jnp.where(
                lane < 64, o[2 * j * BLOCK:(2 * j + 1) * BLOCK], o[(2 * j + 1) * BLOCK:(2 * j + 2) * BLOCK]
            ).astype(BF16)


def _win_call(sinks, q, kd, vdt, kxd, vxdt, *, local):
    bsz, lq, _ = q.shape
    nb = lq // BLOCK
    n_sub = min(WIN_SUB, nb)
    tq = n_sub * BLOCK
    in_specs = [pl.BlockSpec(memory_space=pltpu.SMEM), pl.BlockSpec((None, tq, MIX_W), lambda b, n: (b, n, 0))]
    args = [sinks, q]
    if local:
        prv = lambda n: jnp.maximum(n * n_sub - 1, 0)
        nxt = lambda n: jnp.minimum((n + 1) * n_sub, nb - 1)
        in_specs += [pl.BlockSpec((None, BLOCK, 2 * LANES), lambda b, n: (b, prv(n), 0)),
                     pl.BlockSpec((None, tq, 2 * LANES), lambda b, n: (b, n, 0)),
                     pl.BlockSpec((None, BLOCK, 2 * LANES), lambda b, n: (b, nxt(n), 0)),
                     pl.BlockSpec((None, 2 * LANES, BLOCK), lambda b, n: (b, 0, prv(n))),
                     pl.BlockSpec((None, 2 * LANES, tq), lambda b, n: (b, 0, n)),
                     pl.BlockSpec((None, 2 * LANES, BLOCK), lambda b, n: (b, 0, nxt(n)))]
        args += [kd] * 3 + [vdt] * 3
    n_ctx = kxd.shape[1]
    in_specs += [pl.BlockSpec((None, n_ctx, 2 * LANES), lambda b, n: (b, 0, 0)),
                 pl.BlockSpec((None, 2 * LANES, n_ctx), lambda b, n: (b, 0, 0))]
    args += [kxd, vxdt]
    return pl.pallas_call(
        functools.partial(_win_kernel, local=local),
        grid=(bsz, lq // tq),
        in_specs=in_specs,
        out_specs=pl.BlockSpec((None, tq, MIX_W), lambda b, n: (b, n, 0)),
        out_shape=jax.ShapeDtypeStruct((bsz, lq, MIX_W), BF16),
        compiler_params=_cparams(2),
        name="win",
    )(*args)


def _merge_kernel(x_ref, mul_ref, sh_ref, gate_ref, wuvg_ref, lng_ref, lnb_ref, ws_ref, bs_ref,
                  ob_ref, oc_ref, wbr_ref, wout_ref, o_ref):
    x = x_ref[...]
    tm = x.shape[0]
    h = _rms_mod(x, mul_ref[...], sh_ref[...]).astype(BF16)
    u = _gelu_tanh(_dot(h, wuvg_ref[:, 0:MIX_W]))
    v = _gelu_tanh(_dot(h, wuvg_ref[:, MIX_W:2 * MIX_W]))
    mu = jnp.mean(v, axis=-1, keepdims=True)
    vc = v - mu
    var = jnp.mean(vc * vc, axis=-1, keepdims=True)
    vn = (vc * lax.rsqrt(var + EPS) * lng_ref[...] + lnb_ref[...]).astype(BF16)
    n_chunks = tm // BLOCK
    mixed = []
    for g in range(A_GROUPS):
        gs = slice(g * LANES, (g + 1) * LANES)
        rhs = jnp.concatenate([vn[c * BLOCK:(c + 1) * BLOCK, gs] for c in range(n_chunks)], axis=1)
        mixed.append(_dot(ws_ref[g], rhs))
    o_a = (u * jnp.concatenate(
        [jnp.concatenate([mixed[g][:, c * LANES:(c + 1) * LANES] + bs_ref[g] for g in range(A_GROUPS)], axis=1)
         for c in range(n_chunks)], axis=0)).astype(BF16)
    y = None
    for i, o_i in enumerate((o_a, ob_ref[...], oc_ref[...])):
        col0 = 2 * MIX_W + i * D_MODEL
        gate = jax.nn.sigmoid(_dot(h, wuvg_ref[:, col0:col0 + D_MODEL]))
        t = gate * _dot(o_i, wbr_ref[i])
        y = t if y is None else y + t
    o_ref[...] = x + gate_ref[...] * _dot(y.astype(BF16), wout_ref[...])


def _merge_call(x, mul, sh, gate, w_uvg, ln_g, ln_b, w_s, b_s, o_b, o_c, w_br, w_out, *, layer, tm):
    bsz, length, _ = x.shape
    tok = lambda w: pl.BlockSpec((None, tm, w), lambda b, i: (b, i, 0))
    vec = pl.BlockSpec((None, 1, D_MODEL), lambda b, i: (b, 0, 0))
    full = functools.partial(_layer_spec, layer=layer)
    return pl.pallas_call(
        _merge_kernel,
        grid=(bsz, length // tm),
        in_specs=[tok(D_MODEL), vec, vec, vec, full(w_uvg), full(ln_g), full(ln_b), full(w_s), full(b_s),
                  tok(MIX_W), tok(MIX_W), full(w_br), full(w_out)],
        out_specs=tok(D_MODEL),
        out_shape=jax.ShapeDtypeStruct(x.shape, F32),
        compiler_params=_cparams(2),
        name="merge",
    )(x, mul, sh, gate, w_uvg, ln_g, ln_b, w_s, b_s, o_b, o_c, w_br, w_out)


HALO = 8
MXU_TILE = 256
FF_EDGES = (0, 4 * MXU_TILE, 8 * MXU_TILE, D_FF)


def _ffn_kernel(x_ref, xp_ref, xn_ref, mul_ref, sh_ref, gate_ref, wg_ref, cw_ref, cb_ref, wu_ref, wd_ref,
                fg_ref, o_ref, *, final):
    i = pl.program_id(1)
    nt = pl.num_programs(1)
    x = x_ref[...]
    tm = x.shape[0]
    xe = jnp.concatenate([xp_ref[...], x, xn_ref[...]], axis=0)
    he = _rms_mod(xe, mul_ref[...], sh_ref[...]).astype(BF16)
    hm = he[HALO:HALO + tm]
    rowe = lax.broadcasted_iota(jnp.int32, (tm + 2 * HALO, 1), 0)
    keep = ((rowe >= HALO) | (i > 0)) & ((rowe < HALO + tm) | (i < nt - 1))
    spans = [slice(c0, c1) for c0, c1 in zip(FF_EDGES[:-1], FF_EDGES[1:])]
    a_l = [jnp.where(keep, _dot(he, wg_ref[:, cs]), 0.0) for cs in spans]
    u_l = [_dot(hm, wu_ref[:, cs]) for cs in spans]
    z_l = []
    for cs, a, u in zip(spans, a_l, u_l):
        a_prev = pltpu.roll(a, 1, 0)[HALO:HALO + tm]
        a_next = pltpu.roll(a, tm + 2 * HALO - 1, 0)[HALO:HALO + tm]
        a = (a_prev * cw_ref[0:1, cs] + a[HALO:HALO + tm] * cw_ref[1:2, cs] + a_next * cw_ref[2:3, cs]
             + cb_ref[:, cs])
        z_l.append((a * jax.nn.sigmoid(a) * u).astype(BF16))
    acc = None
    for cs, z in zip(spans, z_l):
        t = _dot(z, wd_ref[cs, :])
        acc = t if acc is None else acc + t
    out = x + gate_ref[...] * acc
    if final:
        ms = jnp.mean(out * out, axis=-1, keepdims=True)
        out = out * lax.rsqrt(ms + EPS) * fg_ref[...]
    o_ref[...] = out


def _ffn_call(x, mul, sh, gate, w_gate, conv_w, conv_b, w_up, w_down, final_g, *, layer, tm, final):
    bsz, length, _ = x.shape
    per = tm // HALO
    last = length // HALO - 1
    tok = pl.BlockSpec((None, tm, D_MODEL), lambda b, i: (b, i, 0))
    prv = pl.BlockSpec((None, HALO, D_MODEL), lambda b, i: (b, jnp.maximum(i * per - 1, 0), 0))
    nxt = pl.BlockSpec((None, HALO, D_MODEL), lambda b, i: (b, jnp.minimum((i + 1) * per, last), 0))
    vec = pl.BlockSpec((None, 1, D_MODEL), lambda b, i: (b, 0, 0))
    full = functools.partial(_layer_spec, layer=layer)
    return pl.pallas_call(
        functools.partial(_ffn_kernel, final=final),
        grid=(bsz, length // tm),
        in_specs=[tok, prv, nxt, vec, vec, vec, full(w_gate), full(conv_w), full(conv_b), full(w_up),
                  full(w_down), _const_spec(final_g)],
        out_specs=tok,
        out_shape=jax.ShapeDtypeStruct(x.shape, F32),
        compiler_params=_cparams(2),
        name="ffn",
    )(x, x, x, mul, sh, gate, w_gate, conv_w, conv_b, w_up, w_down, final_g)


def _to_rotary_layout(w):
    lead = w.shape[:-1]
    w = w.reshape(lead + (-1, 2, 2, 2, 16))
    return jnp.swapaxes(jnp.swapaxes(w, -3, -2), -4, -3).reshape(lead + (-1,))


def _rope_tables(length):
    pos = jnp.arange(length)
    rows = (pos // GRID_W).astype(F32)
    cols = (pos % GRID_W).astype(F32)
    half = HEAD_DIM // 2
    inv = ROPE_BASE ** (-jnp.arange(0, half, 2, dtype=F32) / half)
    ang = jnp.concatenate([rows[:, None] * inv[None, :], cols[:, None] * inv[None, :]], axis=1)
    cos = jnp.tile(jnp.cos(ang), (1, 4))
    sin = jnp.concatenate([-jnp.sin(ang)] * 2 + [jnp.sin(ang)] * 2, axis=1)
    return cos, sin


def kernel(x, c, ctx, c_ctx, w_ada, b_ada, norm1_g, w_in, sgu_ln_g, sgu_ln_b, w_s, b_s, lam_q1, lam_k1,
           lam_q2, lam_k2, diff_subln_g, sinks, w_branch, w_out, norm2_g, w_gate, conv_w, conv_b, w_up,
           w_down, final_g):
    bsz, length, _ = x.shape
    n_ctx = ctx.shape[1]
    tables = _rope_tables(length)
    ctx_tables = tuple(t[:n_ctx] for t in tables)

    rows = -(-(bsz + 1) // 8) * 8
    vpad = jnp.zeros((rows, D_MODEL), F32).at[:bsz].set(c).at[bsz].set(c_ctx)
    ada = _ada_call(vpad, w_ada, b_ada)

    qscale = jnp.ones((ATTN_COLS,), F32).at[0:512].set(SCALE * LOG2E).at[1536:2048].set(SCALE * LOG2E)
    wa = w_in[:, :, 2 * MIX_W:2 * MIX_W + ATTN_COLS] * qscale
    rot, keep = _to_rotary_layout, lambda w: w
    w_attn = jnp.concatenate(
        [f(wa[:, :, a:b]) for f, a, b in ((rot, 0, 1024), (keep, 1024, 1536), (rot, 1536, 2176),
                                          (keep, 2176, ATTN_COLS))], axis=2).astype(BF16)
    w_uvg = jnp.concatenate([w_in[:, :, :2 * MIX_W], w_in[:, :, GATE_COL0:]], axis=2).astype(BF16)
    w_s_b = w_s.astype(BF16)
    b_s_b = jnp.broadcast_to(b_s[..., None], b_s.shape + (LANES,))
    w_br_b = w_branch.astype(BF16)
    w_out_b = w_out.astype(BF16)
    w_gate_b = w_gate.astype(BF16)
    w_up_b = w_up.astype(BF16)
    w_down_b = w_down.astype(BF16)
    final_g2 = final_g.reshape(1, D_MODEL)
    mix_w = (w_uvg, sgu_ln_g.reshape(DEPTH, 1, MIX_W), sgu_ln_b.reshape(DEPTH, 1, MIX_W), w_s_b, b_s_b)
    ffn_w = (w_gate_b, conv_w, conv_b.reshape(DEPTH, 1, D_FF), w_up_b, w_down_b, final_g2)

    for l in range(DEPTH):
        last = l == DEPTH - 1
        m = ada[l]
        sh1, sc1, g1, sh2, sc2, g2 = [m[:, j * D_MODEL:(j + 1) * D_MODEL] for j in range(6)]
        mul1 = norm1_g[l][None, :] * (1.0 + sc1)
        mul2 = norm2_g[l][None, :] * (1.0 + sc2)
        lat = lambda a: a[:bsz, None, :]
        cx = lambda a: jnp.broadcast_to(a[bsz][None, None, :], (bsz, 1, D_MODEL))

        lam_init = 0.8 - 0.6 * float(np.exp(-0.3 * l))
        lam = jnp.exp(jnp.sum(lam_q1[l] * lam_k1[l])) - jnp.exp(jnp.sum(lam_q2[l] * lam_k2[l])) + lam_init
        scal = jnp.stack([lam, jnp.float32(1.0 - lam_init)]).astype(F32)
        subln = diff_subln_g[l].reshape(1, LANES)

        bq, bk, bvt, cq, ckd, cvdt = _inproj_call(x, lat(mul1), lat(sh1), w_attn, tables, layer=l, tm=512,
                                                  rope=True)
        xbq, xbk, xbvt, xcq, xckd, xcvdt = _inproj_call(ctx, cx(mul1), cx(sh1), w_attn, ctx_tables, layer=l,
                                                        tm=n_ctx, rope=False)

        o_b = _diff_call(scal, bq, subln, [(bk, bvt), (xbk, xbvt)], tq=512, kc=512)
        o_c = _win_call(sinks[l], cq, ckd, cvdt, xckd, xcvdt, local=True)
        x_mid = _merge_call(x, lat(mul1), lat(sh1), lat(g1), *mix_w, o_b, o_c, w_br_b, w_out_b, layer=l, tm=512)
        if not last:
            xo_b = _diff_call(scal, xbq, subln, [(xbk, xbvt)], tq=n_ctx, kc=512)
            xo_c = _win_call(sinks[l], xcq, None, None, xckd, xcvdt, local=False)
            ctx_mid = _merge_call(ctx, cx(mul1), cx(sh1), cx(g1), *mix_w, xo_b, xo_c, w_br_b, w_out_b, layer=l,
                                  tm=n_ctx)
        x = _ffn_call(x_mid, lat(mul2), lat(sh2), lat(g2), *ffn_w, layer=l, tm=512, final=last)
        if not last:
            ctx = _ffn_call(ctx_mid, cx(mul2), cx(sh2), cx(g2), *ffn_w, layer=l, tm=n_ctx, final=False)
    return x
```

```python
import functools

import numpy as np
import jax
import jax.numpy as jnp
from jax import lax
from jax.experimental import pallas as pl
from jax.experimental.pallas import tpu as pltpu

F32 = jnp.float32
BF16 = jnp.bfloat16

D_MODEL = 1024
DEPTH = 4
GRID_W = 64
HEAD_DIM = 64
BLOCK = 128
MIX_W = 512
A_GROUPS = 4
B_HEADS = 4
C_Q_HEADS = 8
D_FF = 2816
ROPE_BASE = 10000.0
EPS = 1e-6
SCALE = HEAD_DIM ** -0.5
NEG_INF = -1e30
LOG2E = float(np.log2(np.e))

LANES = 128
ATTN_COLS = 2304
GATE_COL0 = 3328
VMEM_LIMIT = 56 * 1024 * 1024


def _cparams(n_axes):
    return pltpu.CompilerParams(
        dimension_semantics=("arbitrary",) * n_axes, vmem_limit_bytes=VMEM_LIMIT)


def _const_spec(a):
    return pl.BlockSpec(a.shape, lambda *_: (0,) * a.ndim, pipeline_mode=pl.Buffered(1))


def _layer_spec(a, layer):
    return pl.BlockSpec((None,) + a.shape[1:], lambda *_: (layer,) + (0,) * (a.ndim - 1),
                        pipeline_mode=pl.Buffered(1))


def _rms_mod(x, mul, sh):
    ms = jnp.mean(x * x, axis=-1, keepdims=True)
    return x * lax.rsqrt(ms + EPS) * mul + sh


def _gelu_tanh(x):
    c = np.float32(np.sqrt(2.0 / np.pi))
    return 0.5 * x * (1.0 + jnp.tanh(c * (x + 0.044715 * (x * x * x))))


def _dot(a, b):
    return jnp.dot(a, b, preferred_element_type=F32)


def _dot_nt(a, b):
    return lax.dot_general(a, b, (((1,), (1,)), ((), ())), preferred_element_type=F32)


def _ada_kernel(v_ref, w_ref, b_ref, o_ref):
    v = v_ref[...]
    s = v * jax.nn.sigmoid(v)
    o_ref[...] = _dot(s.astype(BF16), w_ref[...].astype(BF16)) + b_ref[...]


def _ada_call(vpad, w_ada, b_ada):
    rows = vpad.shape[0]
    tn = 1536
    return pl.pallas_call(
        _ada_kernel,
        grid=(DEPTH, 6 * D_MODEL // tn),
        in_specs=[
            pl.BlockSpec((rows, D_MODEL), lambda l, j: (0, 0)),
            pl.BlockSpec((None, D_MODEL, tn), lambda l, j: (l, 0, j)),
            pl.BlockSpec((None, 1, tn), lambda l, j: (l, 0, j)),
        ],
        out_specs=pl.BlockSpec((None, rows, tn), lambda l, j: (l, 0, j)),
        out_shape=jax.ShapeDtypeStruct((DEPTH, rows, 6 * D_MODEL), F32),
        compiler_params=_cparams(2),
        name="ada",
    )(vpad, w_ada, b_ada.reshape(DEPTH, 1, 6 * D_MODEL))


def _inproj_kernel(x_ref, mul_ref, sh_ref, w_ref, cos_ref, sa_ref, sb_ref,
                   bq_ref, bk_ref, bvt_ref, cq_ref, ckd_ref, cvdt_ref, *, rope):
    h = _rms_mod(x_ref[...], mul_ref[...], sh_ref[...]).astype(BF16)
    if rope:
        cos, sa, sb = cos_ref[...], sa_ref[...], sb_ref[...]

    def rot(y):
        if not rope:
            return y
        return y * cos + pltpu.roll(y, LANES - 16, 1) * sa + pltpu.roll(y, 16, 1) * sb

    def dup(y):
        lane = lax.broadcasted_iota(jnp.int32, y.shape, 1)
        sw = pltpu.roll(y, 64, 1)
        return jnp.concatenate([jnp.where(lane < 64, y, sw), jnp.where(lane < 64, sw, y)], axis=1)

    for col0, ref in ((0, bq_ref), (512, bk_ref), (1536, cq_ref)):
        y = _dot(h, w_ref[:, col0:col0 + 512])
        for g in range(4):
            ref[:, g * LANES:(g + 1) * LANES] = rot(y[:, g * LANES:(g + 1) * LANES]).astype(BF16)
    ykv = _dot(h, w_ref[:, 2048:2304])
    ckd_ref[...] = dup(rot(ykv[:, :LANES])).astype(BF16)
    cvdt_ref[...] = dup(ykv[:, LANES:]).T.astype(BF16)
    bvt_ref[...] = _dot(h, w_ref[:, 1024:1536]).T.astype(BF16)


def _inproj_call(x, mul, sh, w_attn, tables, *, layer, tm, rope):
    bsz, length, _ = x.shape
    tok = lambda w: pl.BlockSpec((None, tm, w), lambda b, i: (b, i, 0))
    vec = pl.BlockSpec((None, 1, D_MODEL), lambda b, i: (b, 0, 0))
    tab = pl.BlockSpec((tm, LANES), lambda b, i: (i, 0))
    tok_t = lambda w: pl.BlockSpec((None, w, tm), lambda b, i: (b, 0, i))
    out = lambda w: jax.ShapeDtypeStruct((bsz, length, w), BF16)
    out_t = lambda w: jax.ShapeDtypeStruct((bsz, w, length), BF16)
    return pl.pallas_call(
        functools.partial(_inproj_kernel, rope=rope),
        grid=(bsz, length // tm),
        in_specs=[tok(D_MODEL), vec, vec,
                  _layer_spec(w_attn, layer), tab, tab, tab],
        out_specs=[tok(512), tok(512), tok_t(MIX_W), tok(512), tok(256), tok_t(2 * LANES)],
        out_shape=[out(512), out(512), out_t(MIX_W), out(512), out(256), out_t(2 * LANES)],
        compiler_params=_cparams(2),
        name="inproj",
    )(x, mul, sh, w_attn, *tables)


SAFE_EXP2_RANGE = 90.0


def _diff_kernel(sc_ref, q_ref, qall_ref, g_ref, ind_ref, *refs, seg_lens, kc, nq):
    o_ref, p_scr, f_scr, safe_scr = refs[-4:]
    t = pl.program_id(0)
    tq = q_ref.shape[0]
    lanei = lax.broadcasted_iota(jnp.int32, (1, 2 * tq), 1)

    @pl.when(t == 0)
    def _():
        p_scr[...] = jnp.zeros(p_scr.shape, BF16)
        f_scr[...] = jnp.zeros(f_scr.shape, F32)

    chunks = []
    row0 = 0
    for si, n_keys in enumerate(seg_lens):
        for st in range(0, n_keys, kc):
            size = min(kc, n_keys - st)
            chunks.append((refs[2 * si], refs[2 * si + 1], st, size, row0))
            row0 += size

    def max_sq_norm(x):
        xf = x.astype(F32)
        return jnp.max(_dot((xf * xf).astype(BF16), ind_ref[...]), axis=0, keepdims=True)

    @pl.when(t % nq == 0)
    def _():
        kn = None
        for k_ref, _, st, size, _ in chunks:
            n2 = max_sq_norm(k_ref[st:st + size, :])
            kn = n2 if kn is None else jnp.maximum(kn, n2)
        qn = None
        n_q = qall_ref.shape[0]
        for st in range(0, n_q, 2 * kc):
            n2 = max_sq_norm(qall_ref[st:min(st + 2 * kc, n_q), :])
            qn = n2 if qn is None else jnp.maximum(qn, n2)
        bound2 = jnp.max(kn * qn)
        safe_scr[0] = (bound2 < SAFE_EXP2_RANGE * SAFE_EXP2_RANGE).astype(jnp.int32)

    safe = safe_scr[0] == 1

    def step(max_free):
        q = q_ref[...]
        lane = lax.broadcasted_iota(jnp.int32, q.shape, 1)
        zero = jnp.zeros_like(q)
        qq = jnp.concatenate([jnp.where(lane < 64, q, zero), jnp.where(lane >= 64, q, zero)], axis=0)
        f_prev = f_scr[...]
        mcs, lcs = [], []
        acc = None
        for c, (k_ref, vt_ref, st, size, r0) in enumerate(chunks):
            f1 = f_prev[c:c + 1, 0:tq].astype(BF16)
            f2 = f_prev[c:c + 1, tq:2 * tq].astype(BF16)
            a_t = p_scr[r0:r0 + size, 0:tq] * f1 - p_scr[r0:r0 + size, tq:2 * tq] * f2
            part = _dot(vt_ref[:, st:st + size], a_t)
            acc = part if acc is None else acc + part
            s = _dot_nt(k_ref[st:st + size, :], qq)
            if max_free:
                p = jnp.exp2(s)
            else:
                mc = jnp.max(s, axis=0, keepdims=True)
                p = jnp.exp2(s - mc)
                mcs.append(mc)
            lcs.append(jnp.sum(p, axis=0, keepdims=True))
            p_scr[r0:r0 + size, :] = p.astype(BF16)

        if max_free:
            es = [None] * len(chunks)
            l_all = functools.reduce(lambda a, b: a + b, lcs)
        else:
            m_all = functools.reduce(jnp.maximum, mcs)
            es = [jnp.exp2(mc - m_all) for mc in mcs]
            l_all = functools.reduce(lambda a, b: a + b, [lc * e for lc, e in zip(lcs, es)])
        coef = jnp.where(lanei < tq, 1.0, sc_ref[0]) / l_all
        for c, e in enumerate(es):
            f_scr[c:c + 1, :] = coef if e is None else e * coef

        o = acc.T
        ms = jnp.mean(o * o, axis=-1, keepdims=True)
        o_ref[...] = (o * lax.rsqrt(ms + EPS) * g_ref[...] * sc_ref[1]).astype(BF16)

    pl.when(safe)(lambda: step(True))
    pl.when(jnp.logical_not(safe))(lambda: step(False))


def _diff_call(scal, q, subln_g, segs, *, tq, kc):
    bsz, lq, _ = q.shape
    nq = lq // tq
    n_blocks = bsz * B_HEADS * nq
    half = np.arange(LANES) // HEAD_DIM
    ind = jnp.asarray(half[:, None] == half[None, :], BF16)

    def block(t):
        b, r = t // (B_HEADS * nq), t % (B_HEADS * nq)
        return b, r // nq, r % nq

    cur = lambda t: block(jnp.minimum(t, n_blocks - 1))
    prev = lambda t: block(jnp.maximum(t - 1, 0))

    def q_map(t):
        b, h, i = cur(t)
        return b, i, h

    def kq_map(t):
        b, h, _ = cur(t)
        return b, 0, h

    def vt_map(t):
        b, h, _ = prev(t)
        return b, h, 0

    def o_map(t):
        b, h, i = prev(t)
        return b, i, h

    in_specs = [pl.BlockSpec(memory_space=pltpu.SMEM),
                pl.BlockSpec((None, tq, LANES), q_map),
                pl.BlockSpec((None, lq, LANES), kq_map),
                pl.BlockSpec((1, LANES), lambda t: (0, 0)),
                _const_spec(ind)]
    args = [scal, q, q, subln_g, ind]
    for k, vt in segs:
        n_keys = k.shape[1]
        in_specs += [pl.BlockSpec((None, n_keys, LANES), kq_map),
                     pl.BlockSpec((None, LANES, n_keys), vt_map)]
        args += [k, vt]
    seg_lens = tuple(k.shape[1] for k, _ in segs)
    n_chunks = sum(-(-n // kc) for n in seg_lens)
    return pl.pallas_call(
        functools.partial(_diff_kernel, seg_lens=seg_lens, kc=kc, nq=nq),
        grid=(n_blocks + 1,),
        in_specs=in_specs,
        out_specs=pl.BlockSpec((None, tq, LANES), o_map),
        out_shape=jax.ShapeDtypeStruct((bsz, lq, MIX_W), BF16),
        scratch_shapes=[pltpu.VMEM((sum(seg_lens), 2 * tq), BF16),
                        pltpu.VMEM((-(-n_chunks // 8) * 8, 2 * tq), F32),
                        pltpu.SMEM((1,), jnp.int32)],
        compiler_params=_cparams(1),
        name="diff",
    )(*args)


WIN_SUB = 8


def _win_kernel(sink_ref, q_ref, *refs, local):
    o_ref = refs[-1]
    n_sub = q_ref.shape[0] // BLOCK
    kx_ref, vxt_ref = refs[-3], refs[-2]
    n = pl.program_id(1)
    nb = pl.num_programs(1)
    nq = C_Q_HEADS // 2 * BLOCK
    lane = lax.broadcasted_iota(jnp.int32, (BLOCK, LANES), 1)
    headi = lax.broadcasted_iota(jnp.int32, (1, nq), 1) // BLOCK
    if local:
        kp_ref, kc_ref, kn_ref, vpt_ref, vct_ref, vnt_ref = refs[:6]
        key = lax.broadcasted_iota(jnp.int32, (BLOCK, nq), 0)
        qry = lax.broadcasted_iota(jnp.int32, (BLOCK, nq), 1) & (BLOCK - 1)
        tri_prev, tri_next = key >= qry, key <= qry

    probs = [(sub, kvh) for sub in range(n_sub) for kvh in range(2)]
    s_l, sink_l, vt_l = [], [], []
    for sub, kvh in probs:
        ksl = slice(kvh * LANES, (kvh + 1) * LANES)
        rows = slice(sub * BLOCK, (sub + 1) * BLOCK)
        parts = []
        for g in range(4):
            grp = kvh * 2 + g // 2
            qg = q_ref[rows, grp * LANES:(grp + 1) * LANES]
            parts.append(jnp.where((lane < 64) if g % 2 == 0 else (lane >= 64), qg, jnp.zeros_like(qg)))
        qs = jnp.concatenate(parts, axis=0)
        sink = jnp.zeros((1, nq), F32)
        for g in range(4):
            sink = jnp.where(headi == g, sink_ref[kvh * 4 + g] * LOG2E, sink)
        if local:
            kb, vb = [], []
            for j in (sub - 1, sub, sub + 1):
                if j < 0:
                    kb.append(kp_ref[:, ksl]), vb.append(vpt_ref[ksl, :])
                elif j >= n_sub:
                    kb.append(kn_ref[:, ksl]), vb.append(vnt_ref[ksl, :])
                else:
                    kb.append(kc_ref[j * BLOCK:(j + 1) * BLOCK, ksl])
                    vb.append(vct_ref[ksl, j * BLOCK:(j + 1) * BLOCK])
            k_all = jnp.concatenate(kb + [kx_ref[:, ksl]], axis=0)
            vt_all = jnp.concatenate(vb + [vxt_ref[ksl, :]], axis=1)
        else:
            k_all, vt_all = kx_ref[:, ksl], vxt_ref[ksl, :]
        s = _dot_nt(k_all, qs)
        if local:
            valid_prev = tri_prev if sub > 0 else tri_prev & (n > 0)
            valid_next = tri_next if sub < n_sub - 1 else tri_next & (n < nb - 1)
            s = jnp.concatenate([jnp.where(valid_prev, s[0:BLOCK], NEG_INF), s[BLOCK:2 * BLOCK],
                                 jnp.where(valid_next, s[2 * BLOCK:3 * BLOCK], NEG_INF), s[3 * BLOCK:]], axis=0)
        s_l.append(s)
        sink_l.append(sink)
        vt_l.append(vt_all)
    m_l = [jnp.maximum(jnp.max(s, axis=0, keepdims=True), sink) for s, sink in zip(s_l, sink_l)]
    p_l = [jnp.exp2(s - m) for s, m in zip(s_l, m_l)]
    l_l = [jnp.sum(p, axis=0, keepdims=True) + jnp.exp2(sink - m) for p, sink, m in zip(p_l, sink_l, m_l)]
    ot_l = [_dot(vt, p.astype(BF16)) / l for vt, p, l in zip(vt_l, p_l, l_l)]
    for (sub, kvh), o_t in zip(probs, ot_l):
        o = o_t.T
        for j in range(2):
            grp = kvh * 2 + j
            o_ref[sub * BLOCK:(sub + 1) * BLOCK, grp * LANES:(grp + 1) * LANES] = jnp.where(
                lane < 64, o[2 * j * BLOCK:(2 * j + 1) * BLOCK], o[(2 * j + 1) * BLOCK:(2 * j + 2) * BLOCK]
            ).astype(BF16)


def _win_call(sinks, q, kd, vdt, kxd, vxdt, *, local):
    bsz, lq, _ = q.shape
    nb = lq // BLOCK
    n_sub = min(WIN_SUB, nb)
    tq = n_sub * BLOCK
    in_specs = [pl.BlockSpec(memory_space=pltpu.SMEM), pl.BlockSpec((None, tq, MIX_W), lambda b, n: (b, n, 0))]
    args = [sinks, q]
    if local:
        prv = lambda n: jnp.maximum(n * n_sub - 1, 0)
        nxt = lambda n: jnp.minimum((n + 1) * n_sub, nb - 1)
        in_specs += [pl.BlockSpec((None, BLOCK, 2 * LANES), lambda b, n: (b, prv(n), 0)),
                     pl.BlockSpec((None, tq, 2 * LANES), lambda b, n: (b, n, 0)),
                     pl.BlockSpec((None, BLOCK, 2 * LANES), lambda b, n: (b, nxt(n), 0)),
                     pl.BlockSpec((None, 2 * LANES, BLOCK), lambda b, n: (b, 0, prv(n))),
                     pl.BlockSpec((None, 2 * LANES, tq), lambda b, n: (b, 0, n)),
                     pl.BlockSpec((None, 2 * LANES, BLOCK), lambda b, n: (b, 0, nxt(n)))]
        args += [kd] * 3 + [vdt] * 3
    n_ctx = kxd.shape[1]
    in_specs += [pl.BlockSpec((None, n_ctx, 2 * LANES), lambda b, n: (b, 0, 0)),
                 pl.BlockSpec((None, 2 * LANES, n_ctx), lambda b, n: (b, 0, 0))]
    args += [kxd, vxdt]
    return pl.pallas_call(
        functools.partial(_win_kernel, local=local),
        grid=(bsz, lq // tq),
        in_specs=in_specs,
        out_specs=pl.BlockSpec((None, tq, MIX_W), lambda b, n: (b, n, 0)),
        out_shape=jax.ShapeDtypeStruct((bsz, lq, MIX_W), BF16),
        compiler_params=_cparams(2),
        name="win",
    )(*args)


def _merge_kernel(x_ref, mul_ref, sh_ref, gate_ref, wuvg_ref, lng_ref, lnb_ref, ws_ref, bs_ref,
                  ob_ref, oc_ref, wbr_ref, wout_ref, o_ref):
    x = x_ref[...]
    tm = x.shape[0]
    h = _rms_mod(x, mul_ref[...], sh_ref[...]).astype(BF16)
    u = _gelu_tanh(_dot(h, wuvg_ref[:, 0:MIX_W]))
    v = _gelu_tanh(_dot(h, wuvg_ref[:, MIX_W:2 * MIX_W]))
    mu = jnp.mean(v, axis=-1, keepdims=True)
    vc = v - mu
    var = jnp.mean(vc * vc, axis=-1, keepdims=True)
    vn = (vc * lax.rsqrt(var + EPS) * lng_ref[...] + lnb_ref[...]).astype(BF16)
    n_chunks = tm // BLOCK
    mixed = []
    for g in range(A_GROUPS):
        gs = slice(g * LANES, (g + 1) * LANES)
        rhs = jnp.concatenate([vn[c * BLOCK:(c + 1) * BLOCK, gs] for c in range(n_chunks)], axis=1)
        mixed.append(_dot(ws_ref[g], rhs))
    o_a = (u * jnp.concatenate(
        [jnp.concatenate([mixed[g][:, c * LANES:(c + 1) * LANES] + bs_ref[g] for g in range(A_GROUPS)], axis=1)
         for c in range(n_chunks)], axis=0)).astype(BF16)
    y = None
    for i, o_i in enumerate((o_a, ob_ref[...], oc_ref[...])):
        col0 = 2 * MIX_W + i * D_MODEL
        gate = jax.nn.sigmoid(_dot(h, wuvg_ref[:, col0:col0 + D_MODEL]))
        t = gate * _dot(o_i, wbr_ref[i])
        y = t if y is None else y + t
    o_ref[...] = x + gate_ref[...] * _dot(y.astype(BF16), wout_ref[...])


def _merge_call(x, mul, sh, gate, w_uvg, ln_g, ln_b, w_s, b_s, o_b, o_c, w_br, w_out, *, layer, tm):
    bsz, length, _ = x.shape
    tok = lambda w: pl.BlockSpec((None, tm, w), lambda b, i: (b, i, 0))
    vec = pl.BlockSpec((None, 1, D_MODEL), lambda b, i: (b, 0, 0))
    full = functools.partial(_layer_spec, layer=layer)
    return pl.pallas_call(
        _merge_kernel,
        grid=(bsz, length // tm),
        in_specs=[tok(D_MODEL), vec, vec, vec, full(w_uvg), full(ln_g), full(ln_b), full(w_s), full(b_s),
                  tok(MIX_W), tok(MIX_W), full(w_br), full(w_out)],
        out_specs=tok(D_MODEL),
        out_shape=jax.ShapeDtypeStruct(x.shape, F32),
        compiler_params=_cparams(2),
        name="merge",
    )(x, mul, sh, gate, w_uvg, ln_g, ln_b, w_s, b_s, o_b, o_c, w_br, w_out)


HALO = 8
MXU_TILE = 256
FF_EDGES = (0, 4 * MXU_TILE, 8 * MXU_TILE, D_FF)


def _ffn_kernel(x_ref, xp_ref, xn_ref, mul_ref, sh_ref, gate_ref, wg_ref, cw_ref, cb_ref, wu_ref, wd_ref,
                fg_ref, o_ref, *, final):
    i = pl.program_id(1)
    nt = pl.num_programs(1)
    x = x_ref[...]
    tm = x.shape[0]
    xe = jnp.concatenate([xp_ref[...], x, xn_ref[...]], axis=0)
    he = _rms_mod(xe, mul_ref[...], sh_ref[...]).astype(BF16)
    hm = he[HALO:HALO + tm]
    rowe = lax.broadcasted_iota(jnp.int32, (tm + 2 * HALO, 1), 0)
    keep = ((rowe >= HALO) | (i > 0)) & ((rowe < HALO + tm) | (i < nt - 1))
    spans = [slice(c0, c1) for c0, c1 in zip(FF_EDGES[:-1], FF_EDGES[1:])]
    a_l = [jnp.where(keep, _dot(he, wg_ref[:, cs]), 0.0) for cs in spans]
    u_l = [_dot(hm, wu_ref[:, cs]) for cs in spans]
    z_l = []
    for cs, a, u in zip(spans, a_l, u_l):
        a_prev = pltpu.roll(a, 1, 0)[HALO:HALO + tm]
        a_next = pltpu.roll(a, tm + 2 * HALO - 1, 0)[HALO:HALO + tm]
        a = (a_prev * cw_ref[0:1, cs] + a[HALO:HALO + tm] * cw_ref[1:2, cs] + a_next * cw_ref[2:3, cs]
             + cb_ref[:, cs])
        z_l.append((a * jax.nn.sigmoid(a) * u).astype(BF16))
    acc = None
    for cs, z in zip(spans, z_l):
        t = _dot(z, wd_ref[cs, :])
        acc = t if acc is None else acc + t
    out = x + gate_ref[...] * acc
    if final:
        ms = jnp.mean(out * out, axis=-1, keepdims=True)
        out = out * lax.rsqrt(ms + EPS) * fg_ref[...]
    o_ref[...] = out


def _ffn_call(x, mul, sh, gate, w_gate, conv_w, conv_b, w_up, w_down, final_g, *, layer, tm, final):
    bsz, length, _ = x.shape
    per = tm // HALO
    last = length // HALO - 1
    tok = pl.BlockSpec((None, tm, D_MODEL), lambda b, i: (b, i, 0))
    prv = pl.BlockSpec((None, HALO, D_MODEL), lambda b, i: (b, jnp.maximum(i * per - 1, 0), 0))
    nxt = pl.BlockSpec((None, HALO, D_MODEL), lambda b, i: (b, jnp.minimum((i + 1) * per, last), 0))
    vec = pl.BlockSpec((None, 1, D_MODEL), lambda b, i: (b, 0, 0))
    full = functools.partial(_layer_spec, layer=layer)
    return pl.pallas_call(
        functools.partial(_ffn_kernel, final=final),
        grid=(bsz, length // tm),
        in_specs=[tok, prv, nxt, vec, vec, vec, full(w_gate), full(conv_w), full(conv_b), full(w_up),
                  full(w_down), _const_spec(final_g)],
        out_specs=tok,
        out_shape=jax.ShapeDtypeStruct(x.shape, F32),
        compiler_params=_cparams(2),
        name="ffn",
    )(x, x, x, mul, sh, gate, w_gate, conv_w, conv_b, w_up, w_down, final_g)


def _rope_tables(length):
    pos = jnp.arange(length)
    rows = (pos // GRID_W).astype(F32)
    cols = (pos % GRID_W).astype(F32)
    half = HEAD_DIM // 2
    inv = ROPE_BASE ** (-jnp.arange(0, half, 2, dtype=F32) / half)
    ang_r = rows[:, None] * inv[None, :]
    ang_c = cols[:, None] * inv[None, :]
    zero = jnp.zeros_like(ang_r)
    cos = jnp.concatenate([jnp.cos(ang_r)] * 2 + [jnp.cos(ang_c)] * 2, axis=1)
    sa = jnp.concatenate([-jnp.sin(ang_r), zero, -jnp.sin(ang_c), zero], axis=1)
    sb = jnp.concatenate([zero, jnp.sin(ang_r), zero, jnp.sin(ang_c)], axis=1)
    return tuple(jnp.tile(t, (1, LANES // HEAD_DIM)) for t in (cos, sa, sb))


def kernel(x, c, ctx, c_ctx, w_ada, b_ada, norm1_g, w_in, sgu_ln_g, sgu_ln_b, w_s, b_s, lam_q1, lam_k1,
           lam_q2, lam_k2, diff_subln_g, sinks, w_branch, w_out, norm2_g, w_gate, conv_w, conv_b, w_up,
           w_down, final_g):
    bsz, length, _ = x.shape
    n_ctx = ctx.shape[1]
    tables = _rope_tables(length)
    ctx_tables = tuple(t[:n_ctx] for t in tables)

    rows = -(-(bsz + 1) // 8) * 8
    vpad = jnp.zeros((rows, D_MODEL), F32).at[:bsz].set(c).at[bsz].set(c_ctx)
    ada = _ada_call(vpad, w_ada, b_ada)

    qscale = jnp.ones((ATTN_COLS,), F32).at[0:512].set(SCALE * LOG2E).at[1536:2048].set(SCALE * LOG2E)
    w_attn = (w_in[:, :, 2 * MIX_W:2 * MIX_W + ATTN_COLS] * qscale).astype(BF16)
    w_uvg = jnp.concatenate([w_in[:, :, :2 * MIX_W], w_in[:, :, GATE_COL0:]], axis=2).astype(BF16)
    w_s_b = w_s.astype(BF16)
    b_s_b = jnp.broadcast_to(b_s[..., None], b_s.shape + (LANES,))
    w_br_b = w_branch.astype(BF16)
    w_out_b = w_out.astype(BF16)
    w_gate_b = w_gate.astype(BF16)
    w_up_b = w_up.astype(BF16)
    w_down_b = w_down.astype(BF16)
    final_g2 = final_g.reshape(1, D_MODEL)
    mix_w = (w_uvg, sgu_ln_g.reshape(DEPTH, 1, MIX_W), sgu_ln_b.reshape(DEPTH, 1, MIX_W), w_s_b, b_s_b)
    ffn_w = (w_gate_b, conv_w, conv_b.reshape(DEPTH, 1, D_FF), w_up_b, w_down_b, final_g2)

    for l in range(DEPTH):
        last = l == DEPTH - 1
        m = ada[l]
        sh1, sc1, g1, sh2, sc2, g2 = [m[:, j * D_MODEL:(j + 1) * D_MODEL] for j in range(6)]
        mul1 = norm1_g[l][None, :] * (1.0 + sc1)
        mul2 = norm2_g[l][None, :] * (1.0 + sc2)
        lat = lambda a: a[:bsz, None, :]
        cx = lambda a: jnp.broadcast_to(a[bsz][None, None, :], (bsz, 1, D_MODEL))

        lam_init = 0.8 - 0.6 * float(np.exp(-0.3 * l))
        lam = jnp.exp(jnp.sum(lam_q1[l] * lam_k1[l])) - jnp.exp(jnp.sum(lam_q2[l] * lam_k2[l])) + lam_init
        scal = jnp.stack([lam, jnp.float32(1.0 - lam_init)]).astype(F32)
        subln = diff_subln_g[l].reshape(1, LANES)

        bq, bk, bvt, cq, ckd, cvdt = _inproj_call(x, lat(mul1), lat(sh1), w_attn, tables, layer=l, tm=512,
                                                  rope=True)
        xbq, xbk, xbvt, xcq, xckd, xcvdt = _inproj_call(ctx, cx(mul1), cx(sh1), w_attn, ctx_tables, layer=l,
                                                        tm=n_ctx, rope=False)

        o_b = _diff_call(scal, bq, subln, [(bk, bvt), (xbk, xbvt)], tq=512, kc=256)
        o_c = _win_call(sinks[l], cq, ckd, cvdt, xckd, xcvdt, local=True)
        x_mid = _merge_call(x, lat(mul1), lat(sh1), lat(g1), *mix_w, o_b, o_c, w_br_b, w_out_b, layer=l, tm=1024)
        if not last:
            xo_b = _diff_call(scal, xbq, subln, [(xbk, xbvt)], tq=n_ctx, kc=512)
            xo_c = _win_call(sinks[l], xcq, None, None, xckd, xcvdt, local=False)
            ctx_mid = _merge_call(ctx, cx(mul1), cx(sh1), cx(g1), *mix_w, xo_b, xo_c, w_br_b, w_out_b, layer=l,
                                  tm=n_ctx)
        x = _ffn_call(x_mid, lat(mul2), lat(sh2), lat(g2), *ffn_w, layer=l, tm=1024, final=last)
        if not last:
            ctx = _ffn_call(ctx_mid, cx(mul2), cx(sh2), cx(g2), *ffn_w, layer=l, tm=n_ctx, final=False)
    return x
```

```python
import functools

import numpy as np
import jax
import jax.numpy as jnp
from jax import lax
from jax.experimental import pallas as pl
from jax.experimental.pallas import tpu as pltpu

F32 = jnp.float32
BF16 = jnp.bfloat16

D_MODEL = 1024
DEPTH = 4
GRID_W = 64
HEAD_DIM = 64
BLOCK = 128
MIX_W = 512
A_GROUPS = 4
B_HEADS = 4
C_Q_HEADS = 8
D_FF = 2816
ROPE_BASE = 10000.0
EPS = 1e-6
SCALE = HEAD_DIM ** -0.5
NEG_INF = -1e30
LOG2E = float(np.log2(np.e))

LANES = 128
ATTN_COLS = 2304
GATE_COL0 = 3328
VMEM_LIMIT = 56 * 1024 * 1024


def _cparams(n_axes):
    return pltpu.CompilerParams(
        dimension_semantics=("arbitrary",) * n_axes, vmem_limit_bytes=VMEM_LIMIT)


def _const_spec(a):
    return pl.BlockSpec(a.shape, lambda *_: (0,) * a.ndim, pipeline_mode=pl.Buffered(1))


def _layer_spec(a, layer):
    return pl.BlockSpec((None,) + a.shape[1:], lambda *_: (layer,) + (0,) * (a.ndim - 1),
                        pipeline_mode=pl.Buffered(1))


def _rms_mod(x, mul, sh):
    ms = jnp.mean(x * x, axis=-1, keepdims=True)
    return x * lax.rsqrt(ms + EPS) * mul + sh


def _gelu_tanh(x):
    c = np.float32(np.sqrt(2.0 / np.pi))
    return 0.5 * x * (1.0 + jnp.tanh(c * (x + 0.044715 * (x * x * x))))


def _dot(a, b):
    return jnp.dot(a, b, preferred_element_type=F32)


def _dot_nt(a, b):
    return lax.dot_general(a, b, (((1,), (1,)), ((), ())), preferred_element_type=F32)


def _ada_kernel(v_ref, w_ref, b_ref, o_ref):
    v = v_ref[...]
    s = v * jax.nn.sigmoid(v)
    o_ref[...] = _dot(s.astype(BF16), w_ref[...].astype(BF16)) + b_ref[...]


def _ada_call(vpad, w_ada, b_ada):
    rows = vpad.shape[0]
    tn = 1536
    return pl.pallas_call(
        _ada_kernel,
        grid=(DEPTH, 6 * D_MODEL // tn),
        in_specs=[
            pl.BlockSpec((rows, D_MODEL), lambda l, j: (0, 0)),
            pl.BlockSpec((None, D_MODEL, tn), lambda l, j: (l, 0, j)),
            pl.BlockSpec((None, 1, tn), lambda l, j: (l, 0, j)),
        ],
        out_specs=pl.BlockSpec((None, rows, tn), lambda l, j: (l, 0, j)),
        out_shape=jax.ShapeDtypeStruct((DEPTH, rows, 6 * D_MODEL), F32),
        compiler_params=_cparams(2),
        name="ada",
    )(vpad, w_ada, b_ada.reshape(DEPTH, 1, 6 * D_MODEL))


def _inproj_kernel(x_ref, mul_ref, sh_ref, w_ref, cos_ref, sa_ref, sb_ref,
                   bq_ref, bk_ref, bvt_ref, cq_ref, ckd_ref, cvdt_ref, *, rope):
    h = _rms_mod(x_ref[...], mul_ref[...], sh_ref[...]).astype(BF16)
    if rope:
        cos, sa, sb = cos_ref[...], sa_ref[...], sb_ref[...]

    def rot(y):
        if not rope:
            return y
        return y * cos + pltpu.roll(y, LANES - 16, 1) * sa + pltpu.roll(y, 16, 1) * sb

    def dup(y):
        lane = lax.broadcasted_iota(jnp.int32, y.shape, 1)
        sw = pltpu.roll(y, 64, 1)
        return jnp.concatenate([jnp.where(lane < 64, y, sw), jnp.where(lane < 64, sw, y)], axis=1)

    for col0, ref in ((0, bq_ref), (512, bk_ref), (1536, cq_ref)):
        y = _dot(h, w_ref[:, col0:col0 + 512])
        for g in range(4):
            ref[:, g * LANES:(g + 1) * LANES] = rot(y[:, g * LANES:(g + 1) * LANES]).astype(BF16)
    ykv = _dot(h, w_ref[:, 2048:2304])
    ckd_ref[...] = dup(rot(ykv[:, :LANES])).astype(BF16)
    cvdt_ref[...] = dup(ykv[:, LANES:]).T.astype(BF16)
    bvt_ref[...] = _dot(h, w_ref[:, 1024:1536]).T.astype(BF16)


def _inproj_call(x, mul, sh, w_attn, tables, *, layer, tm, rope):
    bsz, length, _ = x.shape
    tok = lambda w: pl.BlockSpec((None, tm, w), lambda b, i: (b, i, 0))
    vec = pl.BlockSpec((None, 1, D_MODEL), lambda b, i: (b, 0, 0))
    tab = pl.BlockSpec((tm, LANES), lambda b, i: (i, 0))
    tok_t = lambda w: pl.BlockSpec((None, w, tm), lambda b, i: (b, 0, i))
    out = lambda w: jax.ShapeDtypeStruct((bsz, length, w), BF16)
    out_t = lambda w: jax.ShapeDtypeStruct((bsz, w, length), BF16)
    return pl.pallas_call(
        functools.partial(_inproj_kernel, rope=rope),
        grid=(bsz, length // tm),
        in_specs=[tok(D_MODEL), vec, vec,
                  _layer_spec(w_attn, layer), tab, tab, tab],
        out_specs=[tok(512), tok(512), tok_t(MIX_W), tok(512), tok(256), tok_t(2 * LANES)],
        out_shape=[out(512), out(512), out_t(MIX_W), out(512), out(256), out_t(2 * LANES)],
        compiler_params=_cparams(2),
        name="inproj",
    )(x, mul, sh, w_attn, *tables)


SAFE_EXP2_RANGE = 90.0


def _diff_kernel(sc_ref, q_ref, qall_ref, g_ref, ind_ref, *refs, seg_lens, kc, nq):
    o_ref, p_scr, f_scr, safe_scr = refs[-4:]
    t = pl.program_id(0)
    tq = q_ref.shape[0]
    lanei = lax.broadcasted_iota(jnp.int32, (1, 2 * tq), 1)

    @pl.when(t == 0)
    def _():
        p_scr[...] = jnp.zeros(p_scr.shape, BF16)
        f_scr[...] = jnp.zeros(f_scr.shape, F32)

    chunks = []
    row0 = 0
    for si, n_keys in enumerate(seg_lens):
        for st in range(0, n_keys, kc):
            size = min(kc, n_keys - st)
            chunks.append((refs[2 * si], refs[2 * si + 1], st, size, row0))
            row0 += size

    def max_sq_norm(x):
        xf = x.astype(F32)
        return jnp.max(_dot((xf * xf).astype(BF16), ind_ref[...]), axis=0, keepdims=True)

    @pl.when(t % nq == 0)
    def _():
        kn = None
        for k_ref, _, st, size, _ in chunks:
            n2 = max_sq_norm(k_ref[st:st + size, :])
            kn = n2 if kn is None else jnp.maximum(kn, n2)
        qn = None
        n_q = qall_ref.shape[0]
        for st in range(0, n_q, 2 * kc):
            n2 = max_sq_norm(qall_ref[st:min(st + 2 * kc, n_q), :])
            qn = n2 if qn is None else jnp.maximum(qn, n2)
        bound2 = jnp.max(kn * qn)
        safe_scr[0] = (bound2 < SAFE_EXP2_RANGE * SAFE_EXP2_RANGE).astype(jnp.int32)

    safe = safe_scr[0] == 1

    def step(max_free):
        q = q_ref[...]
        lane = lax.broadcasted_iota(jnp.int32, q.shape, 1)
        zero = jnp.zeros_like(q)
        qq = jnp.concatenate([jnp.where(lane < 64, q, zero), jnp.where(lane >= 64, q, zero)], axis=0)
        f_prev = f_scr[...]
        mcs, lcs = [], []
        acc = None
        for c, (k_ref, vt_ref, st, size, r0) in enumerate(chunks):
            f1 = f_prev[c:c + 1, 0:tq].astype(BF16)
            f2 = f_prev[c:c + 1, tq:2 * tq].astype(BF16)
            a_t = p_scr[r0:r0 + size, 0:tq] * f1 - p_scr[r0:r0 + size, tq:2 * tq] * f2
            part = _dot(vt_ref[:, st:st + size], a_t)
            acc = part if acc is None else acc + part
            s = _dot_nt(k_ref[st:st + size, :], qq)
            if max_free:
                p = jnp.exp2(s)
            else:
                mc = jnp.max(s, axis=0, keepdims=True)
                p = jnp.exp2(s - mc)
                mcs.append(mc)
            lcs.append(jnp.sum(p, axis=0, keepdims=True))
            p_scr[r0:r0 + size, :] = p.astype(BF16)

        if max_free:
            es = [None] * len(chunks)
            l_all = functools.reduce(lambda a, b: a + b, lcs)
        else:
            m_all = functools.reduce(jnp.maximum, mcs)
            es = [jnp.exp2(mc - m_all) for mc in mcs]
            l_all = functools.reduce(lambda a, b: a + b, [lc * e for lc, e in zip(lcs, es)])
        coef = jnp.where(lanei < tq, 1.0, sc_ref[0]) / l_all
        for c, e in enumerate(es):
            f_scr[c:c + 1, :] = coef if e is None else e * coef

        o = acc.T
        ms = jnp.mean(o * o, axis=-1, keepdims=True)
        o_ref[...] = (o * lax.rsqrt(ms + EPS) * g_ref[...] * sc_ref[1]).astype(BF16)

    pl.when(safe)(lambda: step(True))
    pl.when(jnp.logical_not(safe))(lambda: step(False))


def _diff_call(scal, q, subln_g, segs, *, tq, kc):
    bsz, lq, _ = q.shape
    nq = lq // tq
    n_blocks = bsz * B_HEADS * nq
    half = np.arange(LANES) // HEAD_DIM
    ind = jnp.asarray(half[:, None] == half[None, :], BF16)

    def block(t):
        b, r = t // (B_HEADS * nq), t % (B_HEADS * nq)
        return b, r // nq, r % nq

    cur = lambda t: block(jnp.minimum(t, n_blocks - 1))
    prev = lambda t: block(jnp.maximum(t - 1, 0))

    def q_map(t):
        b, h, i = cur(t)
        return b, i, h

    def kq_map(t):
        b, h, _ = cur(t)
        return b, 0, h

    def vt_map(t):
        b, h, _ = prev(t)
        return b, h, 0

    def o_map(t):
        b, h, i = prev(t)
        return b, i, h

    in_specs = [pl.BlockSpec(memory_space=pltpu.SMEM),
                pl.BlockSpec((None, tq, LANES), q_map),
                pl.BlockSpec((None, lq, LANES), kq_map),
                pl.BlockSpec((1, LANES), lambda t: (0, 0)),
                _const_spec(ind)]
    args = [scal, q, q, subln_g, ind]
    for k, vt in segs:
        n_keys = k.shape[1]
        in_specs += [pl.BlockSpec((None, n_keys, LANES), kq_map),
                     pl.BlockSpec((None, LANES, n_keys), vt_map)]
        args += [k, vt]
    seg_lens = tuple(k.shape[1] for k, _ in segs)
    n_chunks = sum(-(-n // kc) for n in seg_lens)
    return pl.pallas_call(
        functools.partial(_diff_kernel, seg_lens=seg_lens, kc=kc, nq=nq),
        grid=(n_blocks + 1,),
        in_specs=in_specs,
        out_specs=pl.BlockSpec((None, tq, LANES), o_map),
        out_shape=jax.ShapeDtypeStruct((bsz, lq, MIX_W), BF16),
        scratch_shapes=[pltpu.VMEM((sum(seg_lens), 2 * tq), BF16),
                        pltpu.VMEM((-(-n_chunks // 8) * 8, 2 * tq), F32),
                        pltpu.SMEM((1,), jnp.int32)],
        compiler_params=_cparams(1),
        name="diff",
    )(*args)


WIN_SUB = 8


def _win_kernel(sink_ref, q_ref, *refs, local):
    o_ref = refs[-1]
    n_sub = q_ref.shape[0] // BLOCK
    kx_ref, vxt_ref = refs[-3], refs[-2]
    n = pl.program_id(1)
    nb = pl.num_programs(1)
    nq = C_Q_HEADS // 2 * BLOCK
    lane = lax.broadcasted_iota(jnp.int32, (BLOCK, LANES), 1)
    headi = lax.broadcasted_iota(jnp.int32, (1, nq), 1) // BLOCK
    if local:
        kp_ref, kc_ref, kn_ref, vpt_ref, vct_ref, vnt_ref = refs[:6]
        key = lax.broadcasted_iota(jnp.int32, (BLOCK, nq), 0)
        qry = lax.broadcasted_iota(jnp.int32, (BLOCK, nq), 1) & (BLOCK - 1)
        tri_prev, tri_next = key >= qry, key <= qry

    probs = [(sub, kvh) for sub in range(n_sub) for kvh in range(2)]
    s_l, sink_l, vt_l = [], [], []
    for sub, kvh in probs:
        ksl = slice(kvh * LANES, (kvh + 1) * LANES)
        rows = slice(sub * BLOCK, (sub + 1) * BLOCK)
        parts = []
        for g in range(4):
            grp = kvh * 2 + g // 2
            qg = q_ref[rows, grp * LANES:(grp + 1) * LANES]
            parts.append(jnp.where((lane < 64) if g % 2 == 0 else (lane >= 64), qg, jnp.zeros_like(qg)))
        qs = jnp.concatenate(parts, axis=0)
        sink = jnp.zeros((1, nq), F32)
        for g in range(4):
            sink = jnp.where(headi == g, sink_ref[kvh * 4 + g] * LOG2E, sink)
        if local:
            kb, vb = [], []
            for j in (sub - 1, sub, sub + 1):
                if j < 0:
                    kb.append(kp_ref[:, ksl]), vb.append(vpt_ref[ksl, :])
                elif j >= n_sub:
                    kb.append(kn_ref[:, ksl]), vb.append(vnt_ref[ksl, :])
                else:
                    kb.append(kc_ref[j * BLOCK:(j + 1) * BLOCK, ksl])
                    vb.append(vct_ref[ksl, j * BLOCK:(j + 1) * BLOCK])
            k_all = jnp.concatenate(kb + [kx_ref[:, ksl]], axis=0)
            vt_all = jnp.concatenate(vb + [vxt_ref[ksl, :]], axis=1)
        else:
            k_all, vt_all = kx_ref[:, ksl], vxt_ref[ksl, :]
        s = _dot_nt(k_all, qs)
        if local:
            valid_prev = tri_prev if sub > 0 else tri_prev & (n > 0)
            valid_next = tri_next if sub < n_sub - 1 else tri_next & (n < nb - 1)
            s = jnp.concatenate([jnp.where(valid_prev, s[0:BLOCK], NEG_INF), s[BLOCK:2 * BLOCK],
                                 jnp.where(valid_next, s[2 * BLOCK:3 * BLOCK], NEG_INF), s[3 * BLOCK:]], axis=0)
        s_l.append(s)
        sink_l.append(sink)
        vt_l.append(vt_all)
    m_l = [jnp.maximum(jnp.max(s, axis=0, keepdims=True), sink) for s, sink in zip(s_l, sink_l)]
    p_l = [jnp.exp2(s - m) for s, m in zip(s_l, m_l)]
    l_l = [jnp.sum(p, axis=0, keepdims=True) + jnp.exp2(sink - m) for p, sink, m in zip(p_l, sink_l, m_l)]
    ot_l = [_dot(vt, p.astype(BF16)) / l for vt, p, l in zip(vt_l, p_l, l_l)]
    for (sub, kvh), o_t in zip(probs, ot_l):
        o = o_t.T
        for j in range(2):
            grp = kvh * 2 + j
            o_ref[sub * BLOCK:(sub + 1) * BLOCK, grp * LANES:(grp + 1) * LANES] = jnp.where(
                lane < 64, o[2 * j * BLOCK:(2 * j + 1) * BLOCK], o[(2 * j + 1) * BLOCK:(2 * j + 2) * BLOCK]
            ).astype(BF16)


def _win_call(sinks, q, kd, vdt, kxd, vxdt, *, local):
    bsz, lq, _ = q.shape
    nb = lq // BLOCK
    n_sub = min(WIN_SUB, nb)
    tq = n_sub * BLOCK
    in_specs = [pl.BlockSpec(memory_space=pltpu.SMEM), pl.BlockSpec((None, tq, MIX_W), lambda b, n: (b, n, 0))]
    args = [sinks, q]
    if local:
        prv = lambda n: jnp.maximum(n * n_sub - 1, 0)
        nxt = lambda n: jnp.minimum((n + 1) * n_sub, nb - 1)
        in_specs += [pl.BlockSpec((None, BLOCK, 2 * LANES), lambda b, n: (b, prv(n), 0)),
                     pl.BlockSpec((None, tq, 2 * LANES), lambda b, n: (b, n, 0)),
                     pl.BlockSpec((None, BLOCK, 2 * LANES), lambda b, n: (b, nxt(n), 0)),
                     pl.BlockSpec((None, 2 * LANES, BLOCK), lambda b, n: (b, 0, prv(n))),
                     pl.BlockSpec((None, 2 * LANES, tq), lambda b, n: (b, 0, n)),
                     pl.BlockSpec((None, 2 * LANES, BLOCK), lambda b, n: (b, 0, nxt(n)))]
        args += [kd] * 3 + [vdt] * 3
    n_ctx = kxd.shape[1]
    in_specs += [pl.BlockSpec((None, n_ctx, 2 * LANES), lambda b, n: (b, 0, 0)),
                 pl.BlockSpec((None, 2 * LANES, n_ctx), lambda b, n: (b, 0, 0))]
    args += [kxd, vxdt]
    return pl.pallas_call(
        functools.partial(_win_kernel, local=local),
        grid=(bsz, lq // tq),
        in_specs=in_specs,
        out_specs=pl.BlockSpec((None, tq, MIX_W), lambda b, n: (b, n, 0)),
        out_shape=jax.ShapeDtypeStruct((bsz, lq, MIX_W), BF16),
        compiler_params=_cparams(2),
        name="win",
    )(*args)


def _merge_kernel(x_ref, mul_ref, sh_ref, gate_ref, wuv_ref, wg_ref, lng_ref, lnb_ref, ws_ref, bs_ref,
                  ob_ref, oc_ref, wbr_ref, wout_ref, o_ref):
    x = x_ref[...]
    tm = x.shape[0]
    h = _rms_mod(x, mul_ref[...], sh_ref[...]).astype(BF16)
    u = _gelu_tanh(_dot(h, wuv_ref[:, 0:MIX_W]))
    v = _gelu_tanh(_dot(h, wuv_ref[:, MIX_W:2 * MIX_W]))
    mu = jnp.mean(v, axis=-1, keepdims=True)
    vc = v - mu
    var = jnp.mean(vc * vc, axis=-1, keepdims=True)
    vn = (vc * lax.rsqrt(var + EPS) * lng_ref[...] + lnb_ref[...]).astype(BF16)
    n_chunks = tm // BLOCK
    mixed = []
    for g in range(A_GROUPS):
        gs = slice(g * LANES, (g + 1) * LANES)
        rhs = jnp.concatenate([vn[c * BLOCK:(c + 1) * BLOCK, gs] for c in range(n_chunks)], axis=1)
        mixed.append(_dot(ws_ref[g], rhs))
    o_a = (u * jnp.concatenate(
        [jnp.concatenate([mixed[g][:, c * LANES:(c + 1) * LANES] + bs_ref[g] for g in range(A_GROUPS)], axis=1)
         for c in range(n_chunks)], axis=0)).astype(BF16)
    y = None
    for i, o_i in enumerate((o_a, ob_ref[...], oc_ref[...])):
        gate = jax.nn.sigmoid(_dot(h, wg_ref[:, i * D_MODEL:(i + 1) * D_MODEL]))
        t = gate * _dot(o_i, wbr_ref[i])
        y = t if y is None else y + t
    o_ref[...] = x + gate_ref[...] * _dot(y.astype(BF16), wout_ref[...])


def _merge_call(x, mul, sh, gate, w_uv, w_g, ln_g, ln_b, w_s, b_s, o_b, o_c, w_br, w_out, *, layer, tm):
    bsz, length, _ = x.shape
    tok = lambda w: pl.BlockSpec((None, tm, w), lambda b, i: (b, i, 0))
    vec = pl.BlockSpec((None, 1, D_MODEL), lambda b, i: (b, 0, 0))
    full = functools.partial(_layer_spec, layer=layer)
    return pl.pallas_call(
        _merge_kernel,
        grid=(bsz, length // tm),
        in_specs=[tok(D_MODEL), vec, vec, vec, full(w_uv), full(w_g), full(ln_g), full(ln_b), full(w_s), full(b_s),
                  tok(MIX_W), tok(MIX_W), full(w_br), full(w_out)],
        out_specs=tok(D_MODEL),
        out_shape=jax.ShapeDtypeStruct(x.shape, F32),
        compiler_params=_cparams(2),
        name="merge",
    )(x, mul, sh, gate, w_uv, w_g, ln_g, ln_b, w_s, b_s, o_b, o_c, w_br, w_out)


HALO = 8
MXU_TILE = 256
FF_EDGES = (0, 4 * MXU_TILE, 8 * MXU_TILE, D_FF)


def _ffn_kernel(x_ref, xp_ref, xn_ref, mul_ref, sh_ref, gate_ref, wg_ref, cw_ref, cb_ref, wu_ref, wd_ref,
                fg_ref, o_ref, *, final):
    i = pl.program_id(1)
    nt = pl.num_programs(1)
    x = x_ref[...]
    tm = x.shape[0]
    xe = jnp.concatenate([xp_ref[...], x, xn_ref[...]], axis=0)
    he = _rms_mod(xe, mul_ref[...], sh_ref[...]).astype(BF16)
    hm = he[HALO:HALO + tm]
    rowe = lax.broadcasted_iota(jnp.int32, (tm + 2 * HALO, 1), 0)
    keep = ((rowe >= HALO) | (i > 0)) & ((rowe < HALO + tm) | (i < nt - 1))
    spans = [slice(c0, c1) for c0, c1 in zip(FF_EDGES[:-1], FF_EDGES[1:])]
    a_l = [jnp.where(keep, _dot(he, wg_ref[:, cs]), 0.0) for cs in spans]
    u_l = [_dot(hm, wu_ref[:, cs]) for cs in spans]
    z_l = []
    for cs, a, u in zip(spans, a_l, u_l):
        a_prev = pltpu.roll(a, 1, 0)[HALO:HALO + tm]
        a_next = pltpu.roll(a, tm + 2 * HALO - 1, 0)[HALO:HALO + tm]
        a = (a_prev * cw_ref[0:1, cs] + a[HALO:HALO + tm] * cw_ref[1:2, cs] + a_next * cw_ref[2:3, cs]
             + cb_ref[:, cs])
        z_l.append((a * jax.nn.sigmoid(a) * u).astype(BF16))
    acc = None
    for cs, z in zip(spans, z_l):
        t = _dot(z, wd_ref[cs, :])
        acc = t if acc is None else acc + t
    out = x + gate_ref[...] * acc
    if final:
        ms = jnp.mean(out * out, axis=-1, keepdims=True)
        out = out * lax.rsqrt(ms + EPS) * fg_ref[...]
    o_ref[...] = out


def _ffn_call(x, mul, sh, gate, w_gate, conv_w, conv_b, w_up, w_down, final_g, *, layer, tm, final):
    bsz, length, _ = x.shape
    per = tm // HALO
    last = length // HALO - 1
    tok = pl.BlockSpec((None, tm, D_MODEL), lambda b, i: (b, i, 0))
    prv = pl.BlockSpec((None, HALO, D_MODEL), lambda b, i: (b, jnp.maximum(i * per - 1, 0), 0))
    nxt = pl.BlockSpec((None, HALO, D_MODEL), lambda b, i: (b, jnp.minimum((i + 1) * per, last), 0))
    vec = pl.BlockSpec((None, 1, D_MODEL), lambda b, i: (b, 0, 0))
    full = functools.partial(_layer_spec, layer=layer)
    return pl.pallas_call(
        functools.partial(_ffn_kernel, final=final),
        grid=(bsz, length // tm),
        in_specs=[tok, prv, nxt, vec, vec, vec, full(w_gate), full(conv_w), full(conv_b), full(w_up),
                  full(w_down), _const_spec(final_g)],
        out_specs=tok,
        out_shape=jax.ShapeDtypeStruct(x.shape, F32),
        compiler_params=_cparams(2),
        name="ffn",
    )(x, x, x, mul, sh, gate, w_gate, conv_w, conv_b, w_up, w_down, final_g)


def _rope_tables(length):
    pos = jnp.arange(length)
    rows = (pos // GRID_W).astype(F32)
    cols = (pos % GRID_W).astype(F32)
    half = HEAD_DIM // 2
    inv = ROPE_BASE ** (-jnp.arange(0, half, 2, dtype=F32) / half)
    ang_r = rows[:, None] * inv[None, :]
    ang_c = cols[:, None] * inv[None, :]
    zero = jnp.zeros_like(ang_r)
    cos = jnp.concatenate([jnp.cos(ang_r)] * 2 + [jnp.cos(ang_c)] * 2, axis=1)
    sa = jnp.concatenate([-jnp.sin(ang_r), zero, -jnp.sin(ang_c), zero], axis=1)
    sb = jnp.concatenate([zero, jnp.sin(ang_r), zero, jnp.sin(ang_c)], axis=1)
    return tuple(jnp.tile(t, (1, LANES // HEAD_DIM)) for t in (cos, sa, sb))


def kernel(x, c, ctx, c_ctx, w_ada, b_ada, norm1_g, w_in, sgu_ln_g, sgu_ln_b, w_s, b_s, lam_q1, lam_k1,
           lam_q2, lam_k2, diff_subln_g, sinks, w_branch, w_out, norm2_g, w_gate, conv_w, conv_b, w_up,
           w_down, final_g):
    bsz, length, _ = x.shape
    n_ctx = ctx.shape[1]
    tables = _rope_tables(length)
    ctx_tables = tuple(t[:n_ctx] for t in tables)

    rows = -(-(bsz + 1) // 8) * 8
    vpad = jnp.zeros((rows, D_MODEL), F32).at[:bsz].set(c).at[bsz].set(c_ctx)
    ada = _ada_call(vpad, w_ada, b_ada)

    qscale = jnp.ones((ATTN_COLS,), F32).at[0:512].set(SCALE * LOG2E).at[1536:2048].set(SCALE * LOG2E)
    w_attn = (w_in[:, :, 2 * MIX_W:2 * MIX_W + ATTN_COLS] * qscale).astype(BF16)
    w_uv = w_in[:, :, :2 * MIX_W].astype(BF16)
    w_g = w_in[:, :, GATE_COL0:].astype(BF16)
    w_s_b = w_s.astype(BF16)
    b_s_b = jnp.broadcast_to(b_s[..., None], b_s.shape + (LANES,))
    w_br_b = w_branch.astype(BF16)
    w_out_b = w_out.astype(BF16)
    w_gate_b = w_gate.astype(BF16)
    w_up_b = w_up.astype(BF16)
    w_down_b = w_down.astype(BF16)
    final_g2 = final_g.reshape(1, D_MODEL)
    mix_w = (w_uv, w_g, sgu_ln_g.reshape(DEPTH, 1, MIX_W), sgu_ln_b.reshape(DEPTH, 1, MIX_W), w_s_b, b_s_b)
    ffn_w = (w_gate_b, conv_w, conv_b.reshape(DEPTH, 1, D_FF), w_up_b, w_down_b, final_g2)

    for l in range(DEPTH):
        last = l == DEPTH - 1
        m = ada[l]
        sh1, sc1, g1, sh2, sc2, g2 = [m[:, j * D_MODEL:(j + 1) * D_MODEL] for j in range(6)]
        mul1 = norm1_g[l][None, :] * (1.0 + sc1)
        mul2 = norm2_g[l][None, :] * (1.0 + sc2)
        lat = lambda a: a[:bsz, None, :]
        cx = lambda a: jnp.broadcast_to(a[bsz][None, None, :], (bsz, 1, D_MODEL))

        lam_init = 0.8 - 0.6 * float(np.exp(-0.3 * l))
        lam = jnp.exp(jnp.sum(lam_q1[l] * lam_k1[l])) - jnp.exp(jnp.sum(lam_q2[l] * lam_k2[l])) + lam_init
        scal = jnp.stack([lam, jnp.float32(1.0 - lam_init)]).astype(F32)
        subln = diff_subln_g[l].reshape(1, LANES)

        bq, bk, bvt, cq, ckd, cvdt = _inproj_call(x, lat(mul1), lat(sh1), w_attn, tables, layer=l, tm=1024,
                                                  rope=True)
        xbq, xbk, xbvt, xcq, xckd, xcvdt = _inproj_call(ctx, cx(mul1), cx(sh1), w_attn, ctx_tables, layer=l,
                                                        tm=n_ctx, rope=False)

        o_b = _diff_call(scal, bq, subln, [(bk, bvt), (xbk, xbvt)], tq=512, kc=256)
        o_c = _win_call(sinks[l], cq, ckd, cvdt, xckd, xcvdt, local=True)
        x_mid = _merge_call(x, lat(mul1), lat(sh1), lat(g1), *mix_w, o_b, o_c, w_br_b, w_out_b, layer=l, tm=1024)
        if not last:
            xo_b = _diff_call(scal, xbq, subln, [(xbk, xbvt)], tq=n_ctx, kc=512)
            xo_c = _win_call(sinks[l], xcq, None, None, xckd, xcvdt, local=False)
            ctx_mid = _merge_call(ctx, cx(mul1), cx(sh1), cx(g1), *mix_w, xo_b, xo_c, w_br_b, w_out_b, layer=l,
                                  tm=n_ctx)
        x = _ffn_call(x_mid, lat(mul2), lat(sh2), lat(g2), *ffn_w, layer=l, tm=1024, final=last)
        if not last:
            ctx = _ffn_call(ctx_mid, cx(mul2), cx(sh2), cx(g2), *ffn_w, layer=l, tm=n_ctx, final=False)
    return x
```

```python
import functools

import numpy as np
import jax
import jax.numpy as jnp
from jax import lax
from jax.experimental import pallas as pl
from jax.experimental.pallas import tpu as pltpu

F32 = jnp.float32
BF16 = jnp.bfloat16

D_MODEL = 1024
DEPTH = 4
GRID_W = 64
HEAD_DIM = 64
BLOCK = 128
MIX_W = 512
A_GROUPS = 4
B_HEADS = 4
C_Q_HEADS = 8
D_FF = 2816
ROPE_BASE = 10000.0
EPS = 1e-6
SCALE = HEAD_DIM ** -0.5
NEG_INF = -1e30
LOG2E = float(np.log2(np.e))

LANES = 128
ATTN_COLS = 2304
GATE_COL0 = 3328
VMEM_LIMIT = 56 * 1024 * 1024

TOKEN_TILE = 1024
DIFF_Q_TILE = 512
DIFF_KEY_CHUNK = 256


def _cparams(n_axes):
    return pltpu.CompilerParams(
        dimension_semantics=("arbitrary",) * n_axes, vmem_limit_bytes=VMEM_LIMIT)


def _const_spec(a):
    return pl.BlockSpec(a.shape, lambda *_: (0,) * a.ndim, pipeline_mode=pl.Buffered(1))


def _layer_spec(a, layer):
    return pl.BlockSpec((None,) + a.shape[1:], lambda *_: (layer,) + (0,) * (a.ndim - 1),
                        pipeline_mode=pl.Buffered(1))


def _rms_mod(x, mul, sh):
    ms = jnp.mean(x * x, axis=-1, keepdims=True)
    return x * lax.rsqrt(ms + EPS) * mul + sh


def _gelu_tanh(x):
    c = np.float32(np.sqrt(2.0 / np.pi))
    return 0.5 * x * (1.0 + jnp.tanh(c * (x + 0.044715 * (x * x * x))))


def _dot(a, b):
    return jnp.dot(a, b, preferred_element_type=F32)


def _dot_nt(a, b):
    return lax.dot_general(a, b, (((1,), (1,)), ((), ())), preferred_element_type=F32)


def _ada_kernel(v_ref, w_ref, b_ref, o_ref):
    v = v_ref[...]
    s = v * jax.nn.sigmoid(v)
    o_ref[...] = _dot(s.astype(BF16), w_ref[...].astype(BF16)) + b_ref[...]


def _ada_call(vpad, w_ada, b_ada):
    rows = vpad.shape[0]
    tn = 1536
    return pl.pallas_call(
        _ada_kernel,
        grid=(DEPTH, 6 * D_MODEL // tn),
        in_specs=[
            pl.BlockSpec((rows, D_MODEL), lambda l, j: (0, 0)),
            pl.BlockSpec((None, D_MODEL, tn), lambda l, j: (l, 0, j)),
            pl.BlockSpec((None, 1, tn), lambda l, j: (l, 0, j)),
        ],
        out_specs=pl.BlockSpec((None, rows, tn), lambda l, j: (l, 0, j)),
        out_shape=jax.ShapeDtypeStruct((DEPTH, rows, 6 * D_MODEL), F32),
        compiler_params=_cparams(2),
        name="ada",
    )(vpad, w_ada, b_ada.reshape(DEPTH, 1, 6 * D_MODEL))


def _inproj_kernel(x_ref, mul_ref, sh_ref, w_ref, cos_ref, sa_ref, sb_ref,
                   bq_ref, bk_ref, bvt_ref, cq_ref, ckd_ref, cvdt_ref, *, rope):
    h = _rms_mod(x_ref[...], mul_ref[...], sh_ref[...]).astype(BF16)
    if rope:
        cos, sa, sb = cos_ref[...], sa_ref[...], sb_ref[...]

    def rot(y):
        if not rope:
            return y
        return y * cos + pltpu.roll(y, LANES - 16, 1) * sa + pltpu.roll(y, 16, 1) * sb

    def dup(y):
        lane = lax.broadcasted_iota(jnp.int32, y.shape, 1)
        sw = pltpu.roll(y, 64, 1)
        return jnp.concatenate([jnp.where(lane < 64, y, sw), jnp.where(lane < 64, sw, y)], axis=1)

    for col0, ref in ((0, bq_ref), (512, bk_ref), (1536, cq_ref)):
        y = _dot(h, w_ref[:, col0:col0 + 512])
        for g in range(4):
            ref[:, g * LANES:(g + 1) * LANES] = rot(y[:, g * LANES:(g + 1) * LANES]).astype(BF16)
    ykv = _dot(h, w_ref[:, 2048:2304])
    ckd_ref[...] = dup(rot(ykv[:, :LANES])).astype(BF16)
    cvdt_ref[...] = dup(ykv[:, LANES:]).T.astype(BF16)
    bvt_ref[...] = _dot(h, w_ref[:, 1024:1536]).T.astype(BF16)


def _inproj_call(x, mul, sh, w_attn, tables, *, layer, tm, rope):
    bsz, length, _ = x.shape
    tok = lambda w: pl.BlockSpec((None, tm, w), lambda b, i: (b, i, 0))
    vec = pl.BlockSpec((None, 1, D_MODEL), lambda b, i: (b, 0, 0))
    tab = pl.BlockSpec((tm, LANES), lambda b, i: (i, 0))
    tok_t = lambda w: pl.BlockSpec((None, w, tm), lambda b, i: (b, 0, i))
    out = lambda w: jax.ShapeDtypeStruct((bsz, length, w), BF16)
    out_t = lambda w: jax.ShapeDtypeStruct((bsz, w, length), BF16)
    return pl.pallas_call(
        functools.partial(_inproj_kernel, rope=rope),
        grid=(bsz, length // tm),
        in_specs=[tok(D_MODEL), vec, vec,
                  _layer_spec(w_attn, layer), tab, tab, tab],
        out_specs=[tok(512), tok(512), tok_t(MIX_W), tok(512), tok(256), tok_t(2 * LANES)],
        out_shape=[out(512), out(512), out_t(MIX_W), out(512), out(256), out_t(2 * LANES)],
        compiler_params=_cparams(2),
        name="inproj",
    )(x, mul, sh, w_attn, *tables)


SAFE_EXP2_RANGE = 90.0


def _diff_kernel(sc_ref, q_ref, qall_ref, g_ref, ind_ref, *refs, seg_lens, kc, nq):
    o_ref, p_scr, f_scr, safe_scr = refs[-4:]
    t = pl.program_id(0)
    tq = q_ref.shape[0]
    lanei = lax.broadcasted_iota(jnp.int32, (1, 2 * tq), 1)

    @pl.when(t == 0)
    def _():
        p_scr[...] = jnp.zeros(p_scr.shape, BF16)
        f_scr[...] = jnp.zeros(f_scr.shape, F32)

    chunks = []
    row0 = 0
    for si, n_keys in enumerate(seg_lens):
        for st in range(0, n_keys, kc):
            size = min(kc, n_keys - st)
            chunks.append((refs[2 * si], refs[2 * si + 1], st, size, row0))
            row0 += size

    def max_sq_norm(x):
        xf = x.astype(F32)
        return jnp.max(_dot((xf * xf).astype(BF16), ind_ref[...]), axis=0, keepdims=True)

    @pl.when(t % nq == 0)
    def _():
        kn = None
        for k_ref, _, st, size, _ in chunks:
            n2 = max_sq_norm(k_ref[st:st + size, :])
            kn = n2 if kn is None else jnp.maximum(kn, n2)
        qn = None
        n_q = qall_ref.shape[0]
        for st in range(0, n_q, 2 * kc):
            n2 = max_sq_norm(qall_ref[st:min(st + 2 * kc, n_q), :])
            qn = n2 if qn is None else jnp.maximum(qn, n2)
        bound2 = jnp.max(kn * qn)
        safe_scr[0] = (bound2 < SAFE_EXP2_RANGE * SAFE_EXP2_RANGE).astype(jnp.int32)

    safe = safe_scr[0] == 1

    def step(max_free):
        q = q_ref[...]
        lane = lax.broadcasted_iota(jnp.int32, q.shape, 1)
        zero = jnp.zeros_like(q)
        qq = jnp.concatenate([jnp.where(lane < 64, q, zero), jnp.where(lane >= 64, q, zero)], axis=0)
        f_prev = f_scr[...]
        mcs, lcs = [], []
        acc = None
        for c, (k_ref, vt_ref, st, size, r0) in enumerate(chunks):
            f1 = f_prev[c:c + 1, 0:tq].astype(BF16)
            f2 = f_prev[c:c + 1, tq:2 * tq].astype(BF16)
            a_t = p_scr[r0:r0 + size, 0:tq] * f1 - p_scr[r0:r0 + size, tq:2 * tq] * f2
            part = _dot(vt_ref[:, st:st + size], a_t)
            acc = part if acc is None else acc + part
            s = _dot_nt(k_ref[st:st + size, :], qq)
            if max_free:
                p = jnp.exp2(s)
            else:
                mc = jnp.max(s, axis=0, keepdims=True)
                p = jnp.exp2(s - mc)
                mcs.append(mc)
            lcs.append(jnp.sum(p, axis=0, keepdims=True))
            p_scr[r0:r0 + size, :] = p.astype(BF16)

        if max_free:
            es = [None] * len(chunks)
            l_all = functools.reduce(lambda a, b: a + b, lcs)
        else:
            m_all = functools.reduce(jnp.maximum, mcs)
            es = [jnp.exp2(mc - m_all) for mc in mcs]
            l_all = functools.reduce(lambda a, b: a + b, [lc * e for lc, e in zip(lcs, es)])
        coef = jnp.where(lanei < tq, 1.0, sc_ref[0]) / l_all
        for c, e in enumerate(es):
            f_scr[c:c + 1, :] = coef if e is None else e * coef

        o = acc.T
        ms = jnp.mean(o * o, axis=-1, keepdims=True)
        o_ref[...] = (o * lax.rsqrt(ms + EPS) * g_ref[...] * sc_ref[1]).astype(BF16)

    pl.when(safe)(lambda: step(True))
    pl.when(jnp.logical_not(safe))(lambda: step(False))


def _diff_call(scal, q, subln_g, segs, *, tq, kc):
    bsz, lq, _ = q.shape
    nq = lq // tq
    n_blocks = bsz * B_HEADS * nq
    half = np.arange(LANES) // HEAD_DIM
    ind = jnp.asarray(half[:, None] == half[None, :], BF16)

    def block(t):
        b, r = t // (B_HEADS * nq), t % (B_HEADS * nq)
        return b, r // nq, r % nq

    cur = lambda t: block(jnp.minimum(t, n_blocks - 1))
    prev = lambda t: block(jnp.maximum(t - 1, 0))

    def q_map(t):
        b, h, i = cur(t)
        return b, i, h

    def kq_map(t):
        b, h, _ = cur(t)
        return b, 0, h

    def vt_map(t):
        b, h, _ = prev(t)
        return b, h, 0

    def o_map(t):
        b, h, i = prev(t)
        return b, i, h

    in_specs = [pl.BlockSpec(memory_space=pltpu.SMEM),
                pl.BlockSpec((None, tq, LANES), q_map),
                pl.BlockSpec((None, lq, LANES), kq_map),
                pl.BlockSpec((1, LANES), lambda t: (0, 0)),
                _const_spec(ind)]
    args = [scal, q, q, subln_g, ind]
    for k, vt in segs:
        n_keys = k.shape[1]
        in_specs += [pl.BlockSpec((None, n_keys, LANES), kq_map),
                     pl.BlockSpec((None, LANES, n_keys), vt_map)]
        args += [k, vt]
    seg_lens = tuple(k.shape[1] for k, _ in segs)
    n_chunks = sum(-(-n // kc) for n in seg_lens)
    return pl.pallas_call(
        functools.partial(_diff_kernel, seg_lens=seg_lens, kc=kc, nq=nq),
        grid=(n_blocks + 1,),
        in_specs=in_specs,
        out_specs=pl.BlockSpec((None, tq, LANES), o_map),
        out_shape=jax.ShapeDtypeStruct((bsz, lq, MIX_W), BF16),
        scratch_shapes=[pltpu.VMEM((sum(seg_lens), 2 * tq), BF16),
                        pltpu.VMEM((-(-n_chunks // 8) * 8, 2 * tq), F32),
                        pltpu.SMEM((1,), jnp.int32)],
        compiler_params=_cparams(1),
        name="diff",
    )(*args)


WIN_SUB = 8


def _win_kernel(sink_ref, q_ref, *refs, local):
    o_ref = refs[-1]
    n_sub = q_ref.shape[0] // BLOCK
    kx_ref, vxt_ref = refs[-3], refs[-2]
    n = pl.program_id(1)
    nb = pl.num_programs(1)
    nq = C_Q_HEADS // 2 * BLOCK
    lane = lax.broadcasted_iota(jnp.int32, (BLOCK, LANES), 1)
    headi = lax.broadcasted_iota(jnp.int32, (1, nq), 1) // BLOCK
    if local:
        kp_ref, kc_ref, kn_ref, vpt_ref, vct_ref, vnt_ref = refs[:6]
        key = lax.broadcasted_iota(jnp.int32, (BLOCK, nq), 0)
        qry = lax.broadcasted_iota(jnp.int32, (BLOCK, nq), 1) & (BLOCK - 1)
        tri_prev, tri_next = key >= qry, key <= qry

    probs = [(sub, kvh) for sub in range(n_sub) for kvh in range(2)]
    s_l, sink_l, vt_l = [], [], []
    for sub, kvh in probs:
        ksl = slice(kvh * LANES, (kvh + 1) * LANES)
        rows = slice(sub * BLOCK, (sub + 1) * BLOCK)
        parts = []
        for g in range(4):
            grp = kvh * 2 + g // 2
            qg = q_ref[rows, grp * LANES:(grp + 1) * LANES]
            parts.append(jnp.where((lane < 64) if g % 2 == 0 else (lane >= 64), qg, jnp.zeros_like(qg)))
        qs = jnp.concatenate(parts, axis=0)
        sink = jnp.zeros((1, nq), F32)
        for g in range(4):
            sink = jnp.where(headi == g, sink_ref[kvh * 4 + g] * LOG2E, sink)
        if local:
            kb, vb = [], []
            for j in (sub - 1, sub, sub + 1):
                if j < 0:
                    kb.append(kp_ref[:, ksl]), vb.append(vpt_ref[ksl, :])
                elif j >= n_sub:
                    kb.append(kn_ref[:, ksl]), vb.append(vnt_ref[ksl, :])
                else:
                    kb.append(kc_ref[j * BLOCK:(j + 1) * BLOCK, ksl])
                    vb.append(vct_ref[ksl, j * BLOCK:(j + 1) * BLOCK])
            k_all = jnp.concatenate(kb + [kx_ref[:, ksl]], axis=0)
            vt_all = jnp.concatenate(vb + [vxt_ref[ksl, :]], axis=1)
        else:
            k_all, vt_all = kx_ref[:, ksl], vxt_ref[ksl, :]
        s = _dot_nt(k_all, qs)
        if local:
            valid_prev = tri_prev if sub > 0 else tri_prev & (n > 0)
            valid_next = tri_next if sub < n_sub - 1 else tri_next & (n < nb - 1)
            s = jnp.concatenate([jnp.where(valid_prev, s[0:BLOCK], NEG_INF), s[BLOCK:2 * BLOCK],
                                 jnp.where(valid_next, s[2 * BLOCK:3 * BLOCK], NEG_INF), s[3 * BLOCK:]], axis=0)
        s_l.append(s)
        sink_l.append(sink)
        vt_l.append(vt_all)
    m_l = [jnp.maximum(jnp.max(s, axis=0, keepdims=True), sink) for s, sink in zip(s_l, sink_l)]
    p_l = [jnp.exp2(s - m) for s, m in zip(s_l, m_l)]
    l_l = [jnp.sum(p, axis=0, keepdims=True) + jnp.exp2(sink - m) for p, sink, m in zip(p_l, sink_l, m_l)]
    ot_l = [_dot(vt, p.astype(BF16)) / l for vt, p, l in zip(vt_l, p_l, l_l)]
    for (sub, kvh), o_t in zip(probs, ot_l):
        o = o_t.T
        for j in range(2):
            grp = kvh * 2 + j
            o_ref[sub * BLOCK:(sub + 1) * BLOCK, grp * LANES:(grp + 1) * LANES] = jnp.where(
                lane < 64, o[2 * j * BLOCK:(2 * j + 1) * BLOCK], o[(2 * j + 1) * BLOCK:(2 * j + 2) * BLOCK]
            ).astype(BF16)


def _win_call(sinks, q, kd, vdt, kxd, vxdt, *, local):
    bsz, lq, _ = q.shape
    nb = lq // BLOCK
    n_sub = min(WIN_SUB, nb)
    tq = n_sub * BLOCK
    in_specs = [pl.BlockSpec(memory_space=pltpu.SMEM), pl.BlockSpec((None, tq, MIX_W), lambda b, n: (b, n, 0))]
    args = [sinks, q]
    if local:
        prv = lambda n: jnp.maximum(n * n_sub - 1, 0)
        nxt = lambda n: jnp.minimum((n + 1) * n_sub, nb - 1)
        in_specs += [pl.BlockSpec((None, BLOCK, 2 * LANES), lambda b, n: (b, prv(n), 0)),
                     pl.BlockSpec((None, tq, 2 * LANES), lambda b, n: (b, n, 0)),
                     pl.BlockSpec((None, BLOCK, 2 * LANES), lambda b, n: (b, nxt(n), 0)),
                     pl.BlockSpec((None, 2 * LANES, BLOCK), lambda b, n: (b, 0, prv(n))),
                     pl.BlockSpec((None, 2 * LANES, tq), lambda b, n: (b, 0, n)),
                     pl.BlockSpec((None, 2 * LANES, BLOCK), lambda b, n: (b, 0, nxt(n)))]
        args += [kd] * 3 + [vdt] * 3
    n_ctx = kxd.shape[1]
    in_specs += [pl.BlockSpec((None, n_ctx, 2 * LANES), lambda b, n: (b, 0, 0)),
                 pl.BlockSpec((None, 2 * LANES, n_ctx), lambda b, n: (b, 0, 0))]
    args += [kxd, vxdt]
    return pl.pallas_call(
        functools.partial(_win_kernel, local=local),
        grid=(bsz, lq // tq),
        in_specs=in_specs,
        out_specs=pl.BlockSpec((None, tq, MIX_W), lambda b, n: (b, n, 0)),
        out_shape=jax.ShapeDtypeStruct((bsz, lq, MIX_W), BF16),
        compiler_params=_cparams(2),
        name="win",
    )(*args)


def _merge_kernel(x_ref, mul_ref, sh_ref, gate_ref, wuv_ref, wg_ref, lng_ref, lnb_ref, ws_ref, bs_ref,
                  ob_ref, oc_ref, wbr_ref, wout_ref, o_ref):
    x = x_ref[...]
    tm = x.shape[0]
    h = _rms_mod(x, mul_ref[...], sh_ref[...]).astype(BF16)
    u = _gelu_tanh(_dot(h, wuv_ref[:, 0:MIX_W]))
    v = _gelu_tanh(_dot(h, wuv_ref[:, MIX_W:2 * MIX_W]))
    mu = jnp.mean(v, axis=-1, keepdims=True)
    vc = v - mu
    var = jnp.mean(vc * vc, axis=-1, keepdims=True)
    vn = (vc * lax.rsqrt(var + EPS) * lng_ref[...] + lnb_ref[...]).astype(BF16)
    n_chunks = tm // BLOCK
    mixed = []
    for g in range(A_GROUPS):
        gs = slice(g * LANES, (g + 1) * LANES)
        rhs = jnp.concatenate([vn[c * BLOCK:(c + 1) * BLOCK, gs] for c in range(n_chunks)], axis=1)
        mixed.append(_dot(ws_ref[g], rhs))
    o_a = (u * jnp.concatenate(
        [jnp.concatenate([mixed[g][:, c * LANES:(c + 1) * LANES] + bs_ref[g] for g in range(A_GROUPS)], axis=1)
         for c in range(n_chunks)], axis=0)).astype(BF16)
    y = None
    for i, o_i in enumerate((o_a, ob_ref[...], oc_ref[...])):
        gate = jax.nn.sigmoid(_dot(h, wg_ref[:, i * D_MODEL:(i + 1) * D_MODEL]))
        t = gate * _dot(o_i, wbr_ref[i])
        y = t if y is None else y + t
    o_ref[...] = x + gate_ref[...] * _dot(y.astype(BF16), wout_ref[...])


def _merge_call(x, mul, sh, gate, w_uv, w_g, ln_g, ln_b, w_s, b_s, o_b, o_c, w_br, w_out, *, layer, tm):
    bsz, length, _ = x.shape
    tok = lambda w: pl.BlockSpec((None, tm, w), lambda b, i: (b, i, 0))
    vec = pl.BlockSpec((None, 1, D_MODEL), lambda b, i: (b, 0, 0))
    full = functools.partial(_layer_spec, layer=layer)
    return pl.pallas_call(
        _merge_kernel,
        grid=(bsz, length // tm),
        in_specs=[tok(D_MODEL), vec, vec, vec, full(w_uv), full(w_g), full(ln_g), full(ln_b), full(w_s), full(b_s),
                  tok(MIX_W), tok(MIX_W), full(w_br), full(w_out)],
        out_specs=tok(D_MODEL),
        out_shape=jax.ShapeDtypeStruct(x.shape, F32),
        compiler_params=_cparams(2),
        name="merge",
    )(x, mul, sh, gate, w_uv, w_g, ln_g, ln_b, w_s, b_s, o_b, o_c, w_br, w_out)


HALO = 8
MXU_TILE = 256
FF_EDGES = (0, 6 * MXU_TILE, D_FF)


def _ffn_kernel(x_ref, xp_ref, xn_ref, mul_ref, sh_ref, gate_ref, wg_ref, cw_ref, cb_ref, wu_ref, wd_ref,
                fg_ref, o_ref, *, final):
    i = pl.program_id(1)
    nt = pl.num_programs(1)
    x = x_ref[...]
    tm = x.shape[0]
    xe = jnp.concatenate([xp_ref[...], x, xn_ref[...]], axis=0)
    he = _rms_mod(xe, mul_ref[...], sh_ref[...]).astype(BF16)
    hm = he[HALO:HALO + tm]
    rowe = lax.broadcasted_iota(jnp.int32, (tm + 2 * HALO, 1), 0)
    keep = ((rowe >= HALO) | (i > 0)) & ((rowe < HALO + tm) | (i < nt - 1))
    spans = [slice(c0, c1) for c0, c1 in zip(FF_EDGES[:-1], FF_EDGES[1:])]
    a_l = [jnp.where(keep, _dot(he, wg_ref[:, cs]), 0.0) for cs in spans]
    u_l = [_dot(hm, wu_ref[:, cs]) for cs in spans]
    z_l = []
    for cs, a, u in zip(spans, a_l, u_l):
        a_prev = pltpu.roll(a, 1, 0)[HALO:HALO + tm]
        a_next = pltpu.roll(a, tm + 2 * HALO - 1, 0)[HALO:HALO + tm]
        a = (a_prev * cw_ref[0:1, cs] + a[HALO:HALO + tm] * cw_ref[1:2, cs] + a_next * cw_ref[2:3, cs]
             + cb_ref[:, cs])
        z_l.append((a * jax.nn.sigmoid(a) * u).astype(BF16))
    acc = None
    for cs, z in zip(spans, z_l):
        t = _dot(z, wd_ref[cs, :])
        acc = t if acc is None else acc + t
    out = x + gate_ref[...] * acc
    if final:
        ms = jnp.mean(out * out, axis=-1, keepdims=True)
        out = out * lax.rsqrt(ms + EPS) * fg_ref[...]
    o_ref[...] = out


def _ffn_call(x, mul, sh, gate, w_gate, conv_w, conv_b, w_up, w_down, final_g, *, layer, tm, final):
    bsz, length, _ = x.shape
    per = tm // HALO
    last = length // HALO - 1
    tok = pl.BlockSpec((None, tm, D_MODEL), lambda b, i: (b, i, 0))
    prv = pl.BlockSpec((None, HALO, D_MODEL), lambda b, i: (b, jnp.maximum(i * per - 1, 0), 0))
    nxt = pl.BlockSpec((None, HALO, D_MODEL), lambda b, i: (b, jnp.minimum((i + 1) * per, last), 0))
    vec = pl.BlockSpec((None, 1, D_MODEL), lambda b, i: (b, 0, 0))
    full = functools.partial(_layer_spec, layer=layer)
    return pl.pallas_call(
        functools.partial(_ffn_kernel, final=final),
        grid=(bsz, length // tm),
        in_specs=[tok, prv, nxt, vec, vec, vec, full(w_gate), full(conv_w), full(conv_b), full(w_up),
                  full(w_down), _const_spec(final_g)],
        out_specs=tok,
        out_shape=jax.ShapeDtypeStruct(x.shape, F32),
        compiler_params=_cparams(2),
        name="ffn",
    )(x, x, x, mul, sh, gate, w_gate, conv_w, conv_b, w_up, w_down, final_g)


def _rope_tables(length):
    pos = jnp.arange(length)
    rows = (pos // GRID_W).astype(F32)
    cols = (pos % GRID_W).astype(F32)
    half = HEAD_DIM // 2
    inv = ROPE_BASE ** (-jnp.arange(0, half, 2, dtype=F32) / half)
    ang_r = rows[:, None] * inv[None, :]
    ang_c = cols[:, None] * inv[None, :]
    zero = jnp.zeros_like(ang_r)
    cos = jnp.concatenate([jnp.cos(ang_r)] * 2 + [jnp.cos(ang_c)] * 2, axis=1)
    sa = jnp.concatenate([-jnp.sin(ang_r), zero, -jnp.sin(ang_c), zero], axis=1)
    sb = jnp.concatenate([zero, jnp.sin(ang_r), zero, jnp.sin(ang_c)], axis=1)
    return tuple(jnp.tile(t, (1, LANES // HEAD_DIM)) for t in (cos, sa, sb))


def kernel(x, c, ctx, c_ctx, w_ada, b_ada, norm1_g, w_in, sgu_ln_g, sgu_ln_b, w_s, b_s, lam_q1, lam_k1,
           lam_q2, lam_k2, diff_subln_g, sinks, w_branch, w_out, norm2_g, w_gate, conv_w, conv_b, w_up,
           w_down, final_g):
    bsz, length, _ = x.shape
    n_ctx = ctx.shape[1]
    tables = _rope_tables(length)
    ctx_tables = tuple(t[:n_ctx] for t in tables)

    rows = -(-(bsz + 1) // 8) * 8
    vpad = jnp.zeros((rows, D_MODEL), F32).at[:bsz].set(c).at[bsz].set(c_ctx)
    ada = _ada_call(vpad, w_ada, b_ada)

    qscale = jnp.ones((ATTN_COLS,), F32).at[0:512].set(SCALE * LOG2E).at[1536:2048].set(SCALE * LOG2E)
    w_attn = (w_in[:, :, 2 * MIX_W:2 * MIX_W + ATTN_COLS] * qscale).astype(BF16)
    w_uv = w_in[:, :, :2 * MIX_W].astype(BF16)
    w_g = w_in[:, :, GATE_COL0:].astype(BF16)
    w_s_b = w_s.astype(BF16)
    b_s_b = jnp.broadcast_to(b_s[..., None], b_s.shape + (LANES,))
    w_br_b = w_branch.astype(BF16)
    w_out_b = w_out.astype(BF16)
    w_gate_b = w_gate.astype(BF16)
    w_up_b = w_up.astype(BF16)
    w_down_b = w_down.astype(BF16)
    final_g2 = final_g.reshape(1, D_MODEL)
    mix_w = (w_uv, w_g, sgu_ln_g.reshape(DEPTH, 1, MIX_W), sgu_ln_b.reshape(DEPTH, 1, MIX_W), w_s_b, b_s_b)
    ffn_w = (w_gate_b, conv_w, conv_b.reshape(DEPTH, 1, D_FF), w_up_b, w_down_b, final_g2)

    for l in range(DEPTH):
        last = l == DEPTH - 1
        m = ada[l]
        sh1, sc1, g1, sh2, sc2, g2 = [m[:, j * D_MODEL:(j + 1) * D_MODEL] for j in range(6)]
        mul1 = norm1_g[l][None, :] * (1.0 + sc1)
        mul2 = norm2_g[l][None, :] * (1.0 + sc2)
        lat = lambda a: a[:bsz, None, :]
        cx = lambda a: jnp.broadcast_to(a[bsz][None, None, :], (bsz, 1, D_MODEL))

        lam_init = 0.8 - 0.6 * float(np.exp(-0.3 * l))
        lam = jnp.exp(jnp.sum(lam_q1[l] * lam_k1[l])) - jnp.exp(jnp.sum(lam_q2[l] * lam_k2[l])) + lam_init
        scal = jnp.stack([lam, jnp.float32(1.0 - lam_init)]).astype(F32)
        subln = diff_subln_g[l].reshape(1, LANES)

        bq, bk, bvt, cq, ckd, cvdt = _inproj_call(x, lat(mul1), lat(sh1), w_attn, tables, layer=l, tm=TOKEN_TILE,
                                                  rope=True)
        xbq, xbk, xbvt, xcq, xckd, xcvdt = _inproj_call(ctx, cx(mul1), cx(sh1), w_attn, ctx_tables, layer=l,
                                                        tm=n_ctx, rope=False)

        o_b = _diff_call(scal, bq, subln, [(bk, bvt), (xbk, xbvt)], tq=DIFF_Q_TILE, kc=DIFF_KEY_CHUNK)
        o_c = _win_call(sinks[l], cq, ckd, cvdt, xckd, xcvdt, local=True)
        x_mid = _merge_call(x, lat(mul1), lat(sh1), lat(g1), *mix_w, o_b, o_c, w_br_b, w_out_b, layer=l,
                            tm=TOKEN_TILE)
        if not last:
            xo_b = _diff_call(scal, xbq, subln, [(xbk, xbvt)], tq=n_ctx, kc=n_ctx)
            xo_c = _win_call(sinks[l], xcq, None, None, xckd, xcvdt, local=False)
            ctx_mid = _merge_call(ctx, cx(mul1), cx(sh1), cx(g1), *mix_w, xo_b, xo_c, w_br_b, w_out_b, layer=l,
                                  tm=n_ctx)
        x = _ffn_call(x_mid, lat(mul2), lat(sh2), lat(g2), *ffn_w, layer=l, tm=TOKEN_TILE, final=last)
        if not last:
            ctx = _ffn_call(ctx_mid, cx(mul2), cx(sh2), cx(g2), *ffn_w, layer=l, tm=n_ctx, final=False)
    return x
```

```python
import functools

import numpy as np
import jax
import jax.numpy as jnp
from jax import lax
from jax.experimental import pallas as pl
from jax.experimental.pallas import tpu as pltpu

F32 = jnp.float32
BF16 = jnp.bfloat16

D_MODEL = 1024
DEPTH = 4
GRID_W = 64
HEAD_DIM = 64
BLOCK = 128
MIX_W = 512
A_GROUPS = 4
B_HEADS = 4
C_Q_HEADS = 8
D_FF = 2816
ROPE_BASE = 10000.0
EPS = 1e-6
SCALE = HEAD_DIM ** -0.5
NEG_INF = -1e30
LOG2E = float(np.log2(np.e))

LANES = 128
ATTN_COLS = 2304
GATE_COL0 = 3328
VMEM_LIMIT = 56 * 1024 * 1024

TOKEN_TILE = 1024
DIFF_Q_TILE = 512
DIFF_KEY_CHUNK = 256


def _cparams(n_axes):
    return pltpu.CompilerParams(
        dimension_semantics=("arbitrary",) * n_axes, vmem_limit_bytes=VMEM_LIMIT)


def _const_spec(a):
    return pl.BlockSpec(a.shape, lambda *_: (0,) * a.ndim, pipeline_mode=pl.Buffered(1))


def _layer_spec(a, layer):
    return pl.BlockSpec((None,) + a.shape[1:], lambda *_: (layer,) + (0,) * (a.ndim - 1),
                        pipeline_mode=pl.Buffered(1))


def _rms_mod(x, mul, sh):
    ms = jnp.mean(x * x, axis=-1, keepdims=True)
    return x * lax.rsqrt(ms + EPS) * mul + sh


def _gelu_tanh(x):
    c = np.float32(np.sqrt(2.0 / np.pi))
    return 0.5 * x * (1.0 + jnp.tanh(c * (x + 0.044715 * (x * x * x))))


def _dot(a, b):
    return jnp.dot(a, b, preferred_element_type=F32)


def _dot_nt(a, b):
    return lax.dot_general(a, b, (((1,), (1,)), ((), ())), preferred_element_type=F32)


def _ada_kernel(v_ref, w_ref, b_ref, o_ref):
    v = v_ref[...]
    s = v * jax.nn.sigmoid(v)
    o_ref[...] = _dot(s.astype(BF16), w_ref[...].astype(BF16)) + b_ref[...]


def _ada_call(vpad, w_ada, b_ada):
    rows = vpad.shape[0]
    tn = 1536
    return pl.pallas_call(
        _ada_kernel,
        grid=(DEPTH, 6 * D_MODEL // tn),
        in_specs=[
            pl.BlockSpec((rows, D_MODEL), lambda l, j: (0, 0)),
            pl.BlockSpec((None, D_MODEL, tn), lambda l, j: (l, 0, j)),
            pl.BlockSpec((None, 1, tn), lambda l, j: (l, 0, j)),
        ],
        out_specs=pl.BlockSpec((None, rows, tn), lambda l, j: (l, 0, j)),
        out_shape=jax.ShapeDtypeStruct((DEPTH, rows, 6 * D_MODEL), F32),
        compiler_params=_cparams(2),
        name="ada",
    )(vpad, w_ada, b_ada.reshape(DEPTH, 1, 6 * D_MODEL))


def _inproj_kernel(x_ref, mul_ref, sh_ref, w_ref, cos_ref, sa_ref, sb_ref,
                   bq_ref, bk_ref, bvt_ref, cq_ref, ckd_ref, cvdt_ref, *, rope):
    h = _rms_mod(x_ref[...], mul_ref[...], sh_ref[...]).astype(BF16)
    if rope:
        cos, sa, sb = cos_ref[...], sa_ref[...], sb_ref[...]

    def rot(y):
        if not rope:
            return y
        return y * cos + pltpu.roll(y, LANES - 16, 1) * sa + pltpu.roll(y, 16, 1) * sb

    def dup(y):
        lane = lax.broadcasted_iota(jnp.int32, y.shape, 1)
        sw = pltpu.roll(y, 64, 1)
        return jnp.concatenate([jnp.where(lane < 64, y, sw), jnp.where(lane < 64, sw, y)], axis=1)

    for col0, ref in ((0, bq_ref), (512, bk_ref), (1536, cq_ref)):
        y = _dot(h, w_ref[:, col0:col0 + 512])
        for g in range(4):
            ref[:, g * LANES:(g + 1) * LANES] = rot(y[:, g * LANES:(g + 1) * LANES]).astype(BF16)
    ykv = _dot(h, w_ref[:, 2048:2304])
    ckd_ref[...] = dup(rot(ykv[:, :LANES])).astype(BF16)
    cvdt_ref[...] = dup(ykv[:, LANES:]).T.astype(BF16)
    bvt_ref[...] = _dot(h, w_ref[:, 1024:1536]).T.astype(BF16)


def _inproj_call(x, mul, sh, w_attn, tables, *, layer, tm, rope):
    bsz, length, _ = x.shape
    tok = lambda w: pl.BlockSpec((None, tm, w), lambda b, i: (b, i, 0))
    vec = pl.BlockSpec((None, 1, D_MODEL), lambda b, i: (b, 0, 0))
    tab = pl.BlockSpec((tm, LANES), lambda b, i: (i, 0))
    tok_t = lambda w: pl.BlockSpec((None, w, tm), lambda b, i: (b, 0, i))
    out = lambda w: jax.ShapeDtypeStruct((bsz, length, w), BF16)
    out_t = lambda w: jax.ShapeDtypeStruct((bsz, w, length), BF16)
    return pl.pallas_call(
        functools.partial(_inproj_kernel, rope=rope),
        grid=(bsz, length // tm),
        in_specs=[tok(D_MODEL), vec, vec,
                  _layer_spec(w_attn, layer), tab, tab, tab],
        out_specs=[tok(512), tok(512), tok_t(MIX_W), tok(512), tok(256), tok_t(2 * LANES)],
        out_shape=[out(512), out(512), out_t(MIX_W), out(512), out(256), out_t(2 * LANES)],
        compiler_params=_cparams(2),
        name="inproj",
    )(x, mul, sh, w_attn, *tables)


SAFE_EXP2_RANGE = 90.0


def _diff_kernel(sc_ref, q_ref, qall_ref, g_ref, ind_ref, *refs, seg_lens, kc, nq):
    o_ref, p_scr, f_scr, safe_scr = refs[-4:]
    t = pl.program_id(0)
    tq = q_ref.shape[0]
    lanei = lax.broadcasted_iota(jnp.int32, (1, 2 * tq), 1)

    @pl.when(t == 0)
    def _():
        p_scr[...] = jnp.zeros(p_scr.shape, BF16)
        f_scr[...] = jnp.zeros(f_scr.shape, F32)

    chunks = []
    row0 = 0
    for si, n_keys in enumerate(seg_lens):
        for st in range(0, n_keys, kc):
            size = min(kc, n_keys - st)
            chunks.append((refs[2 * si], refs[2 * si + 1], st, size, row0))
            row0 += size

    def max_sq_norm(x):
        xf = x.astype(F32)
        return jnp.max(_dot((xf * xf).astype(BF16), ind_ref[...]), axis=0, keepdims=True)

    @pl.when(t % nq == 0)
    def _():
        kn = None
        for k_ref, _, st, size, _ in chunks:
            n2 = max_sq_norm(k_ref[st:st + size, :])
            kn = n2 if kn is None else jnp.maximum(kn, n2)
        qn = None
        n_q = qall_ref.shape[0]
        for st in range(0, n_q, 2 * kc):
            n2 = max_sq_norm(qall_ref[st:min(st + 2 * kc, n_q), :])
            qn = n2 if qn is None else jnp.maximum(qn, n2)
        bound2 = jnp.max(kn * qn)
        safe_scr[0] = (bound2 < SAFE_EXP2_RANGE * SAFE_EXP2_RANGE).astype(jnp.int32)

    safe = safe_scr[0] == 1

    def step(max_free):
        q = q_ref[...]
        lane = lax.broadcasted_iota(jnp.int32, q.shape, 1)
        zero = jnp.zeros_like(q)
        qq = jnp.concatenate([jnp.where(lane < 64, q, zero), jnp.where(lane >= 64, q, zero)], axis=0)
        f_prev = f_scr[...]
        mcs, lcs = [], []
        acc = None
        for c, (k_ref, vt_ref, st, size, r0) in enumerate(chunks):
            f1 = f_prev[c:c + 1, 0:tq].astype(BF16)
            f2 = f_prev[c:c + 1, tq:2 * tq].astype(BF16)
            a_t = p_scr[r0:r0 + size, 0:tq] * f1 - p_scr[r0:r0 + size, tq:2 * tq] * f2
            part = _dot(vt_ref[:, st:st + size], a_t)
            acc = part if acc is None else acc + part
            s = _dot_nt(k_ref[st:st + size, :], qq)
            if max_free:
                p = jnp.exp2(s)
            else:
                mc = jnp.max(s, axis=0, keepdims=True)
                p = jnp.exp2(s - mc)
                mcs.append(mc)
            lcs.append(jnp.sum(p, axis=0, keepdims=True))
            p_scr[r0:r0 + size, :] = p.astype(BF16)

        if max_free:
            es = [None] * len(chunks)
            l_all = functools.reduce(lambda a, b: a + b, lcs)
        else:
            m_all = functools.reduce(jnp.maximum, mcs)
            es = [jnp.exp2(mc - m_all) for mc in mcs]
            l_all = functools.reduce(lambda a, b: a + b, [lc * e for lc, e in zip(lcs, es)])
        coef = jnp.where(lanei < tq, 1.0, sc_ref[0]) / l_all
        for c, e in enumerate(es):
            f_scr[c:c + 1, :] = coef if e is None else e * coef

        o = acc.T
        ms = jnp.mean(o * o, axis=-1, keepdims=True)
        o_ref[...] = (o * lax.rsqrt(ms + EPS) * g_ref[...] * sc_ref[1]).astype(BF16)

    pl.when(safe)(lambda: step(True))
    pl.when(jnp.logical_not(safe))(lambda: step(False))


def _diff_call(scal, q, subln_g, segs, *, tq, kc):
    bsz, lq, _ = q.shape
    nq = lq // tq
    n_blocks = bsz * B_HEADS * nq
    half = np.arange(LANES) // HEAD_DIM
    ind = jnp.asarray(half[:, None] == half[None, :], BF16)

    def block(t):
        b, r = t // (B_HEADS * nq), t % (B_HEADS * nq)
        return b, r // nq, r % nq

    cur = lambda t: block(jnp.minimum(t, n_blocks - 1))
    prev = lambda t: block(jnp.maximum(t - 1, 0))

    def q_map(t):
        b, h, i = cur(t)
        return b, i, h

    def kq_map(t):
        b, h, _ = cur(t)
        return b, 0, h

    def vt_map(t):
        b, h, _ = prev(t)
        return b, h, 0

    def o_map(t):
        b, h, i = prev(t)
        return b, i, h

    in_specs = [pl.BlockSpec(memory_space=pltpu.SMEM),
                pl.BlockSpec((None, tq, LANES), q_map),
                pl.BlockSpec((None, lq, LANES), kq_map),
                pl.BlockSpec((1, LANES), lambda t: (0, 0)),
                _const_spec(ind)]
    args = [scal, q, q, subln_g, ind]
    for k, vt in segs:
        n_keys = k.shape[1]
        in_specs += [pl.BlockSpec((None, n_keys, LANES), kq_map),
                     pl.BlockSpec((None, LANES, n_keys), vt_map)]
        args += [k, vt]
    seg_lens = tuple(k.shape[1] for k, _ in segs)
    n_chunks = sum(-(-n // kc) for n in seg_lens)
    return pl.pallas_call(
        functools.partial(_diff_kernel, seg_lens=seg_lens, kc=kc, nq=nq),
        grid=(n_blocks + 1,),
        in_specs=in_specs,
        out_specs=pl.BlockSpec((None, tq, LANES), o_map),
        out_shape=jax.ShapeDtypeStruct((bsz, lq, MIX_W), BF16),
        scratch_shapes=[pltpu.VMEM((sum(seg_lens), 2 * tq), BF16),
                        pltpu.VMEM((-(-n_chunks // 8) * 8, 2 * tq), F32),
                        pltpu.SMEM((1,), jnp.int32)],
        compiler_params=_cparams(1),
        name="diff",
    )(*args)


WIN_SUB = 8


def _win_kernel(sink_ref, q_ref, *refs, local):
    o_ref = refs[-1]
    n_sub = q_ref.shape[0] // BLOCK
    kx_ref, vxt_ref = refs[-3], refs[-2]
    n = pl.program_id(1)
    nb = pl.num_programs(1)
    nq = C_Q_HEADS // 2 * BLOCK
    lane = lax.broadcasted_iota(jnp.int32, (BLOCK, LANES), 1)
    headi = lax.broadcasted_iota(jnp.int32, (1, nq), 1) // BLOCK
    if local:
        kp_ref, kc_ref, kn_ref, vpt_ref, vct_ref, vnt_ref = refs[:6]
        key = lax.broadcasted_iota(jnp.int32, (BLOCK, nq), 0)
        qry = lax.broadcasted_iota(jnp.int32, (BLOCK, nq), 1) & (BLOCK - 1)
        tri_prev, tri_next = key >= qry, key <= qry

    probs = [(sub, kvh) for sub in range(n_sub) for kvh in range(2)]
    s_l, sink_l, vt_l = [], [], []
    for sub, kvh in probs:
        ksl = slice(kvh * LANES, (kvh + 1) * LANES)
        rows = slice(sub * BLOCK, (sub + 1) * BLOCK)
        parts = []
        for g in range(4):
            grp = kvh * 2 + g // 2
            qg = q_ref[rows, grp * LANES:(grp + 1) * LANES]
            parts.append(jnp.where((lane < 64) if g % 2 == 0 else (lane >= 64), qg, jnp.zeros_like(qg)))
        qs = jnp.concatenate(parts, axis=0)
        sink = jnp.zeros((1, nq), F32)
        for g in range(4):
            sink = jnp.where(headi == g, sink_ref[kvh * 4 + g] * LOG2E, sink)
        if local:
            kb, vb = [], []
            for j in (sub - 1, sub, sub + 1):
                if j < 0:
                    kb.append(kp_ref[:, ksl]), vb.append(vpt_ref[ksl, :])
                elif j >= n_sub:
                    kb.append(kn_ref[:, ksl]), vb.append(vnt_ref[ksl, :])
                else:
                    kb.append(kc_ref[j * BLOCK:(j + 1) * BLOCK, ksl])
                    vb.append(vct_ref[ksl, j * BLOCK:(j + 1) * BLOCK])
            k_all = jnp.concatenate(kb + [kx_ref[:, ksl]], axis=0)
            vt_all = jnp.concatenate(vb + [vxt_ref[ksl, :]], axis=1)
        else:
            k_all, vt_all = kx_ref[:, ksl], vxt_ref[ksl, :]
        s = _dot_nt(k_all, qs)
        if local:
            valid_prev = tri_prev if sub > 0 else tri_prev & (n > 0)
            valid_next = tri_next if sub < n_sub - 1 else tri_next & (n < nb - 1)
            s = jnp.concatenate([jnp.where(valid_prev, s[0:BLOCK], NEG_INF), s[BLOCK:2 * BLOCK],
                                 jnp.where(valid_next, s[2 * BLOCK:3 * BLOCK], NEG_INF), s[3 * BLOCK:]], axis=0)
        s_l.append(s)
        sink_l.append(sink)
        vt_l.append(vt_all)
    m_l = [jnp.maximum(jnp.max(s, axis=0, keepdims=True), sink) for s, sink in zip(s_l, sink_l)]
    p_l = [jnp.exp2(s - m) for s, m in zip(s_l, m_l)]
    l_l = [jnp.sum(p, axis=0, keepdims=True) + jnp.exp2(sink - m) for p, sink, m in zip(p_l, sink_l, m_l)]
    ot_l = [_dot(vt, p.astype(BF16)) / l for vt, p, l in zip(vt_l, p_l, l_l)]
    for (sub, kvh), o_t in zip(probs, ot_l):
        o = o_t.T
        for j in range(2):
            grp = kvh * 2 + j
            o_ref[sub * BLOCK:(sub + 1) * BLOCK, grp * LANES:(grp + 1) * LANES] = jnp.where(
                lane < 64, o[2 * j * BLOCK:(2 * j + 1) * BLOCK], o[(2 * j + 1) * BLOCK:(2 * j + 2) * BLOCK]
            ).astype(BF16)


def _win_call(sinks, q, kd, vdt, kxd, vxdt, *, local):
    bsz, lq, _ = q.shape
    nb = lq // BLOCK
    n_sub = min(WIN_SUB, nb)
    tq = n_sub * BLOCK
    in_specs = [pl.BlockSpec(memory_space=pltpu.SMEM), pl.BlockSpec((None, tq, MIX_W), lambda b, n: (b, n, 0))]
    args = [sinks, q]
    if local:
        prv = lambda n: jnp.maximum(n * n_sub - 1, 0)
        nxt = lambda n: jnp.minimum((n + 1) * n_sub, nb - 1)
        in_specs += [pl.BlockSpec((None, BLOCK, 2 * LANES), lambda b, n: (b, prv(n), 0)),
                     pl.BlockSpec((None, tq, 2 * LANES), lambda b, n: (b, n, 0)),
                     pl.BlockSpec((None, BLOCK, 2 * LANES), lambda b, n: (b, nxt(n), 0)),
                     pl.BlockSpec((None, 2 * LANES, BLOCK), lambda b, n: (b, 0, prv(n))),
                     pl.BlockSpec((None, 2 * LANES, tq), lambda b, n: (b, 0, n)),
                     pl.BlockSpec((None, 2 * LANES, BLOCK), lambda b, n: (b, 0, nxt(n)))]
        args += [kd] * 3 + [vdt] * 3
    n_ctx = kxd.shape[1]
    in_specs += [pl.BlockSpec((None, n_ctx, 2 * LANES), lambda b, n: (b, 0, 0)),
                 pl.BlockSpec((None, 2 * LANES, n_ctx), lambda b, n: (b, 0, 0))]
    args += [kxd, vxdt]
    return pl.pallas_call(
        functools.partial(_win_kernel, local=local),
        grid=(bsz, lq // tq),
        in_specs=in_specs,
        out_specs=pl.BlockSpec((None, tq, MIX_W), lambda b, n: (b, n, 0)),
        out_shape=jax.ShapeDtypeStruct((bsz, lq, MIX_W), BF16),
        compiler_params=_cparams(2),
        name="win",
    )(*args)


def _merge_kernel(x_ref, mul_ref, sh_ref, gate_ref, wuv_ref, wg_ref, lng_ref, lnb_ref, ws_ref, bs_ref,
                  ob_ref, oc_ref, wbr_ref, wout_ref, o_ref):
    x = x_ref[...]
    tm = x.shape[0]
    h = _rms_mod(x, mul_ref[...], sh_ref[...]).astype(BF16)
    v = _gelu_tanh(_dot(h, wuv_ref[:, MIX_W:2 * MIX_W]))
    u = _gelu_tanh(_dot(h, wuv_ref[:, 0:MIX_W]))

    def branch(i, o_i):
        return jax.nn.sigmoid(_dot(h, wg_ref[:, i * D_MODEL:(i + 1) * D_MODEL])) * _dot(o_i, wbr_ref[i])

    y = branch(1, ob_ref[...]) + branch(2, oc_ref[...])
    mu = jnp.mean(v, axis=-1, keepdims=True)
    vc = v - mu
    var = jnp.mean(vc * vc, axis=-1, keepdims=True)
    vn = (vc * lax.rsqrt(var + EPS) * lng_ref[...] + lnb_ref[...]).astype(BF16)
    n_chunks = tm // BLOCK
    mixed = []
    for g in range(A_GROUPS):
        gs = slice(g * LANES, (g + 1) * LANES)
        rhs = jnp.concatenate([vn[c * BLOCK:(c + 1) * BLOCK, gs] for c in range(n_chunks)], axis=1)
        mixed.append(_dot(ws_ref[g], rhs))
    o_a = (u * jnp.concatenate(
        [jnp.concatenate([mixed[g][:, c * LANES:(c + 1) * LANES] + bs_ref[g] for g in range(A_GROUPS)], axis=1)
         for c in range(n_chunks)], axis=0)).astype(BF16)
    y = y + branch(0, o_a)
    o_ref[...] = x + gate_ref[...] * _dot(y.astype(BF16), wout_ref[...])


def _merge_call(x, mul, sh, gate, w_uv, w_g, ln_g, ln_b, w_s, b_s, o_b, o_c, w_br, w_out, *, layer, tm):
    bsz, length, _ = x.shape
    tok = lambda w: pl.BlockSpec((None, tm, w), lambda b, i: (b, i, 0))
    vec = pl.BlockSpec((None, 1, D_MODEL), lambda b, i: (b, 0, 0))
    full = functools.partial(_layer_spec, layer=layer)
    return pl.pallas_call(
        _merge_kernel,
        grid=(bsz, length // tm),
        in_specs=[tok(D_MODEL), vec, vec, vec, full(w_uv), full(w_g), full(ln_g), full(ln_b), full(w_s), full(b_s),
                  tok(MIX_W), tok(MIX_W), full(w_br), full(w_out)],
        out_specs=tok(D_MODEL),
        out_shape=jax.ShapeDtypeStruct(x.shape, F32),
        compiler_params=_cparams(2),
        name="merge",
    )(x, mul, sh, gate, w_uv, w_g, ln_g, ln_b, w_s, b_s, o_b, o_c, w_br, w_out)


HALO = 8
MXU_TILE = 256
FF_EDGES = (0, 6 * MXU_TILE, D_FF)


def _ffn_kernel(x_ref, xp_ref, xn_ref, mul_ref, sh_ref, gate_ref, wg_ref, cw_ref, cb_ref, wu_ref, wd_ref,
                fg_ref, o_ref, *, final):
    i = pl.program_id(1)
    nt = pl.num_programs(1)
    x = x_ref[...]
    tm = x.shape[0]
    xe = jnp.concatenate([xp_ref[...], x, xn_ref[...]], axis=0)
    he = _rms_mod(xe, mul_ref[...], sh_ref[...]).astype(BF16)
    hm = he[HALO:HALO + tm]
    rowe = lax.broadcasted_iota(jnp.int32, (tm + 2 * HALO, 1), 0)
    keep = ((rowe >= HALO) | (i > 0)) & ((rowe < HALO + tm) | (i < nt - 1))
    spans = [slice(c0, c1) for c0, c1 in zip(FF_EDGES[:-1], FF_EDGES[1:])]
    a_l = [jnp.where(keep, _dot(he, wg_ref[:, cs]), 0.0) for cs in spans]
    u_l = [_dot(hm, wu_ref[:, cs]) for cs in spans]
    z_l = []
    for cs, a, u in zip(spans, a_l, u_l):
        a_prev = pltpu.roll(a, 1, 0)[HALO:HALO + tm]
        a_next = pltpu.roll(a, tm + 2 * HALO - 1, 0)[HALO:HALO + tm]
        a = (a_prev * cw_ref[0:1, cs] + a[HALO:HALO + tm] * cw_ref[1:2, cs] + a_next * cw_ref[2:3, cs]
             + cb_ref[:, cs])
        z_l.append((a * jax.nn.sigmoid(a) * u).astype(BF16))
    acc = None
    for cs, z in zip(spans, z_l):
        t = _dot(z, wd_ref[cs, :])
        acc = t if acc is None else acc + t
    out = x + gate_ref[...] * acc
    if final:
        ms = jnp.mean(out * out, axis=-1, keepdims=True)
        out = out * lax.rsqrt(ms + EPS) * fg_ref[...]
    o_ref[...] = out


def _ffn_call(x, mul, sh, gate, w_gate, conv_w, conv_b, w_up, w_down, final_g, *, layer, tm, final):
    bsz, length, _ = x.shape
    per = tm // HALO
    last = length // HALO - 1
    tok = pl.BlockSpec((None, tm, D_MODEL), lambda b, i: (b, i, 0))
    prv = pl.BlockSpec((None, HALO, D_MODEL), lambda b, i: (b, jnp.maximum(i * per - 1, 0), 0))
    nxt = pl.BlockSpec((None, HALO, D_MODEL), lambda b, i: (b, jnp.minimum((i + 1) * per, last), 0))
    vec = pl.BlockSpec((None, 1, D_MODEL), lambda b, i: (b, 0, 0))
    full = functools.partial(_layer_spec, layer=layer)
    return pl.pallas_call(
        functools.partial(_ffn_kernel, final=final),
        grid=(bsz, length // tm),
        in_specs=[tok, prv, nxt, vec, vec, vec, full(w_gate), full(conv_w), full(conv_b), full(w_up),
                  full(w_down), _const_spec(final_g)],
        out_specs=tok,
        out_shape=jax.ShapeDtypeStruct(x.shape, F32),
        compiler_params=_cparams(2),
        name="ffn",
    )(x, x, x, mul, sh, gate, w_gate, conv_w, conv_b, w_up, w_down, final_g)


def _rope_tables(length):
    pos = jnp.arange(length)
    rows = (pos // GRID_W).astype(F32)
    cols = (pos % GRID_W).astype(F32)
    half = HEAD_DIM // 2
    inv = ROPE_BASE ** (-jnp.arange(0, half, 2, dtype=F32) / half)
    ang_r = rows[:, None] * inv[None, :]
    ang_c = cols[:, None] * inv[None, :]
    zero = jnp.zeros_like(ang_r)
    cos = jnp.concatenate([jnp.cos(ang_r)] * 2 + [jnp.cos(ang_c)] * 2, axis=1)
    sa = jnp.concatenate([-jnp.sin(ang_r), zero, -jnp.sin(ang_c), zero], axis=1)
    sb = jnp.concatenate([zero, jnp.sin(ang_r), zero, jnp.sin(ang_c)], axis=1)
    return tuple(jnp.tile(t, (1, LANES // HEAD_DIM)) for t in (cos, sa, sb))


def kernel(x, c, ctx, c_ctx, w_ada, b_ada, norm1_g, w_in, sgu_ln_g, sgu_ln_b, w_s, b_s, lam_q1, lam_k1,
           lam_q2, lam_k2, diff_subln_g, sinks, w_branch, w_out, norm2_g, w_gate, conv_w, conv_b, w_up,
           w_down, final_g):
    bsz, length, _ = x.shape
    n_ctx = ctx.shape[1]
    tables = _rope_tables(length)
    ctx_tables = tuple(t[:n_ctx] for t in tables)

    rows = -(-(bsz + 1) // 8) * 8
    vpad = jnp.zeros((rows, D_MODEL), F32).at[:bsz].set(c).at[bsz].set(c_ctx)
    ada = _ada_call(vpad, w_ada, b_ada)

    qscale = jnp.ones((ATTN_COLS,), F32).at[0:512].set(SCALE * LOG2E).at[1536:2048].set(SCALE * LOG2E)
    w_attn = (w_in[:, :, 2 * MIX_W:2 * MIX_W + ATTN_COLS] * qscale).astype(BF16)
    w_uv = w_in[:, :, :2 * MIX_W].astype(BF16)
    w_g = w_in[:, :, GATE_COL0:].astype(BF16)
    w_s_b = w_s.astype(BF16)
    b_s_b = jnp.broadcast_to(b_s[..., None], b_s.shape + (LANES,))
    w_br_b = w_branch.astype(BF16)
    w_out_b = w_out.astype(BF16)
    w_gate_b = w_gate.astype(BF16)
    w_up_b = w_up.astype(BF16)
    w_down_b = w_down.astype(BF16)
    final_g2 = final_g.reshape(1, D_MODEL)
    mix_w = (w_uv, w_g, sgu_ln_g.reshape(DEPTH, 1, MIX_W), sgu_ln_b.reshape(DEPTH, 1, MIX_W), w_s_b, b_s_b)
    ffn_w = (w_gate_b, conv_w, conv_b.reshape(DEPTH, 1, D_FF), w_up_b, w_down_b, final_g2)

    for l in range(DEPTH):
        last = l == DEPTH - 1
        m = ada[l]
        sh1, sc1, g1, sh2, sc2, g2 = [m[:, j * D_MODEL:(j + 1) * D_MODEL] for j in range(6)]
        mul1 = norm1_g[l][None, :] * (1.0 + sc1)
        mul2 = norm2_g[l][None, :] * (1.0 + sc2)
        lat = lambda a: a[:bsz, None, :]
        cx = lambda a: jnp.broadcast_to(a[bsz][None, None, :], (bsz, 1, D_MODEL))

        lam_init = 0.8 - 0.6 * float(np.exp(-0.3 * l))
        lam = jnp.exp(jnp.sum(lam_q1[l] * lam_k1[l])) - jnp.exp(jnp.sum(lam_q2[l] * lam_k2[l])) + lam_init
        scal = jnp.stack([lam, jnp.float32(1.0 - lam_init)]).astype(F32)
        subln = diff_subln_g[l].reshape(1, LANES)

        bq, bk, bvt, cq, ckd, cvdt = _inproj_call(x, lat(mul1), lat(sh1), w_attn, tables, layer=l, tm=TOKEN_TILE,
                                                  rope=True)
        xbq, xbk, xbvt, xcq, xckd, xcvdt = _inproj_call(ctx, cx(mul1), cx(sh1), w_attn, ctx_tables, layer=l,
                                                        tm=n_ctx, rope=False)

        o_b = _diff_call(scal, bq, subln, [(bk, bvt), (xbk, xbvt)], tq=DIFF_Q_TILE, kc=DIFF_KEY_CHUNK)
        o_c = _win_call(sinks[l], cq, ckd, cvdt, xckd, xcvdt, local=True)
        x_mid = _merge_call(x, lat(mul1), lat(sh1), lat(g1), *mix_w, o_b, o_c, w_br_b, w_out_b, layer=l,
                            tm=TOKEN_TILE)
        if not last:
            xo_b = _diff_call(scal, xbq, subln, [(xbk, xbvt)], tq=n_ctx, kc=n_ctx)
            xo_c = _win_call(sinks[l], xcq, None, None, xckd, xcvdt, local=False)
            ctx_mid = _merge_call(ctx, cx(mul1), cx(sh1), cx(g1), *mix_w, xo_b, xo_c, w_br_b, w_out_b, layer=l,
                                  tm=n_ctx)
        x = _ffn_call(x_mid, lat(mul2), lat(sh2), lat(g2), *ffn_w, layer=l, tm=TOKEN_TILE, final=last)
        if not last:
            ctx = _ffn_call(ctx_mid, cx(mul2), cx(sh2), cx(g2), *ffn_w, layer=l, tm=n_ctx, final=False)
    return x
```

```python
import functools

import numpy as np
import jax
import jax.numpy as jnp
from jax import lax
from jax.experimental import pallas as pl
from jax.experimental.pallas import tpu as pltpu

F32 = jnp.float32
BF16 = jnp.bfloat16

D_MODEL = 1024
DEPTH = 4
GRID_W = 64
HEAD_DIM = 64
BLOCK = 128
MIX_W = 512
A_GROUPS = 4
B_HEADS = 4
C_Q_HEADS = 8
D_FF = 2816
ROPE_BASE = 10000.0
EPS = 1e-6
SCALE = HEAD_DIM ** -0.5
NEG_INF = -1e30
LOG2E = float(np.log2(np.e))

LANES = 128
ATTN_COLS = 2304
GATE_COL0 = 3328
VMEM_LIMIT = 56 * 1024 * 1024

TOKEN_TILE = 1024
DIFF_Q_TILE = 256
DIFF_KEY_CHUNK = 256


def _cparams(n_axes):
    return pltpu.CompilerParams(
        dimension_semantics=("arbitrary",) * n_axes, vmem_limit_bytes=VMEM_LIMIT)


def _const_spec(a):
    return pl.BlockSpec(a.shape, lambda *_: (0,) * a.ndim, pipeline_mode=pl.Buffered(1))


def _layer_spec(a, layer):
    return pl.BlockSpec((None,) + a.shape[1:], lambda *_: (layer,) + (0,) * (a.ndim - 1),
                        pipeline_mode=pl.Buffered(1))


def _layer_cols_spec(a, layer, col0, width):
    return pl.BlockSpec((pl.Element(1), pl.Element(a.shape[1]), pl.Element(width)), lambda *_: (layer, 0, col0),
                        pipeline_mode=pl.Buffered(1))


def _rms_mod(x, mul, sh):
    ms = jnp.mean(x * x, axis=-1, keepdims=True)
    return x * lax.rsqrt(ms + EPS) * mul + sh


def _gelu_tanh(x):
    c = np.float32(np.sqrt(2.0 / np.pi))
    return 0.5 * x * (1.0 + jnp.tanh(c * (x + 0.044715 * (x * x * x))))


def _dot(a, b):
    return jnp.dot(a, b, preferred_element_type=F32)


def _dot_nt(a, b):
    return lax.dot_general(a, b, (((1,), (1,)), ((), ())), preferred_element_type=F32)


def _ada_kernel(v_ref, w_ref, b_ref, o_ref):
    v = v_ref[...]
    s = v * jax.nn.sigmoid(v)
    o_ref[...] = _dot(s.astype(BF16), w_ref[...].astype(BF16)) + b_ref[...]


def _ada_call(vpad, w_ada, b_ada):
    rows = vpad.shape[0]
    tn = 1536
    return pl.pallas_call(
        _ada_kernel,
        grid=(DEPTH, 6 * D_MODEL // tn),
        in_specs=[
            pl.BlockSpec((rows, D_MODEL), lambda l, j: (0, 0)),
            pl.BlockSpec((None, D_MODEL, tn), lambda l, j: (l, 0, j)),
            pl.BlockSpec((None, 1, tn), lambda l, j: (l, 0, j)),
        ],
        out_specs=pl.BlockSpec((None, rows, tn), lambda l, j: (l, 0, j)),
        out_shape=jax.ShapeDtypeStruct((DEPTH, rows, 6 * D_MODEL), F32),
        compiler_params=_cparams(2),
        name="ada",
    )(vpad, w_ada, b_ada.reshape(DEPTH, 1, 6 * D_MODEL))


def _inproj_kernel(x_ref, mul_ref, sh_ref, w_ref, cos_ref, sa_ref, sb_ref,
                   bq_ref, bk_ref, bvt_ref, cq_ref, ckd_ref, cvdt_ref, *, rope):
    h = _rms_mod(x_ref[...], mul_ref[...], sh_ref[...]).astype(BF16)
    if rope:
        cos, sa, sb = cos_ref[...], sa_ref[...], sb_ref[...]

    def rot(y):
        if not rope:
            return y
        return y * cos + pltpu.roll(y, LANES - 16, 1) * sa + pltpu.roll(y, 16, 1) * sb

    def dup(y):
        lane = lax.broadcasted_iota(jnp.int32, y.shape, 1)
        sw = pltpu.roll(y, 64, 1)
        return jnp.concatenate([jnp.where(lane < 64, y, sw), jnp.where(lane < 64, sw, y)], axis=1)

    for col0, ref, scale in ((0, bq_ref, SCALE * LOG2E), (512, bk_ref, None), (1536, cq_ref, SCALE * LOG2E)):
        y = _dot(h, w_ref[0, :, col0:col0 + 512])
        if scale is not None:
            y = y * scale
        for g in range(4):
            ref[:, g * LANES:(g + 1) * LANES] = rot(y[:, g * LANES:(g + 1) * LANES]).astype(BF16)
    ykv = _dot(h, w_ref[0, :, 2048:2304])
    ckd_ref[...] = dup(rot(ykv[:, :LANES])).astype(BF16)
    cvdt_ref[...] = dup(ykv[:, LANES:]).T.astype(BF16)
    bvt_ref[...] = _dot(h, w_ref[0, :, 1024:1536]).T.astype(BF16)


def _inproj_call(x, mul, sh, w_in, tables, *, layer, tm, rope):
    bsz, length, _ = x.shape
    tok = lambda w: pl.BlockSpec((None, tm, w), lambda b, i: (b, i, 0))
    vec = pl.BlockSpec((None, 1, D_MODEL), lambda b, i: (b, 0, 0))
    tab = pl.BlockSpec((tm, LANES), lambda b, i: (i, 0))
    tok_t = lambda w: pl.BlockSpec((None, w, tm), lambda b, i: (b, 0, i))
    out = lambda w: jax.ShapeDtypeStruct((bsz, length, w), BF16)
    out_t = lambda w: jax.ShapeDtypeStruct((bsz, w, length), BF16)
    return pl.pallas_call(
        functools.partial(_inproj_kernel, rope=rope),
        grid=(bsz, length // tm),
        in_specs=[tok(D_MODEL), vec, vec,
                  _layer_cols_spec(w_in, layer, 2 * MIX_W, ATTN_COLS), tab, tab, tab],
        out_specs=[tok(512), tok(512), tok_t(MIX_W), tok(512), tok(256), tok_t(2 * LANES)],
        out_shape=[out(512), out(512), out_t(MIX_W), out(512), out(256), out_t(2 * LANES)],
        compiler_params=_cparams(2),
        name="inproj",
    )(x, mul, sh, w_in, *tables)


SAFE_EXP2_RANGE = 90.0


def _diff_kernel(sc_ref, q_ref, qall_ref, g_ref, ind_ref, *refs, seg_lens, kc, nq):
    o_ref, p_scr, f_scr, safe_scr = refs[-4:]
    t = pl.program_id(0)
    tq = q_ref.shape[0]
    lanei = lax.broadcasted_iota(jnp.int32, (1, 2 * tq), 1)

    @pl.when(t == 0)
    def _():
        p_scr[...] = jnp.zeros(p_scr.shape, BF16)
        f_scr[...] = jnp.zeros(f_scr.shape, F32)

    chunks = []
    row0 = 0
    for si, n_keys in enumerate(seg_lens):
        for st in range(0, n_keys, kc):
            size = min(kc, n_keys - st)
            chunks.append((refs[2 * si], refs[2 * si + 1], st, size, row0))
            row0 += size

    def max_sq_norm(x):
        xf = x.astype(F32)
        return jnp.max(_dot((xf * xf).astype(BF16), ind_ref[...]), axis=0, keepdims=True)

    @pl.when(t % nq == 0)
    def _():
        kn = None
        for k_ref, _, st, size, _ in chunks:
            n2 = max_sq_norm(k_ref[st:st + size, :])
            kn = n2 if kn is None else jnp.maximum(kn, n2)
        qn = None
        n_q = qall_ref.shape[0]
        for st in range(0, n_q, 2 * kc):
            n2 = max_sq_norm(qall_ref[st:min(st + 2 * kc, n_q), :])
            qn = n2 if qn is None else jnp.maximum(qn, n2)
        bound2 = jnp.max(kn * qn)
        safe_scr[0] = (bound2 < SAFE_EXP2_RANGE * SAFE_EXP2_RANGE).astype(jnp.int32)

    safe = safe_scr[0] == 1

    def step(max_free):
        q = q_ref[...]
        lane = lax.broadcasted_iota(jnp.int32, q.shape, 1)
        zero = jnp.zeros_like(q)
        qq = jnp.concatenate([jnp.where(lane < 64, q, zero), jnp.where(lane >= 64, q, zero)], axis=0)
        f_prev = f_scr[...]
        mcs, lcs = [], []
        acc = None
        for c, (k_ref, vt_ref, st, size, r0) in enumerate(chunks):
            f1 = f_prev[c:c + 1, 0:tq].astype(BF16)
            f2 = f_prev[c:c + 1, tq:2 * tq].astype(BF16)
            a_t = p_scr[r0:r0 + size, 0:tq] * f1 - p_scr[r0:r0 + size, tq:2 * tq] * f2
            part = _dot(vt_ref[:, st:st + size], a_t)
            acc = part if acc is None else acc + part
            s = _dot_nt(k_ref[st:st + size, :], qq)
            if max_free:
                p = jnp.exp2(s)
            else:
                mc = jnp.max(s, axis=0, keepdims=True)
                p = jnp.exp2(s - mc)
                mcs.append(mc)
            lcs.append(jnp.sum(p, axis=0, keepdims=True))
            p_scr[r0:r0 + size, :] = p.astype(BF16)

        if max_free:
            es = [None] * len(chunks)
            l_all = functools.reduce(lambda a, b: a + b, lcs)
        else:
            m_all = functools.reduce(jnp.maximum, mcs)
            es = [jnp.exp2(mc - m_all) for mc in mcs]
            l_all = functools.reduce(lambda a, b: a + b, [lc * e for lc, e in zip(lcs, es)])
        coef = jnp.where(lanei < tq, 1.0, sc_ref[0]) / l_all
        for c, e in enumerate(es):
            f_scr[c:c + 1, :] = coef if e is None else e * coef

        o = acc.T
        ms = jnp.mean(o * o, axis=-1, keepdims=True)
        o_ref[...] = (o * lax.rsqrt(ms + EPS) * g_ref[...] * sc_ref[1]).astype(BF16)

    pl.when(safe)(lambda: step(True))
    pl.when(jnp.logical_not(safe))(lambda: step(False))


def _diff_call(scal, q, subln_g, segs, *, tq, kc):
    bsz, lq, _ = q.shape
    nq = lq // tq
    n_blocks = bsz * B_HEADS * nq
    half = np.arange(LANES) // HEAD_DIM
    ind = jnp.asarray(half[:, None] == half[None, :], BF16)

    def block(t):
        b, r = t // (B_HEADS * nq), t % (B_HEADS * nq)
        return b, r // nq, r % nq

    cur = lambda t: block(jnp.minimum(t, n_blocks - 1))
    prev = lambda t: block(jnp.maximum(t - 1, 0))

    def q_map(t):
        b, h, i = cur(t)
        return b, i, h

    def kq_map(t):
        b, h, _ = cur(t)
        return b, 0, h

    def vt_map(t):
        b, h, _ = prev(t)
        return b, h, 0

    def o_map(t):
        b, h, i = prev(t)
        return b, i, h

    in_specs = [pl.BlockSpec(memory_space=pltpu.SMEM),
                pl.BlockSpec((None, tq, LANES), q_map),
                pl.BlockSpec((None, lq, LANES), kq_map),
                pl.BlockSpec((1, LANES), lambda t: (0, 0)),
                _const_spec(ind)]
    args = [scal, q, q, subln_g, ind]
    for k, vt in segs:
        n_keys = k.shape[1]
        in_specs += [pl.BlockSpec((None, n_keys, LANES), kq_map),
                     pl.BlockSpec((None, LANES, n_keys), vt_map)]
        args += [k, vt]
    seg_lens = tuple(k.shape[1] for k, _ in segs)
    n_chunks = sum(-(-n // kc) for n in seg_lens)
    return pl.pallas_call(
        functools.partial(_diff_kernel, seg_lens=seg_lens, kc=kc, nq=nq),
        grid=(n_blocks + 1,),
        in_specs=in_specs,
        out_specs=pl.BlockSpec((None, tq, LANES), o_map),
        out_shape=jax.ShapeDtypeStruct((bsz, lq, MIX_W), BF16),
        scratch_shapes=[pltpu.VMEM((sum(seg_lens), 2 * tq), BF16),
                        pltpu.VMEM((-(-n_chunks // 8) * 8, 2 * tq), F32),
                        pltpu.SMEM((1,), jnp.int32)],
        compiler_params=_cparams(1),
        name="diff",
    )(*args)


WIN_SUB = 8


def _win_kernel(sink_ref, q_ref, *refs, local):
    o_ref = refs[-1]
    n_sub = q_ref.shape[0] // BLOCK
    kx_ref, vxt_ref = refs[-3], refs[-2]
    n = pl.program_id(1)
    nb = pl.num_programs(1)
    nq = C_Q_HEADS // 2 * BLOCK
    lane = lax.broadcasted_iota(jnp.int32, (BLOCK, LANES), 1)
    headi = lax.broadcasted_iota(jnp.int32, (1, nq), 1) // BLOCK
    if local:
        kp_ref, kc_ref, kn_ref, vpt_ref, vct_ref, vnt_ref = refs[:6]
        key = lax.broadcasted_iota(jnp.int32, (BLOCK, nq), 0)
        qry = lax.broadcasted_iota(jnp.int32, (BLOCK, nq), 1) & (BLOCK - 1)
        tri_prev, tri_next = key >= qry, key <= qry

    probs = [(sub, kvh) for sub in range(n_sub) for kvh in range(2)]
    s_l, sink_l, vt_l = [], [], []
    for sub, kvh in probs:
        ksl = slice(kvh * LANES, (kvh + 1) * LANES)
        rows = slice(sub * BLOCK, (sub + 1) * BLOCK)
        parts = []
        for g in range(4):
            grp = kvh * 2 + g // 2
            qg = q_ref[rows, grp * LANES:(grp + 1) * LANES]
            parts.append(jnp.where((lane < 64) if g % 2 == 0 else (lane >= 64), qg, jnp.zeros_like(qg)))
        qs = jnp.concatenate(parts, axis=0)
        sink = jnp.zeros((1, nq), F32)
        for g in range(4):
            sink = jnp.where(headi == g, sink_ref[kvh * 4 + g] * LOG2E, sink)
        if local:
            kb, vb = [], []
            for j in (sub - 1, sub, sub + 1):
                if j < 0:
                    kb.append(kp_ref[:, ksl]), vb.append(vpt_ref[ksl, :])
                elif j >= n_sub:
                    kb.append(kn_ref[:, ksl]), vb.append(vnt_ref[ksl, :])
                else:
                    kb.append(kc_ref[j * BLOCK:(j + 1) * BLOCK, ksl])
                    vb.append(vct_ref[ksl, j * BLOCK:(j + 1) * BLOCK])
            k_all = jnp.concatenate(kb + [kx_ref[:, ksl]], axis=0)
            vt_all = jnp.concatenate(vb + [vxt_ref[ksl, :]], axis=1)
        else:
            k_all, vt_all = kx_ref[:, ksl], vxt_ref[ksl, :]
        s = _dot_nt(k_all, qs)
        if local:
            valid_prev = tri_prev if sub > 0 else tri_prev & (n > 0)
            valid_next = tri_next if sub < n_sub - 1 else tri_next & (n < nb - 1)
            s = jnp.concatenate([jnp.where(valid_prev, s[0:BLOCK], NEG_INF), s[BLOCK:2 * BLOCK],
                                 jnp.where(valid_next, s[2 * BLOCK:3 * BLOCK], NEG_INF), s[3 * BLOCK:]], axis=0)
        s_l.append(s)
        sink_l.append(sink)
        vt_l.append(vt_all)
    m_l = [jnp.maximum(jnp.max(s, axis=0, keepdims=True), sink) for s, sink in zip(s_l, sink_l)]
    p_l = [jnp.exp2(s - m) for s, m in zip(s_l, m_l)]
    l_l = [jnp.sum(p, axis=0, keepdims=True) + jnp.exp2(sink - m) for p, sink, m in zip(p_l, sink_l, m_l)]
    ot_l = [_dot(vt, p.astype(BF16)) / l for vt, p, l in zip(vt_l, p_l, l_l)]
    for (sub, kvh), o_t in zip(probs, ot_l):
        o = o_t.T
        for j in range(2):
            grp = kvh * 2 + j
            o_ref[sub * BLOCK:(sub + 1) * BLOCK, grp * LANES:(grp + 1) * LANES] = jnp.where(
                lane < 64, o[2 * j * BLOCK:(2 * j + 1) * BLOCK], o[(2 * j + 1) * BLOCK:(2 * j + 2) * BLOCK]
            ).astype(BF16)


def _win_call(sinks, q, kd, vdt, kxd, vxdt, *, local):
    bsz, lq, _ = q.shape
    nb = lq // BLOCK
    n_sub = min(WIN_SUB, nb)
    tq = n_sub * BLOCK
    in_specs = [pl.BlockSpec(memory_space=pltpu.SMEM), pl.BlockSpec((None, tq, MIX_W), lambda b, n: (b, n, 0))]
    args = [sinks, q]
    if local:
        prv = lambda n: jnp.maximum(n * n_sub - 1, 0)
        nxt = lambda n: jnp.minimum((n + 1) * n_sub, nb - 1)
        in_specs += [pl.BlockSpec((None, BLOCK, 2 * LANES), lambda b, n: (b, prv(n), 0)),
                     pl.BlockSpec((None, tq, 2 * LANES), lambda b, n: (b, n, 0)),
                     pl.BlockSpec((None, BLOCK, 2 * LANES), lambda b, n: (b, nxt(n), 0)),
                     pl.BlockSpec((None, 2 * LANES, BLOCK), lambda b, n: (b, 0, prv(n))),
                     pl.BlockSpec((None, 2 * LANES, tq), lambda b, n: (b, 0, n)),
                     pl.BlockSpec((None, 2 * LANES, BLOCK), lambda b, n: (b, 0, nxt(n)))]
        args += [kd] * 3 + [vdt] * 3
    n_ctx = kxd.shape[1]
    in_specs += [pl.BlockSpec((None, n_ctx, 2 * LANES), lambda b, n: (b, 0, 0)),
                 pl.BlockSpec((None, 2 * LANES, n_ctx), lambda b, n: (b, 0, 0))]
    args += [kxd, vxdt]
    return pl.pallas_call(
        functools.partial(_win_kernel, local=local),
        grid=(bsz, lq // tq),
        in_specs=in_specs,
        out_specs=pl.BlockSpec((None, tq, MIX_W), lambda b, n: (b, n, 0)),
        out_shape=jax.ShapeDtypeStruct((bsz, lq, MIX_W), BF16),
        compiler_params=_cparams(2),
        name="win",
    )(*args)


def _merge_kernel(x_ref, mul_ref, sh_ref, gate_ref, wuv_ref, wg_ref, lng_ref, lnb_ref, ws_ref, bs_ref,
                  ob_ref, oc_ref, wbr_ref, wout_ref, o_ref):
    x = x_ref[...]
    tm = x.shape[0]
    h = _rms_mod(x, mul_ref[...], sh_ref[...]).astype(BF16)
    u = _gelu_tanh(_dot(h, wuv_ref[0, :, 0:MIX_W]))
    v = _gelu_tanh(_dot(h, wuv_ref[0, :, MIX_W:2 * MIX_W]))
    mu = jnp.mean(v, axis=-1, keepdims=True)
    vc = v - mu
    var = jnp.mean(vc * vc, axis=-1, keepdims=True)
    vn = (vc * lax.rsqrt(var + EPS) * lng_ref[...] + lnb_ref[...]).astype(BF16)
    n_chunks = tm // BLOCK
    mixed = []
    for g in range(A_GROUPS):
        gs = slice(g * LANES, (g + 1) * LANES)
        rhs = jnp.concatenate([vn[c * BLOCK:(c + 1) * BLOCK, gs] for c in range(n_chunks)], axis=1)
        mixed.append(_dot(ws_ref[g], rhs))
    o_a = (u * jnp.concatenate(
        [jnp.concatenate([mixed[g][:, c * LANES:(c + 1) * LANES] + bs_ref[g] for g in range(A_GROUPS)], axis=1)
         for c in range(n_chunks)], axis=0)).astype(BF16)
    y = None
    for i, o_i in enumerate((o_a, ob_ref[...], oc_ref[...])):
        gate = jax.nn.sigmoid(_dot(h, wg_ref[0, :, i * D_MODEL:(i + 1) * D_MODEL]))
        t = gate * _dot(o_i, wbr_ref[i])
        y = t if y is None else y + t
    o_ref[...] = x + gate_ref[...] * _dot(y.astype(BF16), wout_ref[...])


def _merge_call(x, mul, sh, gate, w_in, ln_g, ln_b, w_s, b_s, o_b, o_c, w_br, w_out, *, layer, tm):
    bsz, length, _ = x.shape
    tok = lambda w: pl.BlockSpec((None, tm, w), lambda b, i: (b, i, 0))
    vec = pl.BlockSpec((None, 1, D_MODEL), lambda b, i: (b, 0, 0))
    full = functools.partial(_layer_spec, layer=layer)
    return pl.pallas_call(
        _merge_kernel,
        grid=(bsz, length // tm),
        in_specs=[tok(D_MODEL), vec, vec, vec, _layer_cols_spec(w_in, layer, 0, 2 * MIX_W),
                  _layer_cols_spec(w_in, layer, GATE_COL0, 3 * D_MODEL), full(ln_g), full(ln_b), full(w_s), full(b_s),
                  tok(MIX_W), tok(MIX_W), full(w_br), full(w_out)],
        out_specs=tok(D_MODEL),
        out_shape=jax.ShapeDtypeStruct(x.shape, F32),
        compiler_params=_cparams(2),
        name="merge",
    )(x, mul, sh, gate, w_in, w_in, ln_g, ln_b, w_s, b_s, o_b, o_c, w_br, w_out)


HALO = 8
MXU_TILE = 256
FF_EDGES = (0, 6 * MXU_TILE, D_FF)


def _ffn_kernel(x_ref, xp_ref, xn_ref, mul_ref, sh_ref, gate_ref, wg_ref, cw_ref, cb_ref, wu_ref, wd_ref,
                fg_ref, o_ref, *, final):
    i = pl.program_id(1)
    nt = pl.num_programs(1)
    x = x_ref[...]
    tm = x.shape[0]
    xe = jnp.concatenate([xp_ref[...], x, xn_ref[...]], axis=0)
    he = _rms_mod(xe, mul_ref[...], sh_ref[...]).astype(BF16)
    hm = he[HALO:HALO + tm]
    rowe = lax.broadcasted_iota(jnp.int32, (tm + 2 * HALO, 1), 0)
    keep = ((rowe >= HALO) | (i > 0)) & ((rowe < HALO + tm) | (i < nt - 1))
    spans = [slice(c0, c1) for c0, c1 in zip(FF_EDGES[:-1], FF_EDGES[1:])]
    a_l = [jnp.where(keep, _dot(he, wg_ref[:, cs]), 0.0) for cs in spans]
    u_l = [_dot(hm, wu_ref[:, cs]) for cs in spans]
    z_l = []
    for cs, a, u in zip(spans, a_l, u_l):
        a_prev = pltpu.roll(a, 1, 0)[HALO:HALO + tm]
        a_next = pltpu.roll(a, tm + 2 * HALO - 1, 0)[HALO:HALO + tm]
        a = (a_prev * cw_ref[0:1, cs] + a[HALO:HALO + tm] * cw_ref[1:2, cs] + a_next * cw_ref[2:3, cs]
             + cb_ref[:, cs])
        z_l.append((a * jax.nn.sigmoid(a) * u).astype(BF16))
    acc = None
    for cs, z in zip(spans, z_l):
        t = _dot(z, wd_ref[cs, :])
        acc = t if acc is None else acc + t
    out = x + gate_ref[...] * acc
    if final:
        ms = jnp.mean(out * out, axis=-1, keepdims=True)
        out = out * lax.rsqrt(ms + EPS) * fg_ref[...]
    o_ref[...] = out


def _ffn_call(x, mul, sh, gate, w_gate, conv_w, conv_b, w_up, w_down, final_g, *, layer, tm, final):
    bsz, length, _ = x.shape
    per = tm // HALO
    last = length // HALO - 1
    tok = pl.BlockSpec((None, tm, D_MODEL), lambda b, i: (b, i, 0))
    prv = pl.BlockSpec((None, HALO, D_MODEL), lambda b, i: (b, jnp.maximum(i * per - 1, 0), 0))
    nxt = pl.BlockSpec((None, HALO, D_MODEL), lambda b, i: (b, jnp.minimum((i + 1) * per, last), 0))
    vec = pl.BlockSpec((None, 1, D_MODEL), lambda b, i: (b, 0, 0))
    full = functools.partial(_layer_spec, layer=layer)
    return pl.pallas_call(
        functools.partial(_ffn_kernel, final=final),
        grid=(bsz, length // tm),
        in_specs=[tok, prv, nxt, vec, vec, vec, full(w_gate), full(conv_w), full(conv_b), full(w_up),
                  full(w_down), _const_spec(final_g)],
        out_specs=tok,
        out_shape=jax.ShapeDtypeStruct(x.shape, F32),
        compiler_params=_cparams(2),
        name="ffn",
    )(x, x, x, mul, sh, gate, w_gate, conv_w, conv_b, w_up, w_down, final_g)


def _rope_tables(length):
    pos = jnp.arange(length)
    rows = (pos // GRID_W).astype(F32)
    cols = (pos % GRID_W).astype(F32)
    half = HEAD_DIM // 2
    inv = ROPE_BASE ** (-jnp.arange(0, half, 2, dtype=F32) / half)
    ang_r = rows[:, None] * inv[None, :]
    ang_c = cols[:, None] * inv[None, :]
    zero = jnp.zeros_like(ang_r)
    cos = jnp.concatenate([jnp.cos(ang_r)] * 2 + [jnp.cos(ang_c)] * 2, axis=1)
    sa = jnp.concatenate([-jnp.sin(ang_r), zero, -jnp.sin(ang_c), zero], axis=1)
    sb = jnp.concatenate([zero, jnp.sin(ang_r), zero, jnp.sin(ang_c)], axis=1)
    return tuple(jnp.tile(t, (1, LANES // HEAD_DIM)) for t in (cos, sa, sb))


def kernel(x, c, ctx, c_ctx, w_ada, b_ada, norm1_g, w_in, sgu_ln_g, sgu_ln_b, w_s, b_s, lam_q1, lam_k1,
           lam_q2, lam_k2, diff_subln_g, sinks, w_branch, w_out, norm2_g, w_gate, conv_w, conv_b, w_up,
           w_down, final_g):
    bsz, length, _ = x.shape
    n_ctx = ctx.shape[1]
    tables = _rope_tables(length)
    ctx_tables = tuple(t[:n_ctx] for t in tables)

    rows = -(-(bsz + 1) // 8) * 8
    vpad = jnp.zeros((rows, D_MODEL), F32).at[:bsz].set(c).at[bsz].set(c_ctx)
    ada = _ada_call(vpad, w_ada, b_ada)

    w_in_b = w_in.astype(BF16)
    w_s_b = w_s.astype(BF16)
    b_s_b = jnp.broadcast_to(b_s[..., None], b_s.shape + (LANES,))
    w_br_b = w_branch.astype(BF16)
    w_out_b = w_out.astype(BF16)
    w_gate_b = w_gate.astype(BF16)
    w_up_b = w_up.astype(BF16)
    w_down_b = w_down.astype(BF16)
    final_g2 = final_g.reshape(1, D_MODEL)
    mix_w = (w_in_b, sgu_ln_g.reshape(DEPTH, 1, MIX_W), sgu_ln_b.reshape(DEPTH, 1, MIX_W), w_s_b, b_s_b)
    ffn_w = (w_gate_b, conv_w, conv_b.reshape(DEPTH, 1, D_FF), w_up_b, w_down_b, final_g2)

    for l in range(DEPTH):
        last = l == DEPTH - 1
        m = ada[l]
        sh1, sc1, g1, sh2, sc2, g2 = [m[:, j * D_MODEL:(j + 1) * D_MODEL] for j in range(6)]
        mul1 = norm1_g[l][None, :] * (1.0 + sc1)
        mul2 = norm2_g[l][None, :] * (1.0 + sc2)
        lat = lambda a: a[:bsz, None, :]
        cx = lambda a: jnp.broadcast_to(a[bsz][None, None, :], (bsz, 1, D_MODEL))

        lam_init = 0.8 - 0.6 * float(np.exp(-0.3 * l))
        lam = jnp.exp(jnp.sum(lam_q1[l] * lam_k1[l])) - jnp.exp(jnp.sum(lam_q2[l] * lam_k2[l])) + lam_init
        scal = jnp.stack([lam, jnp.float32(1.0 - lam_init)]).astype(F32)
        subln = diff_subln_g[l].reshape(1, LANES)

        bq, bk, bvt, cq, ckd, cvdt = _inproj_call(x, lat(mul1), lat(sh1), w_in_b, tables, layer=l, tm=TOKEN_TILE,
                                                  rope=True)
        xbq, xbk, xbvt, xcq, xckd, xcvdt = _inproj_call(ctx, cx(mul1), cx(sh1), w_in_b, ctx_tables, layer=l,
                                                        tm=n_ctx, rope=False)

        o_b = _diff_call(scal, bq, subln, [(bk, bvt), (xbk, xbvt)], tq=DIFF_Q_TILE, kc=DIFF_KEY_CHUNK)
        o_c = _win_call(sinks[l], cq, ckd, cvdt, xckd, xcvdt, local=True)
        x_mid = _merge_call(x, lat(mul1), lat(sh1), lat(g1), *mix_w, o_b, o_c, w_br_b, w_out_b, layer=l,
                            tm=TOKEN_TILE)
        if not last:
            xo_b = _diff_call(scal, xbq, subln, [(xbk, xbvt)], tq=n_ctx, kc=n_ctx)
            xo_c = _win_call(sinks[l], xcq, None, None, xckd, xcvdt, local=False)
            ctx_mid = _merge_call(ctx, cx(mul1), cx(sh1), cx(g1), *mix_w, xo_b, xo_c, w_br_b, w_out_b, layer=l,
                                  tm=n_ctx)
        x = _ffn_call(x_mid, lat(mul2), lat(sh2), lat(g2), *ffn_w, layer=l, tm=TOKEN_TILE, final=last)
        if not last:
            ctx = _ffn_call(ctx_mid, cx(mul2), cx(sh2), cx(g2), *ffn_w, layer=l, tm=n_ctx, final=False)
    return x
```

```python
import functools

import numpy as np
import jax
import jax.numpy as jnp
from jax import lax
from jax.experimental import pallas as pl
from jax.experimental.pallas import tpu as pltpu

F32 = jnp.float32
BF16 = jnp.bfloat16

D_MODEL = 1024
DEPTH = 4
GRID_W = 64
HEAD_DIM = 64
BLOCK = 128
MIX_W = 512
A_GROUPS = 4
B_HEADS = 4
C_Q_HEADS = 8
D_FF = 2816
ROPE_BASE = 10000.0
EPS = 1e-6
SCALE = HEAD_DIM ** -0.5
NEG_INF = -1e30
LOG2E = float(np.log2(np.e))

LANES = 128
ATTN_COLS = 2304
GATE_COL0 = 3328
VMEM_LIMIT = 56 * 1024 * 1024

TOKEN_TILE = 1024
DIFF_Q_TILE = 512
DIFF_KEY_CHUNK = 512


def _cparams(n_axes):
    return pltpu.CompilerParams(
        dimension_semantics=("arbitrary",) * n_axes, vmem_limit_bytes=VMEM_LIMIT)


def _const_spec(a):
    return pl.BlockSpec(a.shape, lambda *_: (0,) * a.ndim, pipeline_mode=pl.Buffered(1))


def _layer_spec(a, layer):
    return pl.BlockSpec((None,) + a.shape[1:], lambda *_: (layer,) + (0,) * (a.ndim - 1),
                        pipeline_mode=pl.Buffered(1))


def _layer_cols_spec(a, layer, col0, width):
    return pl.BlockSpec((pl.Element(1), pl.Element(a.shape[1]), pl.Element(width)), lambda *_: (layer, 0, col0),
                        pipeline_mode=pl.Buffered(1))


def _rms_mod(x, mul, sh):
    ms = jnp.mean(x * x, axis=-1, keepdims=True)
    return x * lax.rsqrt(ms + EPS) * mul + sh


def _gelu_tanh(x):
    c = np.float32(np.sqrt(2.0 / np.pi))
    return 0.5 * x * (1.0 + jnp.tanh(c * (x + 0.044715 * (x * x * x))))


def _dot(a, b):
    return jnp.dot(a, b, preferred_element_type=F32)


def _dot_nt(a, b):
    return lax.dot_general(a, b, (((1,), (1,)), ((), ())), preferred_element_type=F32)


def _ada_kernel(v_ref, w_ref, b_ref, o_ref):
    v = v_ref[...]
    s = v * jax.nn.sigmoid(v)
    o_ref[...] = _dot(s.astype(BF16), w_ref[...].astype(BF16)) + b_ref[...]


def _ada_call(vpad, w_ada, b_ada):
    rows = vpad.shape[0]
    tn = 1536
    return pl.pallas_call(
        _ada_kernel,
        grid=(DEPTH, 6 * D_MODEL // tn),
        in_specs=[
            pl.BlockSpec((rows, D_MODEL), lambda l, j: (0, 0)),
            pl.BlockSpec((None, D_MODEL, tn), lambda l, j: (l, 0, j)),
            pl.BlockSpec((None, 1, tn), lambda l, j: (l, 0, j)),
        ],
        out_specs=pl.BlockSpec((None, rows, tn), lambda l, j: (l, 0, j)),
        out_shape=jax.ShapeDtypeStruct((DEPTH, rows, 6 * D_MODEL), F32),
        compiler_params=_cparams(2),
        name="ada",
    )(vpad, w_ada, b_ada.reshape(DEPTH, 1, 6 * D_MODEL))


def _inproj_kernel(x_ref, mul_ref, sh_ref, w_ref, cos_ref, sa_ref, sb_ref,
                   bq_ref, bk_ref, bvt_ref, cq_ref, ckd_ref, cvdt_ref, *, rope):
    h = _rms_mod(x_ref[...], mul_ref[...], sh_ref[...]).astype(BF16)
    if rope:
        cos, sa, sb = cos_ref[...], sa_ref[...], sb_ref[...]

    def rot(y):
        if not rope:
            return y
        return y * cos + pltpu.roll(y, LANES - 16, 1) * sa + pltpu.roll(y, 16, 1) * sb

    def dup(y):
        lane = lax.broadcasted_iota(jnp.int32, y.shape, 1)
        sw = pltpu.roll(y, 64, 1)
        return jnp.concatenate([jnp.where(lane < 64, y, sw), jnp.where(lane < 64, sw, y)], axis=1)

    for col0, ref, scale in ((0, bq_ref, SCALE * LOG2E), (512, bk_ref, None), (1536, cq_ref, SCALE * LOG2E)):
        y = _dot(h, w_ref[0, :, col0:col0 + 512])
        if scale is not None:
            y = y * scale
        for g in range(4):
            ref[:, g * LANES:(g + 1) * LANES] = rot(y[:, g * LANES:(g + 1) * LANES]).astype(BF16)
    ykv = _dot(h, w_ref[0, :, 2048:2304])
    ckd_ref[...] = dup(rot(ykv[:, :LANES])).astype(BF16)
    cvdt_ref[...] = dup(ykv[:, LANES:]).T.astype(BF16)
    bvt_ref[...] = _dot(h, w_ref[0, :, 1024:1536]).T.astype(BF16)


def _inproj_call(x, mul, sh, w_in, tables, *, layer, tm, rope):
    bsz, length, _ = x.shape
    tok = lambda w: pl.BlockSpec((None, tm, w), lambda b, i: (b, i, 0))
    vec = pl.BlockSpec((None, 1, D_MODEL), lambda b, i: (b, 0, 0))
    tab = pl.BlockSpec((tm, LANES), lambda b, i: (i, 0))
    tok_t = lambda w: pl.BlockSpec((None, w, tm), lambda b, i: (b, 0, i))
    out = lambda w: jax.ShapeDtypeStruct((bsz, length, w), BF16)
    out_t = lambda w: jax.ShapeDtypeStruct((bsz, w, length), BF16)
    return pl.pallas_call(
        functools.partial(_inproj_kernel, rope=rope),
        grid=(bsz, length // tm),
        in_specs=[tok(D_MODEL), vec, vec,
                  _layer_cols_spec(w_in, layer, 2 * MIX_W, ATTN_COLS), tab, tab, tab],
        out_specs=[tok(512), tok(512), tok_t(MIX_W), tok(512), tok(256), tok_t(2 * LANES)],
        out_shape=[out(512), out(512), out_t(MIX_W), out(512), out(256), out_t(2 * LANES)],
        compiler_params=_cparams(2),
        name="inproj",
    )(x, mul, sh, w_in, *tables)


SAFE_EXP2_RANGE = 90.0


def _diff_kernel(sc_ref, q_ref, qall_ref, g_ref, ind_ref, *refs, seg_lens, kc, nq):
    o_ref, p_scr, f_scr, safe_scr = refs[-4:]
    t = pl.program_id(0)
    tq = q_ref.shape[0]
    lanei = lax.broadcasted_iota(jnp.int32, (1, 2 * tq), 1)

    @pl.when(t == 0)
    def _():
        p_scr[...] = jnp.zeros(p_scr.shape, BF16)
        f_scr[...] = jnp.zeros(f_scr.shape, F32)

    chunks = []
    row0 = 0
    for si, n_keys in enumerate(seg_lens):
        for st in range(0, n_keys, kc):
            size = min(kc, n_keys - st)
            chunks.append((refs[2 * si], refs[2 * si + 1], st, size, row0))
            row0 += size

    def max_sq_norm(x):
        xf = x.astype(F32)
        return jnp.max(_dot((xf * xf).astype(BF16), ind_ref[...]), axis=0, keepdims=True)

    @pl.when(t % nq == 0)
    def _():
        kn = None
        for k_ref, _, st, size, _ in chunks:
            n2 = max_sq_norm(k_ref[st:st + size, :])
            kn = n2 if kn is None else jnp.maximum(kn, n2)
        qn = None
        n_q = qall_ref.shape[0]
        for st in range(0, n_q, 2 * kc):
            n2 = max_sq_norm(qall_ref[st:min(st + 2 * kc, n_q), :])
            qn = n2 if qn is None else jnp.maximum(qn, n2)
        bound2 = jnp.max(kn * qn)
        safe_scr[0] = (bound2 < SAFE_EXP2_RANGE * SAFE_EXP2_RANGE).astype(jnp.int32)

    safe = safe_scr[0] == 1

    def step(max_free):
        q = q_ref[...]
        lane = lax.broadcasted_iota(jnp.int32, q.shape, 1)
        zero = jnp.zeros_like(q)
        qq = jnp.concatenate([jnp.where(lane < 64, q, zero), jnp.where(lane >= 64, q, zero)], axis=0)
        f_prev = f_scr[...]
        mcs, lcs = [], []
        acc = None
        for c, (k_ref, vt_ref, st, size, r0) in enumerate(chunks):
            f1 = f_prev[c:c + 1, 0:tq].astype(BF16)
            f2 = f_prev[c:c + 1, tq:2 * tq].astype(BF16)
            a_t = p_scr[r0:r0 + size, 0:tq] * f1 - p_scr[r0:r0 + size, tq:2 * tq] * f2
            part = _dot(vt_ref[:, st:st + size], a_t)
            acc = part if acc is None else acc + part
            s = _dot_nt(k_ref[st:st + size, :], qq)
            if max_free:
                p = jnp.exp2(s)
            else:
                mc = jnp.max(s, axis=0, keepdims=True)
                p = jnp.exp2(s - mc)
                mcs.append(mc)
            lcs.append(jnp.sum(p, axis=0, keepdims=True))
            p_scr[r0:r0 + size, :] = p.astype(BF16)

        if max_free:
            es = [None] * len(chunks)
            l_all = functools.reduce(lambda a, b: a + b, lcs)
        else:
            m_all = functools.reduce(jnp.maximum, mcs)
            es = [jnp.exp2(mc - m_all) for mc in mcs]
            l_all = functools.reduce(lambda a, b: a + b, [lc * e for lc, e in zip(lcs, es)])
        coef = jnp.where(lanei < tq, 1.0, sc_ref[0]) / l_all
        for c, e in enumerate(es):
            f_scr[c:c + 1, :] = coef if e is None else e * coef

        o = acc.T
        ms = jnp.mean(o * o, axis=-1, keepdims=True)
        o_ref[...] = (o * lax.rsqrt(ms + EPS) * g_ref[...] * sc_ref[1]).astype(BF16)

    pl.when(safe)(lambda: step(True))
    pl.when(jnp.logical_not(safe))(lambda: step(False))


def _diff_call(scal, q, subln_g, segs, *, tq, kc):
    bsz, lq, _ = q.shape
    nq = lq // tq
    n_blocks = bsz * B_HEADS * nq
    half = np.arange(LANES) // HEAD_DIM
    ind = jnp.asarray(half[:, None] == half[None, :], BF16)

    def block(t):
        b, r = t // (B_HEADS * nq), t % (B_HEADS * nq)
        return b, r // nq, r % nq

    cur = lambda t: block(jnp.minimum(t, n_blocks - 1))
    prev = lambda t: block(jnp.maximum(t - 1, 0))

    def q_map(t):
        b, h, i = cur(t)
        return b, i, h

    def kq_map(t):
        b, h, _ = cur(t)
        return b, 0, h

    def vt_map(t):
        b, h, _ = prev(t)
        return b, h, 0

    def o_map(t):
        b, h, i = prev(t)
        return b, i, h

    in_specs = [pl.BlockSpec(memory_space=pltpu.SMEM),
                pl.BlockSpec((None, tq, LANES), q_map),
                pl.BlockSpec((None, lq, LANES), kq_map),
                pl.BlockSpec((1, LANES), lambda t: (0, 0)),
                _const_spec(ind)]
    args = [scal, q, q, subln_g, ind]
    for k, vt in segs:
        n_keys = k.shape[1]
        in_specs += [pl.BlockSpec((None, n_keys, LANES), kq_map),
                     pl.BlockSpec((None, LANES, n_keys), vt_map)]
        args += [k, vt]
    seg_lens = tuple(k.shape[1] for k, _ in segs)
    n_chunks = sum(-(-n // kc) for n in seg_lens)
    return pl.pallas_call(
        functools.partial(_diff_kernel, seg_lens=seg_lens, kc=kc, nq=nq),
        grid=(n_blocks + 1,),
        in_specs=in_specs,
        out_specs=pl.BlockSpec((None, tq, LANES), o_map),
        out_shape=jax.ShapeDtypeStruct((bsz, lq, MIX_W), BF16),
        scratch_shapes=[pltpu.VMEM((sum(seg_lens), 2 * tq), BF16),
                        pltpu.VMEM((-(-n_chunks // 8) * 8, 2 * tq), F32),
                        pltpu.SMEM((1,), jnp.int32)],
        compiler_params=_cparams(1),
        name="diff",
    )(*args)


WIN_SUB = 8


def _win_kernel(sink_ref, q_ref, *refs, local):
    o_ref = refs[-1]
    n_sub = q_ref.shape[0] // BLOCK
    kx_ref, vxt_ref = refs[-3], refs[-2]
    n = pl.program_id(1)
    nb = pl.num_programs(1)
    nq = C_Q_HEADS // 2 * BLOCK
    lane = lax.broadcasted_iota(jnp.int32, (BLOCK, LANES), 1)
    headi = lax.broadcasted_iota(jnp.int32, (1, nq), 1) // BLOCK
    if local:
        kp_ref, kc_ref, kn_ref, vpt_ref, vct_ref, vnt_ref = refs[:6]
        key = lax.broadcasted_iota(jnp.int32, (BLOCK, nq), 0)
        qry = lax.broadcasted_iota(jnp.int32, (BLOCK, nq), 1) & (BLOCK - 1)
        tri_prev, tri_next = key >= qry, key <= qry

    probs = [(sub, kvh) for sub in range(n_sub) for kvh in range(2)]
    s_l, sink_l, vt_l = [], [], []
    for sub, kvh in probs:
        ksl = slice(kvh * LANES, (kvh + 1) * LANES)
        rows = slice(sub * BLOCK, (sub + 1) * BLOCK)
        parts = []
        for g in range(4):
            grp = kvh * 2 + g // 2
            qg = q_ref[rows, grp * LANES:(grp + 1) * LANES]
            parts.append(jnp.where((lane < 64) if g % 2 == 0 else (lane >= 64), qg, jnp.zeros_like(qg)))
        qs = jnp.concatenate(parts, axis=0)
        sink = jnp.zeros((1, nq), F32)
        for g in range(4):
            sink = jnp.where(headi == g, sink_ref[kvh * 4 + g] * LOG2E, sink)
        if local:
            kb, vb = [], []
            for j in (sub - 1, sub, sub + 1):
                if j < 0:
                    kb.append(kp_ref[:, ksl]), vb.append(vpt_ref[ksl, :])
                elif j >= n_sub:
                    kb.append(kn_ref[:, ksl]), vb.append(vnt_ref[ksl, :])
                else:
                    kb.append(kc_ref[j * BLOCK:(j + 1) * BLOCK, ksl])
                    vb.append(vct_ref[ksl, j * BLOCK:(j + 1) * BLOCK])
            k_all = jnp.concatenate(kb + [kx_ref[:, ksl]], axis=0)
            vt_all = jnp.concatenate(vb + [vxt_ref[ksl, :]], axis=1)
        else:
            k_all, vt_all = kx_ref[:, ksl], vxt_ref[ksl, :]
        s = _dot_nt(k_all, qs)
        if local:
            valid_prev = tri_prev if sub > 0 else tri_prev & (n > 0)
            valid_next = tri_next if sub < n_sub - 1 else tri_next & (n < nb - 1)
            s = jnp.concatenate([jnp.where(valid_prev, s[0:BLOCK], NEG_INF), s[BLOCK:2 * BLOCK],
                                 jnp.where(valid_next, s[2 * BLOCK:3 * BLOCK], NEG_INF), s[3 * BLOCK:]], axis=0)
        s_l.append(s)
        sink_l.append(sink)
        vt_l.append(vt_all)
    m_l = [jnp.maximum(jnp.max(s, axis=0, keepdims=True), sink) for s, sink in zip(s_l, sink_l)]
    p_l = [jnp.exp2(s - m) for s, m in zip(s_l, m_l)]
    l_l = [jnp.sum(p, axis=0, keepdims=True) + jnp.exp2(sink - m) for p, sink, m in zip(p_l, sink_l, m_l)]
    ot_l = [_dot(vt, p.astype(BF16)) / l for vt, p, l in zip(vt_l, p_l, l_l)]
    for (sub, kvh), o_t in zip(probs, ot_l):
        o = o_t.T
        for j in range(2):
            grp = kvh * 2 + j
            o_ref[sub * BLOCK:(sub + 1) * BLOCK, grp * LANES:(grp + 1) * LANES] = jnp.where(
                lane < 64, o[2 * j * BLOCK:(2 * j + 1) * BLOCK], o[(2 * j + 1) * BLOCK:(2 * j + 2) * BLOCK]
            ).astype(BF16)


def _win_call(sinks, q, kd, vdt, kxd, vxdt, *, local):
    bsz, lq, _ = q.shape
    nb = lq // BLOCK
    n_sub = min(WIN_SUB, nb)
    tq = n_sub * BLOCK
    in_specs = [pl.BlockSpec(memory_space=pltpu.SMEM), pl.BlockSpec((None, tq, MIX_W), lambda b, n: (b, n, 0))]
    args = [sinks, q]
    if local:
        prv = lambda n: jnp.maximum(n * n_sub - 1, 0)
        nxt = lambda n: jnp.minimum((n + 1) * n_sub, nb - 1)
        in_specs += [pl.BlockSpec((None, BLOCK, 2 * LANES), lambda b, n: (b, prv(n), 0)),
                     pl.BlockSpec((None, tq, 2 * LANES), lambda b, n: (b, n, 0)),
                     pl.BlockSpec((None, BLOCK, 2 * LANES), lambda b, n: (b, nxt(n), 0)),
                     pl.BlockSpec((None, 2 * LANES, BLOCK), lambda b, n: (b, 0, prv(n))),
                     pl.BlockSpec((None, 2 * LANES, tq), lambda b, n: (b, 0, n)),
                     pl.BlockSpec((None, 2 * LANES, BLOCK), lambda b, n: (b, 0, nxt(n)))]
        args += [kd] * 3 + [vdt] * 3
    n_ctx = kxd.shape[1]
    in_specs += [pl.BlockSpec((None, n_ctx, 2 * LANES), lambda b, n: (b, 0, 0)),
                 pl.BlockSpec((None, 2 * LANES, n_ctx), lambda b, n: (b, 0, 0))]
    args += [kxd, vxdt]
    return pl.pallas_call(
        functools.partial(_win_kernel, local=local),
        grid=(bsz, lq // tq),
        in_specs=in_specs,
        out_specs=pl.BlockSpec((None, tq, MIX_W), lambda b, n: (b, n, 0)),
        out_shape=jax.ShapeDtypeStruct((bsz, lq, MIX_W), BF16),
        compiler_params=_cparams(2),
        name="win",
    )(*args)


def _merge_kernel(x_ref, mul_ref, sh_ref, gate_ref, wuv_ref, wg_ref, lng_ref, lnb_ref, ws_ref, bs_ref,
                  ob_ref, oc_ref, wbr_ref, wout_ref, o_ref):
    x = x_ref[...]
    tm = x.shape[0]
    h = _rms_mod(x, mul_ref[...], sh_ref[...]).astype(BF16)
    u = _gelu_tanh(_dot(h, wuv_ref[0, :, 0:MIX_W]))
    v = _gelu_tanh(_dot(h, wuv_ref[0, :, MIX_W:2 * MIX_W]))
    mu = jnp.mean(v, axis=-1, keepdims=True)
    vc = v - mu
    var = jnp.mean(vc * vc, axis=-1, keepdims=True)
    vn = (vc * lax.rsqrt(var + EPS) * lng_ref[...] + lnb_ref[...]).astype(BF16)
    n_chunks = tm // BLOCK
    mixed = []
    for g in range(A_GROUPS):
        gs = slice(g * LANES, (g + 1) * LANES)
        rhs = jnp.concatenate([vn[c * BLOCK:(c + 1) * BLOCK, gs] for c in range(n_chunks)], axis=1)
        mixed.append(_dot(ws_ref[g], rhs))
    o_a = (u * jnp.concatenate(
        [jnp.concatenate([mixed[g][:, c * LANES:(c + 1) * LANES] + bs_ref[g] for g in range(A_GROUPS)], axis=1)
         for c in range(n_chunks)], axis=0)).astype(BF16)
    y = None
    for i, o_i in enumerate((o_a, ob_ref[...], oc_ref[...])):
        gate = jax.nn.sigmoid(_dot(h, wg_ref[0, :, i * D_MODEL:(i + 1) * D_MODEL]))
        t = gate * _dot(o_i, wbr_ref[i])
        y = t if y is None else y + t
    o_ref[...] = x + gate_ref[...] * _dot(y.astype(BF16), wout_ref[...])


def _merge_call(x, mul, sh, gate, w_in, ln_g, ln_b, w_s, b_s, o_b, o_c, w_br, w_out, *, layer, tm):
    bsz, length, _ = x.shape
    tok = lambda w: pl.BlockSpec((None, tm, w), lambda b, i: (b, i, 0))
    vec = pl.BlockSpec((None, 1, D_MODEL), lambda b, i: (b, 0, 0))
    full = functools.partial(_layer_spec, layer=layer)
    return pl.pallas_call(
        _merge_kernel,
        grid=(bsz, length // tm),
        in_specs=[tok(D_MODEL), vec, vec, vec, _layer_cols_spec(w_in, layer, 0, 2 * MIX_W),
                  _layer_cols_spec(w_in, layer, GATE_COL0, 3 * D_MODEL), full(ln_g), full(ln_b), full(w_s), full(b_s),
                  tok(MIX_W), tok(MIX_W), full(w_br), full(w_out)],
        out_specs=tok(D_MODEL),
        out_shape=jax.ShapeDtypeStruct(x.shape, F32),
        compiler_params=_cparams(2),
        name="merge",
    )(x, mul, sh, gate, w_in, w_in, ln_g, ln_b, w_s, b_s, o_b, o_c, w_br, w_out)


HALO = 8
MXU_TILE = 256
FF_EDGES = (0, 6 * MXU_TILE, D_FF)


def _ffn_kernel(x_ref, xp_ref, xn_ref, mul_ref, sh_ref, gate_ref, wg_ref, cw_ref, cb_ref, wu_ref, wd_ref,
                fg_ref, o_ref, *, final):
    i = pl.program_id(1)
    nt = pl.num_programs(1)
    x = x_ref[...]
    tm = x.shape[0]
    xe = jnp.concatenate([xp_ref[...], x, xn_ref[...]], axis=0)
    he = _rms_mod(xe, mul_ref[...], sh_ref[...]).astype(BF16)
    hm = he[HALO:HALO + tm]
    rowe = lax.broadcasted_iota(jnp.int32, (tm + 2 * HALO, 1), 0)
    keep = ((rowe >= HALO) | (i > 0)) & ((rowe < HALO + tm) | (i < nt - 1))
    spans = [slice(c0, c1) for c0, c1 in zip(FF_EDGES[:-1], FF_EDGES[1:])]
    a_l = [jnp.where(keep, _dot(he, wg_ref[:, cs]), 0.0) for cs in spans]
    u_l = [_dot(hm, wu_ref[:, cs]) for cs in spans]
    z_l = []
    for cs, a, u in zip(spans, a_l, u_l):
        a_prev = pltpu.roll(a, 1, 0)[HALO:HALO + tm]
        a_next = pltpu.roll(a, tm + 2 * HALO - 1, 0)[HALO:HALO + tm]
        a = (a_prev * cw_ref[0:1, cs] + a[HALO:HALO + tm] * cw_ref[1:2, cs] + a_next * cw_ref[2:3, cs]
             + cb_ref[:, cs])
        z_l.append((a * jax.nn.sigmoid(a) * u).astype(BF16))
    acc = None
    for cs, z in zip(spans, z_l):
        t = _dot(z, wd_ref[cs, :])
        acc = t if acc is None else acc + t
    out = x + gate_ref[...] * acc
    if final:
        ms = jnp.mean(out * out, axis=-1, keepdims=True)
        out = out * lax.rsqrt(ms + EPS) * fg_ref[...]
    o_ref[...] = out


def _ffn_call(x, mul, sh, gate, w_gate, conv_w, conv_b, w_up, w_down, final_g, *, layer, tm, final):
    bsz, length, _ = x.shape
    per = tm // HALO
    last = length // HALO - 1
    tok = pl.BlockSpec((None, tm, D_MODEL), lambda b, i: (b, i, 0))
    prv = pl.BlockSpec((None, HALO, D_MODEL), lambda b, i: (b, jnp.maximum(i * per - 1, 0), 0))
    nxt = pl.BlockSpec((None, HALO, D_MODEL), lambda b, i: (b, jnp.minimum((i + 1) * per, last), 0))
    vec = pl.BlockSpec((None, 1, D_MODEL), lambda b, i: (b, 0, 0))
    full = functools.partial(_layer_spec, layer=layer)
    return pl.pallas_call(
        functools.partial(_ffn_kernel, final=final),
        grid=(bsz, length // tm),
        in_specs=[tok, prv, nxt, vec, vec, vec, full(w_gate), full(conv_w), full(conv_b), full(w_up),
                  full(w_down), _const_spec(final_g)],
        out_specs=tok,
        out_shape=jax.ShapeDtypeStruct(x.shape, F32),
        compiler_params=_cparams(2),
        name="ffn",
    )(x, x, x, mul, sh, gate, w_gate, conv_w, conv_b, w_up, w_down, final_g)


def _rope_tables(length):
    pos = jnp.arange(length)
    rows = (pos // GRID_W).astype(F32)
    cols = (pos % GRID_W).astype(F32)
    half = HEAD_DIM // 2
    inv = ROPE_BASE ** (-jnp.arange(0, half, 2, dtype=F32) / half)
    ang_r = rows[:, None] * inv[None, :]
    ang_c = cols[:, None] * inv[None, :]
    zero = jnp.zeros_like(ang_r)
    cos = jnp.concatenate([jnp.cos(ang_r)] * 2 + [jnp.cos(ang_c)] * 2, axis=1)
    sa = jnp.concatenate([-jnp.sin(ang_r), zero, -jnp.sin(ang_c), zero], axis=1)
    sb = jnp.concatenate([zero, jnp.sin(ang_r), zero, jnp.sin(ang_c)], axis=1)
    return tuple(jnp.tile(t, (1, LANES // HEAD_DIM)) for t in (cos, sa, sb))


def kernel(x, c, ctx, c_ctx, w_ada, b_ada, norm1_g, w_in, sgu_ln_g, sgu_ln_b, w_s, b_s, lam_q1, lam_k1,
           lam_q2, lam_k2, diff_subln_g, sinks, w_branch, w_out, norm2_g, w_gate, conv_w, conv_b, w_up,
           w_down, final_g):
    bsz, length, _ = x.shape
    n_ctx = ctx.shape[1]
    tables = _rope_tables(length)
    ctx_tables = tuple(t[:n_ctx] for t in tables)

    rows = -(-(bsz + 1) // 8) * 8
    vpad = jnp.zeros((rows, D_MODEL), F32).at[:bsz].set(c).at[bsz].set(c_ctx)
    ada = _ada_call(vpad, w_ada, b_ada)

    w_in_b = w_in.astype(BF16)
    w_s_b = w_s.astype(BF16)
    b_s_b = jnp.broadcast_to(b_s[..., None], b_s.shape + (LANES,))
    w_br_b = w_branch.astype(BF16)
    w_out_b = w_out.astype(BF16)
    w_gate_b = w_gate.astype(BF16)
    w_up_b = w_up.astype(BF16)
    w_down_b = w_down.astype(BF16)
    final_g2 = final_g.reshape(1, D_MODEL)
    mix_w = (w_in_b, sgu_ln_g.reshape(DEPTH, 1, MIX_W), sgu_ln_b.reshape(DEPTH, 1, MIX_W), w_s_b, b_s_b)
    ffn_w = (w_gate_b, conv_w, conv_b.reshape(DEPTH, 1, D_FF), w_up_b, w_down_b, final_g2)

    for l in range(DEPTH):
        last = l == DEPTH - 1
        m = ada[l]
        sh1, sc1, g1, sh2, sc2, g2 = [m[:, j * D_MODEL:(j + 1) * D_MODEL] for j in range(6)]
        mul1 = norm1_g[l][None, :] * (1.0 + sc1)
        mul2 = norm2_g[l][None, :] * (1.0 + sc2)
        lat = lambda a: a[:bsz, None, :]
        cx = lambda a: jnp.broadcast_to(a[bsz][None, None, :], (bsz, 1, D_MODEL))

        lam_init = 0.8 - 0.6 * float(np.exp(-0.3 * l))
        lam = jnp.exp(jnp.sum(lam_q1[l] * lam_k1[l])) - jnp.exp(jnp.sum(lam_q2[l] * lam_k2[l])) + lam_init
        scal = jnp.stack([lam, jnp.float32(1.0 - lam_init)]).astype(F32)
        subln = diff_subln_g[l].reshape(1, LANES)

        bq, bk, bvt, cq, ckd, cvdt = _inproj_call(x, lat(mul1), lat(sh1), w_in_b, tables, layer=l, tm=TOKEN_TILE,
                                                  rope=True)
        xbq, xbk, xbvt, xcq, xckd, xcvdt = _inproj_call(ctx, cx(mul1), cx(sh1), w_in_b, ctx_tables, layer=l,
                                                        tm=n_ctx, rope=False)

        o_b = _diff_call(scal, bq, subln, [(bk, bvt), (xbk, xbvt)], tq=DIFF_Q_TILE, kc=DIFF_KEY_CHUNK)
        o_c = _win_call(sinks[l], cq, ckd, cvdt, xckd, xcvdt, local=True)
        x_mid = _merge_call(x, lat(mul1), lat(sh1), lat(g1), *mix_w, o_b, o_c, w_br_b, w_out_b, layer=l,
                            tm=TOKEN_TILE)
        if not last:
            xo_b = _diff_call(scal, xbq, subln, [(xbk, xbvt)], tq=n_ctx, kc=n_ctx)
            xo_c = _win_call(sinks[l], xcq, None, None, xckd, xcvdt, local=False)
            ctx_mid = _merge_call(ctx, cx(mul1), cx(sh1), cx(g1), *mix_w, xo_b, xo_c, w_br_b, w_out_b, layer=l,
                                  tm=n_ctx)
        x = _ffn_call(x_mid, lat(mul2), lat(sh2), lat(g2), *ffn_w, layer=l, tm=TOKEN_TILE, final=last)
        if not last:
            ctx = _ffn_call(ctx_mid, cx(mul2), cx(sh2), cx(g2), *ffn_w, layer=l, tm=n_ctx, final=False)
    return x
```

```python
import functools

import numpy as np
import jax
import jax.numpy as jnp
from jax import lax
from jax.experimental import pallas as pl
from jax.experimental.pallas import tpu as pltpu

F32 = jnp.float32
BF16 = jnp.bfloat16

D_MODEL = 1024
DEPTH = 4
GRID_W = 64
HEAD_DIM = 64
BLOCK = 128
MIX_W = 512
A_GROUPS = 4
B_HEADS = 4
C_Q_HEADS = 8
D_FF = 2816
ROPE_BASE = 10000.0
EPS = 1e-6
SCALE = HEAD_DIM ** -0.5
NEG_INF = -1e30
LOG2E = float(np.log2(np.e))

LANES = 128
ATTN_COLS = 2304
GATE_COL0 = 3328
VMEM_LIMIT = 56 * 1024 * 1024

TOKEN_TILE = 1024
DIFF_Q_TILE = 512
DIFF_KEY_CHUNK = 256


def _cparams(n_axes):
    return pltpu.CompilerParams(
        dimension_semantics=("arbitrary",) * n_axes, vmem_limit_bytes=VMEM_LIMIT)


def _const_spec(a):
    return pl.BlockSpec(a.shape, lambda *_: (0,) * a.ndim, pipeline_mode=pl.Buffered(1))


def _layer_spec(a, layer):
    return pl.BlockSpec((None,) + a.shape[1:], lambda *_: (layer,) + (0,) * (a.ndim - 1),
                        pipeline_mode=pl.Buffered(1))


def _layer_cols_spec(a, layer, col0, width):
    return pl.BlockSpec((pl.Element(1), pl.Element(a.shape[1]), pl.Element(width)), lambda *_: (layer, 0, col0),
                        pipeline_mode=pl.Buffered(1))


def _rms_mod(x, mul, sh):
    ms = jnp.mean(x * x, axis=-1, keepdims=True)
    return x * lax.rsqrt(ms + EPS) * mul + sh


def _gelu_tanh(x):
    c = np.float32(np.sqrt(2.0 / np.pi))
    return 0.5 * x * (1.0 + jnp.tanh(c * (x + 0.044715 * (x * x * x))))


def _dot(a, b):
    return jnp.dot(a, b, preferred_element_type=F32)


def _dot_nt(a, b):
    return lax.dot_general(a, b, (((1,), (1,)), ((), ())), preferred_element_type=F32)


def _ada_kernel(v_ref, w_ref, b_ref, o_ref):
    v = v_ref[...]
    s = v * jax.nn.sigmoid(v)
    o_ref[...] = _dot(s.astype(BF16), w_ref[...].astype(BF16)) + b_ref[...]


def _ada_call(vpad, w_ada, b_ada):
    rows = vpad.shape[0]
    tn = 1536
    return pl.pallas_call(
        _ada_kernel,
        grid=(DEPTH, 6 * D_MODEL // tn),
        in_specs=[
            pl.BlockSpec((rows, D_MODEL), lambda l, j: (0, 0)),
            pl.BlockSpec((None, D_MODEL, tn), lambda l, j: (l, 0, j)),
            pl.BlockSpec((None, 1, tn), lambda l, j: (l, 0, j)),
        ],
        out_specs=pl.BlockSpec((None, rows, tn), lambda l, j: (l, 0, j)),
        out_shape=jax.ShapeDtypeStruct((DEPTH, rows, 6 * D_MODEL), F32),
        compiler_params=_cparams(2),
        name="ada",
    )(vpad, w_ada, b_ada.reshape(DEPTH, 1, 6 * D_MODEL))


def _inproj_kernel(x_ref, mul_ref, sh_ref, w_ref, cos_ref, sa_ref, sb_ref,
                   bq_ref, bk_ref, bvt_ref, cq_ref, ckd_ref, cvdt_ref, *, rope):
    h = _rms_mod(x_ref[...], mul_ref[...], sh_ref[...]).astype(BF16)
    if rope:
        cos, sa, sb = cos_ref[...], sa_ref[...], sb_ref[...]

    def rot(y):
        if not rope:
            return y
        return y * cos + pltpu.roll(y, LANES - 16, 1) * sa + pltpu.roll(y, 16, 1) * sb

    def dup(y):
        lane = lax.broadcasted_iota(jnp.int32, y.shape, 1)
        sw = pltpu.roll(y, 64, 1)
        return jnp.concatenate([jnp.where(lane < 64, y, sw), jnp.where(lane < 64, sw, y)], axis=1)

    for col0, ref, scale in ((0, bq_ref, SCALE * LOG2E), (512, bk_ref, None), (1536, cq_ref, SCALE * LOG2E)):
        y = _dot(h, w_ref[0, :, col0:col0 + 512])
        if scale is not None:
            y = y * scale
        for g in range(4):
            ref[:, g * LANES:(g + 1) * LANES] = rot(y[:, g * LANES:(g + 1) * LANES]).astype(BF16)
    ykv = _dot(h, w_ref[0, :, 2048:2304])
    ckd_ref[...] = dup(rot(ykv[:, :LANES])).astype(BF16)
    cvdt_ref[...] = dup(ykv[:, LANES:]).T.astype(BF16)
    bvt_ref[...] = _dot(h, w_ref[0, :, 1024:1536]).T.astype(BF16)


def _inproj_call(x, mul, sh, w_in, tables, *, layer, tm, rope):
    bsz, length, _ = x.shape
    tok = lambda w: pl.BlockSpec((None, tm, w), lambda b, i: (b, i, 0))
    vec = pl.BlockSpec((None, 1, D_MODEL), lambda b, i: (b, 0, 0))
    tab = pl.BlockSpec((tm, LANES), lambda b, i: (i, 0))
    tok_t = lambda w: pl.BlockSpec((None, w, tm), lambda b, i: (b, 0, i))
    out = lambda w: jax.ShapeDtypeStruct((bsz, length, w), BF16)
    out_t = lambda w: jax.ShapeDtypeStruct((bsz, w, length), BF16)
    return pl.pallas_call(
        functools.partial(_inproj_kernel, rope=rope),
        grid=(bsz, length // tm),
        in_specs=[tok(D_MODEL), vec, vec,
                  _layer_cols_spec(w_in, layer, 2 * MIX_W, ATTN_COLS), tab, tab, tab],
        out_specs=[tok(512), tok(512), tok_t(MIX_W), tok(512), tok(256), tok_t(2 * LANES)],
        out_shape=[out(512), out(512), out_t(MIX_W), out(512), out(256), out_t(2 * LANES)],
        compiler_params=_cparams(2),
        name="inproj",
    )(x, mul, sh, w_in, *tables)


SAFE_EXP2_RANGE = 90.0


def _diff_kernel(sc_ref, q_ref, qall_ref, g_ref, ind_ref, *refs, seg_lens, kc, nq):
    o_ref, p_scr, f_scr, safe_scr = refs[-4:]
    t = pl.program_id(0)
    tq = q_ref.shape[0]
    lanei = lax.broadcasted_iota(jnp.int32, (1, 2 * tq), 1)

    @pl.when(t == 0)
    def _():
        p_scr[...] = jnp.zeros(p_scr.shape, BF16)
        f_scr[...] = jnp.zeros(f_scr.shape, F32)

    chunks = []
    row0 = 0
    for si, n_keys in enumerate(seg_lens):
        for st in range(0, n_keys, kc):
            size = min(kc, n_keys - st)
            chunks.append((refs[2 * si], refs[2 * si + 1], st, size, row0))
            row0 += size

    def max_sq_norm(x):
        xf = x.astype(F32)
        return jnp.max(_dot((xf * xf).astype(BF16), ind_ref[...]), axis=0, keepdims=True)

    @pl.when(t % nq == 0)
    def _():
        kn = None
        for k_ref, _, st, size, _ in chunks:
            n2 = max_sq_norm(k_ref[st:st + size, :])
            kn = n2 if kn is None else jnp.maximum(kn, n2)
        qn = None
        n_q = qall_ref.shape[0]
        for st in range(0, n_q, 2 * kc):
            n2 = max_sq_norm(qall_ref[st:min(st + 2 * kc, n_q), :])
            qn = n2 if qn is None else jnp.maximum(qn, n2)
        bound2 = jnp.max(kn * qn)
        safe_scr[0] = (bound2 < SAFE_EXP2_RANGE * SAFE_EXP2_RANGE).astype(jnp.int32)

    safe = safe_scr[0] == 1

    def step(max_free):
        q = q_ref[...]
        lane = lax.broadcasted_iota(jnp.int32, q.shape, 1)
        zero = jnp.zeros_like(q)
        qq = jnp.concatenate([jnp.where(lane < 64, q, zero), jnp.where(lane >= 64, q, zero)], axis=0)
        f_prev = f_scr[...]
        mcs, lcs = [], []
        acc = None
        for c, (k_ref, vt_ref, st, size, r0) in enumerate(chunks):
            f1 = f_prev[c:c + 1, 0:tq].astype(BF16)
            f2 = f_prev[c:c + 1, tq:2 * tq].astype(BF16)
            a_t = p_scr[r0:r0 + size, 0:tq] * f1 - p_scr[r0:r0 + size, tq:2 * tq] * f2
            part = _dot(vt_ref[:, st:st + size], a_t)
            acc = part if acc is None else acc + part
            s = _dot_nt(k_ref[st:st + size, :], qq)
            if max_free:
                p = jnp.exp2(s)
            else:
                mc = jnp.max(s, axis=0, keepdims=True)
                p = jnp.exp2(s - mc)
                mcs.append(mc)
            lcs.append(jnp.sum(p, axis=0, keepdims=True))
            p_scr[r0:r0 + size, :] = p.astype(BF16)

        if max_free:
            es = [None] * len(chunks)
            l_all = functools.reduce(lambda a, b: a + b, lcs)
        else:
            m_all = functools.reduce(jnp.maximum, mcs)
            es = [jnp.exp2(mc - m_all) for mc in mcs]
            l_all = functools.reduce(lambda a, b: a + b, [lc * e for lc, e in zip(lcs, es)])
        coef = jnp.where(lanei < tq, 1.0, sc_ref[0]) / l_all
        for c, e in enumerate(es):
            f_scr[c:c + 1, :] = coef if e is None else e * coef

        o = acc.T
        ms = jnp.mean(o * o, axis=-1, keepdims=True)
        o_ref[...] = (o * lax.rsqrt(ms + EPS) * g_ref[...] * sc_ref[1]).astype(BF16)

    pl.when(safe)(lambda: step(True))
    pl.when(jnp.logical_not(safe))(lambda: step(False))


def _diff_call(scal, q, subln_g, segs, *, tq, kc):
    bsz, lq, _ = q.shape
    nq = lq // tq
    n_blocks = bsz * B_HEADS * nq
    half = np.arange(LANES) // HEAD_DIM
    ind = jnp.asarray(half[:, None] == half[None, :], BF16)

    def block(t):
        b, r = t // (B_HEADS * nq), t % (B_HEADS * nq)
        return b, r // nq, r % nq

    cur = lambda t: block(jnp.minimum(t, n_blocks - 1))
    prev = lambda t: block(jnp.maximum(t - 1, 0))

    def q_map(t):
        b, h, i = cur(t)
        return b, i, h

    def kq_map(t):
        b, h, _ = cur(t)
        return b, 0, h

    def vt_map(t):
        b, h, _ = prev(t)
        return b, h, 0

    def o_map(t):
        b, h, i = prev(t)
        return b, i, h

    in_specs = [pl.BlockSpec(memory_space=pltpu.SMEM),
                pl.BlockSpec((None, tq, LANES), q_map),
                pl.BlockSpec((None, lq, LANES), kq_map),
                pl.BlockSpec((1, LANES), lambda t: (0, 0)),
                _const_spec(ind)]
    args = [scal, q, q, subln_g, ind]
    for k, vt in segs:
        n_keys = k.shape[1]
        in_specs += [pl.BlockSpec((None, n_keys, LANES), kq_map),
                     pl.BlockSpec((None, LANES, n_keys), vt_map)]
        args += [k, vt]
    seg_lens = tuple(k.shape[1] for k, _ in segs)
    n_chunks = sum(-(-n // kc) for n in seg_lens)
    return pl.pallas_call(
        functools.partial(_diff_kernel, seg_lens=seg_lens, kc=kc, nq=nq),
        grid=(n_blocks + 1,),
        in_specs=in_specs,
        out_specs=pl.BlockSpec((None, tq, LANES), o_map),
        out_shape=jax.ShapeDtypeStruct((bsz, lq, MIX_W), BF16),
        scratch_shapes=[pltpu.VMEM((sum(seg_lens), 2 * tq), BF16),
                        pltpu.VMEM((-(-n_chunks // 8) * 8, 2 * tq), F32),
                        pltpu.SMEM((1,), jnp.int32)],
        compiler_params=_cparams(1),
        name="diff",
    )(*args)


WIN_SUB = 8


def _win_kernel(sink_ref, q_ref, *refs, local):
    o_ref = refs[-1]
    n_sub = q_ref.shape[0] // BLOCK
    kx_ref, vxt_ref = refs[-3], refs[-2]
    n = pl.program_id(1)
    nb = pl.num_programs(1)
    nq = C_Q_HEADS // 2 * BLOCK
    lane = lax.broadcasted_iota(jnp.int32, (BLOCK, LANES), 1)
    headi = lax.broadcasted_iota(jnp.int32, (1, nq), 1) // BLOCK
    if local:
        kp_ref, kc_ref, kn_ref, vpt_ref, vct_ref, vnt_ref = refs[:6]
        key = lax.broadcasted_iota(jnp.int32, (BLOCK, nq), 0)
        qry = lax.broadcasted_iota(jnp.int32, (BLOCK, nq), 1) & (BLOCK - 1)
        tri_prev, tri_next = key >= qry, key <= qry

    probs = [(sub, kvh) for sub in range(n_sub) for kvh in range(2)]
    s_l, sink_l, vt_l = [], [], []
    for sub, kvh in probs:
        ksl = slice(kvh * LANES, (kvh + 1) * LANES)
        rows = slice(sub * BLOCK, (sub + 1) * BLOCK)
        parts = []
        for g in range(4):
            grp = kvh * 2 + g // 2
            qg = q_ref[rows, grp * LANES:(grp + 1) * LANES]
            parts.append(jnp.where((lane < 64) if g % 2 == 0 else (lane >= 64), qg, jnp.zeros_like(qg)))
        qs = jnp.concatenate(parts, axis=0)
        sink = jnp.zeros((1, nq), F32)
        for g in range(4):
            sink = jnp.where(headi == g, sink_ref[kvh * 4 + g] * LOG2E, sink)
        if local:
            kb, vb = [], []
            for j in (sub - 1, sub, sub + 1):
                if j < 0:
                    kb.append(kp_ref[:, ksl]), vb.append(vpt_ref[ksl, :])
                elif j >= n_sub:
                    kb.append(kn_ref[:, ksl]), vb.append(vnt_ref[ksl, :])
                else:
                    kb.append(kc_ref[j * BLOCK:(j + 1) * BLOCK, ksl])
                    vb.append(vct_ref[ksl, j * BLOCK:(j + 1) * BLOCK])
            k_all = jnp.concatenate(kb + [kx_ref[:, ksl]], axis=0)
            vt_all = jnp.concatenate(vb + [vxt_ref[ksl, :]], axis=1)
        else:
            k_all, vt_all = kx_ref[:, ksl], vxt_ref[ksl, :]
        s = _dot_nt(k_all, qs)
        if local:
            valid_prev = tri_prev if sub > 0 else tri_prev & (n > 0)
            valid_next = tri_next if sub < n_sub - 1 else tri_next & (n < nb - 1)
            s = jnp.concatenate([jnp.where(valid_prev, s[0:BLOCK], NEG_INF), s[BLOCK:2 * BLOCK],
                                 jnp.where(valid_next, s[2 * BLOCK:3 * BLOCK], NEG_INF), s[3 * BLOCK:]], axis=0)
        s_l.append(s)
        sink_l.append(sink)
        vt_l.append(vt_all)
    m_l = [jnp.maximum(jnp.max(s, axis=0, keepdims=True), sink) for s, sink in zip(s_l, sink_l)]
    p_l = [jnp.exp2(s - m) for s, m in zip(s_l, m_l)]
    l_l = [jnp.sum(p, axis=0, keepdims=True) + jnp.exp2(sink - m) for p, sink, m in zip(p_l, sink_l, m_l)]
    ot_l = [_dot(vt, p.astype(BF16)) / l for vt, p, l in zip(vt_l, p_l, l_l)]
    for (sub, kvh), o_t in zip(probs, ot_l):
        o = o_t.T
        for j in range(2):
            grp = kvh * 2 + j
            o_ref[sub * BLOCK:(sub + 1) * BLOCK, grp * LANES:(grp + 1) * LANES] = jnp.where(
                lane < 64, o[2 * j * BLOCK:(2 * j + 1) * BLOCK], o[(2 * j + 1) * BLOCK:(2 * j + 2) * BLOCK]
            ).astype(BF16)


def _win_call(sinks, q, kd, vdt, kxd, vxdt, *, local):
    bsz, lq, _ = q.shape
    nb = lq // BLOCK
    n_sub = min(WIN_SUB, nb)
    tq = n_sub * BLOCK
    in_specs = [pl.BlockSpec(memory_space=pltpu.SMEM), pl.BlockSpec((None, tq, MIX_W), lambda b, n: (b, n, 0))]
    args = [sinks, q]
    if local:
        prv = lambda n: jnp.maximum(n * n_sub - 1, 0)
        nxt = lambda n: jnp.minimum((n + 1) * n_sub, nb - 1)
        in_specs += [pl.BlockSpec((None, BLOCK, 2 * LANES), lambda b, n: (b, prv(n), 0)),
                     pl.BlockSpec((None, tq, 2 * LANES), lambda b, n: (b, n, 0)),
                     pl.BlockSpec((None, BLOCK, 2 * LANES), lambda b, n: (b, nxt(n), 0)),
                     pl.BlockSpec((None, 2 * LANES, BLOCK), lambda b, n: (b, 0, prv(n))),
                     pl.BlockSpec((None, 2 * LANES, tq), lambda b, n: (b, 0, n)),
                     pl.BlockSpec((None, 2 * LANES, BLOCK), lambda b, n: (b, 0, nxt(n)))]
        args += [kd] * 3 + [vdt] * 3
    n_ctx = kxd.shape[1]
    in_specs += [pl.BlockSpec((None, n_ctx, 2 * LANES), lambda b, n: (b, 0, 0)),
                 pl.BlockSpec((None, 2 * LANES, n_ctx), lambda b, n: (b, 0, 0))]
    args += [kxd, vxdt]
    return pl.pallas_call(
        functools.partial(_win_kernel, local=local),
        grid=(bsz, lq // tq),
        in_specs=in_specs,
        out_specs=pl.BlockSpec((None, tq, MIX_W), lambda b, n: (b, n, 0)),
        out_shape=jax.ShapeDtypeStruct((bsz, lq, MIX_W), BF16),
        compiler_params=_cparams(2),
        name="win",
    )(*args)


def _merge_kernel(x_ref, mul_ref, sh_ref, gate_ref, wuv_ref, wg_ref, lng_ref, lnb_ref, ws_ref, bs_ref,
                  ob_ref, oc_ref, wbr_ref, wout_ref, o_ref):
    x = x_ref[...]
    tm = x.shape[0]
    h = _rms_mod(x, mul_ref[...], sh_ref[...]).astype(BF16)
    u = _gelu_tanh(_dot(h, wuv_ref[0, :, 0:MIX_W]))
    v = _gelu_tanh(_dot(h, wuv_ref[0, :, MIX_W:2 * MIX_W]))
    mu = jnp.mean(v, axis=-1, keepdims=True)
    vc = v - mu
    var = jnp.mean(vc * vc, axis=-1, keepdims=True)
    vn = (vc * lax.rsqrt(var + EPS) * lng_ref[...] + lnb_ref[...]).astype(BF16)
    n_chunks = tm // BLOCK
    mixed = []
    for g in range(A_GROUPS):
        gs = slice(g * LANES, (g + 1) * LANES)
        rhs = jnp.concatenate([vn[c * BLOCK:(c + 1) * BLOCK, gs] for c in range(n_chunks)], axis=1)
        mixed.append(_dot(ws_ref[g], rhs))
    o_a = (u * jnp.concatenate(
        [jnp.concatenate([mixed[g][:, c * LANES:(c + 1) * LANES] + bs_ref[g] for g in range(A_GROUPS)], axis=1)
         for c in range(n_chunks)], axis=0)).astype(BF16)
    y = None
    for i, o_i in enumerate((o_a, ob_ref[...], oc_ref[...])):
        gate = jax.nn.sigmoid(_dot(h, wg_ref[0, :, i * D_MODEL:(i + 1) * D_MODEL]))
        t = gate * _dot(o_i, wbr_ref[i])
        y = t if y is None else y + t
    o_ref[...] = x + gate_ref[...] * _dot(y.astype(BF16), wout_ref[...])


def _merge_call(x, mul, sh, gate, w_in, ln_g, ln_b, w_s, b_s, o_b, o_c, w_br, w_out, *, layer, tm):
    bsz, length, _ = x.shape
    tok = lambda w: pl.BlockSpec((None, tm, w), lambda b, i: (b, i, 0))
    vec = pl.BlockSpec((None, 1, D_MODEL), lambda b, i: (b, 0, 0))
    full = functools.partial(_layer_spec, layer=layer)
    return pl.pallas_call(
        _merge_kernel,
        grid=(bsz, length // tm),
        in_specs=[tok(D_MODEL), vec, vec, vec, _layer_cols_spec(w_in, layer, 0, 2 * MIX_W),
                  _layer_cols_spec(w_in, layer, GATE_COL0, 3 * D_MODEL), full(ln_g), full(ln_b), full(w_s), full(b_s),
                  tok(MIX_W), tok(MIX_W), full(w_br), full(w_out)],
        out_specs=tok(D_MODEL),
        out_shape=jax.ShapeDtypeStruct(x.shape, F32),
        compiler_params=_cparams(2),
        name="merge",
    )(x, mul, sh, gate, w_in, w_in, ln_g, ln_b, w_s, b_s, o_b, o_c, w_br, w_out)


HALO = 8
MXU_TILE = 256
FF_EDGES = (0, D_FF)


def _ffn_kernel(x_ref, xp_ref, xn_ref, mul_ref, sh_ref, gate_ref, wg_ref, cw_ref, cb_ref, wu_ref, wd_ref,
                fg_ref, o_ref, *, final):
    i = pl.program_id(1)
    nt = pl.num_programs(1)
    x = x_ref[...]
    tm = x.shape[0]
    xe = jnp.concatenate([xp_ref[...], x, xn_ref[...]], axis=0)
    he = _rms_mod(xe, mul_ref[...], sh_ref[...]).astype(BF16)
    hm = he[HALO:HALO + tm]
    rowe = lax.broadcasted_iota(jnp.int32, (tm + 2 * HALO, 1), 0)
    keep = ((rowe >= HALO) | (i > 0)) & ((rowe < HALO + tm) | (i < nt - 1))
    spans = [slice(c0, c1) for c0, c1 in zip(FF_EDGES[:-1], FF_EDGES[1:])]
    a_l = [jnp.where(keep, _dot(he, wg_ref[:, cs]), 0.0) for cs in spans]
    u_l = [_dot(hm, wu_ref[:, cs]) for cs in spans]
    z_l = []
    for cs, a, u in zip(spans, a_l, u_l):
        a_prev = pltpu.roll(a, 1, 0)[HALO:HALO + tm]
        a_next = pltpu.roll(a, tm + 2 * HALO - 1, 0)[HALO:HALO + tm]
        a = (a_prev * cw_ref[0:1, cs] + a[HALO:HALO + tm] * cw_ref[1:2, cs] + a_next * cw_ref[2:3, cs]
             + cb_ref[:, cs])
        z_l.append((a * jax.nn.sigmoid(a) * u).astype(BF16))
    acc = None
    for cs, z in zip(spans, z_l):
        t = _dot(z, wd_ref[cs, :])
        acc = t if acc is None else acc + t
    out = x + gate_ref[...] * acc
    if final:
        ms = jnp.mean(out * out, axis=-1, keepdims=True)
        out = out * lax.rsqrt(ms + EPS) * fg_ref[...]
    o_ref[...] = out


def _ffn_call(x, mul, sh, gate, w_gate, conv_w, conv_b, w_up, w_down, final_g, *, layer, tm, final):
    bsz, length, _ = x.shape
    per = tm // HALO
    last = length // HALO - 1
    tok = pl.BlockSpec((None, tm, D_MODEL), lambda b, i: (b, i, 0))
    prv = pl.BlockSpec((None, HALO, D_MODEL), lambda b, i: (b, jnp.maximum(i * per - 1, 0), 0))
    nxt = pl.BlockSpec((None, HALO, D_MODEL), lambda b, i: (b, jnp.minimum((i + 1) * per, last), 0))
    vec = pl.BlockSpec((None, 1, D_MODEL), lambda b, i: (b, 0, 0))
    full = functools.partial(_layer_spec, layer=layer)
    return pl.pallas_call(
        functools.partial(_ffn_kernel, final=final),
        grid=(bsz, length // tm),
        in_specs=[tok, prv, nxt, vec, vec, vec, full(w_gate), full(conv_w), full(conv_b), full(w_up),
                  full(w_down), _const_spec(final_g)],
        out_specs=tok,
        out_shape=jax.ShapeDtypeStruct(x.shape, F32),
        compiler_params=_cparams(2),
        name="ffn",
    )(x, x, x, mul, sh, gate, w_gate, conv_w, conv_b, w_up, w_down, final_g)


def _rope_tables(length):
    pos = jnp.arange(length)
    rows = (pos // GRID_W).astype(F32)
    cols = (pos % GRID_W).astype(F32)
    half = HEAD_DIM // 2
    inv = ROPE_BASE ** (-jnp.arange(0, half, 2, dtype=F32) / half)
    ang_r = rows[:, None] * inv[None, :]
    ang_c = cols[:, None] * inv[None, :]
    zero = jnp.zeros_like(ang_r)
    cos = jnp.concatenate([jnp.cos(ang_r)] * 2 + [jnp.cos(ang_c)] * 2, axis=1)
    sa = jnp.concatenate([-jnp.sin(ang_r), zero, -jnp.sin(ang_c), zero], axis=1)
    sb = jnp.concatenate([zero, jnp.sin(ang_r), zero, jnp.sin(ang_c)], axis=1)
    return tuple(jnp.tile(t, (1, LANES // HEAD_DIM)) for t in (cos, sa, sb))


def kernel(x, c, ctx, c_ctx, w_ada, b_ada, norm1_g, w_in, sgu_ln_g, sgu_ln_b, w_s, b_s, lam_q1, lam_k1,
           lam_q2, lam_k2, diff_subln_g, sinks, w_branch, w_out, norm2_g, w_gate, conv_w, conv_b, w_up,
           w_down, final_g):
    bsz, length, _ = x.shape
    n_ctx = ctx.shape[1]
    tables = _rope_tables(length)
    ctx_tables = tuple(t[:n_ctx] for t in tables)

    rows = -(-(bsz + 1) // 8) * 8
    vpad = jnp.zeros((rows, D_MODEL), F32).at[:bsz].set(c).at[bsz].set(c_ctx)
    ada = _ada_call(vpad, w_ada, b_ada)

    w_in_b = w_in.astype(BF16)
    w_s_b = w_s.astype(BF16)
    b_s_b = jnp.broadcast_to(b_s[..., None], b_s.shape + (LANES,))
    w_br_b = w_branch.astype(BF16)
    w_out_b = w_out.astype(BF16)
    w_gate_b = w_gate.astype(BF16)
    w_up_b = w_up.astype(BF16)
    w_down_b = w_down.astype(BF16)
    final_g2 = final_g.reshape(1, D_MODEL)
    mix_w = (w_in_b, sgu_ln_g.reshape(DEPTH, 1, MIX_W), sgu_ln_b.reshape(DEPTH, 1, MIX_W), w_s_b, b_s_b)
    ffn_w = (w_gate_b, conv_w, conv_b.reshape(DEPTH, 1, D_FF), w_up_b, w_down_b, final_g2)

    for l in range(DEPTH):
        last = l == DEPTH - 1
        m = ada[l]
        sh1, sc1, g1, sh2, sc2, g2 = [m[:, j * D_MODEL:(j + 1) * D_MODEL] for j in range(6)]
        mul1 = norm1_g[l][None, :] * (1.0 + sc1)
        mul2 = norm2_g[l][None, :] * (1.0 + sc2)
        lat = lambda a: a[:bsz, None, :]
        cx = lambda a: jnp.broadcast_to(a[bsz][None, None, :], (bsz, 1, D_MODEL))

        lam_init = 0.8 - 0.6 * float(np.exp(-0.3 * l))
        lam = jnp.exp(jnp.sum(lam_q1[l] * lam_k1[l])) - jnp.exp(jnp.sum(lam_q2[l] * lam_k2[l])) + lam_init
        scal = jnp.stack([lam, jnp.float32(1.0 - lam_init)]).astype(F32)
        subln = diff_subln_g[l].reshape(1, LANES)

        bq, bk, bvt, cq, ckd, cvdt = _inproj_call(x, lat(mul1), lat(sh1), w_in_b, tables, layer=l, tm=TOKEN_TILE,
                                                  rope=True)
        xbq, xbk, xbvt, xcq, xckd, xcvdt = _inproj_call(ctx, cx(mul1), cx(sh1), w_in_b, ctx_tables, layer=l,
                                                        tm=n_ctx, rope=False)

        o_b = _diff_call(scal, bq, subln, [(bk, bvt), (xbk, xbvt)], tq=DIFF_Q_TILE, kc=DIFF_KEY_CHUNK)
        o_c = _win_call(sinks[l], cq, ckd, cvdt, xckd, xcvdt, local=True)
        x_mid = _merge_call(x, lat(mul1), lat(sh1), lat(g1), *mix_w, o_b, o_c, w_br_b, w_out_b, layer=l,
                            tm=TOKEN_TILE)
        if not last:
            xo_b = _diff_call(scal, xbq, subln, [(xbk, xbvt)], tq=n_ctx, kc=n_ctx)
            xo_c = _win_call(sinks[l], xcq, None, None, xckd, xcvdt, local=False)
            ctx_mid = _merge_call(ctx, cx(mul1), cx(sh1), cx(g1), *mix_w, xo_b, xo_c, w_br_b, w_out_b, layer=l,
                                  tm=n_ctx)
        x = _ffn_call(x_mid, lat(mul2), lat(sh2), lat(g2), *ffn_w, layer=l, tm=TOKEN_TILE, final=last)
        if not last:
            ctx = _ffn_call(ctx_mid, cx(mul2), cx(sh2), cx(g2), *ffn_w, layer=l, tm=n_ctx, final=False)
    return x
```

```python
import functools

import numpy as np
import jax
import jax.numpy as jnp
from jax import lax
from jax.experimental import pallas as pl
from jax.experimental.pallas import tpu as pltpu

F32 = jnp.float32
BF16 = jnp.bfloat16

D_MODEL = 1024
DEPTH = 4
GRID_W = 64
HEAD_DIM = 64
BLOCK = 128
MIX_W = 512
A_GROUPS = 4
B_HEADS = 4
C_Q_HEADS = 8
D_FF = 2816
ROPE_BASE = 10000.0
EPS = 1e-6
SCALE = HEAD_DIM ** -0.5
NEG_INF = -1e30
LOG2E = float(np.log2(np.e))

LANES = 128
ATTN_COLS = 2304
GATE_COL0 = 3328
VMEM_LIMIT = 56 * 1024 * 1024

TOKEN_TILE = 1024
DIFF_Q_TILE = 512
DIFF_KEY_CHUNK = 256


def _cparams(n_axes):
    return pltpu.CompilerParams(
        dimension_semantics=("arbitrary",) * n_axes, vmem_limit_bytes=VMEM_LIMIT)


def _const_spec(a):
    return pl.BlockSpec(a.shape, lambda *_: (0,) * a.ndim, pipeline_mode=pl.Buffered(1))


def _layer_spec(a, layer):
    return pl.BlockSpec((None,) + a.shape[1:], lambda *_: (layer,) + (0,) * (a.ndim - 1),
                        pipeline_mode=pl.Buffered(1))


def _layer_cols_spec(a, layer, col0, width):
    return pl.BlockSpec((pl.Element(1), pl.Element(a.shape[1]), pl.Element(width)), lambda *_: (layer, 0, col0),
                        pipeline_mode=pl.Buffered(1))


def _rms_mod(x, mul, sh):
    ms = jnp.mean(x * x, axis=-1, keepdims=True)
    return x * lax.rsqrt(ms + EPS) * mul + sh


def _gelu_tanh(x):
    c = np.float32(np.sqrt(2.0 / np.pi))
    return 0.5 * x * (1.0 + jnp.tanh(c * (x + 0.044715 * (x * x * x))))


def _dot(a, b):
    return jnp.dot(a, b, preferred_element_type=F32)


def _dot_nt(a, b):
    return lax.dot_general(a, b, (((1,), (1,)), ((), ())), preferred_element_type=F32)


def _ada_kernel(v_ref, w_ref, b_ref, o_ref):
    v = v_ref[...]
    s = v * jax.nn.sigmoid(v)
    o_ref[...] = _dot(s.astype(BF16), w_ref[...].astype(BF16)) + b_ref[...]


def _ada_call(vpad, w_ada, b_ada):
    rows = vpad.shape[0]
    tn = 1536
    return pl.pallas_call(
        _ada_kernel,
        grid=(DEPTH, 6 * D_MODEL // tn),
        in_specs=[
            pl.BlockSpec((rows, D_MODEL), lambda l, j: (0, 0)),
            pl.BlockSpec((None, D_MODEL, tn), lambda l, j: (l, 0, j)),
            pl.BlockSpec((None, 1, tn), lambda l, j: (l, 0, j)),
        ],
        out_specs=pl.BlockSpec((None, rows, tn), lambda l, j: (l, 0, j)),
        out_shape=jax.ShapeDtypeStruct((DEPTH, rows, 6 * D_MODEL), F32),
        compiler_params=_cparams(2),
        name="ada",
    )(vpad, w_ada, b_ada.reshape(DEPTH, 1, 6 * D_MODEL))


def _inproj_kernel(x_ref, mul_ref, sh_ref, w_ref, cos_ref, sa_ref, sb_ref,
                   bq_ref, bk_ref, bvt_ref, cq_ref, ckd_ref, cvdt_ref, *, rope):
    h = _rms_mod(x_ref[...], mul_ref[...], sh_ref[...]).astype(BF16)
    if rope:
        cos, sa, sb = cos_ref[...], sa_ref[...], sb_ref[...]

    def rot(y):
        if not rope:
            return y
        return y * cos + pltpu.roll(y, LANES - 16, 1) * sa + pltpu.roll(y, 16, 1) * sb

    def dup(y):
        lane = lax.broadcasted_iota(jnp.int32, y.shape, 1)
        sw = pltpu.roll(y, 64, 1)
        return jnp.concatenate([jnp.where(lane < 64, y, sw), jnp.where(lane < 64, sw, y)], axis=1)

    for col0, ref, scale in ((0, bq_ref, SCALE * LOG2E), (512, bk_ref, None), (1536, cq_ref, SCALE * LOG2E)):
        y = _dot(h, w_ref[0, :, col0:col0 + 512])
        if scale is not None:
            y = y * scale
        for g in range(4):
            ref[:, g * LANES:(g + 1) * LANES] = rot(y[:, g * LANES:(g + 1) * LANES]).astype(BF16)
    ykv = _dot(h, w_ref[0, :, 2048:2304])
    ckd_ref[...] = dup(rot(ykv[:, :LANES])).astype(BF16)
    cvdt_ref[...] = dup(ykv[:, LANES:]).T.astype(BF16)
    bvt_ref[...] = _dot(h, w_ref[0, :, 1024:1536]).T.astype(BF16)


def _inproj_call(x, mul, sh, w_in, tables, *, layer, tm, rope):
    bsz, length, _ = x.shape
    tok = lambda w: pl.BlockSpec((None, tm, w), lambda b, i: (b, i, 0))
    vec = pl.BlockSpec((None, 1, D_MODEL), lambda b, i: (b, 0, 0))
    tab = pl.BlockSpec((tm, LANES), lambda b, i: (i, 0))
    tok_t = lambda w: pl.BlockSpec((None, w, tm), lambda b, i: (b, 0, i))
    out = lambda w: jax.ShapeDtypeStruct((bsz, length, w), BF16)
    out_t = lambda w: jax.ShapeDtypeStruct((bsz, w, length), BF16)
    return pl.pallas_call(
        functools.partial(_inproj_kernel, rope=rope),
        grid=(bsz, length // tm),
        in_specs=[tok(D_MODEL), vec, vec,
                  _layer_cols_spec(w_in, layer, 2 * MIX_W, ATTN_COLS), tab, tab, tab],
        out_specs=[tok(512), tok(512), tok_t(MIX_W), tok(512), tok(256), tok_t(2 * LANES)],
        out_shape=[out(512), out(512), out_t(MIX_W), out(512), out(256), out_t(2 * LANES)],
        compiler_params=_cparams(2),
        name="inproj",
    )(x, mul, sh, w_in, *tables)


SAFE_EXP2_RANGE = 90.0


def _diff_kernel(sc_ref, q_ref, qall_ref, g_ref, ind_ref, *refs, seg_lens, kc, nq):
    o_ref, p_scr, f_scr, safe_scr = refs[-4:]
    t = pl.program_id(0)
    tq = q_ref.shape[0]
    lanei = lax.broadcasted_iota(jnp.int32, (1, 2 * tq), 1)

    @pl.when(t == 0)
    def _():
        p_scr[...] = jnp.zeros(p_scr.shape, BF16)
        f_scr[...] = jnp.zeros(f_scr.shape, F32)

    chunks = []
    row0 = 0
    for si, n_keys in enumerate(seg_lens):
        for st in range(0, n_keys, kc):
            size = min(kc, n_keys - st)
            chunks.append((refs[2 * si], refs[2 * si + 1], st, size, row0))
            row0 += size

    def max_sq_norm(x):
        xf = x.astype(F32)
        return jnp.max(_dot((xf * xf).astype(BF16), ind_ref[...]), axis=0, keepdims=True)

    @pl.when(t % nq == 0)
    def _():
        kn = None
        for k_ref, _, st, size, _ in chunks:
            n2 = max_sq_norm(k_ref[st:st + size, :])
            kn = n2 if kn is None else jnp.maximum(kn, n2)
        qn = None
        n_q = qall_ref.shape[0]
        for st in range(0, n_q, 2 * kc):
            n2 = max_sq_norm(qall_ref[st:min(st + 2 * kc, n_q), :])
            qn = n2 if qn is None else jnp.maximum(qn, n2)
        bound2 = jnp.max(kn * qn)
        safe_scr[0] = (bound2 < SAFE_EXP2_RANGE * SAFE_EXP2_RANGE).astype(jnp.int32)

    safe = safe_scr[0] == 1

    def step(max_free):
        q = q_ref[...]
        lane = lax.broadcasted_iota(jnp.int32, q.shape, 1)
        zero = jnp.zeros_like(q)
        qq = jnp.concatenate([jnp.where(lane < 64, q, zero), jnp.where(lane >= 64, q, zero)], axis=0)
        f_prev = f_scr[...]
        mcs, lcs = [], []
        acc = None
        for c, (k_ref, vt_ref, st, size, r0) in enumerate(chunks):
            f1 = f_prev[c:c + 1, 0:tq].astype(BF16)
            f2 = f_prev[c:c + 1, tq:2 * tq].astype(BF16)
            a_t = p_scr[r0:r0 + size, 0:tq] * f1 - p_scr[r0:r0 + size, tq:2 * tq] * f2
            part = _dot(vt_ref[:, st:st + size], a_t)
            acc = part if acc is None else acc + part
            s = _dot_nt(k_ref[st:st + size, :], qq)
            if max_free:
                p = jnp.exp2(s)
            else:
                mc = jnp.max(s, axis=0, keepdims=True)
                p = jnp.exp2(s - mc)
                mcs.append(mc)
            lcs.append(jnp.sum(p, axis=0, keepdims=True))
            p_scr[r0:r0 + size, :] = p.astype(BF16)

        if max_free:
            es = [None] * len(chunks)
            l_all = functools.reduce(lambda a, b: a + b, lcs)
        else:
            m_all = functools.reduce(jnp.maximum, mcs)
            es = [jnp.exp2(mc - m_all) for mc in mcs]
            l_all = functools.reduce(lambda a, b: a + b, [lc * e for lc, e in zip(lcs, es)])
        coef = jnp.where(lanei < tq, 1.0, sc_ref[0]) / l_all
        for c, e in enumerate(es):
            f_scr[c:c + 1, :] = coef if e is None else e * coef

        o = acc.T
        ms = jnp.mean(o * o, axis=-1, keepdims=True)
        o_ref[...] = (o * lax.rsqrt(ms + EPS) * g_ref[...] * sc_ref[1]).astype(BF16)

    pl.when(safe)(lambda: step(True))
    pl.when(jnp.logical_not(safe))(lambda: step(False))


def _diff_call(scal, q, subln_g, segs, *, tq, kc):
    bsz, lq, _ = q.shape
    nq = lq // tq
    n_blocks = bsz * B_HEADS * nq
    half = np.arange(LANES) // HEAD_DIM
    ind = jnp.asarray(half[:, None] == half[None, :], BF16)

    def block(t):
        b, r = t // (B_HEADS * nq), t % (B_HEADS * nq)
        return b, r // nq, r % nq

    cur = lambda t: block(jnp.minimum(t, n_blocks - 1))
    prev = lambda t: block(jnp.maximum(t - 1, 0))

    def q_map(t):
        b, h, i = cur(t)
        return b, i, h

    def kq_map(t):
        b, h, _ = cur(t)
        return b, 0, h

    def vt_map(t):
        b, h, _ = prev(t)
        return b, h, 0

    def o_map(t):
        b, h, i = prev(t)
        return b, i, h

    in_specs = [pl.BlockSpec(memory_space=pltpu.SMEM),
                pl.BlockSpec((None, tq, LANES), q_map),
                pl.BlockSpec((None, lq, LANES), kq_map),
                pl.BlockSpec((1, LANES), lambda t: (0, 0)),
                _const_spec(ind)]
    args = [scal, q, q, subln_g, ind]
    for k, vt in segs:
        n_keys = k.shape[1]
        in_specs += [pl.BlockSpec((None, n_keys, LANES), kq_map),
                     pl.BlockSpec((None, LANES, n_keys), vt_map)]
        args += [k, vt]
    seg_lens = tuple(k.shape[1] for k, _ in segs)
    n_chunks = sum(-(-n // kc) for n in seg_lens)
    return pl.pallas_call(
        functools.partial(_diff_kernel, seg_lens=seg_lens, kc=kc, nq=nq),
        grid=(n_blocks + 1,),
        in_specs=in_specs,
        out_specs=pl.BlockSpec((None, tq, LANES), o_map),
        out_shape=jax.ShapeDtypeStruct((bsz, lq, MIX_W), BF16),
        scratch_shapes=[pltpu.VMEM((sum(seg_lens), 2 * tq), BF16),
                        pltpu.VMEM((-(-n_chunks // 8) * 8, 2 * tq), F32),
                        pltpu.SMEM((1,), jnp.int32)],
        compiler_params=_cparams(1),
        name="diff",
    )(*args)


WIN_SUB = 8


def _win_kernel(sink_ref, q_ref, *refs, local):
    o_ref = refs[-1]
    n_sub = q_ref.shape[0] // BLOCK
    kx_ref, vxt_ref = refs[-3], refs[-2]
    n = pl.program_id(1)
    nb = pl.num_programs(1)
    nq = C_Q_HEADS // 2 * BLOCK
    lane = lax.broadcasted_iota(jnp.int32, (BLOCK, LANES), 1)
    headi = lax.broadcasted_iota(jnp.int32, (1, nq), 1) // BLOCK
    if local:
        kp_ref, kc_ref, kn_ref, vpt_ref, vct_ref, vnt_ref = refs[:6]
        key = lax.broadcasted_iota(jnp.int32, (BLOCK, nq), 0)
        qry = lax.broadcasted_iota(jnp.int32, (BLOCK, nq), 1) & (BLOCK - 1)
        tri_prev, tri_next = key >= qry, key <= qry

    probs = [(sub, kvh) for sub in range(n_sub) for kvh in range(2)]
    s_l, sink_l, vt_l = [], [], []
    for sub, kvh in probs:
        ksl = slice(kvh * LANES, (kvh + 1) * LANES)
        rows = slice(sub * BLOCK, (sub + 1) * BLOCK)
        parts = []
        for g in range(4):
            grp = kvh * 2 + g // 2
            qg = q_ref[rows, grp * LANES:(grp + 1) * LANES]
            parts.append(jnp.where((lane < 64) if g % 2 == 0 else (lane >= 64), qg, jnp.zeros_like(qg)))
        qs = jnp.concatenate(parts, axis=0)
        sink = jnp.zeros((1, nq), F32)
        for g in range(4):
            sink = jnp.where(headi == g, sink_ref[kvh * 4 + g] * LOG2E, sink)
        if local:
            kb, vb = [], []
            for j in (sub - 1, sub, sub + 1):
                if j < 0:
                    kb.append(kp_ref[:, ksl]), vb.append(vpt_ref[ksl, :])
                elif j >= n_sub:
                    kb.append(kn_ref[:, ksl]), vb.append(vnt_ref[ksl, :])
                else:
                    kb.append(kc_ref[j * BLOCK:(j + 1) * BLOCK, ksl])
                    vb.append(vct_ref[ksl, j * BLOCK:(j + 1) * BLOCK])
            k_all = jnp.concatenate(kb + [kx_ref[:, ksl]], axis=0)
            vt_all = jnp.concatenate(vb + [vxt_ref[ksl, :]], axis=1)
        else:
            k_all, vt_all = kx_ref[:, ksl], vxt_ref[ksl, :]
        s = _dot_nt(k_all, qs)
        if local:
            valid_prev = tri_prev if sub > 0 else tri_prev & (n > 0)
            valid_next = tri_next if sub < n_sub - 1 else tri_next & (n < nb - 1)
            s = jnp.concatenate([jnp.where(valid_prev, s[0:BLOCK], NEG_INF), s[BLOCK:2 * BLOCK],
                                 jnp.where(valid_next, s[2 * BLOCK:3 * BLOCK], NEG_INF), s[3 * BLOCK:]], axis=0)
        s_l.append(s)
        sink_l.append(sink)
        vt_l.append(vt_all)
    m_l = [jnp.maximum(jnp.max(s, axis=0, keepdims=True), sink) for s, sink in zip(s_l, sink_l)]
    p_l = [jnp.exp2(s - m) for s, m in zip(s_l, m_l)]
    l_l = [jnp.sum(p, axis=0, keepdims=True) + jnp.exp2(sink - m) for p, sink, m in zip(p_l, sink_l, m_l)]
    ot_l = [_dot(vt, p.astype(BF16)) / l for vt, p, l in zip(vt_l, p_l, l_l)]
    for (sub, kvh), o_t in zip(probs, ot_l):
        o = o_t.T
        for j in range(2):
            grp = kvh * 2 + j
            o_ref[sub * BLOCK:(sub + 1) * BLOCK, grp * LANES:(grp + 1) * LANES] = jnp.where(
                lane < 64, o[2 * j * BLOCK:(2 * j + 1) * BLOCK], o[(2 * j + 1) * BLOCK:(2 * j + 2) * BLOCK]
            ).astype(BF16)


def _win_call(sinks, q, kd, vdt, kxd, vxdt, *, local):
    bsz, lq, _ = q.shape
    nb = lq // BLOCK
    n_sub = min(WIN_SUB, nb)
    tq = n_sub * BLOCK
    in_specs = [pl.BlockSpec(memory_space=pltpu.SMEM), pl.BlockSpec((None, tq, MIX_W), lambda b, n: (b, n, 0))]
    args = [sinks, q]
    if local:
        prv = lambda n: jnp.maximum(n * n_sub - 1, 0)
        nxt = lambda n: jnp.minimum((n + 1) * n_sub, nb - 1)
        in_specs += [pl.BlockSpec((None, BLOCK, 2 * LANES), lambda b, n: (b, prv(n), 0)),
                     pl.BlockSpec((None, tq, 2 * LANES), lambda b, n: (b, n, 0)),
                     pl.BlockSpec((None, BLOCK, 2 * LANES), lambda b, n: (b, nxt(n), 0)),
                     pl.BlockSpec((None, 2 * LANES, BLOCK), lambda b, n: (b, 0, prv(n))),
                     pl.BlockSpec((None, 2 * LANES, tq), lambda b, n: (b, 0, n)),
                     pl.BlockSpec((None, 2 * LANES, BLOCK), lambda b, n: (b, 0, nxt(n)))]
        args += [kd] * 3 + [vdt] * 3
    n_ctx = kxd.shape[1]
    in_specs += [pl.BlockSpec((None, n_ctx, 2 * LANES), lambda b, n: (b, 0, 0)),
                 pl.BlockSpec((None, 2 * LANES, n_ctx), lambda b, n: (b, 0, 0))]
    args += [kxd, vxdt]
    return pl.pallas_call(
        functools.partial(_win_kernel, local=local),
        grid=(bsz, lq // tq),
        in_specs=in_specs,
        out_specs=pl.BlockSpec((None, tq, MIX_W), lambda b, n: (b, n, 0)),
        out_shape=jax.ShapeDtypeStruct((bsz, lq, MIX_W), BF16),
        compiler_params=_cparams(2),
        name="win",
    )(*args)


def _merge_kernel(x_ref, mul_ref, sh_ref, gate_ref, wuv_ref, wg_ref, lng_ref, lnb_ref, ws_ref, bs_ref,
                  ob_ref, oc_ref, wbr_ref, wout_ref, o_ref):
    x = x_ref[...]
    tm = x.shape[0]
    h = _rms_mod(x, mul_ref[...], sh_ref[...]).astype(BF16)
    u = _gelu_tanh(_dot(h, wuv_ref[0, :, 0:MIX_W]))
    v = _gelu_tanh(_dot(h, wuv_ref[0, :, MIX_W:2 * MIX_W]))
    mu = jnp.mean(v, axis=-1, keepdims=True)
    vc = v - mu
    var = jnp.mean(vc * vc, axis=-1, keepdims=True)
    vn = (vc * lax.rsqrt(var + EPS) * lng_ref[...] + lnb_ref[...]).astype(BF16)
    n_chunks = tm // BLOCK
    mixed = []
    for g in range(A_GROUPS):
        gs = slice(g * LANES, (g + 1) * LANES)
        rhs = jnp.concatenate([vn[c * BLOCK:(c + 1) * BLOCK, gs] for c in range(n_chunks)], axis=1)
        mixed.append(_dot(ws_ref[g], rhs))
    o_a = (u * jnp.concatenate(
        [jnp.concatenate([mixed[g][:, c * LANES:(c + 1) * LANES] + bs_ref[g] for g in range(A_GROUPS)], axis=1)
         for c in range(n_chunks)], axis=0)).astype(BF16)
    y = None
    for i, o_i in enumerate((o_a, ob_ref[...], oc_ref[...])):
        gate = jax.nn.sigmoid(_dot(h, wg_ref[0, :, i * D_MODEL:(i + 1) * D_MODEL]))
        t = gate * _dot(o_i, wbr_ref[i])
        y = t if y is None else y + t
    o_ref[...] = x + gate_ref[...] * _dot(y.astype(BF16), wout_ref[...])


def _merge_call(x, mul, sh, gate, w_in, ln_g, ln_b, w_s, b_s, o_b, o_c, w_br, w_out, *, layer, tm):
    bsz, length, _ = x.shape
    tok = lambda w: pl.BlockSpec((None, tm, w), lambda b, i: (b, i, 0))
    vec = pl.BlockSpec((None, 1, D_MODEL), lambda b, i: (b, 0, 0))
    full = functools.partial(_layer_spec, layer=layer)
    return pl.pallas_call(
        _merge_kernel,
        grid=(bsz, length // tm),
        in_specs=[tok(D_MODEL), vec, vec, vec, _layer_cols_spec(w_in, layer, 0, 2 * MIX_W),
                  _layer_cols_spec(w_in, layer, GATE_COL0, 3 * D_MODEL), full(ln_g), full(ln_b), full(w_s), full(b_s),
                  tok(MIX_W), tok(MIX_W), full(w_br), full(w_out)],
        out_specs=tok(D_MODEL),
        out_shape=jax.ShapeDtypeStruct(x.shape, F32),
        compiler_params=_cparams(2),
        name="merge",
    )(x, mul, sh, gate, w_in, w_in, ln_g, ln_b, w_s, b_s, o_b, o_c, w_br, w_out)


HALO = 8
MXU_TILE = 256
FF_EDGES = (0, 6 * MXU_TILE, D_FF)


def _ffn_kernel(x_ref, xp_ref, xn_ref, mul_ref, sh_ref, gate_ref, wg_ref, cw_ref, cb_ref, wu_ref, wd_ref,
                fg_ref, o_ref, *, final):
    i = pl.program_id(1)
    nt = pl.num_programs(1)
    x = x_ref[...]
    tm = x.shape[0]
    xe = jnp.concatenate([xp_ref[...], x, xn_ref[...]], axis=0)
    he = _rms_mod(xe, mul_ref[...], sh_ref[...]).astype(BF16)
    hm = he[HALO:HALO + tm]
    rowe = lax.broadcasted_iota(jnp.int32, (tm + 2 * HALO, 1), 0)
    keep = ((rowe >= HALO) | (i > 0)) & ((rowe < HALO + tm) | (i < nt - 1))
    spans = [slice(c0, c1) for c0, c1 in zip(FF_EDGES[:-1], FF_EDGES[1:])]
    a_l = [jnp.where(keep, _dot(he, wg_ref[:, cs]), 0.0) for cs in spans]
    u_l = [_dot(hm, wu_ref[:, cs]) for cs in spans]
    z_l = []
    for cs, a, u in zip(spans, a_l, u_l):
        a_prev = pltpu.roll(a, 1, 0)[HALO:HALO + tm]
        a_next = pltpu.roll(a, tm + 2 * HALO - 1, 0)[HALO:HALO + tm]
        a = (a_prev * cw_ref[0:1, cs] + a[HALO:HALO + tm] * cw_ref[1:2, cs] + a_next * cw_ref[2:3, cs]
             + cb_ref[:, cs])
        z_l.append((a * jax.nn.sigmoid(a) * u).astype(BF16))
    acc = None
    for cs, z in zip(spans, z_l):
        t = _dot(z, wd_ref[cs, :])
        acc = t if acc is None else acc + t
    out = x + gate_ref[...] * acc
    if final:
        ms = jnp.mean(out * out, axis=-1, keepdims=True)
        out = out * lax.rsqrt(ms + EPS) * fg_ref[...]
    o_ref[...] = out


def _ffn_call(x, mul, sh, gate, w_gate, conv_w, conv_b, w_up, w_down, final_g, *, layer, tm, final):
    bsz, length, _ = x.shape
    per = tm // HALO
    last = length // HALO - 1
    tok = pl.BlockSpec((None, tm, D_MODEL), lambda b, i: (b, i, 0))
    prv = pl.BlockSpec((None, HALO, D_MODEL), lambda b, i: (b, jnp.maximum(i * per - 1, 0), 0))
    nxt = pl.BlockSpec((None, HALO, D_MODEL), lambda b, i: (b, jnp.minimum((i + 1) * per, last), 0))
    vec = pl.BlockSpec((None, 1, D_MODEL), lambda b, i: (b, 0, 0))
    full = functools.partial(_layer_spec, layer=layer)
    return pl.pallas_call(
        functools.partial(_ffn_kernel, final=final),
        grid=(bsz, length // tm),
        in_specs=[tok, prv, nxt, vec, vec, vec, full(w_gate), full(conv_w), full(conv_b), full(w_up),
                  full(w_down), _const_spec(final_g)],
        out_specs=tok,
        out_shape=jax.ShapeDtypeStruct(x.shape, F32),
        compiler_params=_cparams(2),
        name="ffn",
    )(x, x, x, mul, sh, gate, w_gate, conv_w, conv_b, w_up, w_down, final_g)


def _rope_tables(length):
    pos = jnp.arange(length)
    rows = (pos // GRID_W).astype(F32)
    cols = (pos % GRID_W).astype(F32)
    half = HEAD_DIM // 2
    inv = ROPE_BASE ** (-jnp.arange(0, half, 2, dtype=F32) / half)
    ang_r = rows[:, None] * inv[None, :]
    ang_c = cols[:, None] * inv[None, :]
    zero = jnp.zeros_like(ang_r)
    cos = jnp.concatenate([jnp.cos(ang_r)] * 2 + [jnp.cos(ang_c)] * 2, axis=1)
    sa = jnp.concatenate([-jnp.sin(ang_r), zero, -jnp.sin(ang_c), zero], axis=1)
    sb = jnp.concatenate([zero, jnp.sin(ang_r), zero, jnp.sin(ang_c)], axis=1)
    return tuple(jnp.tile(t, (1, LANES // HEAD_DIM)) for t in (cos, sa, sb))


def kernel(x, c, ctx, c_ctx, w_ada, b_ada, norm1_g, w_in, sgu_ln_g, sgu_ln_b, w_s, b_s, lam_q1, lam_k1,
           lam_q2, lam_k2, diff_subln_g, sinks, w_branch, w_out, norm2_g, w_gate, conv_w, conv_b, w_up,
           w_down, final_g):
    bsz, length, _ = x.shape
    n_ctx = ctx.shape[1]
    tables = _rope_tables(length)
    ctx_tables = tuple(t[:n_ctx] for t in tables)

    rows = -(-(bsz + 1) // 8) * 8
    vpad = jnp.zeros((rows, D_MODEL), F32).at[:bsz].set(c).at[bsz].set(c_ctx)
    ada = _ada_call(vpad, w_ada, b_ada)

    w_in_b = w_in.astype(BF16)
    w_s_b = w_s.astype(BF16)
    b_s_b = jnp.broadcast_to(b_s[..., None], b_s.shape + (LANES,))
    w_br_b = w_branch.astype(BF16)
    w_out_b = w_out.astype(BF16)
    w_gate_b = w_gate.astype(BF16)
    w_up_b = w_up.astype(BF16)
    w_down_b = w_down.astype(BF16)
    final_g2 = final_g.reshape(1, D_MODEL)
    mix_w = (w_in_b, sgu_ln_g.reshape(DEPTH, 1, MIX_W), sgu_ln_b.reshape(DEPTH, 1, MIX_W), w_s_b, b_s_b)
    ffn_w = (w_gate_b, conv_w, conv_b.reshape(DEPTH, 1, D_FF), w_up_b, w_down_b, final_g2)

    for l in range(DEPTH):
        last = l == DEPTH - 1
        m = ada[l]
        sh1, sc1, g1, sh2, sc2, g2 = [m[:, j * D_MODEL:(j + 1) * D_MODEL] for j in range(6)]
        mul1 = norm1_g[l][None, :] * (1.0 + sc1)
        mul2 = norm2_g[l][None, :] * (1.0 + sc2)
        lat = lambda a: a[:bsz, None, :]
        cx = lambda a: jnp.broadcast_to(a[bsz][None, None, :], (bsz, 1, D_MODEL))

        lam_init = 0.8 - 0.6 * float(np.exp(-0.3 * l))
        lam = jnp.exp(jnp.sum(lam_q1[l] * lam_k1[l])) - jnp.exp(jnp.sum(lam_q2[l] * lam_k2[l])) + lam_init
        scal = jnp.stack([lam, jnp.float32(1.0 - lam_init)]).astype(F32)
        subln = diff_subln_g[l].reshape(1, LANES)

        bq, bk, bvt, cq, ckd, cvdt = _inproj_call(x, lat(mul1), lat(sh1), w_in_b, tables, layer=l, tm=TOKEN_TILE,
                                                  rope=True)
        xbq, xbk, xbvt, xcq, xckd, xcvdt = _inproj_call(ctx, cx(mul1), cx(sh1), w_in_b, ctx_tables, layer=l,
                                                        tm=n_ctx, rope=False)

        o_b = _diff_call(scal, bq, subln, [(bk, bvt), (xbk, xbvt)], tq=DIFF_Q_TILE, kc=DIFF_KEY_CHUNK)
        o_c = _win_call(sinks[l], cq, ckd, cvdt, xckd, xcvdt, local=True)
        x_mid = _merge_call(x, lat(mul1), lat(sh1), lat(g1), *mix_w, o_b, o_c, w_br_b, w_out_b, layer=l,
                            tm=TOKEN_TILE)
        if not last:
            xo_b = _diff_call(scal, xbq, subln, [(xbk, xbvt)], tq=n_ctx, kc=n_ctx)
            xo_c = _win_call(sinks[l], xcq, None, None, xckd, xcvdt, local=False)
            flat = lambda a: a.reshape(1, bsz * n_ctx, a.shape[-1])
            one = lambda a: a[bsz][None, None, :]
            ctx_mid = _merge_call(flat(ctx), one(mul1), one(sh1), one(g1), *mix_w, flat(xo_b), flat(xo_c), w_br_b,
                                  w_out_b, layer=l, tm=min(TOKEN_TILE, bsz * n_ctx)).reshape(ctx.shape)
        x = _ffn_call(x_mid, lat(mul2), lat(sh2), lat(g2), *ffn_w, layer=l, tm=TOKEN_TILE, final=last)
        if not last:
            ctx = _ffn_call(ctx_mid, cx(mul2), cx(sh2), cx(g2), *ffn_w, layer=l, tm=n_ctx, final=False)
    return x
```
